```python
import jax, jax.numpy as jnp
from jax import lax
import numpy as np

D_MODEL = 1024
BATCH = 8
SEQ = 2048
DEPTH = 1
DEC_BATCH = 128
DEC_SEQ = 4
PAST_LEN = 16384
PAGE_SIZE = 128

N_META = 16
D_FF = 2816
CONV_A_WIDTH = 3
D_CONV_A = D_MODEL
SSM_EXPAND = 2
D_SSM = SSM_EXPAND * D_MODEL
SSM_HEAD_DIM = 64
SSM_HEADS = D_SSM // SSM_HEAD_DIM
SSM_GROUPS = 4
HEADS_PER_GROUP = SSM_HEADS // SSM_GROUPS
SSM_STATE = 128
SSM_CONV_WIDTH = 4
SSM_CHUNK = 128
D_XBC = D_SSM + 2 * SSM_GROUPS * SSM_STATE
PROJ_SPLITS = (D_CONV_A, D_CONV_A, D_CONV_A, D_SSM, D_XBC, SSM_HEADS, D_MODEL, D_MODEL)
D_IN_PROJ = 3 * D_CONV_A + D_SSM + D_XBC + SSM_HEADS + 2 * D_MODEL
EPS = 1e-6

kernel_name = "hybrid_shortconv_ssd_macaron_step"


def rmsnorm(x, w):
    xf = x.astype(jnp.float32)
    y = xf * lax.rsqrt(jnp.mean(xf * xf, axis=-1, keepdims=True) + EPS)
    return (y * w.astype(jnp.float32)).astype(x.dtype)


def swiglu(x, w_gu, w_down):
    g, u = jnp.split(x @ w_gu, 2, axis=-1)
    return (jax.nn.silu(g) * u) @ w_down


def causal_dwconv(x, buf, w):
    width = w.shape[0]
    seqlen = x.shape[1]
    xp = jnp.concatenate([buf.astype(x.dtype), x], axis=1)
    y = xp[:, 0:seqlen] * w[0]
    for k in range(1, width):
        y = y + xp[:, k:k + seqlen] * w[k]
    return y, xp[:, seqlen:]


def ssd_chunked(xh, dt, a, bm, cm, h0, chunk):
    b, l = xh.shape[:2]
    c = l // chunk
    x = xh.reshape(b, c, chunk, SSM_GROUPS, HEADS_PER_GROUP, SSM_HEAD_DIM)
    dtc = dt.reshape(b, c, chunk, SSM_GROUPS, HEADS_PER_GROUP)
    bc = bm.reshape(b, c, chunk, SSM_GROUPS, SSM_STATE)
    cc = cm.reshape(b, c, chunk, SSM_GROUPS, SSM_STATE)
    acs = jnp.cumsum(dtc * a.reshape(SSM_GROUPS, HEADS_PER_GROUP), axis=2)
    xdt = x * dtc[..., None]
    mask = jnp.tril(jnp.ones((chunk, chunk), dtype=bool))[:, :, None, None]
    seg = acs[:, :, :, None] - acs[:, :, None, :]
    lmat = jnp.exp(jnp.where(mask, seg, -jnp.inf))
    cb = jnp.einsum('bcqgn,bcsgn->bcqsg', cc, bc)
    y_diag = jnp.einsum('bcqsg,bcqsgr,bcsgrp->bcqgrp', cb, lmat, xdt)
    decay = jnp.exp(acs[:, :, -1:] - acs)
    states = jnp.einsum('bcsgn,bcsgr,bcsgrp->bcgrpn', bc, decay, xdt)
    chunk_decay = jnp.exp(acs[:, :, -1])
    h_init = h0.reshape(b, SSM_GROUPS, HEADS_PER_GROUP, SSM_HEAD_DIM, SSM_STATE)

    def step(h, inp):
        s, d = inp
        return h * d[..., None, None] + s, h

    h_last, h_prev = lax.scan(step, h_init, (jnp.swapaxes(states, 0, 1), jnp.swapaxes(chunk_decay, 0, 1)))
    h_prev = jnp.swapaxes(h_prev, 0, 1)
    y_off = jnp.einsum('bcqgn,bcgrpn,bcqgr->bcqgrp', cc, h_prev, jnp.exp(acs))
    y = (y_diag + y_off).reshape(b, l, SSM_HEADS, SSM_HEAD_DIM)
    return y, h_last.reshape(b, SSM_HEADS, SSM_HEAD_DIM, SSM_STATE)


def token_mix(u, buf_a, buf_ssm_conv, h_ssm, segments, w_in, conv_a_w, w_a_out,
              ssm_conv_w, ssm_conv_b, dt_bias, a_log, d_skip, ssm_norm_w, w_b_out, w_o):
    f32 = jnp.float32
    bsz, seqlen, _ = u.shape
    idx = np.cumsum(PROJ_SPLITS)[:-1].tolist()
    a_b, a_c, a_h, z, xbc, dt_raw, g_a, g_b = jnp.split(u @ w_in, idx, axis=-1)
    conv_a, new_buf_a = causal_dwconv(a_c * a_h, buf_a, conv_a_w)
    y_a = (a_b * conv_a) @ w_a_out
    xbc_c, new_buf_ssm = causal_dwconv(xbc, buf_ssm_conv, ssm_conv_w)
    xbc_c = jax.nn.silu(xbc_c + ssm_conv_b)
    xs, bm, cm = jnp.split(xbc_c, [D_SSM, D_SSM + SSM_GROUPS * SSM_STATE], axis=-1)
    xh = xs.astype(f32).reshape(bsz, seqlen, SSM_HEADS, SSM_HEAD_DIM)
    bm = bm.astype(f32).reshape(bsz, seqlen, SSM_GROUPS, SSM_STATE)
    cm = cm.astype(f32).reshape(bsz, seqlen, SSM_GROUPS, SSM_STATE)
    dt = jax.nn.softplus(dt_raw.astype(f32) + dt_bias.astype(f32))
    a = -jnp.exp(a_log.astype(f32))
    h = h_ssm.astype(f32)
    ys = []
    start = 0
    for seg_len, chunk in segments:
        y_seg, h = ssd_chunked(xh[:, start:start + seg_len], dt[:, start:start + seg_len], a,
                               bm[:, start:start + seg_len], cm[:, start:start + seg_len], h, chunk)
        ys.append(y_seg)
        start += seg_len
    y = jnp.concatenate(ys, axis=1) + d_skip.astype(f32)[:, None] * xh
    y = y.reshape(bsz, seqlen, D_SSM) * jax.nn.silu(z.astype(f32))
    yg = y.reshape(bsz, seqlen, SSM_GROUPS, D_SSM // SSM_GROUPS)
    yg = yg * lax.rsqrt(jnp.mean(yg * yg, axis=-1, keepdims=True) + EPS)
    y_b = (yg.reshape(bsz, seqlen, D_SSM) * ssm_norm_w.astype(f32)).astype(u.dtype) @ w_b_out
    merged = jax.nn.sigmoid(g_a) * y_a + jax.nn.sigmoid(g_b) * y_b
    return merged @ w_o, new_buf_a, new_buf_ssm, h.astype(h_ssm.dtype)


def run_trunk(x, bufs_a, bufs_ssm_conv, hs_ssm, segments, norm_ffn1, ffn1_w_gu, ffn1_w_down,
              norm_mix, w_in, conv_a_w, w_a_out, ssm_conv_w, ssm_conv_b, dt_bias, a_log,
              d_skip, ssm_norm_w, w_b_out, w_o, norm_ffn2, ffn2_w_gu, ffn2_w_down, norm_final):
    new_a, new_c, new_h = [], [], []
    h = x
    for i in range(DEPTH):
        h = h + 0.5 * swiglu(rmsnorm(h, norm_ffn1[i]), ffn1_w_gu[i], ffn1_w_down[i])
        m, ba, bc, hs = token_mix(rmsnorm(h, norm_mix[i]), bufs_a[i], bufs_ssm_conv[i], hs_ssm[i],
                                  segments, w_in[i], conv_a_w[i], w_a_out[i], ssm_conv_w[i],
                                  ssm_conv_b[i], dt_bias[i], a_log[i], d_skip[i], ssm_norm_w[i],
                                  w_b_out[i], w_o[i])
        h = h + m
        h = h + 0.5 * swiglu(rmsnorm(h, norm_ffn2[i]), ffn2_w_gu[i], ffn2_w_down[i])
        new_a.append(ba)
        new_c.append(bc)
        new_h.append(hs)
    return rmsnorm(h, norm_final), jnp.stack(new_a), jnp.stack(new_c), jnp.stack(new_h)


def setup_inputs(seed: int = 0) -> dict:
    key = jax.random.key(seed)
    ks = list(jax.random.split(key, 32))
    nrm = jax.random.normal
    f32 = jnp.float32

    def gain(k):
        return 1.0 + 0.01 * nrm(k, (DEPTH, D_MODEL), f32)

    dt0 = jnp.exp(jax.random.uniform(ks[20], (DEPTH, SSM_HEADS), f32, np.log(1e-3), np.log(1e-1)))
    return {
        "x_prompt": nrm(ks[0], (BATCH, SEQ, D_MODEL), f32),
        "x_sample": nrm(ks[1], (DEC_BATCH, DEC_SEQ, D_MODEL), f32),
        "state_conv_a": 0.5 * nrm(ks[2], (DEPTH, DEC_BATCH, CONV_A_WIDTH - 1, D_CONV_A), f32),
        "state_ssm_conv": nrm(ks[3], (DEPTH, DEC_BATCH, SSM_CONV_WIDTH - 1, D_XBC), f32),
        "state_ssm": 0.1 * nrm(ks[4], (DEPTH, DEC_BATCH, SSM_HEADS, SSM_HEAD_DIM, SSM_STATE), f32),
        "meta_tokens": nrm(ks[5], (N_META, D_MODEL), f32),
        "norm_ffn1": gain(ks[6]),
        "ffn1_w_gu": nrm(ks[7], (DEPTH, D_MODEL, 2 * D_FF), f32) * D_MODEL ** -0.5,
        "ffn1_w_down": nrm(ks[8], (DEPTH, D_FF, D_MODEL), f32) * D_FF ** -0.5,
        "norm_mix": gain(ks[9]),
        "w_in": nrm(ks[10], (DEPTH, D_MODEL, D_IN_PROJ), f32) * D_MODEL ** -0.5,
        "conv_a_w": nrm(ks[11], (DEPTH, CONV_A_WIDTH, D_CONV_A), f32) * CONV_A_WIDTH ** -0.5,
        "w_a_out": nrm(ks[12], (DEPTH, D_CONV_A, D_MODEL), f32) * D_CONV_A ** -0.5,
        "ssm_conv_w": nrm(ks[13], (DEPTH, SSM_CONV_WIDTH, D_XBC), f32) * SSM_CONV_WIDTH ** -0.5,
        "ssm_conv_b": 0.01 * nrm(ks[14], (DEPTH, D_XBC), f32),
        "dt_bias": dt0 + jnp.log(-jnp.expm1(-dt0)),
        "a_log": jnp.log(jax.random.uniform(ks[15], (DEPTH, SSM_HEADS), f32, 1.0, 16.0)),
        "d_skip": 1.0 + 0.01 * nrm(ks[16], (DEPTH, SSM_HEADS), f32),
        "ssm_norm_w": 1.0 + 0.01 * nrm(ks[17], (DEPTH, D_SSM), f32),
        "w_b_out": nrm(ks[18], (DEPTH, D_SSM, D_MODEL), f32) * D_SSM ** -0.5,
        "w_o": nrm(ks[19], (DEPTH, D_MODEL, D_MODEL), f32) * D_MODEL ** -0.5,
        "norm_ffn2": gain(ks[21]),
        "ffn2_w_gu": nrm(ks[22], (DEPTH, D_MODEL, 2 * D_FF), f32) * D_MODEL ** -0.5,
        "ffn2_w_down": nrm(ks[23], (DEPTH, D_FF, D_MODEL), f32) * D_FF ** -0.5,
        "norm_final": 1.0 + 0.01 * nrm(ks[24], (D_MODEL,), f32),
    }


def reference(x_prompt, x_sample, state_conv_a, state_ssm_conv, state_ssm, meta_tokens,
              norm_ffn1, ffn1_w_gu, ffn1_w_down, norm_mix, w_in, conv_a_w, w_a_out,
              ssm_conv_w, ssm_conv_b, dt_bias, a_log, d_skip, ssm_norm_w, w_b_out, w_o,
              norm_ffn2, ffn2_w_gu, ffn2_w_down, norm_final):
    weights = (norm_ffn1, ffn1_w_gu, ffn1_w_down, norm_mix, w_in, conv_a_w, w_a_out,
               ssm_conv_w, ssm_conv_b, dt_bias, a_log, d_skip, ssm_norm_w, w_b_out, w_o,
               norm_ffn2, ffn2_w_gu, ffn2_w_down, norm_final)
    bsz = x_prompt.shape[0]
    dt_p = x_prompt.dtype
    meta = jnp.broadcast_to(meta_tokens.astype(dt_p)[None], (bsz, N_META, D_MODEL))
    xp = jnp.concatenate([meta, x_prompt], axis=1)
    z_a = jnp.zeros((DEPTH, bsz, CONV_A_WIDTH - 1, D_CONV_A), dt_p)
    z_c = jnp.zeros((DEPTH, bsz, SSM_CONV_WIDTH - 1, D_XBC), dt_p)
    z_h = jnp.zeros((DEPTH, bsz, SSM_HEADS, SSM_HEAD_DIM, SSM_STATE), dt_p)
    seg_prompt = ((N_META, N_META), (x_prompt.shape[1], SSM_CHUNK))
    yp, prompt_conv_a, prompt_ssm_conv, prompt_ssm = run_trunk(xp, z_a, z_c, z_h, seg_prompt, *weights)
    y_prompt = yp[:, N_META:]
    seg_sample = ((x_sample.shape[1], x_sample.shape[1]),)
    y_sample, sample_conv_a, sample_ssm_conv, sample_ssm = run_trunk(
        x_sample, state_conv_a, state_ssm_conv, state_ssm, seg_sample, *weights)
    return (y_prompt, y_sample, prompt_conv_a, prompt_ssm_conv, prompt_ssm,
            sample_conv_a, sample_ssm_conv, sample_ssm)
```

```python
import functools

import jax
import jax.numpy as jnp
from jax import lax
from jax.experimental import pallas as pl
from jax.experimental.pallas import tpu as pltpu

D_MODEL = 1024
D_FF = 2816
D_SSM = 2048
SSM_HEADS = 32
SSM_HEAD_DIM = 64
SSM_GROUPS = 4
HEADS_PER_GROUP = SSM_HEADS // SSM_GROUPS
SSM_STATE = 128
D_BC = SSM_GROUPS * SSM_STATE
D_XBC = D_SSM + 2 * D_BC
CONV_A_TAPS = 3
CONV_X_TAPS = 4
N_META = 16
EPS = 1e-6

LANES = 128
SUBLANES = 8
CHUNK = 128
N_PAIRS = SSM_HEADS // 2
D_PROJ = 3 * D_MODEL + D_SSM + D_XBC + LANES + 2 * D_MODEL
VMEM_LIMIT = 56 * 1024 * 1024

F32 = jnp.float32
BF16 = jnp.bfloat16


def _dot(a, b):
    return jnp.dot(a, b, preferred_element_type=F32)


def _rms(x, w):
    return x * lax.rsqrt(jnp.mean(x * x, axis=-1, keepdims=True) + EPS) * w


def _silu(x):
    return x * jax.nn.sigmoid(x)


def _softplus(x):
    return jnp.maximum(x, 0.0) + jnp.log1p(jnp.exp(-jnp.abs(x)))


def _split3(x):
    hi = x.astype(BF16)
    r = x - hi.astype(F32)
    mid = r.astype(BF16)
    lo = (r - mid.astype(F32)).astype(BF16)
    return hi, mid, lo


def _split2(x):
    hi = x.astype(BF16)
    lo = (x - hi.astype(F32)).astype(BF16)
    return hi, lo


def _resident(shape):
    return pl.BlockSpec(shape, lambda *_: (0,) * len(shape), pipeline_mode=pl.Buffered(1))


def _ffn_kernel(x_ref, nw_ref, wg_ref, wu_ref, wd_ref, fw_ref, o_ref, xn_ref, acc_ref, *, final_norm):
    j = pl.program_id(1)

    @pl.when(j == 0)
    def _():
        xn_ref[...] = _rms(x_ref[...], nw_ref[...]).astype(BF16)
        acc_ref[...] = jnp.zeros_like(acc_ref)

    xn = xn_ref[...]
    act = (_silu(_dot(xn, wg_ref[...])) * _dot(xn, wu_ref[...])).astype(BF16)
    acc_ref[...] += _dot(act, wd_ref[...])

    @pl.when(j == pl.num_programs(1) - 1)
    def _():
        h = x_ref[...] + 0.5 * acc_ref[...]
        if final_norm:
            h = _rms(h, fw_ref[...])
        o_ref[...] = h


def _ffn(x, nw, w_gu, w_down, fw, *, tm, tf, final_norm):
    t = x.shape[0]
    nf = D_FF // tf
    return pl.pallas_call(
        functools.partial(_ffn_kernel, final_norm=final_norm),
        grid=(t // tm, nf),
        in_specs=[
            pl.BlockSpec((tm, D_MODEL), lambda i, j: (i, 0)),
            pl.BlockSpec((1, D_MODEL), lambda i, j: (0, 0)),
            pl.BlockSpec((D_MODEL, tf), lambda i, j: (0, j)),
            pl.BlockSpec((D_MODEL, tf), lambda i, j: (0, j + nf)),
            pl.BlockSpec((tf, D_MODEL), lambda i, j: (j, 0)),
            pl.BlockSpec((1, D_MODEL), lambda i, j: (0, 0)),
        ],
        out_specs=pl.BlockSpec((tm, D_MODEL), lambda i, j: (i, 0)),
        out_shape=jax.ShapeDtypeStruct((t, D_MODEL), F32),
        scratch_shapes=[pltpu.VMEM((tm, D_MODEL), BF16), pltpu.VMEM((tm, D_MODEL), F32)],
        compiler_params=pltpu.CompilerParams(
            dimension_semantics=("parallel", "arbitrary"), vmem_limit_bytes=VMEM_LIMIT),
        name="ffn",
    )(x, nw, w_gu, w_gu, w_down, fw)


def _inproj_kernel(h_ref, nw_ref, w_ref, ab_ref, ach_ref, z_ref, xbc_ref, dt_ref, sga_ref, sgb_ref):
    u = _rms(h_ref[...], nw_ref[...]).astype(BF16)

    def proj(c0, width):
        return _dot(u, w_ref[:, c0:c0 + width])

    ab_ref[...] = proj(0, D_MODEL).astype(ab_ref.dtype)
    ach_ref[...] = (proj(D_MODEL, D_MODEL) * proj(2 * D_MODEL, D_MODEL)).astype(ach_ref.dtype)
    c0 = 3 * D_MODEL
    for k in range(D_SSM // D_MODEL):
        z_ref[:, k * D_MODEL:(k + 1) * D_MODEL] = proj(c0 + k * D_MODEL, D_MODEL).astype(z_ref.dtype)
    c0 += D_SSM
    for k in range(D_XBC // D_MODEL):
        xbc_ref[:, k * D_MODEL:(k + 1) * D_MODEL] = proj(c0 + k * D_MODEL, D_MODEL).astype(xbc_ref.dtype)
    c0 += D_XBC
    dt_ref[...] = proj(c0, LANES)
    c0 += LANES
    sga_ref[...] = jax.nn.sigmoid(proj(c0, D_MODEL)).astype(sga_ref.dtype)
    sgb_ref[...] = jax.nn.sigmoid(proj(c0 + D_MODEL, D_MODEL)).astype(sgb_ref.dtype)


def _inproj(h, nw, w_packed, *, tm, act_dtype):
    t = h.shape[0]
    widths = (D_MODEL, D_MODEL, D_SSM, D_XBC, LANES, D_MODEL, D_MODEL)
    dtypes = (act_dtype, act_dtype, act_dtype, act_dtype, F32, act_dtype, act_dtype)
    return pl.pallas_call(
        _inproj_kernel,
        grid=(t // tm,),
        in_specs=[
            pl.BlockSpec((tm, D_MODEL), lambda i: (i, 0)),
            _resident((1, D_MODEL)),
            _resident((D_MODEL, D_PROJ)),
        ],
        out_specs=[pl.BlockSpec((tm, w), lambda i: (i, 0)) for w in widths],
        out_shape=[jax.ShapeDtypeStruct((t, w), d) for w, d in zip(widths, dtypes)],
        compiler_params=pltpu.CompilerParams(
            dimension_semantics=("parallel",), vmem_limit_bytes=VMEM_LIMIT),
        name="inproj",
    )(h, nw, w_packed)


def _outproj_kernel(ya_ref, yb_ref, sga_ref, sgb_ref, h_ref, wa_ref, wb_ref, wo_ref, o_ref):
    y_a = _dot(ya_ref[...].astype(BF16), wa_ref[...])
    y_b = _dot(yb_ref[...].astype(BF16), wb_ref[...])
    merged = sga_ref[...].astype(F32) * y_a + sgb_ref[...].astype(F32) * y_b
    o_ref[...] = h_ref[...] + _dot(merged.astype(BF16), wo_ref[...])


def _outproj(ya, yb, sga, sgb, h, wa, wb, wo, *, tm):
    t = h.shape[0]
    row = lambda w: pl.BlockSpec((tm, w), lambda i: (i, 0))
    return pl.pallas_call(
        _outproj_kernel,
        grid=(t // tm,),
        in_specs=[row(D_MODEL), row(D_SSM), row(D_MODEL), row(D_MODEL), row(D_MODEL),
                  _resident((D_MODEL, D_MODEL)), _resident((D_SSM, D_MODEL)),
                  _resident((D_MODEL, D_MODEL))],
        out_specs=row(D_MODEL),
        out_shape=jax.ShapeDtypeStruct((t, D_MODEL), F32),
        compiler_params=pltpu.CompilerParams(
            dimension_semantics=("parallel",), vmem_limit_bytes=VMEM_LIMIT),
        name="outproj",
    )(ya, yb, sga, sgb, h, wa, wb, wo)


def _gated_group_norm(y, xs, z, dskip, normw):
    y = (y + dskip * xs) * _silu(z)
    gw = D_SSM // SSM_GROUPS
    parts = []
    for g in range(SSM_GROUPS):
        yg = y[:, g * gw:(g + 1) * gw]
        parts.append(yg * lax.rsqrt(jnp.mean(yg * yg, axis=-1, keepdims=True) + EPS))
    return jnp.concatenate(parts, axis=-1) * normw


def _mixer_kernel(ab_ref, ach_ref, z_ref, xbc_ref, dt_ref, caw_ref, cxw_ref, cxb_ref, dtb_ref,
                  alog_ref, dskip_ref, normw_ref, ca0_ref, cx0_ref, h0_ref,
                  ya_ref, yb_ref, cao_ref, cxo_ref, hTo_ref, ho_ref,
                  abuf, xbuf, hT, ybuf, xcbuf, *, tb, valid_len):
    i = pl.program_id(1)
    q = CHUNK

    @pl.when(i == 0)
    def _():
        abuf[0:SUBLANES, :] = ca0_ref[...]
        xbuf[0:SUBLANES, :] = cx0_ref[...]
        hT[...] = h0_ref[...]

    abuf[SUBLANES:SUBLANES + tb, :] = ach_ref[...].astype(F32)
    xbuf[SUBLANES:SUBLANES + tb, :] = xbc_ref[...].astype(F32)

    conv_a = caw_ref[0:1, :] * abuf[SUBLANES - 2:SUBLANES - 2 + tb, :]
    for k in range(1, CONV_A_TAPS):
        conv_a += caw_ref[k:k + 1, :] * abuf[SUBLANES - 2 + k:SUBLANES - 2 + k + tb, :]
    ya_ref[...] = (ab_ref[...].astype(F32) * conv_a).astype(ya_ref.dtype)

    a_neg = -jnp.exp(alog_ref[...])
    rows = lax.broadcasted_iota(jnp.int32, (q, q), 0)
    cols = lax.broadcasted_iota(jnp.int32, (q, q), 1)
    causal = rows >= cols
    tri = causal.astype(BF16)
    first_head = lax.broadcasted_iota(jnp.int32, (q, LANES), 1) < SSM_HEAD_DIM

    def chunk_body(c, carry):
        r0 = pl.multiple_of(c * q, q)
        off = SUBLANES - (CONV_X_TAPS - 1)
        for lt in range(D_XBC // LANES):
            ls = slice(lt * LANES, (lt + 1) * LANES)
            blk = xbuf[pl.ds(r0, q + SUBLANES), ls]
            conv = cxw_ref[0:1, ls] * blk[off:off + q]
            for k in range(1, CONV_X_TAPS):
                conv += cxw_ref[k:k + 1, ls] * blk[off + k:off + k + q]
            xcbuf[:, ls] = _silu(conv + cxb_ref[:, ls])
        xs = xcbuf[:, :D_SSM]
        xs_bf = xs.astype(BF16)

        dt = _softplus(dt_ref[pl.ds(r0, q), :] + dtb_ref[...])
        if valid_len < tb:
            tok = r0 + lax.broadcasted_iota(jnp.int32, (q, LANES), 0)
            dt = jnp.where(tok < valid_len, dt, 0.0)
        hi, mid, lo = _split3(dt * a_neg)
        acs3 = _dot(tri, jnp.concatenate([hi, mid, lo], axis=1))
        acs = acs3[:, :LANES] + acs3[:, LANES:2 * LANES] + acs3[:, 2 * LANES:]
        acs_t = acs.T
        dt_t = dt.T

        for g in range(SSM_GROUPS):
            b_g = xcbuf[:, D_SSM + g * SSM_STATE:D_SSM + (g + 1) * SSM_STATE]
            c_g = xcbuf[:, D_SSM + D_BC + g * SSM_STATE:D_SSM + D_BC + (g + 1) * SSM_STATE]
            b_t = b_g.T
            cb = _dot(c_g.astype(BF16), b_t.astype(BF16))
            for jp in range(HEADS_PER_GROUP // 2):
                pair = g * (HEADS_PER_GROUP // 2) + jp
                x_pair = xs_bf[:, pair * LANES:(pair + 1) * LANES]
                h_prev = hT[pair]
                h_prev_bf = h_prev.astype(BF16)
                y_e, h_e = [], []
                for e in range(2):
                    hd = 2 * pair + e
                    acs_col = jnp.broadcast_to(acs[:, hd:hd + 1], (q, LANES))
                    acs_row = acs_t[hd:hd + 1, :]
                    dt_row = dt_t[hd:hd + 1, :]
                    acs_end = acs_t[hd:hd + 1, q - 1:q]
                    decay = jnp.exp(jnp.where(causal, acs_col - acs_row, -jnp.inf))
                    m = (cb * decay * dt_row).astype(BF16)
                    c_scaled = (c_g * jnp.exp(acs_col)).astype(BF16)
                    y_e.append(_dot(m, x_pair) + _dot(c_scaled, h_prev_bf))
                    w_row = jnp.exp(acs_end - acs_row) * dt_row
                    h_e.append(h_prev * jnp.exp(acs_end) + _dot((b_t * w_row).astype(BF16), x_pair))
                ybuf[:, pair * LANES:(pair + 1) * LANES] = jnp.where(first_head, y_e[0], y_e[1])
                hT[pair] = jnp.where(first_head, h_e[0], h_e[1])

        yn = _gated_group_norm(ybuf[...], xs, z_ref[pl.ds(r0, q), :].astype(F32),
                               dskip_ref[...], normw_ref[...])
        yb_ref[pl.ds(r0, q), :] = yn.astype(yb_ref.dtype)
        return carry

    lax.fori_loop(0, tb // q, chunk_body, 0)

    abuf[0:SUBLANES, :] = abuf[valid_len:valid_len + SUBLANES, :]
    xbuf[0:SUBLANES, :] = xbuf[valid_len:valid_len + SUBLANES, :]

    @pl.when(i == pl.num_programs(1) - 1)
    def _():
        cao_ref[0] = abuf[0:SUBLANES, :]
        cxo_ref[0] = xbuf[0:SUBLANES, :]
        hTo_ref[0] = hT[...]
        for pair in range(N_PAIRS):
            ho_ref[0, pair] = hT[pair].T


def _mixer(ab, ach, z, xbc, dt, p, ca0, cx0, h0, *, batch, seqlen, tb, valid_len, act_dtype):
    nt = seqlen // tb
    row = lambda w: pl.BlockSpec((tb, w), lambda b, i: (b * nt + i, 0))
    full = lambda shape: pl.BlockSpec(shape, lambda b, i: (0,) * len(shape))
    per_b = lambda shape: pl.BlockSpec((1,) + shape, lambda b, i: (b,) + (0,) * len(shape))
    t = batch * seqlen
    return pl.pallas_call(
        functools.partial(_mixer_kernel, tb=tb, valid_len=valid_len),
        grid=(batch, nt),
        in_specs=[row(D_MODEL), row(D_MODEL), row(D_SSM), row(D_XBC), row(LANES),
                  full((CONV_A_TAPS, D_MODEL)), full((CONV_X_TAPS, D_XBC)), full((1, D_XBC)),
                  full((1, LANES)), full((1, LANES)), full((1, D_SSM)), full((1, D_SSM)),
                  full((SUBLANES, D_MODEL)), full((SUBLANES, D_XBC)),
                  full((N_PAIRS, SSM_STATE, LANES))],
        out_specs=[row(D_MODEL), row(D_SSM),
                   per_b((SUBLANES, D_MODEL)), per_b((SUBLANES, D_XBC)),
                   per_b((N_PAIRS, SSM_STATE, LANES)), per_b((N_PAIRS, LANES, SSM_STATE))],
        out_shape=[jax.ShapeDtypeStruct((t, D_MODEL), act_dtype),
                   jax.ShapeDtypeStruct((t, D_SSM), act_dtype),
                   jax.ShapeDtypeStruct((batch, SUBLANES, D_MODEL), F32),
                   jax.ShapeDtypeStruct((batch, SUBLANES, D_XBC), F32),
                   jax.ShapeDtypeStruct((batch, N_PAIRS, SSM_STATE, LANES), F32),
                   jax.ShapeDtypeStruct((batch, N_PAIRS, LANES, SSM_STATE), F32)],
        scratch_shapes=[pltpu.VMEM((SUBLANES + tb, D_MODEL), F32),
                        pltpu.VMEM((SUBLANES + tb, D_XBC), F32),
                        pltpu.VMEM((N_PAIRS, SSM_STATE, LANES), F32),
                        pltpu.VMEM((CHUNK, D_SSM), F32),
                        pltpu.VMEM((CHUNK, D_XBC), F32)],
        compiler_params=pltpu.CompilerParams(
            dimension_semantics=("arbitrary", "arbitrary"), vmem_limit_bytes=VMEM_LIMIT),
        name="mixer",
    )(ab, ach, z, xbc, dt, p["conv_a_w"], p["ssm_conv_w"], p["ssm_conv_b"], p["dt_bias"],
      p["a_log"], p["d_skip"], p["ssm_norm_w"], ca0, cx0, h0)


SEQ_BLOCK = SUBLANES


def _decode_mixer_kernel(ab_ref, ach_ref, z_ref, xbc_ref, dt_ref, sa_ref, sx_ref, h0_ref,
                         caw_ref, cxw_ref, cxb_ref, dtb_ref, alog_ref, dskip_ref, normw_ref,
                         gsum_ref, expand_ref,
                         ya_ref, yb_ref, sao_ref, sxo_ref, ho_ref,
                         xc_buf, bc_buf, os_buf, xw_buf, cd_buf, y_buf, *, n_tok):
    sb = SEQ_BLOCK

    xp = [sa_ref[k] for k in range(CONV_A_TAPS - 1)] + [ach_ref[t] for t in range(n_tok)]
    for t in range(n_tok):
        conv = caw_ref[0:1, :] * xp[t]
        for k in range(1, CONV_A_TAPS):
            conv += caw_ref[k:k + 1, :] * xp[t + k]
        ya_ref[t] = ab_ref[t] * conv
    for k in range(CONV_A_TAPS - 1):
        sao_ref[k] = xp[n_tok + k]

    xp = [sx_ref[k] for k in range(CONV_X_TAPS - 1)] + [xbc_ref[t] for t in range(n_tok)]
    for t in range(n_tok):
        conv = cxw_ref[0:1, :] * xp[t]
        for k in range(1, CONV_X_TAPS):
            conv += cxw_ref[k:k + 1, :] * xp[t + k]
        xc_buf[t * sb:(t + 1) * sb, :] = _silu(conv + cxb_ref[...])
    for k in range(CONV_X_TAPS - 1):
        sxo_ref[k] = xp[n_tok + k]

    a_neg = -jnp.exp(alog_ref[...])
    dts, acs = [], []
    for t in range(n_tok):
        dts.append(_softplus(dt_ref[t] + dtb_ref[...]))
        acs.append(dts[t] * a_neg + (acs[t - 1] if t else 0.0))

    pairs = [(qi, si) for qi in range(n_tok) for si in range(qi + 1)]
    prods = [xc_buf[qi * sb:(qi + 1) * sb, D_SSM + D_BC:] * xc_buf[si * sb:(si + 1) * sb, D_SSM:D_SSM + D_BC]
             for qi, si in pairs]
    p_hi, p_lo = _split2(jnp.concatenate(prods, axis=0))
    cbh = _dot(p_hi, gsum_ref[...]) + _dot(p_lo, gsum_ref[...])
    coef = [jnp.exp(acs[t]) for t in range(n_tok)]
    coef += [cbh[k * sb:(k + 1) * sb] * jnp.exp(acs[qi] - acs[si]) * dts[si]
             for k, (qi, si) in enumerate(pairs)]
    coef += [jnp.exp(acs[n_tok - 1] - acs[s]) * dts[s] for s in range(n_tok)]
    coef.append(jnp.exp(acs[n_tok - 1]))
    c_hi, c_lo = _split2(jnp.concatenate(coef, axis=0))
    coef_x = _dot(c_hi, expand_ref[...]) + _dot(c_lo, expand_ref[...])
    slab = lambda k: coef_x[k * sb:(k + 1) * sb]

    def store_tiles(buf, r0, val):
        for lt in range(val.shape[1] // LANES):
            buf[lt, r0:r0 + val.shape[0], :] = val[:, lt * LANES:(lt + 1) * LANES]

    def load_rows(buf, lt0, n_lt, rows):
        return jnp.concatenate([buf[lt0 + j, rows, :] for j in range(n_lt)], axis=1)

    store_tiles(bc_buf, 0, xc_buf[:, D_SSM:])
    store_tiles(os_buf, 0, coef_x[0:n_tok * sb])
    diag = {pr: slab(n_tok + k) for k, pr in enumerate(pairs)}
    k0 = n_tok + len(pairs)
    for s in range(n_tok):
        store_tiles(xw_buf, s * sb, slab(k0 + s) * xc_buf[s * sb:(s + 1) * sb, :D_SSM])
    cd_hi, cd_lo = _split2(slab(k0 + n_tok))
    store_tiles(cd_buf, 0, cd_hi.astype(F32))
    store_tiles(cd_buf, sb, cd_lo.astype(F32))

    for t in range(n_tok):
        acc = diag[(t, 0)] * xc_buf[0:sb, :D_SSM]
        for s in range(1, t + 1):
            acc += diag[(t, s)] * xc_buf[s * sb:(s + 1) * sb, :D_SSM]
        store_tiles(y_buf, t * sb, acc)

    gw = HEADS_PER_GROUP * SSM_HEAD_DIM
    gt = gw // LANES
    k_rows = 2 * SUBLANES
    krow = lax.broadcasted_iota(jnp.int32, (k_rows, SSM_STATE), 0)
    ones_rows = jnp.where((krow >= n_tok) & (krow < n_tok + 2), 1.0, 0.0).astype(BF16)
    pad_lhs = jnp.zeros((k_rows - n_tok - 2, gw), F32)
    pad_rhs = jnp.zeros((k_rows - n_tok, SSM_STATE), F32)

    def seq_body(b, carry):
        tok_rows = pl.ds(b, n_tok, stride=sb)
        for g in range(SSM_GROUPS):
            b_bg = bc_buf[g, tok_rows, :]
            c_bg = bc_buf[SSM_GROUPS + g, tok_rows, :]
            heads = pl.ds(g * HEADS_PER_GROUP, HEADS_PER_GROUP)
            h_bg = h0_ref[b, heads].reshape(gw, SSM_STATE)
            y_off = lax.dot_general(c_bg.astype(BF16), h_bg.astype(BF16),
                                    (((1,), (1,)), ((), ())), preferred_element_type=F32)
            y_off = y_off * load_rows(os_buf, g * gt, gt, tok_rows)
            for j in range(gt):
                y_buf[g * gt + j, tok_rows, :] = (y_buf[g * gt + j, tok_rows, :]
                                                  + y_off[:, j * LANES:(j + 1) * LANES])
            lhs = jnp.concatenate(
                [load_rows(xw_buf, g * gt, gt, tok_rows),
                 load_rows(cd_buf, g * gt, gt, pl.ds(b, 2, stride=sb)), pad_lhs],
                axis=0).astype(BF16)
            rhs = jnp.concatenate(
                [jnp.concatenate([b_bg, pad_rhs], axis=0).astype(BF16), ones_rows], axis=1)
            upd = lax.dot_general(lhs, rhs, (((0,), (0,)), ((), ())), preferred_element_type=F32)
            new = h_bg * upd[:, SSM_STATE:] + upd[:, :SSM_STATE]
            ho_ref[b, heads] = new.reshape(HEADS_PER_GROUP, SSM_HEAD_DIM, SSM_STATE)
        return carry

    lax.fori_loop(0, sb, seq_body, 0)

    xs_all = xc_buf[:, :D_SSM]
    z_all = jnp.concatenate([z_ref[t] for t in range(n_tok)], axis=0)
    y_all = load_rows(y_buf, 0, D_SSM // LANES, slice(None))
    yn = _gated_group_norm(y_all, xs_all, z_all, dskip_ref[...], normw_ref[...])
    for t in range(n_tok):
        yb_ref[t] = yn[t * sb:(t + 1) * sb]


def _decode_mixer(ab, ach, z, xbc, dt, sa, sx, h0, p, gsum, expand, *, n_tok, n_seq):
    sb = SEQ_BLOCK
    tok = lambda w: pl.BlockSpec((n_tok, sb, w), lambda i: (0, i, 0))
    st = lambda k, w: pl.BlockSpec((k, sb, w), lambda i: (0, i, 0))
    hblk = pl.BlockSpec((sb, SSM_HEADS, SSM_HEAD_DIM, SSM_STATE), lambda i: (i, 0, 0, 0))
    full = lambda shape: pl.BlockSpec(shape, lambda i: (0,) * len(shape))
    return pl.pallas_call(
        functools.partial(_decode_mixer_kernel, n_tok=n_tok),
        grid=(n_seq // sb,),
        in_specs=[tok(D_MODEL), tok(D_MODEL), tok(D_SSM), tok(D_XBC), tok(LANES),
                  st(CONV_A_TAPS - 1, D_MODEL), st(CONV_X_TAPS - 1, D_XBC), hblk,
                  full((CONV_A_TAPS, D_MODEL)), full((CONV_X_TAPS, D_XBC)), full((1, D_XBC)),
                  full((1, LANES)), full((1, LANES)), full((1, D_SSM)), full((1, D_SSM)),
                  full((D_BC, LANES)), full((LANES, D_SSM))],
        out_specs=[tok(D_MODEL), tok(D_SSM), st(CONV_A_TAPS - 1, D_MODEL),
                   st(CONV_X_TAPS - 1, D_XBC), hblk],
        out_shape=[jax.ShapeDtypeStruct((n_tok, n_seq, D_MODEL), F32),
                   jax.ShapeDtypeStruct((n_tok, n_seq, D_SSM), F32),
                   jax.ShapeDtypeStruct((CONV_A_TAPS - 1, n_seq, D_MODEL), F32),
                   jax.ShapeDtypeStruct((CONV_X_TAPS - 1, n_seq, D_XBC), F32),
                   jax.ShapeDtypeStruct((n_seq, SSM_HEADS, SSM_HEAD_DIM, SSM_STATE), F32)],
        scratch_shapes=[pltpu.VMEM((n_tok * sb, D_XBC), F32),
                        pltpu.VMEM((2 * D_BC // LANES, n_tok * sb, LANES), F32),
                        pltpu.VMEM((D_SSM // LANES, n_tok * sb, LANES), F32),
                        pltpu.VMEM((D_SSM // LANES, n_tok * sb, LANES), F32),
                        pltpu.VMEM((D_SSM // LANES, 2 * sb, LANES), F32),
                        pltpu.VMEM((D_SSM // LANES, n_tok * sb, LANES), F32)],
        compiler_params=pltpu.CompilerParams(
            dimension_semantics=("parallel",), vmem_limit_bytes=VMEM_LIMIT),
        name="decode_mixer",
    )(ab, ach, z, xbc, dt, sa, sx, h0, p["conv_a_w"], p["ssm_conv_w"], p["ssm_conv_b"],
      p["dt_bias"], p["a_log"], p["d_skip"], p["ssm_norm_w"], gsum, expand)


def _row_tile(t, cap):
    if t <= cap:
        return t
    best = LANES
    for cand in range(LANES, cap + 1, LANES):
        if t % cand == 0:
            best = cand
    return best


def kernel(x_prompt, x_sample, state_conv_a, state_ssm_conv, state_ssm, meta_tokens, norm_ffn1, ffn1_w_gu, ffn1_w_down, norm_mix, w_in, conv_a_w, w_a_out, ssm_conv_w, ssm_conv_b, dt_bias, a_log, d_skip, ssm_norm_w, w_b_out, w_o, norm_ffn2, ffn2_w_gu, ffn2_w_down, norm_final):
    bsz, seqlen, _ = x_prompt.shape
    n_seq, n_tok, _ = x_sample.shape
    n_dt = SSM_HEADS
    c_dt = 3 * D_MODEL + D_SSM + D_XBC

    w_gu1, w_dn1 = ffn1_w_gu[0].astype(BF16), ffn1_w_down[0].astype(BF16)
    w_gu2, w_dn2 = ffn2_w_gu[0].astype(BF16), ffn2_w_down[0].astype(BF16)
    w_proj = jnp.concatenate(
        [w_in[0, :, :c_dt], jnp.pad(w_in[0, :, c_dt:c_dt + n_dt], ((0, 0), (0, LANES - n_dt))),
         w_in[0, :, c_dt + n_dt:]], axis=1).astype(BF16)
    wa, wb, wo = w_a_out[0].astype(BF16), w_b_out[0].astype(BF16), w_o[0].astype(BF16)
    pad_heads = lambda v: jnp.pad(v[0], (0, LANES - n_dt))[None]
    p = {
        "conv_a_w": conv_a_w[0], "ssm_conv_w": ssm_conv_w[0], "ssm_conv_b": ssm_conv_b[0][None],
        "dt_bias": pad_heads(dt_bias), "a_log": pad_heads(a_log),
        "d_skip": jnp.repeat(d_skip[0], SSM_HEAD_DIM)[None], "ssm_norm_w": ssm_norm_w[0][None],
    }
    n1, nm, n2, nf = norm_ffn1[0][None], norm_mix[0][None], norm_ffn2[0][None], norm_final[None]

    def pre(x, act_dtype):
        t = x.shape[0]
        h = _ffn(x, n1, w_gu1, w_dn1, nf, tm=_row_tile(t, 512), tf=D_FF // 2, final_norm=False)
        return h, _inproj(h, nm, w_proj, tm=_row_tile(t, 256), act_dtype=act_dtype)

    def post(ya, yb, sga, sgb, h):
        t = h.shape[0]
        h = _outproj(ya, yb, sga, sgb, h, wa, wb, wo, tm=_row_tile(t, 512))
        return _ffn(h, n2, w_gu2, w_dn2, nf, tm=_row_tile(t, 512), tf=D_FF // 2, final_norm=True)

    x_meta = jnp.pad(meta_tokens, ((0, CHUNK - N_META), (0, 0)))
    _, (ab, ach, z, xbc, dt, _, _) = pre(x_meta, BF16)
    zeros = lambda *s: jnp.zeros(s, F32)
    _, _, ca_m, cx_m, hT_m, _ = _mixer(
        ab, ach, z, xbc, dt, p, zeros(SUBLANES, D_MODEL), zeros(SUBLANES, D_XBC),
        zeros(N_PAIRS, SSM_STATE, LANES), batch=1, seqlen=CHUNK, tb=CHUNK, valid_len=N_META,
        act_dtype=BF16)

    h_p, (ab, ach, z, xbc, dt, sga, sgb) = pre(x_prompt.reshape(bsz * seqlen, D_MODEL), BF16)
    tb = _row_tile(seqlen, 256)
    ya, yb, ca_p, cx_p, _, h_out_p = _mixer(
        ab, ach, z, xbc, dt, p, ca_m[0], cx_m[0], hT_m[0], batch=bsz, seqlen=seqlen, tb=tb,
        valid_len=tb, act_dtype=BF16)
    y_prompt = post(ya, yb, sga, sgb, h_p).reshape(bsz, seqlen, D_MODEL)
    prompt_conv_a = ca_p[None, :, SUBLANES - (CONV_A_TAPS - 1):, :]
    prompt_ssm_conv = cx_p[None, :, SUBLANES - (CONV_X_TAPS - 1):, :]
    prompt_ssm = h_out_p.reshape(1, bsz, SSM_HEADS, SSM_HEAD_DIM, SSM_STATE)

    x_s = jnp.swapaxes(x_sample, 0, 1).reshape(n_tok * n_seq, D_MODEL)
    h_s, (ab, ach, z, xbc, dt, sga, sgb) = pre(x_s, F32)
    tsf = lambda a: a.reshape(n_tok, n_seq, a.shape[-1])
    hh = lax.broadcasted_iota(jnp.int32, (D_BC, LANES), 1)
    kk = lax.broadcasted_iota(jnp.int32, (D_BC, LANES), 0)
    gsum = ((hh < SSM_HEADS) & (kk // SSM_STATE == hh // HEADS_PER_GROUP)).astype(BF16)
    eh = lax.broadcasted_iota(jnp.int32, (LANES, D_SSM), 0)
    ec = lax.broadcasted_iota(jnp.int32, (LANES, D_SSM), 1)
    expand = (ec // SSM_HEAD_DIM == eh).astype(BF16)
    ya, yb, sa_new, sx_new, sample_ssm = _decode_mixer(
        tsf(ab), tsf(ach), tsf(z), tsf(xbc), tsf(dt),
        jnp.swapaxes(state_conv_a[0], 0, 1), jnp.swapaxes(state_ssm_conv[0], 0, 1), state_ssm[0],
        p, gsum, expand, n_tok=n_tok, n_seq=n_seq)
    flat = lambda a: a.reshape(n_tok * n_seq, a.shape[-1])
    y_s = post(flat(ya), flat(yb), sga, sgb, h_s)
    y_sample = jnp.swapaxes(y_s.reshape(n_tok, n_seq, D_MODEL), 0, 1)
    sample_conv_a = jnp.swapaxes(sa_new, 0, 1)[None]
    sample_ssm_conv = jnp.swapaxes(sx_new, 0, 1)[None]

    return (y_prompt, y_sample, prompt_conv_a, prompt_ssm_conv, prompt_ssm,
            sample_conv_a, sample_ssm_conv, sample_ssm[None])
```

```python
import functools

import jax
import jax.numpy as jnp
from jax import lax
from jax.experimental import pallas as pl
from jax.experimental.pallas import tpu as pltpu

D_MODEL = 1024
D_FF = 2816
D_SSM = 2048
SSM_HEADS = 32
SSM_HEAD_DIM = 64
SSM_GROUPS = 4
HEADS_PER_GROUP = SSM_HEADS // SSM_GROUPS
SSM_STATE = 128
D_BC = SSM_GROUPS * SSM_STATE
D_XBC = D_SSM + 2 * D_BC
CONV_A_TAPS = 3
CONV_X_TAPS = 4
N_META = 16
EPS = 1e-6

LANES = 128
SUBLANES = 8
CHUNK = 128
D_PROJ = 3 * D_MODEL + D_SSM + D_XBC + LANES + 2 * D_MODEL
VMEM_LIMIT = 56 * 1024 * 1024

F32 = jnp.float32
BF16 = jnp.bfloat16


def _dot(a, b):
    return jnp.dot(a, b, preferred_element_type=F32)


def _rms(x, w):
    return x * lax.rsqrt(jnp.mean(x * x, axis=-1, keepdims=True) + EPS) * w


def _silu(x):
    return x * jax.nn.sigmoid(x)


def _softplus(x):
    return jnp.maximum(x, 0.0) + jnp.log1p(jnp.exp(-jnp.abs(x)))


def _split3(x):
    hi = x.astype(BF16)
    r = x - hi.astype(F32)
    mid = r.astype(BF16)
    lo = (r - mid.astype(F32)).astype(BF16)
    return hi, mid, lo


def _split2(x):
    hi = x.astype(BF16)
    lo = (x - hi.astype(F32)).astype(BF16)
    return hi, lo


def _resident(shape):
    return pl.BlockSpec(shape, lambda *_: (0,) * len(shape), pipeline_mode=pl.Buffered(1))


def _ffn_kernel(x_ref, nw_ref, wg_ref, wu_ref, wd_ref, fw_ref, o_ref, xn_ref, acc_ref, *, final_norm):
    j = pl.program_id(1)

    @pl.when(j == 0)
    def _():
        xn_ref[...] = _rms(x_ref[...], nw_ref[...]).astype(BF16)
        acc_ref[...] = jnp.zeros_like(acc_ref)

    xn = xn_ref[...]
    act = (_silu(_dot(xn, wg_ref[...])) * _dot(xn, wu_ref[...])).astype(BF16)
    acc_ref[...] += _dot(act, wd_ref[...])

    @pl.when(j == pl.num_programs(1) - 1)
    def _():
        h = x_ref[...] + 0.5 * acc_ref[...]
        if final_norm:
            h = _rms(h, fw_ref[...])
        o_ref[...] = h


def _ffn(x, nw, w_gu, w_down, fw, *, tm, tf, final_norm):
    t = x.shape[0]
    nf = D_FF // tf
    return pl.pallas_call(
        functools.partial(_ffn_kernel, final_norm=final_norm),
        grid=(t // tm, nf),
        in_specs=[
            pl.BlockSpec((tm, D_MODEL), lambda i, j: (i, 0)),
            pl.BlockSpec((1, D_MODEL), lambda i, j: (0, 0)),
            pl.BlockSpec((D_MODEL, tf), lambda i, j: (0, j)),
            pl.BlockSpec((D_MODEL, tf), lambda i, j: (0, j + nf)),
            pl.BlockSpec((tf, D_MODEL), lambda i, j: (j, 0)),
            pl.BlockSpec((1, D_MODEL), lambda i, j: (0, 0)),
        ],
        out_specs=pl.BlockSpec((tm, D_MODEL), lambda i, j: (i, 0)),
        out_shape=jax.ShapeDtypeStruct((t, D_MODEL), F32),
        scratch_shapes=[pltpu.VMEM((tm, D_MODEL), BF16), pltpu.VMEM((tm, D_MODEL), F32)],
        compiler_params=pltpu.CompilerParams(
            dimension_semantics=("parallel", "arbitrary"), vmem_limit_bytes=VMEM_LIMIT),
        name="ffn",
    )(x, nw, w_gu, w_gu, w_down, fw)


def _inproj_kernel(h_ref, nw_ref, w_ref, ab_ref, ach_ref, zs_ref, xbc_ref, dt_ref, sga_ref, sgb_ref):
    u = _rms(h_ref[...], nw_ref[...]).astype(BF16)

    def proj(c0, width):
        return _dot(u, w_ref[:, c0:c0 + width])

    ab_ref[...] = proj(0, D_MODEL).astype(ab_ref.dtype)
    ach_ref[...] = (proj(D_MODEL, D_MODEL) * proj(2 * D_MODEL, D_MODEL)).astype(ach_ref.dtype)
    c0 = 3 * D_MODEL
    for k in range(D_SSM // D_MODEL):
        zs_ref[:, k * D_MODEL:(k + 1) * D_MODEL] = _silu(proj(c0 + k * D_MODEL, D_MODEL)).astype(zs_ref.dtype)
    c0 += D_SSM
    for k in range(D_XBC // D_MODEL):
        xbc_ref[:, k * D_MODEL:(k + 1) * D_MODEL] = proj(c0 + k * D_MODEL, D_MODEL).astype(xbc_ref.dtype)
    c0 += D_XBC
    dt_ref[...] = proj(c0, LANES)
    c0 += LANES
    sga_ref[...] = jax.nn.sigmoid(proj(c0, D_MODEL)).astype(sga_ref.dtype)
    sgb_ref[...] = jax.nn.sigmoid(proj(c0 + D_MODEL, D_MODEL)).astype(sgb_ref.dtype)


def _inproj(h, nw, w_packed, *, tm, act_dtype):
    t = h.shape[0]
    widths = (D_MODEL, D_MODEL, D_SSM, D_XBC, LANES, D_MODEL, D_MODEL)
    dtypes = (act_dtype, act_dtype, act_dtype, act_dtype, F32, act_dtype, act_dtype)
    return pl.pallas_call(
        _inproj_kernel,
        grid=(t // tm,),
        in_specs=[
            pl.BlockSpec((tm, D_MODEL), lambda i: (i, 0)),
            _resident((1, D_MODEL)),
            _resident((D_MODEL, D_PROJ)),
        ],
        out_specs=[pl.BlockSpec((tm, w), lambda i: (i, 0)) for w in widths],
        out_shape=[jax.ShapeDtypeStruct((t, w), d) for w, d in zip(widths, dtypes)],
        compiler_params=pltpu.CompilerParams(
            dimension_semantics=("parallel",), vmem_limit_bytes=VMEM_LIMIT),
        name="inproj",
    )(h, nw, w_packed)


COL_BLOCK = 512


def _inproj_conv_kernel(h_ref, nw_ref, w_ref, caw_ref, cxw_ref, cxb_ref, ca0_ref, cx0_ref,
                        ya_ref, zs_ref, xc_ref, dt_ref, sga_ref, sgb_ref, cao_ref, cxo_ref,
                        ca_carry, cx_carry, *, tm, seq_tiles, valid_len):
    i = pl.program_id(0)

    @pl.when(i % seq_tiles == 0)
    def _():
        ca_carry[...] = ca0_ref[...]
        cx_carry[...] = cx0_ref[...]

    u = _rms(h_ref[...], nw_ref[...]).astype(BF16)

    def proj(c0, width=COL_BLOCK):
        return _dot(u, w_ref[:, c0:c0 + width])

    def causal_conv(x, carry_ref, w_ref, taps, cols):
        ext = jnp.concatenate([carry_ref[:, cols], x], axis=0)
        acc = w_ref[taps - 1:taps, cols] * x
        for d in range(1, taps):
            acc += w_ref[taps - 1 - d:taps - d, cols] * ext[SUBLANES - d:SUBLANES - d + tm]
        carry_ref[:, cols] = x[valid_len - SUBLANES:valid_len]
        return acc

    for k in range(D_MODEL // COL_BLOCK):
        cols = slice(k * COL_BLOCK, (k + 1) * COL_BLOCK)
        ach = proj(D_MODEL + k * COL_BLOCK) * proj(2 * D_MODEL + k * COL_BLOCK)
        conv = causal_conv(ach, ca_carry, caw_ref, CONV_A_TAPS, cols)
        ya_ref[:, cols] = (proj(k * COL_BLOCK) * conv).astype(ya_ref.dtype)
    c0 = 3 * D_MODEL
    for k in range(D_SSM // COL_BLOCK):
        cols = slice(k * COL_BLOCK, (k + 1) * COL_BLOCK)
        zs_ref[:, cols] = _silu(proj(c0 + k * COL_BLOCK)).astype(zs_ref.dtype)
    c0 += D_SSM
    for k in range(D_XBC // COL_BLOCK):
        cols = slice(k * COL_BLOCK, (k + 1) * COL_BLOCK)
        conv = causal_conv(proj(c0 + k * COL_BLOCK), cx_carry, cxw_ref, CONV_X_TAPS, cols)
        xc_ref[:, cols] = _silu(conv + cxb_ref[:, cols]).astype(xc_ref.dtype)
    c0 += D_XBC
    dt_ref[...] = proj(c0, LANES)
    c0 += LANES
    for k in range(D_MODEL // COL_BLOCK):
        cols = slice(k * COL_BLOCK, (k + 1) * COL_BLOCK)
        sga_ref[:, cols] = jax.nn.sigmoid(proj(c0 + k * COL_BLOCK)).astype(sga_ref.dtype)
        sgb_ref[:, cols] = jax.nn.sigmoid(proj(c0 + D_MODEL + k * COL_BLOCK)).astype(sgb_ref.dtype)
    cao_ref[0] = ca_carry[...]
    cxo_ref[0] = cx_carry[...]


def _inproj_conv(h, nw, w_packed, p, ca0, cx0, *, tm, seq_tiles, valid_len):
    t = h.shape[0]
    n_seq = t // (tm * seq_tiles)
    widths = (D_MODEL, D_SSM, D_XBC, LANES, D_MODEL, D_MODEL)
    dtypes = (BF16, BF16, BF16, F32, BF16, BF16)
    tail = lambda w: pl.BlockSpec((1, SUBLANES, w), lambda i: (i // seq_tiles, 0, 0))
    return pl.pallas_call(
        functools.partial(_inproj_conv_kernel, tm=tm, seq_tiles=seq_tiles, valid_len=valid_len),
        grid=(t // tm,),
        in_specs=[
            pl.BlockSpec((tm, D_MODEL), lambda i: (i, 0)),
            _resident((1, D_MODEL)),
            _resident((D_MODEL, D_PROJ)),
            _resident((CONV_A_TAPS, D_MODEL)), _resident((CONV_X_TAPS, D_XBC)),
            _resident((1, D_XBC)),
            _resident((SUBLANES, D_MODEL)), _resident((SUBLANES, D_XBC)),
        ],
        out_specs=[pl.BlockSpec((tm, w), lambda i: (i, 0)) for w in widths]
        + [tail(D_MODEL), tail(D_XBC)],
        out_shape=[jax.ShapeDtypeStruct((t, w), d) for w, d in zip(widths, dtypes)]
        + [jax.ShapeDtypeStruct((n_seq, SUBLANES, D_MODEL), F32),
           jax.ShapeDtypeStruct((n_seq, SUBLANES, D_XBC), F32)],
        scratch_shapes=[pltpu.VMEM((SUBLANES, D_MODEL), F32), pltpu.VMEM((SUBLANES, D_XBC), F32)],
        compiler_params=pltpu.CompilerParams(
            dimension_semantics=("arbitrary",), vmem_limit_bytes=VMEM_LIMIT),
        name="inproj_conv",
    )(h, nw, w_packed, p["conv_a_w"], p["ssm_conv_w"], p["ssm_conv_b"], ca0, cx0)


def _outproj_kernel(ya_ref, yb_ref, sga_ref, sgb_ref, h_ref, wa_ref, wb_ref, wo_ref, o_ref):
    y_a = _dot(ya_ref[...].astype(BF16), wa_ref[...])
    y_b = _dot(yb_ref[...].astype(BF16), wb_ref[...])
    merged = sga_ref[...].astype(F32) * y_a + sgb_ref[...].astype(F32) * y_b
    o_ref[...] = h_ref[...] + _dot(merged.astype(BF16), wo_ref[...])


def _outproj(ya, yb, sga, sgb, h, wa, wb, wo, *, tm):
    t = h.shape[0]
    row = lambda w: pl.BlockSpec((tm, w), lambda i: (i, 0))
    return pl.pallas_call(
        _outproj_kernel,
        grid=(t // tm,),
        in_specs=[row(D_MODEL), row(D_SSM), row(D_MODEL), row(D_MODEL), row(D_MODEL),
                  _resident((D_MODEL, D_MODEL)), _resident((D_SSM, D_MODEL)),
                  _resident((D_MODEL, D_MODEL))],
        out_specs=row(D_MODEL),
        out_shape=jax.ShapeDtypeStruct((t, D_MODEL), F32),
        compiler_params=pltpu.CompilerParams(
            dimension_semantics=("parallel",), vmem_limit_bytes=VMEM_LIMIT),
        name="outproj",
    )(ya, yb, sga, sgb, h, wa, wb, wo)


def _gated_group_norm(y, xs, zs, dskip, normw):
    y = (y + dskip * xs) * zs
    gw = D_SSM // SSM_GROUPS
    parts = []
    for g in range(SSM_GROUPS):
        yg = y[:, g * gw:(g + 1) * gw]
        parts.append(yg * lax.rsqrt(jnp.mean(yg * yg, axis=-1, keepdims=True) + EPS))
    return jnp.concatenate(parts, axis=-1) * normw


GROUP_W = HEADS_PER_GROUP * SSM_HEAD_DIM
LOG2E = 1.4426950408889634


def _mixer_kernel(zs_ref, xc_ref, dt_ref, dtb_ref, alog_ref, dskip_ref, normw_ref, expand_ref,
                  h0_ref, yb_ref, hTo_ref, ho_ref, hT, ybuf, *, tb, valid_len):
    i = pl.program_id(1)
    q = CHUNK

    @pl.when(i == 0)
    def _():
        hT[...] = h0_ref[...]

    a_neg = -jnp.exp(alog_ref[...])
    rows = lax.broadcasted_iota(jnp.int32, (q, q), 0)
    cols_i = lax.broadcasted_iota(jnp.int32, (q, q), 1)
    causal = rows >= cols_i
    tri = causal.astype(BF16)
    first_head = lax.broadcasted_iota(jnp.int32, (q, LANES), 1) < SSM_HEAD_DIM

    def chunk_body(c, carry):
        r0 = pl.multiple_of(c * q, q)
        tok = pl.ds(r0, q)
        dt = _softplus(dt_ref[tok, :] + dtb_ref[...])
        if valid_len < tb:
            t_idx = r0 + lax.broadcasted_iota(jnp.int32, (q, LANES), 0)
            dt = jnp.where(t_idx < valid_len, dt, 0.0)
        hi, mid, lo = _split3(dt * a_neg)
        acs3 = _dot(tri, jnp.concatenate([hi, mid, lo], axis=1))
        acs = acs3[:, :LANES] + acs3[:, LANES:2 * LANES] + acs3[:, 2 * LANES:]
        acs2 = acs * LOG2E
        row_t = (acs2 - jnp.log2(dt)).T
        acs_end = acs2[q - 1:q, :]
        cd_hi, cd_lo = _split2(jnp.broadcast_to(jnp.exp2(acs_end), (SUBLANES, LANES)))
        stack = jnp.concatenate(
            [jnp.exp2(acs2).astype(BF16), (jnp.exp2(acs_end - acs2) * dt).astype(BF16),
             cd_hi, cd_lo], axis=0)

        for g in range(SSM_GROUPS):
            gcols = slice(g * GROUP_W, (g + 1) * GROUP_W)
            ex = _dot(stack, expand_ref[:, gcols])
            state_decay = ex[2 * q:2 * q + 1] + ex[2 * q + SUBLANES:2 * q + SUBLANES + 1]
            b_t = xc_ref[tok, D_SSM + g * SSM_STATE:D_SSM + (g + 1) * SSM_STATE].astype(F32).T.astype(BF16)
            c_bf = xc_ref[tok, D_SSM + D_BC + g * SSM_STATE:D_SSM + D_BC + (g + 1) * SSM_STATE]
            cb = _dot(c_bf, b_t)
            h_g = hT[g]
            y_g = ex[0:q] * _dot(c_bf, h_g.astype(BF16))
            xw_g = xc_ref[tok, gcols] * ex[q:2 * q].astype(BF16)
            hT[g] = h_g * state_decay + _dot(b_t, xw_g)
            for jp in range(HEADS_PER_GROUP // 2):
                pair = g * (HEADS_PER_GROUP // 2) + jp
                lanes = slice(pair * LANES, (pair + 1) * LANES)
                x_pair = xc_ref[tok, lanes]
                zero = jnp.zeros_like(x_pair)
                x2 = jnp.concatenate([jnp.where(first_head, x_pair, zero),
                                      jnp.where(first_head, zero, x_pair)], axis=0)
                m2 = []
                for e in range(2):
                    hd = 2 * pair + e
                    acs_col = jnp.broadcast_to(acs2[:, hd:hd + 1], (q, q))
                    decay = jnp.exp2(jnp.where(causal, acs_col - row_t[hd:hd + 1, :], -jnp.inf))
                    m2.append((cb * decay).astype(BF16))
                ybuf[:, lanes] = y_g[:, jp * LANES:(jp + 1) * LANES] + _dot(
                    jnp.concatenate(m2, axis=1), x2)

        yn = _gated_group_norm(ybuf[...], xc_ref[tok, :D_SSM].astype(F32),
                               zs_ref[tok, :].astype(F32), dskip_ref[...], normw_ref[...])
        yb_ref[tok, :] = yn.astype(yb_ref.dtype)
        return carry

    lax.fori_loop(0, tb // q, chunk_body, 0)

    @pl.when(i == pl.num_programs(1) - 1)
    def _():
        hTo_ref[0] = hT[...]
        for g in range(SSM_GROUPS):
            ho_ref[0, g] = hT[g].T


def _mixer(zs, xc, dt, p, expand, h0, *, batch, seqlen, tb, valid_len):
    nt = seqlen // tb
    row = lambda w: pl.BlockSpec((tb, w), lambda b, i: (b * nt + i, 0))
    full = lambda shape: pl.BlockSpec(shape, lambda b, i: (0,) * len(shape))
    per_b = lambda shape: pl.BlockSpec((1,) + shape, lambda b, i: (b,) + (0,) * len(shape))
    return pl.pallas_call(
        functools.partial(_mixer_kernel, tb=tb, valid_len=valid_len),
        grid=(batch, nt),
        in_specs=[row(D_SSM), row(D_XBC), row(LANES),
                  full((1, LANES)), full((1, LANES)), full((1, D_SSM)), full((1, D_SSM)),
                  full((LANES, D_SSM)), full((SSM_GROUPS, SSM_STATE, GROUP_W))],
        out_specs=[row(D_SSM), per_b((SSM_GROUPS, SSM_STATE, GROUP_W)),
                   per_b((SSM_GROUPS, GROUP_W, SSM_STATE))],
        out_shape=[jax.ShapeDtypeStruct((batch * seqlen, D_SSM), BF16),
                   jax.ShapeDtypeStruct((batch, SSM_GROUPS, SSM_STATE, GROUP_W), F32),
                   jax.ShapeDtypeStruct((batch, SSM_GROUPS, GROUP_W, SSM_STATE), F32)],
        scratch_shapes=[pltpu.VMEM((SSM_GROUPS, SSM_STATE, GROUP_W), F32),
                        pltpu.VMEM((CHUNK, D_SSM), F32)],
        compiler_params=pltpu.CompilerParams(
            dimension_semantics=("arbitrary", "arbitrary"), vmem_limit_bytes=VMEM_LIMIT),
        name="mixer",
    )(zs, xc, dt, p["dt_bias"], p["a_log"], p["d_skip"], p["ssm_norm_w"], expand, h0)


SEQ_BLOCK = SUBLANES


def _decode_mixer_kernel(ab_ref, ach_ref, zs_ref, xbc_ref, dt_ref, sa_ref, sx_ref, h0_ref,
                         caw_ref, cxw_ref, cxb_ref, dtb_ref, alog_ref, dskip_ref, normw_ref,
                         gsum_ref, expand_ref,
                         ya_ref, yb_ref, sao_ref, sxo_ref, ho_ref,
                         xc_buf, bc_buf, os_buf, xw_buf, cd_buf, y_buf, *, n_tok):
    sb = SEQ_BLOCK

    xp = [sa_ref[k] for k in range(CONV_A_TAPS - 1)] + [ach_ref[t] for t in range(n_tok)]
    for t in range(n_tok):
        conv = caw_ref[0:1, :] * xp[t]
        for k in range(1, CONV_A_TAPS):
            conv += caw_ref[k:k + 1, :] * xp[t + k]
        ya_ref[t] = ab_ref[t] * conv
    for k in range(CONV_A_TAPS - 1):
        sao_ref[k] = xp[n_tok + k]

    xp = [sx_ref[k] for k in range(CONV_X_TAPS - 1)] + [xbc_ref[t] for t in range(n_tok)]
    for t in range(n_tok):
        conv = cxw_ref[0:1, :] * xp[t]
        for k in range(1, CONV_X_TAPS):
            conv += cxw_ref[k:k + 1, :] * xp[t + k]
        xc_buf[t * sb:(t + 1) * sb, :] = _silu(conv + cxb_ref[...])
    for k in range(CONV_X_TAPS - 1):
        sxo_ref[k] = xp[n_tok + k]

    a_neg = -jnp.exp(alog_ref[...])
    dts, acs = [], []
    for t in range(n_tok):
        dts.append(_softplus(dt_ref[t] + dtb_ref[...]))
        acs.append(dts[t] * a_neg + (acs[t - 1] if t else 0.0))

    pairs = [(qi, si) for qi in range(n_tok) for si in range(qi + 1)]
    prods = [xc_buf[qi * sb:(qi + 1) * sb, D_SSM + D_BC:] * xc_buf[si * sb:(si + 1) * sb, D_SSM:D_SSM + D_BC]
             for qi, si in pairs]
    p_hi, p_lo = _split2(jnp.concatenate(prods, axis=0))
    cbh = _dot(p_hi, gsum_ref[...]) + _dot(p_lo, gsum_ref[...])
    coef = [jnp.exp(acs[t]) for t in range(n_tok)]
    coef += [cbh[k * sb:(k + 1) * sb] * jnp.exp(acs[qi] - acs[si]) * dts[si]
             for k, (qi, si) in enumerate(pairs)]
    coef += [jnp.exp(acs[n_tok - 1] - acs[s]) * dts[s] for s in range(n_tok)]
    coef.append(jnp.exp(acs[n_tok - 1]))
    c_hi, c_lo = _split2(jnp.concatenate(coef, axis=0))
    coef_x = _dot(c_hi, expand_ref[...]) + _dot(c_lo, expand_ref[...])
    slab = lambda k: coef_x[k * sb:(k + 1) * sb]

    def store_tiles(buf, r0, val):
        for lt in range(val.shape[1] // LANES):
            buf[lt, r0:r0 + val.shape[0], :] = val[:, lt * LANES:(lt + 1) * LANES]

    def load_rows(buf, lt0, n_lt, rows):
        return jnp.concatenate([buf[lt0 + j, rows, :] for j in range(n_lt)], axis=1)

    store_tiles(bc_buf, 0, xc_buf[:, D_SSM:])
    store_tiles(os_buf, 0, coef_x[0:n_tok * sb])
    diag = {pr: slab(n_tok + k) for k, pr in enumerate(pairs)}
    k0 = n_tok + len(pairs)
    for s in range(n_tok):
        store_tiles(xw_buf, s * sb, slab(k0 + s) * xc_buf[s * sb:(s + 1) * sb, :D_SSM])
    cd_hi, cd_lo = _split2(slab(k0 + n_tok))
    store_tiles(cd_buf, 0, cd_hi.astype(F32))
    store_tiles(cd_buf, sb, cd_lo.astype(F32))

    for t in range(n_tok):
        acc = diag[(t, 0)] * xc_buf[0:sb, :D_SSM]
        for s in range(1, t + 1):
            acc += diag[(t, s)] * xc_buf[s * sb:(s + 1) * sb, :D_SSM]
        store_tiles(y_buf, t * sb, acc)

    gt = GROUP_W // LANES
    k_rows = 2 * SUBLANES
    krow = lax.broadcasted_iota(jnp.int32, (k_rows, SSM_STATE), 0)
    ones_rows = jnp.where((krow >= n_tok) & (krow < n_tok + 2), 1.0, 0.0).astype(BF16)
    pad_lhs = jnp.zeros((k_rows - n_tok - 2, GROUP_W), F32)
    pad_rhs = jnp.zeros((k_rows - n_tok, SSM_STATE), F32)

    def seq_body(b, carry):
        tok_rows = pl.ds(b, n_tok, stride=sb)
        for g in range(SSM_GROUPS):
            b_bg = bc_buf[g, tok_rows, :]
            c_bg = bc_buf[SSM_GROUPS + g, tok_rows, :]
            heads = pl.ds(g * HEADS_PER_GROUP, HEADS_PER_GROUP)
            h_bg = h0_ref[b, heads].reshape(GROUP_W, SSM_STATE)
            y_off = lax.dot_general(c_bg.astype(BF16), h_bg.astype(BF16),
                                    (((1,), (1,)), ((), ())), preferred_element_type=F32)
            y_off = y_off * load_rows(os_buf, g * gt, gt, tok_rows)
            for j in range(gt):
                y_buf[g * gt + j, tok_rows, :] = (y_buf[g * gt + j, tok_rows, :]
                                                  + y_off[:, j * LANES:(j + 1) * LANES])
            lhs = jnp.concatenate(
                [load_rows(xw_buf, g * gt, gt, tok_rows),
                 load_rows(cd_buf, g * gt, gt, pl.ds(b, 2, stride=sb)), pad_lhs],
                axis=0).astype(BF16)
            rhs = jnp.concatenate(
                [jnp.concatenate([b_bg, pad_rhs], axis=0).astype(BF16), ones_rows], axis=1)
            upd = lax.dot_general(lhs, rhs, (((0,), (0,)), ((), ())), preferred_element_type=F32)
            new = h_bg * upd[:, SSM_STATE:] + upd[:, :SSM_STATE]
            ho_ref[b, heads] = new.reshape(HEADS_PER_GROUP, SSM_HEAD_DIM, SSM_STATE)
        return carry

    lax.fori_loop(0, sb, seq_body, 0)

    xs_all = xc_buf[:, :D_SSM]
    zs_all = jnp.concatenate([zs_ref[t] for t in range(n_tok)], axis=0)
    y_all = load_rows(y_buf, 0, D_SSM // LANES, slice(None))
    yn = _gated_group_norm(y_all, xs_all, zs_all, dskip_ref[...], normw_ref[...])
    for t in range(n_tok):
        yb_ref[t] = yn[t * sb:(t + 1) * sb]


def _decode_mixer(ab, ach, zs, xbc, dt, sa, sx, h0, p, gsum, expand, *, n_tok, n_seq):
    sb = SEQ_BLOCK
    tok = lambda w: pl.BlockSpec((n_tok, sb, w), lambda i: (0, i, 0))
    st = lambda k, w: pl.BlockSpec((k, sb, w), lambda i: (0, i, 0))
    hblk = pl.BlockSpec((sb, SSM_HEADS, SSM_HEAD_DIM, SSM_STATE), lambda i: (i, 0, 0, 0))
    full = lambda shape: pl.BlockSpec(shape, lambda i: (0,) * len(shape))
    return pl.pallas_call(
        functools.partial(_decode_mixer_kernel, n_tok=n_tok),
        grid=(n_seq // sb,),
        in_specs=[tok(D_MODEL), tok(D_MODEL), tok(D_SSM), tok(D_XBC), tok(LANES),
                  st(CONV_A_TAPS - 1, D_MODEL), st(CONV_X_TAPS - 1, D_XBC), hblk,
                  full((CONV_A_TAPS, D_MODEL)), full((CONV_X_TAPS, D_XBC)), full((1, D_XBC)),
                  full((1, LANES)), full((1, LANES)), full((1, D_SSM)), full((1, D_SSM)),
                  full((D_BC, LANES)), full((LANES, D_SSM))],
        out_specs=[tok(D_MODEL), tok(D_SSM), st(CONV_A_TAPS - 1, D_MODEL),
                   st(CONV_X_TAPS - 1, D_XBC), hblk],
        out_shape=[jax.ShapeDtypeStruct((n_tok, n_seq, D_MODEL), F32),
                   jax.ShapeDtypeStruct((n_tok, n_seq, D_SSM), F32),
                   jax.ShapeDtypeStruct((CONV_A_TAPS - 1, n_seq, D_MODEL), F32),
                   jax.ShapeDtypeStruct((CONV_X_TAPS - 1, n_seq, D_XBC), F32),
                   jax.ShapeDtypeStruct((n_seq, SSM_HEADS, SSM_HEAD_DIM, SSM_STATE), F32)],
        scratch_shapes=[pltpu.VMEM((n_tok * sb, D_XBC), F32),
                        pltpu.VMEM((2 * D_BC // LANES, n_tok * sb, LANES), F32),
                        pltpu.VMEM((D_SSM // LANES, n_tok * sb, LANES), F32),
                        pltpu.VMEM((D_SSM // LANES, n_tok * sb, LANES), F32),
                        pltpu.VMEM((D_SSM // LANES, 2 * sb, LANES), F32),
                        pltpu.VMEM((D_SSM // LANES, n_tok * sb, LANES), F32)],
        compiler_params=pltpu.CompilerParams(
            dimension_semantics=("parallel",), vmem_limit_bytes=VMEM_LIMIT),
        name="decode_mixer",
    )(ab, ach, zs, xbc, dt, sa, sx, h0, p["conv_a_w"], p["ssm_conv_w"], p["ssm_conv_b"],
      p["dt_bias"], p["a_log"], p["d_skip"], p["ssm_norm_w"], gsum, expand)


def _row_tile(t, cap):
    if t <= cap:
        return t
    best = LANES
    for cand in range(LANES, cap + 1, LANES):
        if t % cand == 0:
            best = cand
    return best


def kernel(x_prompt, x_sample, state_conv_a, state_ssm_conv, state_ssm, meta_tokens, norm_ffn1, ffn1_w_gu, ffn1_w_down, norm_mix, w_in, conv_a_w, w_a_out, ssm_conv_w, ssm_conv_b, dt_bias, a_log, d_skip, ssm_norm_w, w_b_out, w_o, norm_ffn2, ffn2_w_gu, ffn2_w_down, norm_final):
    bsz, seqlen, _ = x_prompt.shape
    n_seq, n_tok, _ = x_sample.shape
    n_dt = SSM_HEADS
    c_dt = 3 * D_MODEL + D_SSM + D_XBC

    w_gu1, w_dn1 = ffn1_w_gu[0].astype(BF16), ffn1_w_down[0].astype(BF16)
    w_gu2, w_dn2 = ffn2_w_gu[0].astype(BF16), ffn2_w_down[0].astype(BF16)
    w_proj = jnp.concatenate(
        [w_in[0, :, :c_dt], jnp.pad(w_in[0, :, c_dt:c_dt + n_dt], ((0, 0), (0, LANES - n_dt))),
         w_in[0, :, c_dt + n_dt:]], axis=1).astype(BF16)
    wa, wb, wo = w_a_out[0].astype(BF16), w_b_out[0].astype(BF16), w_o[0].astype(BF16)
    pad_heads = lambda v: jnp.pad(v[0], (0, LANES - n_dt))[None]
    p = {
        "conv_a_w": conv_a_w[0], "ssm_conv_w": ssm_conv_w[0], "ssm_conv_b": ssm_conv_b[0][None],
        "dt_bias": pad_heads(dt_bias), "a_log": pad_heads(a_log),
        "d_skip": jnp.repeat(d_skip[0], SSM_HEAD_DIM)[None], "ssm_norm_w": ssm_norm_w[0][None],
    }
    n1, nm, n2, nf = norm_ffn1[0][None], norm_mix[0][None], norm_ffn2[0][None], norm_final[None]
    eh = lax.broadcasted_iota(jnp.int32, (LANES, D_SSM), 0)
    ec = lax.broadcasted_iota(jnp.int32, (LANES, D_SSM), 1)
    expand = (ec // SSM_HEAD_DIM == eh).astype(BF16)

    def ffn1(x):
        return _ffn(x, n1, w_gu1, w_dn1, nf, tm=_row_tile(x.shape[0], 512), tf=D_FF // 2,
                    final_norm=False)

    def post(ya, yb, sga, sgb, h):
        t = h.shape[0]
        h = _outproj(ya, yb, sga, sgb, h, wa, wb, wo, tm=_row_tile(t, 512))
        return _ffn(h, n2, w_gu2, w_dn2, nf, tm=_row_tile(t, 512), tf=D_FF // 2, final_norm=True)

    zeros = lambda *s: jnp.zeros(s, F32)
    h_m = ffn1(jnp.pad(meta_tokens, ((0, CHUNK - N_META), (0, 0))))
    _, zs, xc, dt, _, _, ca_m, cx_m = _inproj_conv(
        h_m, nm, w_proj, p, zeros(SUBLANES, D_MODEL), zeros(SUBLANES, D_XBC),
        tm=CHUNK, seq_tiles=1, valid_len=N_META)
    _, hT_m, _ = _mixer(zs, xc, dt, p, expand, zeros(SSM_GROUPS, SSM_STATE, GROUP_W),
                        batch=1, seqlen=CHUNK, tb=CHUNK, valid_len=N_META)

    h_p = ffn1(x_prompt.reshape(bsz * seqlen, D_MODEL))
    tm = _row_tile(seqlen, 256)
    ya, zs, xc, dt, sga, sgb, ca_p, cx_p = _inproj_conv(
        h_p, nm, w_proj, p, ca_m[0], cx_m[0], tm=tm, seq_tiles=seqlen // tm, valid_len=tm)
    tb = _row_tile(seqlen, 512)
    yb, _, h_out_p = _mixer(zs, xc, dt, p, expand, hT_m[0], batch=bsz, seqlen=seqlen, tb=tb,
                            valid_len=tb)
    y_prompt = post(ya, yb, sga, sgb, h_p).reshape(bsz, seqlen, D_MODEL)
    prompt_conv_a = ca_p[None, :, SUBLANES - (CONV_A_TAPS - 1):, :]
    prompt_ssm_conv = cx_p[None, :, SUBLANES - (CONV_X_TAPS - 1):, :]
    prompt_ssm = h_out_p.reshape(1, bsz, SSM_HEADS, SSM_HEAD_DIM, SSM_STATE)

    x_s = jnp.swapaxes(x_sample, 0, 1).reshape(n_tok * n_seq, D_MODEL)
    h_s = ffn1(x_s)
    ab, ach, zs, xbc, dt, sga, sgb = _inproj(h_s, nm, w_proj, tm=_row_tile(n_tok * n_seq, 256), act_dtype=F32)
    tsf = lambda a: a.reshape(n_tok, n_seq, a.shape[-1])
    hh = lax.broadcasted_iota(jnp.int32, (D_BC, LANES), 1)
    kk = lax.broadcasted_iota(jnp.int32, (D_BC, LANES), 0)
    gsum = ((hh < SSM_HEADS) & (kk // SSM_STATE == hh // HEADS_PER_GROUP)).astype(BF16)
    ya, yb, sa_new, sx_new, sample_ssm = _decode_mixer(
        tsf(ab), tsf(ach), tsf(zs), tsf(xbc), tsf(dt),
        jnp.swapaxes(state_conv_a[0], 0, 1), jnp.swapaxes(state_ssm_conv[0], 0, 1), state_ssm[0],
        p, gsum, expand, n_tok=n_tok, n_seq=n_seq)
    flat = lambda a: a.reshape(n_tok * n_seq, a.shape[-1])
    y_s = post(flat(ya), flat(yb), sga, sgb, h_s)
    y_sample = jnp.swapaxes(y_s.reshape(n_tok, n_seq, D_MODEL), 0, 1)
    sample_conv_a = jnp.swapaxes(sa_new, 0, 1)[None]
    sample_ssm_conv = jnp.swapaxes(sx_new, 0, 1)[None]

    return (y_prompt, y_sample, prompt_conv_a, prompt_ssm_conv, prompt_ssm,
            sample_conv_a, sample_ssm_conv, sample_ssm[None])
```

```python
import functools

import jax
import jax.numpy as jnp
from jax import lax
from jax.experimental import pallas as pl
from jax.experimental.pallas import tpu as pltpu

D_MODEL = 1024
D_FF = 2816
D_SSM = 2048
SSM_HEADS = 32
SSM_HEAD_DIM = 64
SSM_GROUPS = 4
HEADS_PER_GROUP = SSM_HEADS // SSM_GROUPS
SSM_STATE = 128
D_BC = SSM_GROUPS * SSM_STATE
D_XBC = D_SSM + 2 * D_BC
CONV_A_TAPS = 3
CONV_X_TAPS = 4
N_META = 16
EPS = 1e-6

LANES = 128
SUBLANES = 8
CHUNK = 128
VMEM_LIMIT = 56 * 1024 * 1024

F32 = jnp.float32
BF16 = jnp.bfloat16


def _dot(a, b):
    return jnp.dot(a, b, preferred_element_type=F32)


def _rms(x, w):
    return x * lax.rsqrt(jnp.mean(x * x, axis=-1, keepdims=True) + EPS) * w


def _silu(x):
    return x * jax.nn.sigmoid(x)


def _softplus(x):
    return jnp.maximum(x, 0.0) + jnp.log1p(jnp.exp(-jnp.abs(x)))


def _split3(x):
    hi = x.astype(BF16)
    r = x - hi.astype(F32)
    mid = r.astype(BF16)
    lo = (r - mid.astype(F32)).astype(BF16)
    return hi, mid, lo


def _split2(x):
    hi = x.astype(BF16)
    lo = (x - hi.astype(F32)).astype(BF16)
    return hi, lo


def _resident(shape):
    return pl.BlockSpec(shape, lambda *_: (0,) * len(shape), pipeline_mode=pl.Buffered(1))


def _ffn_kernel(x_ref, nw_ref, wg_ref, wu_ref, wd_ref, fw_ref, o_ref, xn_ref, acc_ref, *, final_norm):
    j = pl.program_id(1)

    @pl.when(j == 0)
    def _():
        xn_ref[...] = _rms(x_ref[...], nw_ref[...]).astype(BF16)
        acc_ref[...] = jnp.zeros_like(acc_ref)

    xn = xn_ref[...]
    act = (_silu(_dot(xn, wg_ref[...])) * _dot(xn, wu_ref[...])).astype(BF16)
    acc_ref[...] += _dot(act, wd_ref[...])

    @pl.when(j == pl.num_programs(1) - 1)
    def _():
        h = x_ref[...] + 0.5 * acc_ref[...]
        if final_norm:
            h = _rms(h, fw_ref[...])
        o_ref[...] = h


def _ffn(x, nw, w_gu, w_down, fw, *, tm, tf, final_norm):
    t = x.shape[0]
    nf = D_FF // tf
    return pl.pallas_call(
        functools.partial(_ffn_kernel, final_norm=final_norm),
        grid=(t // tm, nf),
        in_specs=[
            pl.BlockSpec((tm, D_MODEL), lambda i, j: (i, 0)),
            pl.BlockSpec((1, D_MODEL), lambda i, j: (0, 0)),
            pl.BlockSpec((D_MODEL, tf), lambda i, j: (0, j)),
            pl.BlockSpec((D_MODEL, tf), lambda i, j: (0, j + nf)),
            pl.BlockSpec((tf, D_MODEL), lambda i, j: (j, 0)),
            pl.BlockSpec((1, D_MODEL), lambda i, j: (0, 0)),
        ],
        out_specs=pl.BlockSpec((tm, D_MODEL), lambda i, j: (i, 0)),
        out_shape=jax.ShapeDtypeStruct((t, D_MODEL), F32),
        scratch_shapes=[pltpu.VMEM((tm, D_MODEL), BF16), pltpu.VMEM((tm, D_MODEL), F32)],
        compiler_params=pltpu.CompilerParams(
            dimension_semantics=("parallel", "arbitrary"), vmem_limit_bytes=VMEM_LIMIT),
        name="ffn",
    )(x, nw, w_gu, w_gu, w_down, fw)


C_DT = 3 * D_MODEL + D_SSM + D_XBC
C_GATE = C_DT + LANES


def _proj_weight(w_refs, c0, width):
    w_main, w_dt, w_gate = w_refs
    if c0 < C_DT:
        return w_main[:, c0:c0 + width]
    if c0 < C_GATE:
        return w_dt[:, c0 - C_DT:c0 - C_DT + width]
    return w_gate[:, c0 - C_GATE:c0 - C_GATE + width]


def _proj_weight_specs():
    return [_resident((D_MODEL, C_DT)), _resident((D_MODEL, LANES)), _resident((D_MODEL, 2 * D_MODEL))]


def _inproj_kernel(h_ref, nw_ref, wm_ref, wdt_ref, wg_ref,
                   ab_ref, ach_ref, zs_ref, xbc_ref, dt_ref, sga_ref, sgb_ref):
    u = _rms(h_ref[...], nw_ref[...]).astype(BF16)

    def proj(c0, width):
        return _dot(u, _proj_weight((wm_ref, wdt_ref, wg_ref), c0, width))

    ab_ref[...] = proj(0, D_MODEL).astype(ab_ref.dtype)
    ach_ref[...] = (proj(D_MODEL, D_MODEL) * proj(2 * D_MODEL, D_MODEL)).astype(ach_ref.dtype)
    c0 = 3 * D_MODEL
    for k in range(D_SSM // D_MODEL):
        zs_ref[:, k * D_MODEL:(k + 1) * D_MODEL] = _silu(proj(c0 + k * D_MODEL, D_MODEL)).astype(zs_ref.dtype)
    c0 += D_SSM
    for k in range(D_XBC // D_MODEL):
        xbc_ref[:, k * D_MODEL:(k + 1) * D_MODEL] = proj(c0 + k * D_MODEL, D_MODEL).astype(xbc_ref.dtype)
    c0 += D_XBC
    dt_ref[...] = proj(c0, LANES)
    c0 += LANES
    sga_ref[...] = jax.nn.sigmoid(proj(c0, D_MODEL)).astype(sga_ref.dtype)
    sgb_ref[...] = jax.nn.sigmoid(proj(c0 + D_MODEL, D_MODEL)).astype(sgb_ref.dtype)


def _inproj(h, nw, w_packed, *, tm, act_dtype):
    t = h.shape[0]
    widths = (D_MODEL, D_MODEL, D_SSM, D_XBC, LANES, D_MODEL, D_MODEL)
    dtypes = (act_dtype, act_dtype, act_dtype, act_dtype, F32, act_dtype, act_dtype)
    return pl.pallas_call(
        _inproj_kernel,
        grid=(t // tm,),
        in_specs=[pl.BlockSpec((tm, D_MODEL), lambda i: (i, 0)), _resident((1, D_MODEL))]
        + _proj_weight_specs(),
        out_specs=[pl.BlockSpec((tm, w), lambda i: (i, 0)) for w in widths],
        out_shape=[jax.ShapeDtypeStruct((t, w), d) for w, d in zip(widths, dtypes)],
        compiler_params=pltpu.CompilerParams(
            dimension_semantics=("parallel",), vmem_limit_bytes=VMEM_LIMIT),
        name="inproj",
    )(h, nw, *w_packed)


COL_BLOCK = 256


def _inproj_conv_kernel(h_ref, nw_ref, wm_ref, wdt_ref, wg_ref, caw_ref, cxw_ref, cxb_ref,
                        ca0_ref, cx0_ref,
                        ya_ref, zs_ref, xc_ref, dt_ref, sga_ref, sgb_ref, cao_ref, cxo_ref,
                        ca_carry, cx_carry, *, tm, seq_tiles, valid_len):
    i = pl.program_id(0)

    @pl.when(i % seq_tiles == 0)
    def _():
        ca_carry[...] = ca0_ref[...]
        cx_carry[...] = cx0_ref[...]

    u = _rms(h_ref[...], nw_ref[...]).astype(BF16)

    def proj(c0, width=COL_BLOCK):
        return _dot(u, _proj_weight((wm_ref, wdt_ref, wg_ref), c0, width))

    n8 = tm // SUBLANES
    sublane = lax.broadcasted_iota(jnp.int32, (n8, SUBLANES, COL_BLOCK), 1)

    def causal_conv(x, carry_ref, w_ref, taps, cols):
        cur = x.reshape(n8, SUBLANES, COL_BLOCK)
        prev = jnp.concatenate([carry_ref[:, cols], x[:tm - SUBLANES]], axis=0).reshape(cur.shape)
        acc = w_ref[taps - 1:taps, cols] * x
        for d in range(1, taps):
            merged = jnp.where(sublane < SUBLANES - d, cur, prev)
            acc += w_ref[taps - 1 - d:taps - d, cols] * pltpu.roll(merged, d, axis=1).reshape(x.shape)
        carry_ref[:, cols] = x[valid_len - SUBLANES:valid_len]
        return acc

    c_z, c_x, c_dt = 3 * D_MODEL, 3 * D_MODEL + D_SSM, 3 * D_MODEL + D_SSM + D_XBC
    c_g = c_dt + LANES

    def tail_a(k, cols, prj):
        conv = causal_conv(prj[1] * prj[2], ca_carry, caw_ref, CONV_A_TAPS, cols)
        ya_ref[:, cols] = (prj[0] * conv).astype(ya_ref.dtype)

    def tail_z(k, cols, prj):
        zs_ref[:, cols] = _silu(prj[0]).astype(zs_ref.dtype)

    def tail_x(k, cols, prj):
        conv = causal_conv(prj[0], cx_carry, cxw_ref, CONV_X_TAPS, cols)
        xc_ref[:, cols] = _silu(conv + cxb_ref[:, cols]).astype(xc_ref.dtype)

    def tail_g(k, cols, prj):
        sga_ref[:, cols] = jax.nn.sigmoid(prj[0]).astype(sga_ref.dtype)
        sgb_ref[:, cols] = jax.nn.sigmoid(prj[1]).astype(sgb_ref.dtype)

    def blocks(tail, width, starts):
        return [(tail, k, [c + k * COL_BLOCK for c in starts]) for k in range(width // COL_BLOCK)]

    light = blocks(tail_z, D_SSM, [c_z])
    for n, (blk_a, blk_g) in enumerate(zip(blocks(tail_a, D_MODEL, [0, D_MODEL, 2 * D_MODEL]),
                                           blocks(tail_g, D_MODEL, [c_g, c_g + D_MODEL]))):
        light.insert(3 * n, blk_a)
        light.insert(3 * n + 2, blk_g)
    heavy = blocks(tail_x, D_XBC, [c_x])
    order = []
    for n in range(max(len(light), len(heavy))):
        order += light[n:n + 1] + heavy[n:n + 1]
    for tail, k, starts in order:
        tail(k, slice(k * COL_BLOCK, (k + 1) * COL_BLOCK), [proj(c) for c in starts])
    dt_ref[...] = proj(c_dt, LANES)
    cao_ref[0] = ca_carry[...]
    cxo_ref[0] = cx_carry[...]


def _inproj_conv(h, nw, w_packed, p, ca0, cx0, *, tm, seq_tiles, valid_len):
    t = h.shape[0]
    n_seq = t // (tm * seq_tiles)
    widths = (D_MODEL, D_SSM, D_XBC, LANES, D_MODEL, D_MODEL)
    dtypes = (BF16, BF16, BF16, F32, BF16, BF16)
    tail = lambda w: pl.BlockSpec((1, SUBLANES, w), lambda i: (i // seq_tiles, 0, 0))
    return pl.pallas_call(
        functools.partial(_inproj_conv_kernel, tm=tm, seq_tiles=seq_tiles, valid_len=valid_len),
        grid=(t // tm,),
        in_specs=[pl.BlockSpec((tm, D_MODEL), lambda i: (i, 0)), _resident((1, D_MODEL))]
        + _proj_weight_specs()
        + [_resident((CONV_A_TAPS, D_MODEL)), _resident((CONV_X_TAPS, D_XBC)),
           _resident((1, D_XBC)),
           _resident((SUBLANES, D_MODEL)), _resident((SUBLANES, D_XBC))],
        out_specs=[pl.BlockSpec((tm, w), lambda i: (i, 0)) for w in widths]
        + [tail(D_MODEL), tail(D_XBC)],
        out_shape=[jax.ShapeDtypeStruct((t, w), d) for w, d in zip(widths, dtypes)]
        + [jax.ShapeDtypeStruct((n_seq, SUBLANES, D_MODEL), F32),
           jax.ShapeDtypeStruct((n_seq, SUBLANES, D_XBC), F32)],
        scratch_shapes=[pltpu.VMEM((SUBLANES, D_MODEL), F32), pltpu.VMEM((SUBLANES, D_XBC), F32)],
        compiler_params=pltpu.CompilerParams(
            dimension_semantics=("arbitrary",), vmem_limit_bytes=VMEM_LIMIT),
        name="inproj_conv",
    )(h, nw, *w_packed, p["conv_a_w"], p["ssm_conv_w"], p["ssm_conv_b"], ca0, cx0)


def _outproj_kernel(ya_ref, yb_ref, sga_ref, sgb_ref, h_ref, wa_ref, wb_ref, wo_ref, o_ref):
    y_a = _dot(ya_ref[...].astype(BF16), wa_ref[...])
    y_b = _dot(yb_ref[...].astype(BF16), wb_ref[...])
    merged = sga_ref[...].astype(F32) * y_a + sgb_ref[...].astype(F32) * y_b
    o_ref[...] = h_ref[...] + _dot(merged.astype(BF16), wo_ref[...])


def _outproj(ya, yb, sga, sgb, h, wa, wb, wo, *, tm):
    t = h.shape[0]
    row = lambda w: pl.BlockSpec((tm, w), lambda i: (i, 0))
    return pl.pallas_call(
        _outproj_kernel,
        grid=(t // tm,),
        in_specs=[row(D_MODEL), row(D_SSM), row(D_MODEL), row(D_MODEL), row(D_MODEL),
                  _resident((D_MODEL, D_MODEL)), _resident((D_SSM, D_MODEL)),
                  _resident((D_MODEL, D_MODEL))],
        out_specs=row(D_MODEL),
        out_shape=jax.ShapeDtypeStruct((t, D_MODEL), F32),
        compiler_params=pltpu.CompilerParams(
            dimension_semantics=("parallel",), vmem_limit_bytes=VMEM_LIMIT),
        name="outproj",
    )(ya, yb, sga, sgb, h, wa, wb, wo)


def _gated_group_norm(y, xs, zs, dskip, normw):
    y = (y + dskip * xs) * zs
    gw = D_SSM // SSM_GROUPS
    parts = []
    for g in range(SSM_GROUPS):
        yg = y[:, g * gw:(g + 1) * gw]
        parts.append(yg * lax.rsqrt(jnp.mean(yg * yg, axis=-1, keepdims=True) + EPS))
    return jnp.concatenate(parts, axis=-1) * normw


GROUP_W = HEADS_PER_GROUP * SSM_HEAD_DIM
LOG2E = 1.4426950408889634


def _mixer_kernel(zs_ref, xc_ref, dt_ref, dtb_ref, alog_ref, dskip_ref, normw_ref, expand_ref,
                  h0_ref, yb_ref, hTo_ref, ho_ref, hT, ybuf, *, tb, valid_len):
    i = pl.program_id(1)
    q = CHUNK

    @pl.when(i == 0)
    def _():
        hT[...] = h0_ref[...]

    a_neg = -jnp.exp(alog_ref[...])
    rows = lax.broadcasted_iota(jnp.int32, (q, q), 0)
    cols_i = lax.broadcasted_iota(jnp.int32, (q, q), 1)
    causal = rows >= cols_i
    tri = causal.astype(BF16)
    first_head = lax.broadcasted_iota(jnp.int32, (q, LANES), 1) < SSM_HEAD_DIM

    def chunk_body(c, carry):
        r0 = pl.multiple_of(c * q, q)
        tok = pl.ds(r0, q)
        dt = _softplus(dt_ref[tok, :] + dtb_ref[...])
        if valid_len < tb:
            t_idx = r0 + lax.broadcasted_iota(jnp.int32, (q, LANES), 0)
            dt = jnp.where(t_idx < valid_len, dt, 0.0)
        hi, mid, lo = _split3(dt * a_neg)
        acs3 = _dot(tri, jnp.concatenate([hi, mid, lo], axis=1))
        acs = acs3[:, :LANES] + acs3[:, LANES:2 * LANES] + acs3[:, 2 * LANES:]
        acs2 = acs * LOG2E
        row_t = (acs2 - jnp.log2(dt)).T
        acs_end = acs2[q - 1:q, :]
        cd_hi, cd_lo = _split2(jnp.broadcast_to(jnp.exp2(acs_end), (SUBLANES, LANES)))
        stack = jnp.concatenate(
            [jnp.exp2(acs2).astype(BF16), (jnp.exp2(acs_end - acs2) * dt).astype(BF16),
             cd_hi, cd_lo], axis=0)

        for g in range(SSM_GROUPS):
            gcols = slice(g * GROUP_W, (g + 1) * GROUP_W)
            ex = _dot(stack, expand_ref[:, gcols])
            state_decay = ex[2 * q:2 * q + 1] + ex[2 * q + SUBLANES:2 * q + SUBLANES + 1]
            b_t = xc_ref[tok, D_SSM + g * SSM_STATE:D_SSM + (g + 1) * SSM_STATE].astype(F32).T.astype(BF16)
            c_bf = xc_ref[tok, D_SSM + D_BC + g * SSM_STATE:D_SSM + D_BC + (g + 1) * SSM_STATE]
            cb = _dot(c_bf, b_t)
            h_g = hT[g]
            y_g = ex[0:q] * _dot(c_bf, h_g.astype(BF16))
            xw_g = xc_ref[tok, gcols] * ex[q:2 * q].astype(BF16)
            hT[g] = h_g * state_decay + _dot(b_t, xw_g)
            for jp in range(HEADS_PER_GROUP // 2):
                pair = g * (HEADS_PER_GROUP // 2) + jp
                lanes = slice(pair * LANES, (pair + 1) * LANES)
                x_pair = xc_ref[tok, lanes]
                zero = jnp.zeros_like(x_pair)
                x2 = jnp.concatenate([jnp.where(first_head, x_pair, zero),
                                      jnp.where(first_head, zero, x_pair)], axis=0)
                m2 = []
                for e in range(2):
                    hd = 2 * pair + e
                    acs_col = jnp.broadcast_to(acs2[:, hd:hd + 1], (q, q))
                    decay = jnp.exp2(jnp.where(causal, acs_col - row_t[hd:hd + 1, :], -jnp.inf))
                    m2.append((cb * decay).astype(BF16))
                ybuf[:, lanes] = y_g[:, jp * LANES:(jp + 1) * LANES] + _dot(
                    jnp.concatenate(m2, axis=1), x2)

        yn = _gated_group_norm(ybuf[...], xc_ref[tok, :D_SSM].astype(F32),
                               zs_ref[tok, :].astype(F32), dskip_ref[...], normw_ref[...])
        yb_ref[tok, :] = yn.astype(yb_ref.dtype)
        return carry

    lax.fori_loop(0, tb // q, chunk_body, 0)

    @pl.when(i == pl.num_programs(1) - 1)
    def _():
        hTo_ref[0] = hT[...]
        for g in range(SSM_GROUPS):
            ho_ref[0, g] = hT[g].T


def _mixer(zs, xc, dt, p, expand, h0, *, batch, seqlen, tb, valid_len):
    nt = seqlen // tb
    row = lambda w: pl.BlockSpec((tb, w), lambda b, i: (b * nt + i, 0))
    full = lambda shape: pl.BlockSpec(shape, lambda b, i: (0,) * len(shape))
    per_b = lambda shape: pl.BlockSpec((1,) + shape, lambda b, i: (b,) + (0,) * len(shape))
    return pl.pallas_call(
        functools.partial(_mixer_kernel, tb=tb, valid_len=valid_len),
        grid=(batch, nt),
        in_specs=[row(D_SSM), row(D_XBC), row(LANES),
                  full((1, LANES)), full((1, LANES)), full((1, D_SSM)), full((1, D_SSM)),
                  full((LANES, D_SSM)), full((SSM_GROUPS, SSM_STATE, GROUP_W))],
        out_specs=[row(D_SSM), per_b((SSM_GROUPS, SSM_STATE, GROUP_W)),
                   per_b((SSM_GROUPS, GROUP_W, SSM_STATE))],
        out_shape=[jax.ShapeDtypeStruct((batch * seqlen, D_SSM), BF16),
                   jax.ShapeDtypeStruct((batch, SSM_GROUPS, SSM_STATE, GROUP_W), F32),
                   jax.ShapeDtypeStruct((batch, SSM_GROUPS, GROUP_W, SSM_STATE), F32)],
        scratch_shapes=[pltpu.VMEM((SSM_GROUPS, SSM_STATE, GROUP_W), F32),
                        pltpu.VMEM((CHUNK, D_SSM), F32)],
        compiler_params=pltpu.CompilerParams(
            dimension_semantics=("arbitrary", "arbitrary"), vmem_limit_bytes=VMEM_LIMIT),
        name="mixer",
    )(zs, xc, dt, p["dt_bias"], p["a_log"], p["d_skip"], p["ssm_norm_w"], expand, h0)


SEQ_BLOCK = SUBLANES


def _decode_mixer_kernel(ab_ref, ach_ref, zs_ref, xbc_ref, dt_ref, sa_ref, sx_ref, h0_ref,
                         caw_ref, cxw_ref, cxb_ref, dtb_ref, alog_ref, dskip_ref, normw_ref,
                         gsum_ref, expand_ref,
                         ya_ref, yb_ref, sao_ref, sxo_ref, ho_ref,
                         xc_buf, bc_buf, os_buf, xw_buf, cd_buf, y_buf, *, n_tok):
    sb = SEQ_BLOCK

    xp = [sa_ref[k] for k in range(CONV_A_TAPS - 1)] + [ach_ref[t] for t in range(n_tok)]
    for t in range(n_tok):
        conv = caw_ref[0:1, :] * xp[t]
        for k in range(1, CONV_A_TAPS):
            conv += caw_ref[k:k + 1, :] * xp[t + k]
        ya_ref[t] = ab_ref[t] * conv
    for k in range(CONV_A_TAPS - 1):
        sao_ref[k] = xp[n_tok + k]

    xp = [sx_ref[k] for k in range(CONV_X_TAPS - 1)] + [xbc_ref[t] for t in range(n_tok)]
    for t in range(n_tok):
        conv = cxw_ref[0:1, :] * xp[t]
        for k in range(1, CONV_X_TAPS):
            conv += cxw_ref[k:k + 1, :] * xp[t + k]
        xc_buf[t * sb:(t + 1) * sb, :] = _silu(conv + cxb_ref[...])
    for k in range(CONV_X_TAPS - 1):
        sxo_ref[k] = xp[n_tok + k]

    a_neg = -jnp.exp(alog_ref[...])
    dts, acs = [], []
    for t in range(n_tok):
        dts.append(_softplus(dt_ref[t] + dtb_ref[...]))
        acs.append(dts[t] * a_neg + (acs[t - 1] if t else 0.0))

    pairs = [(qi, si) for qi in range(n_tok) for si in range(qi + 1)]
    prods = [xc_buf[qi * sb:(qi + 1) * sb, D_SSM + D_BC:] * xc_buf[si * sb:(si + 1) * sb, D_SSM:D_SSM + D_BC]
             for qi, si in pairs]
    p_hi, p_lo = _split2(jnp.concatenate(prods, axis=0))
    cbh = _dot(p_hi, gsum_ref[...]) + _dot(p_lo, gsum_ref[...])
    coef = [jnp.exp(acs[t]) for t in range(n_tok)]
    coef += [cbh[k * sb:(k + 1) * sb] * jnp.exp(acs[qi] - acs[si]) * dts[si]
             for k, (qi, si) in enumerate(pairs)]
    coef += [jnp.exp(acs[n_tok - 1] - acs[s]) * dts[s] for s in range(n_tok)]
    coef.append(jnp.exp(acs[n_tok - 1]))
    c_hi, c_lo = _split2(jnp.concatenate(coef, axis=0))
    coef_x = _dot(c_hi, expand_ref[...]) + _dot(c_lo, expand_ref[...])
    slab = lambda k: coef_x[k * sb:(k + 1) * sb]

    def store_tiles(buf, r0, val):
        for lt in range(val.shape[1] // LANES):
            buf[lt, r0:r0 + val.shape[0], :] = val[:, lt * LANES:(lt + 1) * LANES]

    def load_rows(buf, lt0, n_lt, rows):
        return jnp.concatenate([buf[lt0 + j, rows, :] for j in range(n_lt)], axis=1)

    store_tiles(bc_buf, 0, xc_buf[:, D_SSM:])
    store_tiles(os_buf, 0, coef_x[0:n_tok * sb])
    diag = {pr: slab(n_tok + k) for k, pr in enumerate(pairs)}
    k0 = n_tok + len(pairs)
    for s in range(n_tok):
        store_tiles(xw_buf, s * sb, slab(k0 + s) * xc_buf[s * sb:(s + 1) * sb, :D_SSM])
    cd_hi, cd_lo = _split2(slab(k0 + n_tok))
    store_tiles(cd_buf, 0, cd_hi.astype(F32))
    store_tiles(cd_buf, sb, cd_lo.astype(F32))

    for t in range(n_tok):
        acc = diag[(t, 0)] * xc_buf[0:sb, :D_SSM]
        for s in range(1, t + 1):
            acc += diag[(t, s)] * xc_buf[s * sb:(s + 1) * sb, :D_SSM]
        store_tiles(y_buf, t * sb, acc)

    gt = GROUP_W // LANES
    k_rows = 2 * SUBLANES
    krow = lax.broadcasted_iota(jnp.int32, (k_rows, SSM_STATE), 0)
    ones_rows = jnp.where((krow >= n_tok) & (krow < n_tok + 2), 1.0, 0.0).astype(BF16)
    pad_lhs = jnp.zeros((k_rows - n_tok - 2, GROUP_W), F32)
    pad_rhs = jnp.zeros((k_rows - n_tok, SSM_STATE), F32)

    def seq_body(b, carry):
        tok_rows = pl.ds(b, n_tok, stride=sb)
        for g in range(SSM_GROUPS):
            b_bg = bc_buf[g, tok_rows, :]
            c_bg = bc_buf[SSM_GROUPS + g, tok_rows, :]
            heads = pl.ds(g * HEADS_PER_GROUP, HEADS_PER_GROUP)
            h_bg = h0_ref[b, heads].reshape(GROUP_W, SSM_STATE)
            y_off = lax.dot_general(c_bg.astype(BF16), h_bg.astype(BF16),
                                    (((1,), (1,)), ((), ())), preferred_element_type=F32)
            y_off = y_off * load_rows(os_buf, g * gt, gt, tok_rows)
            for j in range(gt):
                y_buf[g * gt + j, tok_rows, :] = (y_buf[g * gt + j, tok_rows, :]
                                                  + y_off[:, j * LANES:(j + 1) * LANES])
            lhs = jnp.concatenate(
                [load_rows(xw_buf, g * gt, gt, tok_rows),
                 load_rows(cd_buf, g * gt, gt, pl.ds(b, 2, stride=sb)), pad_lhs],
                axis=0).astype(BF16)
            rhs = jnp.concatenate(
                [jnp.concatenate([b_bg, pad_rhs], axis=0).astype(BF16), ones_rows], axis=1)
            upd = lax.dot_general(lhs, rhs, (((0,), (0,)), ((), ())), preferred_element_type=F32)
            new = h_bg * upd[:, SSM_STATE:] + upd[:, :SSM_STATE]
            ho_ref[b, heads] = new.reshape(HEADS_PER_GROUP, SSM_HEAD_DIM, SSM_STATE)
        return carry

    lax.fori_loop(0, sb, seq_body, 0, unroll=2)

    xs_all = xc_buf[:, :D_SSM]
    zs_all = jnp.concatenate([zs_ref[t] for t in range(n_tok)], axis=0)
    y_all = load_rows(y_buf, 0, D_SSM // LANES, slice(None))
    yn = _gated_group_norm(y_all, xs_all, zs_all, dskip_ref[...], normw_ref[...])
    for t in range(n_tok):
        yb_ref[t] = yn[t * sb:(t + 1) * sb]


def _decode_mixer(ab, ach, zs, xbc, dt, sa, sx, h0, p, gsum, expand, *, n_tok, n_seq):
    sb = SEQ_BLOCK
    tok = lambda w: pl.BlockSpec((n_tok, sb, w), lambda i: (0, i, 0))
    st = lambda k, w: pl.BlockSpec((k, sb, w), lambda i: (0, i, 0))
    hblk = pl.BlockSpec((sb, SSM_HEADS, SSM_HEAD_DIM, SSM_STATE), lambda i: (i, 0, 0, 0))
    full = lambda shape: pl.BlockSpec(shape, lambda i: (0,) * len(shape))
    return pl.pallas_call(
        functools.partial(_decode_mixer_kernel, n_tok=n_tok),
        grid=(n_seq // sb,),
        in_specs=[tok(D_MODEL), tok(D_MODEL), tok(D_SSM), tok(D_XBC), tok(LANES),
                  st(CONV_A_TAPS - 1, D_MODEL), st(CONV_X_TAPS - 1, D_XBC), hblk,
                  full((CONV_A_TAPS, D_MODEL)), full((CONV_X_TAPS, D_XBC)), full((1, D_XBC)),
                  full((1, LANES)), full((1, LANES)), full((1, D_SSM)), full((1, D_SSM)),
                  full((D_BC, LANES)), full((LANES, D_SSM))],
        out_specs=[tok(D_MODEL), tok(D_SSM), st(CONV_A_TAPS - 1, D_MODEL),
                   st(CONV_X_TAPS - 1, D_XBC), hblk],
        out_shape=[jax.ShapeDtypeStruct((n_tok, n_seq, D_MODEL), F32),
                   jax.ShapeDtypeStruct((n_tok, n_seq, D_SSM), F32),
                   jax.ShapeDtypeStruct((CONV_A_TAPS - 1, n_seq, D_MODEL), F32),
                   jax.ShapeDtypeStruct((CONV_X_TAPS - 1, n_seq, D_XBC), F32),
                   jax.ShapeDtypeStruct((n_seq, SSM_HEADS, SSM_HEAD_DIM, SSM_STATE), F32)],
        scratch_shapes=[pltpu.VMEM((n_tok * sb, D_XBC), F32),
                        pltpu.VMEM((2 * D_BC // LANES, n_tok * sb, LANES), F32),
                        pltpu.VMEM((D_SSM // LANES, n_tok * sb, LANES), F32),
                        pltpu.VMEM((D_SSM // LANES, n_tok * sb, LANES), F32),
                        pltpu.VMEM((D_SSM // LANES, 2 * sb, LANES), F32),
                        pltpu.VMEM((D_SSM // LANES, n_tok * sb, LANES), F32)],
        compiler_params=pltpu.CompilerParams(
            dimension_semantics=("parallel",), vmem_limit_bytes=VMEM_LIMIT),
        name="decode_mixer",
    )(ab, ach, zs, xbc, dt, sa, sx, h0, p["conv_a_w"], p["ssm_conv_w"], p["ssm_conv_b"],
      p["dt_bias"], p["a_log"], p["d_skip"], p["ssm_norm_w"], gsum, expand)


def _row_tile(t, cap):
    if t <= cap:
        return t
    best = LANES
    for cand in range(LANES, cap + 1, LANES):
        if t % cand == 0:
            best = cand
    return best


def kernel(x_prompt, x_sample, state_conv_a, state_ssm_conv, state_ssm, meta_tokens, norm_ffn1, ffn1_w_gu, ffn1_w_down, norm_mix, w_in, conv_a_w, w_a_out, ssm_conv_w, ssm_conv_b, dt_bias, a_log, d_skip, ssm_norm_w, w_b_out, w_o, norm_ffn2, ffn2_w_gu, ffn2_w_down, norm_final):
    bsz, seqlen, _ = x_prompt.shape
    n_seq, n_tok, _ = x_sample.shape
    n_dt = SSM_HEADS
    c_dt = 3 * D_MODEL + D_SSM + D_XBC

    w_gu1, w_dn1 = ffn1_w_gu[0].astype(BF16), ffn1_w_down[0].astype(BF16)
    w_gu2, w_dn2 = ffn2_w_gu[0].astype(BF16), ffn2_w_down[0].astype(BF16)
    w_proj = (w_in[0, :, :c_dt].astype(BF16),
              jnp.pad(w_in[0, :, c_dt:c_dt + n_dt].astype(BF16), ((0, 0), (0, LANES - n_dt))),
              w_in[0, :, c_dt + n_dt:].astype(BF16))
    wa, wb, wo = w_a_out[0].astype(BF16), w_b_out[0].astype(BF16), w_o[0].astype(BF16)
    pad_heads = lambda v: jnp.pad(v[0], (0, LANES - n_dt))[None]
    p = {
        "conv_a_w": conv_a_w[0], "ssm_conv_w": ssm_conv_w[0], "ssm_conv_b": ssm_conv_b[0][None],
        "dt_bias": pad_heads(dt_bias), "a_log": pad_heads(a_log),
        "d_skip": jnp.repeat(d_skip[0], SSM_HEAD_DIM)[None], "ssm_norm_w": ssm_norm_w[0][None],
    }
    n1, nm, n2, nf = norm_ffn1[0][None], norm_mix[0][None], norm_ffn2[0][None], norm_final[None]
    eh = lax.broadcasted_iota(jnp.int32, (LANES, D_SSM), 0)
    ec = lax.broadcasted_iota(jnp.int32, (LANES, D_SSM), 1)
    expand = (ec // SSM_HEAD_DIM == eh).astype(BF16)

    def ffn1(x):
        return _ffn(x, n1, w_gu1, w_dn1, nf, tm=_row_tile(x.shape[0], 512), tf=D_FF // 2,
                    final_norm=False)

    def post(ya, yb, sga, sgb, h):
        t = h.shape[0]
        h = _outproj(ya, yb, sga, sgb, h, wa, wb, wo, tm=_row_tile(t, 512))
        return _ffn(h, n2, w_gu2, w_dn2, nf, tm=_row_tile(t, 512), tf=D_FF // 2, final_norm=True)

    zeros = lambda *s: jnp.zeros(s, F32)
    h_m = ffn1(jnp.pad(meta_tokens, ((0, CHUNK - N_META), (0, 0))))
    _, zs, xc, dt, _, _, ca_m, cx_m = _inproj_conv(
        h_m, nm, w_proj, p, zeros(SUBLANES, D_MODEL), zeros(SUBLANES, D_XBC),
        tm=CHUNK, seq_tiles=1, valid_len=N_META)
    _, hT_m, _ = _mixer(zs, xc, dt, p, expand, zeros(SSM_GROUPS, SSM_STATE, GROUP_W),
                        batch=1, seqlen=CHUNK, tb=CHUNK, valid_len=N_META)

    h_p = ffn1(x_prompt.reshape(bsz * seqlen, D_MODEL))
    tm = _row_tile(seqlen, 256)
    ya, zs, xc, dt, sga, sgb, ca_p, cx_p = _inproj_conv(
        h_p, nm, w_proj, p, ca_m[0], cx_m[0], tm=tm, seq_tiles=seqlen // tm, valid_len=tm)
    tb = _row_tile(seqlen, 512)
    yb, _, h_out_p = _mixer(zs, xc, dt, p, expand, hT_m[0], batch=bsz, seqlen=seqlen, tb=tb,
                            valid_len=tb)
    y_prompt = post(ya, yb, sga, sgb, h_p).reshape(bsz, seqlen, D_MODEL)
    prompt_conv_a = ca_p[None, :, SUBLANES - (CONV_A_TAPS - 1):, :]
    prompt_ssm_conv = cx_p[None, :, SUBLANES - (CONV_X_TAPS - 1):, :]
    prompt_ssm = h_out_p.reshape(1, bsz, SSM_HEADS, SSM_HEAD_DIM, SSM_STATE)

    x_s = jnp.swapaxes(x_sample, 0, 1).reshape(n_tok * n_seq, D_MODEL)
    h_s = ffn1(x_s)
    ab, ach, zs, xbc, dt, sga, sgb = _inproj(h_s, nm, w_proj, tm=_row_tile(n_tok * n_seq, 256), act_dtype=F32)
    tsf = lambda a: a.reshape(n_tok, n_seq, a.shape[-1])
    hh = lax.broadcasted_iota(jnp.int32, (D_BC, LANES), 1)
    kk = lax.broadcasted_iota(jnp.int32, (D_BC, LANES), 0)
    gsum = ((hh < SSM_HEADS) & (kk // SSM_STATE == hh // HEADS_PER_GROUP)).astype(BF16)
    ya, yb, sa_new, sx_new, sample_ssm = _decode_mixer(
        tsf(ab), tsf(ach), tsf(zs), tsf(xbc), tsf(dt),
        jnp.swapaxes(state_conv_a[0], 0, 1), jnp.swapaxes(state_ssm_conv[0], 0, 1), state_ssm[0],
        p, gsum, expand, n_tok=n_tok, n_seq=n_seq)
    flat = lambda a: a.reshape(n_tok * n_seq, a.shape[-1])
    y_s = post(flat(ya), flat(yb), sga, sgb, h_s)
    y_sample = jnp.swapaxes(y_s.reshape(n_tok, n_seq, D_MODEL), 0, 1)
    sample_conv_a = jnp.swapaxes(sa_new, 0, 1)[None]
    sample_ssm_conv = jnp.swapaxes(sx_new, 0, 1)[None]

    return (y_prompt, y_sample, prompt_conv_a, prompt_ssm_conv, prompt_ssm,
            sample_conv_a, sample_ssm_conv, sample_ssm[None])
```

```python
import functools

import jax
import jax.numpy as jnp
from jax import lax
from jax.experimental import pallas as pl
from jax.experimental.pallas import tpu as pltpu

D_MODEL = 1024
D_FF = 2816
D_SSM = 2048
SSM_HEADS = 32
SSM_HEAD_DIM = 64
SSM_GROUPS = 4
HEADS_PER_GROUP = SSM_HEADS // SSM_GROUPS
SSM_STATE = 128
D_BC = SSM_GROUPS * SSM_STATE
D_XBC = D_SSM + 2 * D_BC
CONV_A_TAPS = 3
CONV_X_TAPS = 4
N_META = 16
EPS = 1e-6

LANES = 128
SUBLANES = 8
CHUNK = 128
VMEM_LIMIT = 56 * 1024 * 1024

F32 = jnp.float32
BF16 = jnp.bfloat16


def _dot(a, b):
    return jnp.dot(a, b, preferred_element_type=F32)


def _rms(x, w):
    return x * lax.rsqrt(jnp.mean(x * x, axis=-1, keepdims=True) + EPS) * w


def _silu(x):
    return x * jax.nn.sigmoid(x)


def _softplus(x):
    return jnp.maximum(x, 0.0) + jnp.log1p(jnp.exp(-jnp.abs(x)))


def _split3(x):
    hi = x.astype(BF16)
    r = x - hi.astype(F32)
    mid = r.astype(BF16)
    lo = (r - mid.astype(F32)).astype(BF16)
    return hi, mid, lo


def _split2(x):
    hi = x.astype(BF16)
    lo = (x - hi.astype(F32)).astype(BF16)
    return hi, lo


def _resident(shape):
    return pl.BlockSpec(shape, lambda *_: (0,) * len(shape), pipeline_mode=pl.Buffered(1))


def _ffn_kernel(x_ref, nw_ref, wg_ref, wu_ref, wd_ref, fw_ref, o_ref, xn_ref, acc_ref, *, final_norm):
    j = pl.program_id(1)

    @pl.when(j == 0)
    def _():
        xn_ref[...] = _rms(x_ref[...], nw_ref[...]).astype(BF16)
        acc_ref[...] = jnp.zeros_like(acc_ref)

    xn = xn_ref[...]
    act = (_silu(_dot(xn, wg_ref[...])) * _dot(xn, wu_ref[...])).astype(BF16)
    acc_ref[...] += _dot(act, wd_ref[...])

    @pl.when(j == pl.num_programs(1) - 1)
    def _():
        h = x_ref[...] + 0.5 * acc_ref[...]
        if final_norm:
            h = _rms(h, fw_ref[...])
        o_ref[...] = h


def _ffn(x, nw, w_gu, w_down, fw, *, tm, tf, final_norm):
    t = x.shape[0]
    nf = D_FF // tf
    return pl.pallas_call(
        functools.partial(_ffn_kernel, final_norm=final_norm),
        grid=(t // tm, nf),
        in_specs=[
            pl.BlockSpec((tm, D_MODEL), lambda i, j: (i, 0)),
            pl.BlockSpec((1, D_MODEL), lambda i, j: (0, 0)),
            pl.BlockSpec((D_MODEL, tf), lambda i, j: (0, j)),
            pl.BlockSpec((D_MODEL, tf), lambda i, j: (0, j + nf)),
            pl.BlockSpec((tf, D_MODEL), lambda i, j: (j, 0)),
            pl.BlockSpec((1, D_MODEL), lambda i, j: (0, 0)),
        ],
        out_specs=pl.BlockSpec((tm, D_MODEL), lambda i, j: (i, 0)),
        out_shape=jax.ShapeDtypeStruct((t, D_MODEL), F32),
        scratch_shapes=[pltpu.VMEM((tm, D_MODEL), BF16), pltpu.VMEM((tm, D_MODEL), F32)],
        compiler_params=pltpu.CompilerParams(
            dimension_semantics=("parallel", "arbitrary"), vmem_limit_bytes=VMEM_LIMIT),
        name="ffn",
    )(x, nw, w_gu, w_gu, w_down, fw)


C_DT = 3 * D_MODEL + D_SSM + D_XBC
C_GATE = C_DT + LANES


def _proj_weight(w_refs, c0, width):
    w_main, w_dt, w_gate = w_refs
    if c0 < C_DT:
        return w_main[:, c0:c0 + width]
    if c0 < C_GATE:
        return w_dt[:, c0 - C_DT:c0 - C_DT + width]
    return w_gate[:, c0 - C_GATE:c0 - C_GATE + width]


def _proj_weight_specs():
    return [_resident((D_MODEL, C_DT + LANES)), _resident((D_MODEL, LANES)), _resident((D_MODEL, 2 * D_MODEL))]


def _inproj_kernel(h_ref, nw_ref, wm_ref, wdt_ref, wg_ref,
                   ab_ref, ach_ref, zs_ref, xbc_ref, dt_ref, sga_ref, sgb_ref):
    u = _rms(h_ref[...], nw_ref[...]).astype(BF16)

    def proj(c0, width):
        return _dot(u, _proj_weight((wm_ref, wdt_ref, wg_ref), c0, width))

    ab_ref[...] = proj(0, D_MODEL).astype(ab_ref.dtype)
    ach_ref[...] = (proj(D_MODEL, D_MODEL) * proj(2 * D_MODEL, D_MODEL)).astype(ach_ref.dtype)
    c0 = 3 * D_MODEL
    for k in range(D_SSM // D_MODEL):
        zs_ref[:, k * D_MODEL:(k + 1) * D_MODEL] = _silu(proj(c0 + k * D_MODEL, D_MODEL)).astype(zs_ref.dtype)
    c0 += D_SSM
    for k in range(D_XBC // D_MODEL):
        xbc_ref[:, k * D_MODEL:(k + 1) * D_MODEL] = proj(c0 + k * D_MODEL, D_MODEL).astype(xbc_ref.dtype)
    c0 += D_XBC
    dt_ref[...] = proj(c0, LANES)
    c0 += LANES
    sga_ref[...] = jax.nn.sigmoid(proj(c0, D_MODEL)).astype(sga_ref.dtype)
    sgb_ref[...] = jax.nn.sigmoid(proj(c0 + D_MODEL, D_MODEL)).astype(sgb_ref.dtype)


def _inproj(h, nw, w_packed, *, tm, act_dtype):
    t = h.shape[0]
    widths = (D_MODEL, D_MODEL, D_SSM, D_XBC, LANES, D_MODEL, D_MODEL)
    dtypes = (act_dtype, act_dtype, act_dtype, act_dtype, F32, act_dtype, act_dtype)
    return pl.pallas_call(
        _inproj_kernel,
        grid=(t // tm,),
        in_specs=[pl.BlockSpec((tm, D_MODEL), lambda i: (i, 0)), _resident((1, D_MODEL))]
        + _proj_weight_specs(),
        out_specs=[pl.BlockSpec((tm, w), lambda i: (i, 0)) for w in widths],
        out_shape=[jax.ShapeDtypeStruct((t, w), d) for w, d in zip(widths, dtypes)],
        compiler_params=pltpu.CompilerParams(
            dimension_semantics=("parallel",), vmem_limit_bytes=VMEM_LIMIT),
        name="inproj",
    )(h, nw, *w_packed)


COL_BLOCK = 512


def _inproj_conv_kernel(h_ref, nw_ref, wm_ref, wdt_ref, wg_ref, caw_ref, cxw_ref, cxb_ref,
                        ca0_ref, cx0_ref,
                        ya_ref, zs_ref, xc_ref, dt_ref, sga_ref, sgb_ref, cao_ref, cxo_ref,
                        ca_carry, cx_carry, *, tm, seq_tiles, valid_len):
    i = pl.program_id(0)

    @pl.when(i % seq_tiles == 0)
    def _():
        ca_carry[...] = ca0_ref[...]
        cx_carry[...] = cx0_ref[...]

    u = _rms(h_ref[...], nw_ref[...]).astype(BF16)

    def proj(c0, width=COL_BLOCK):
        return _dot(u, _proj_weight((wm_ref, wdt_ref, wg_ref), c0, width))

    n8 = tm // SUBLANES
    sublane = lax.broadcasted_iota(jnp.int32, (n8, SUBLANES, COL_BLOCK), 1)

    def causal_conv(x, carry_ref, w_ref, taps, cols):
        cur = x.reshape(n8, SUBLANES, COL_BLOCK)
        prev = jnp.concatenate([carry_ref[:, cols], x[:tm - SUBLANES]], axis=0).reshape(cur.shape)
        acc = w_ref[taps - 1:taps, cols] * x
        for d in range(1, taps):
            merged = jnp.where(sublane < SUBLANES - d, cur, prev)
            acc += w_ref[taps - 1 - d:taps - d, cols] * pltpu.roll(merged, d, axis=1).reshape(x.shape)
        carry_ref[:, cols] = x[valid_len - SUBLANES:valid_len]
        return acc

    c_z, c_x, c_dt = 3 * D_MODEL, 3 * D_MODEL + D_SSM, 3 * D_MODEL + D_SSM + D_XBC
    c_g = c_dt + LANES

    def tail_a(k, cols, prj):
        conv = causal_conv(prj[1] * prj[2], ca_carry, caw_ref, CONV_A_TAPS, cols)
        ya_ref[:, cols] = (prj[0] * conv).astype(ya_ref.dtype)

    def tail_z(k, cols, prj):
        zs_ref[:, cols] = _silu(prj[0]).astype(zs_ref.dtype)

    def tail_x(k, cols, prj):
        conv = causal_conv(prj[0], cx_carry, cxw_ref, CONV_X_TAPS, cols)
        xc_ref[:, cols] = _silu(conv + cxb_ref[:, cols]).astype(xc_ref.dtype)

    def tail_g(k, cols, prj):
        sga_ref[:, cols] = jax.nn.sigmoid(prj[0]).astype(sga_ref.dtype)
        sgb_ref[:, cols] = jax.nn.sigmoid(prj[1]).astype(sgb_ref.dtype)

    def blocks(tail, width, starts):
        return [(tail, k, [c + k * COL_BLOCK for c in starts]) for k in range(width // COL_BLOCK)]

    light = blocks(tail_z, D_SSM, [c_z])
    for n, (blk_a, blk_g) in enumerate(zip(blocks(tail_a, D_MODEL, [0, D_MODEL, 2 * D_MODEL]),
                                           blocks(tail_g, D_MODEL, [c_g, c_g + D_MODEL]))):
        light.insert(3 * n, blk_a)
        light.insert(3 * n + 2, blk_g)
    heavy = blocks(tail_x, D_XBC, [c_x])
    order = []
    for n in range(max(len(light), len(heavy))):
        order += light[n:n + 1] + heavy[n:n + 1]
    for tail, k, starts in order:
        tail(k, slice(k * COL_BLOCK, (k + 1) * COL_BLOCK), [proj(c) for c in starts])
    dt_ref[...] = proj(c_dt, LANES)
    cao_ref[0] = ca_carry[...]
    cxo_ref[0] = cx_carry[...]


def _inproj_conv(h, nw, w_packed, p, ca0, cx0, *, tm, seq_tiles, valid_len):
    t = h.shape[0]
    n_seq = t // (tm * seq_tiles)
    widths = (D_MODEL, D_SSM, D_XBC, LANES, D_MODEL, D_MODEL)
    dtypes = (BF16, BF16, BF16, F32, BF16, BF16)
    tail = lambda w: pl.BlockSpec((1, SUBLANES, w), lambda i: (i // seq_tiles, 0, 0))
    return pl.pallas_call(
        functools.partial(_inproj_conv_kernel, tm=tm, seq_tiles=seq_tiles, valid_len=valid_len),
        grid=(t // tm,),
        in_specs=[pl.BlockSpec((tm, D_MODEL), lambda i: (i, 0)), _resident((1, D_MODEL))]
        + _proj_weight_specs()
        + [_resident((CONV_A_TAPS, D_MODEL)), _resident((CONV_X_TAPS, D_XBC)),
           _resident((1, D_XBC)),
           _resident((SUBLANES, D_MODEL)), _resident((SUBLANES, D_XBC))],
        out_specs=[pl.BlockSpec((tm, w), lambda i: (i, 0)) for w in widths]
        + [tail(D_MODEL), tail(D_XBC)],
        out_shape=[jax.ShapeDtypeStruct((t, w), d) for w, d in zip(widths, dtypes)]
        + [jax.ShapeDtypeStruct((n_seq, SUBLANES, D_MODEL), F32),
           jax.ShapeDtypeStruct((n_seq, SUBLANES, D_XBC), F32)],
        scratch_shapes=[pltpu.VMEM((SUBLANES, D_MODEL), F32), pltpu.VMEM((SUBLANES, D_XBC), F32)],
        compiler_params=pltpu.CompilerParams(
            dimension_semantics=("arbitrary",), vmem_limit_bytes=VMEM_LIMIT),
        name="inproj_conv",
    )(h, nw, *w_packed, p["conv_a_w"], p["ssm_conv_w"], p["ssm_conv_b"], ca0, cx0)


def _outproj_kernel(ya_ref, yb_ref, sga_ref, sgb_ref, h_ref, wa_ref, wb_ref, wo_ref, o_ref):
    y_a = _dot(ya_ref[...].astype(BF16), wa_ref[...])
    y_b = _dot(yb_ref[...].astype(BF16), wb_ref[...])
    merged = sga_ref[...].astype(F32) * y_a + sgb_ref[...].astype(F32) * y_b
    o_ref[...] = h_ref[...] + _dot(merged.astype(BF16), wo_ref[...])


def _outproj(ya, yb, sga, sgb, h, wa, wb, wo, *, tm):
    t = h.shape[0]
    row = lambda w: pl.BlockSpec((tm, w), lambda i: (i, 0))
    return pl.pallas_call(
        _outproj_kernel,
        grid=(t // tm,),
        in_specs=[row(D_MODEL), row(D_SSM), row(D_MODEL), row(D_MODEL), row(D_MODEL),
                  _resident((D_MODEL, D_MODEL)), _resident((D_SSM, D_MODEL)),
                  _resident((D_MODEL, D_MODEL))],
        out_specs=row(D_MODEL),
        out_shape=jax.ShapeDtypeStruct((t, D_MODEL), F32),
        compiler_params=pltpu.CompilerParams(
            dimension_semantics=("parallel",), vmem_limit_bytes=VMEM_LIMIT),
        name="outproj",
    )(ya, yb, sga, sgb, h, wa, wb, wo)


def _gated_group_norm(y, xs, zs, dskip, normw):
    y = (y + dskip * xs) * zs
    gw = D_SSM // SSM_GROUPS
    parts = []
    for g in range(SSM_GROUPS):
        yg = y[:, g * gw:(g + 1) * gw]
        parts.append(yg * lax.rsqrt(jnp.mean(yg * yg, axis=-1, keepdims=True) + EPS))
    return jnp.concatenate(parts, axis=-1) * normw


GROUP_W = HEADS_PER_GROUP * SSM_HEAD_DIM
LOG2E = 1.4426950408889634


def _mixer_kernel(zs_ref, xc_ref, dt_ref, dtb_ref, alog_ref, dskip_ref, normw_ref, expand_ref,
                  h0_ref, yb_ref, hTo_ref, ho_ref, hT, ybuf, *, tb, valid_len):
    i = pl.program_id(1)
    q = CHUNK

    @pl.when(i == 0)
    def _():
        hT[...] = h0_ref[...]

    a_neg = -jnp.exp(alog_ref[...])
    rows = lax.broadcasted_iota(jnp.int32, (q, q), 0)
    cols_i = lax.broadcasted_iota(jnp.int32, (q, q), 1)
    causal = rows >= cols_i
    tri = causal.astype(BF16)
    first_head = lax.broadcasted_iota(jnp.int32, (q, LANES), 1) < SSM_HEAD_DIM

    def chunk_body(c, carry):
        r0 = pl.multiple_of(c * q, q)
        tok = pl.ds(r0, q)
        dt = _softplus(dt_ref[tok, :] + dtb_ref[...])
        if valid_len < tb:
            t_idx = r0 + lax.broadcasted_iota(jnp.int32, (q, LANES), 0)
            dt = jnp.where(t_idx < valid_len, dt, 0.0)
        hi, mid, lo = _split3(dt * a_neg)
        acs3 = _dot(tri, jnp.concatenate([hi, mid, lo], axis=1))
        acs = acs3[:, :LANES] + acs3[:, LANES:2 * LANES] + acs3[:, 2 * LANES:]
        acs2 = acs * LOG2E
        row_t = (acs2 - jnp.log2(dt)).T
        acs_end = acs2[q - 1:q, :]
        cd_hi, cd_lo = _split2(jnp.broadcast_to(jnp.exp2(acs_end), (SUBLANES, LANES)))
        stack = jnp.concatenate(
            [jnp.exp2(acs2).astype(BF16), (jnp.exp2(acs_end - acs2) * dt).astype(BF16),
             cd_hi, cd_lo], axis=0)

        for g in range(SSM_GROUPS):
            gcols = slice(g * GROUP_W, (g + 1) * GROUP_W)
            ex = _dot(stack, expand_ref[:, gcols])
            state_decay = ex[2 * q:2 * q + 1] + ex[2 * q + SUBLANES:2 * q + SUBLANES + 1]
            b_t = xc_ref[tok, D_SSM + g * SSM_STATE:D_SSM + (g + 1) * SSM_STATE].astype(F32).T.astype(BF16)
            c_bf = xc_ref[tok, D_SSM + D_BC + g * SSM_STATE:D_SSM + D_BC + (g + 1) * SSM_STATE]
            cb = _dot(c_bf, b_t)
            h_g = hT[g]
            y_g = ex[0:q] * _dot(c_bf, h_g.astype(BF16))
            xw_g = xc_ref[tok, gcols] * ex[q:2 * q].astype(BF16)
            hT[g] = h_g * state_decay + _dot(b_t, xw_g)
            for jp in range(HEADS_PER_GROUP // 2):
                pair = g * (HEADS_PER_GROUP // 2) + jp
                lanes = slice(pair * LANES, (pair + 1) * LANES)
                x_pair = xc_ref[tok, lanes]
                zero = jnp.zeros_like(x_pair)
                x2 = jnp.concatenate([jnp.where(first_head, x_pair, zero),
                                      jnp.where(first_head, zero, x_pair)], axis=0)
                m2 = []
                for e in range(2):
                    hd = 2 * pair + e
                    acs_col = jnp.broadcast_to(acs2[:, hd:hd + 1], (q, q))
                    decay = jnp.exp2(jnp.where(causal, acs_col - row_t[hd:hd + 1, :], -jnp.inf))
                    m2.append((cb * decay).astype(BF16))
                ybuf[:, lanes] = y_g[:, jp * LANES:(jp + 1) * LANES] + _dot(
                    jnp.concatenate(m2, axis=1), x2)

        yn = _gated_group_norm(ybuf[...], xc_ref[tok, :D_SSM].astype(F32),
                               zs_ref[tok, :].astype(F32), dskip_ref[...], normw_ref[...])
        yb_ref[tok, :] = yn.astype(yb_ref.dtype)
        return carry

    lax.fori_loop(0, tb // q, chunk_body, 0)

    @pl.when(i == pl.num_programs(1) - 1)
    def _():
        hTo_ref[0] = hT[...]
        for g in range(SSM_GROUPS):
            ho_ref[0, g] = hT[g].T


def _mixer(zs, xc, dt, p, expand, h0, *, batch, seqlen, tb, valid_len):
    nt = seqlen // tb
    row = lambda w: pl.BlockSpec((tb, w), lambda b, i: (b * nt + i, 0))
    full = lambda shape: pl.BlockSpec(shape, lambda b, i: (0,) * len(shape))
    per_b = lambda shape: pl.BlockSpec((1,) + shape, lambda b, i: (b,) + (0,) * len(shape))
    return pl.pallas_call(
        functools.partial(_mixer_kernel, tb=tb, valid_len=valid_len),
        grid=(batch, nt),
        in_specs=[row(D_SSM), row(D_XBC), row(LANES),
                  full((1, LANES)), full((1, LANES)), full((1, D_SSM)), full((1, D_SSM)),
                  full((LANES, D_SSM)), full((SSM_GROUPS, SSM_STATE, GROUP_W))],
        out_specs=[row(D_SSM), per_b((SSM_GROUPS, SSM_STATE, GROUP_W)),
                   per_b((SSM_GROUPS, GROUP_W, SSM_STATE))],
        out_shape=[jax.ShapeDtypeStruct((batch * seqlen, D_SSM), BF16),
                   jax.ShapeDtypeStruct((batch, SSM_GROUPS, SSM_STATE, GROUP_W), F32),
                   jax.ShapeDtypeStruct((batch, SSM_GROUPS, GROUP_W, SSM_STATE), F32)],
        scratch_shapes=[pltpu.VMEM((SSM_GROUPS, SSM_STATE, GROUP_W), F32),
                        pltpu.VMEM((CHUNK, D_SSM), F32)],
        compiler_params=pltpu.CompilerParams(
            dimension_semantics=("arbitrary", "arbitrary"), vmem_limit_bytes=VMEM_LIMIT),
        name="mixer",
    )(zs, xc, dt, p["dt_bias"], p["a_log"], p["d_skip"], p["ssm_norm_w"], expand, h0)


SEQ_BLOCK = SUBLANES


def _decode_mixer_kernel(ab_ref, ach_ref, zs_ref, xbc_ref, dt_ref, sa_ref, sx_ref, h0_ref,
                         caw_ref, cxw_ref, cxb_ref, dtb_ref, alog_ref, dskip_ref, normw_ref,
                         gsum_ref, expand_ref,
                         ya_ref, yb_ref, sao_ref, sxo_ref, ho_ref,
                         xc_buf, bc_buf, os_buf, xw_buf, cd_buf, y_buf, *, n_tok):
    sb = SEQ_BLOCK

    xp = [sa_ref[k] for k in range(CONV_A_TAPS - 1)] + [ach_ref[t] for t in range(n_tok)]
    for t in range(n_tok):
        conv = caw_ref[0:1, :] * xp[t]
        for k in range(1, CONV_A_TAPS):
            conv += caw_ref[k:k + 1, :] * xp[t + k]
        ya_ref[t] = ab_ref[t] * conv
    for k in range(CONV_A_TAPS - 1):
        sao_ref[k] = xp[n_tok + k]

    xp = [sx_ref[k] for k in range(CONV_X_TAPS - 1)] + [xbc_ref[t] for t in range(n_tok)]
    for t in range(n_tok):
        conv = cxw_ref[0:1, :] * xp[t]
        for k in range(1, CONV_X_TAPS):
            conv += cxw_ref[k:k + 1, :] * xp[t + k]
        xc_buf[t * sb:(t + 1) * sb, :] = _silu(conv + cxb_ref[...])
    for k in range(CONV_X_TAPS - 1):
        sxo_ref[k] = xp[n_tok + k]

    a_neg = -jnp.exp(alog_ref[...])
    dts, acs = [], []
    for t in range(n_tok):
        dts.append(_softplus(dt_ref[t] + dtb_ref[...]))
        acs.append(dts[t] * a_neg + (acs[t - 1] if t else 0.0))

    pairs = [(qi, si) for qi in range(n_tok) for si in range(qi + 1)]
    prods = [xc_buf[qi * sb:(qi + 1) * sb, D_SSM + D_BC:] * xc_buf[si * sb:(si + 1) * sb, D_SSM:D_SSM + D_BC]
             for qi, si in pairs]
    p_hi, p_lo = _split2(jnp.concatenate(prods, axis=0))
    cbh = _dot(p_hi, gsum_ref[...]) + _dot(p_lo, gsum_ref[...])
    coef = [jnp.exp(acs[t]) for t in range(n_tok)]
    coef += [cbh[k * sb:(k + 1) * sb] * jnp.exp(acs[qi] - acs[si]) * dts[si]
             for k, (qi, si) in enumerate(pairs)]
    coef += [jnp.exp(acs[n_tok - 1] - acs[s]) * dts[s] for s in range(n_tok)]
    coef.append(jnp.exp(acs[n_tok - 1]))
    c_hi, c_lo = _split2(jnp.concatenate(coef, axis=0))
    coef_x = _dot(c_hi, expand_ref[...]) + _dot(c_lo, expand_ref[...])
    slab = lambda k: coef_x[k * sb:(k + 1) * sb]

    def store_tiles(buf, r0, val):
        for lt in range(val.shape[1] // LANES):
            buf[lt, r0:r0 + val.shape[0], :] = val[:, lt * LANES:(lt + 1) * LANES]

    def load_rows(buf, lt0, n_lt, rows):
        return jnp.concatenate([buf[lt0 + j, rows, :] for j in range(n_lt)], axis=1)

    store_tiles(bc_buf, 0, xc_buf[:, D_SSM:])
    store_tiles(os_buf, 0, coef_x[0:n_tok * sb])
    diag = {pr: slab(n_tok + k) for k, pr in enumerate(pairs)}
    k0 = n_tok + len(pairs)
    for s in range(n_tok):
        store_tiles(xw_buf, s * sb, slab(k0 + s) * xc_buf[s * sb:(s + 1) * sb, :D_SSM])
    cd_hi, cd_lo = _split2(slab(k0 + n_tok))
    store_tiles(cd_buf, 0, cd_hi.astype(F32))
    store_tiles(cd_buf, sb, cd_lo.astype(F32))

    for t in range(n_tok):
        acc = diag[(t, 0)] * xc_buf[0:sb, :D_SSM]
        for s in range(1, t + 1):
            acc += diag[(t, s)] * xc_buf[s * sb:(s + 1) * sb, :D_SSM]
        store_tiles(y_buf, t * sb, acc)

    gt = GROUP_W // LANES
    k_rows = 2 * SUBLANES
    krow = lax.broadcasted_iota(jnp.int32, (k_rows, SSM_STATE), 0)
    ones_rows = jnp.where((krow >= n_tok) & (krow < n_tok + 2), 1.0, 0.0).astype(BF16)
    pad_lhs = jnp.zeros((k_rows - n_tok - 2, GROUP_W), F32)
    pad_rhs = jnp.zeros((k_rows - n_tok, SSM_STATE), F32)

    def seq_body(b, carry):
        tok_rows = pl.ds(b, n_tok, stride=sb)
        for g in range(SSM_GROUPS):
            b_bg = bc_buf[g, tok_rows, :]
            c_bg = bc_buf[SSM_GROUPS + g, tok_rows, :]
            heads = pl.ds(g * HEADS_PER_GROUP, HEADS_PER_GROUP)
            h_bg = h0_ref[b, heads].reshape(GROUP_W, SSM_STATE)
            y_off = lax.dot_general(c_bg.astype(BF16), h_bg.astype(BF16),
                                    (((1,), (1,)), ((), ())), preferred_element_type=F32)
            y_off = y_off * load_rows(os_buf, g * gt, gt, tok_rows)
            for j in range(gt):
                y_buf[g * gt + j, tok_rows, :] = (y_buf[g * gt + j, tok_rows, :]
                                                  + y_off[:, j * LANES:(j + 1) * LANES])
            lhs = jnp.concatenate(
                [load_rows(xw_buf, g * gt, gt, tok_rows),
                 load_rows(cd_buf, g * gt, gt, pl.ds(b, 2, stride=sb)), pad_lhs],
                axis=0).astype(BF16)
            rhs = jnp.concatenate(
                [jnp.concatenate([b_bg, pad_rhs], axis=0).astype(BF16), ones_rows], axis=1)
            upd = lax.dot_general(lhs, rhs, (((0,), (0,)), ((), ())), preferred_element_type=F32)
            new = h_bg * upd[:, SSM_STATE:] + upd[:, :SSM_STATE]
            ho_ref[b, heads] = new.reshape(HEADS_PER_GROUP, SSM_HEAD_DIM, SSM_STATE)
        return carry

    lax.fori_loop(0, sb, seq_body, 0, unroll=2)

    xs_all = xc_buf[:, :D_SSM]
    zs_all = jnp.concatenate([zs_ref[t] for t in range(n_tok)], axis=0)
    y_all = load_rows(y_buf, 0, D_SSM // LANES, slice(None))
    yn = _gated_group_norm(y_all, xs_all, zs_all, dskip_ref[...], normw_ref[...])
    for t in range(n_tok):
        yb_ref[t] = yn[t * sb:(t + 1) * sb]


def _decode_mixer(ab, ach, zs, xbc, dt, sa, sx, h0, p, gsum, expand, *, n_tok, n_seq):
    sb = SEQ_BLOCK
    tok = lambda w: pl.BlockSpec((n_tok, sb, w), lambda i: (0, i, 0))
    st = lambda k, w: pl.BlockSpec((k, sb, w), lambda i: (0, i, 0))
    hblk = pl.BlockSpec((sb, SSM_HEADS, SSM_HEAD_DIM, SSM_STATE), lambda i: (i, 0, 0, 0))
    full = lambda shape: pl.BlockSpec(shape, lambda i: (0,) * len(shape))
    return pl.pallas_call(
        functools.partial(_decode_mixer_kernel, n_tok=n_tok),
        grid=(n_seq // sb,),
        in_specs=[tok(D_MODEL), tok(D_MODEL), tok(D_SSM), tok(D_XBC), tok(LANES),
                  st(CONV_A_TAPS - 1, D_MODEL), st(CONV_X_TAPS - 1, D_XBC), hblk,
                  full((CONV_A_TAPS, D_MODEL)), full((CONV_X_TAPS, D_XBC)), full((1, D_XBC)),
                  full((1, LANES)), full((1, LANES)), full((1, D_SSM)), full((1, D_SSM)),
                  full((D_BC, LANES)), full((LANES, D_SSM))],
        out_specs=[tok(D_MODEL), tok(D_SSM), st(CONV_A_TAPS - 1, D_MODEL),
                   st(CONV_X_TAPS - 1, D_XBC), hblk],
        out_shape=[jax.ShapeDtypeStruct((n_tok, n_seq, D_MODEL), F32),
                   jax.ShapeDtypeStruct((n_tok, n_seq, D_SSM), F32),
                   jax.ShapeDtypeStruct((CONV_A_TAPS - 1, n_seq, D_MODEL), F32),
                   jax.ShapeDtypeStruct((CONV_X_TAPS - 1, n_seq, D_XBC), F32),
                   jax.ShapeDtypeStruct((n_seq, SSM_HEADS, SSM_HEAD_DIM, SSM_STATE), F32)],
        scratch_shapes=[pltpu.VMEM((n_tok * sb, D_XBC), F32),
                        pltpu.VMEM((2 * D_BC // LANES, n_tok * sb, LANES), F32),
                        pltpu.VMEM((D_SSM // LANES, n_tok * sb, LANES), F32),
                        pltpu.VMEM((D_SSM // LANES, n_tok * sb, LANES), F32),
                        pltpu.VMEM((D_SSM // LANES, 2 * sb, LANES), F32),
                        pltpu.VMEM((D_SSM // LANES, n_tok * sb, LANES), F32)],
        compiler_params=pltpu.CompilerParams(
            dimension_semantics=("parallel",), vmem_limit_bytes=VMEM_LIMIT),
        name="decode_mixer",
    )(ab, ach, zs, xbc, dt, sa, sx, h0, p["conv_a_w"], p["ssm_conv_w"], p["ssm_conv_b"],
      p["dt_bias"], p["a_log"], p["d_skip"], p["ssm_norm_w"], gsum, expand)


def _row_tile(t, cap):
    if t <= cap:
        return t
    best = LANES
    for cand in range(LANES, cap + 1, LANES):
        if t % cand == 0:
            best = cand
    return best


def kernel(x_prompt, x_sample, state_conv_a, state_ssm_conv, state_ssm, meta_tokens, norm_ffn1, ffn1_w_gu, ffn1_w_down, norm_mix, w_in, conv_a_w, w_a_out, ssm_conv_w, ssm_conv_b, dt_bias, a_log, d_skip, ssm_norm_w, w_b_out, w_o, norm_ffn2, ffn2_w_gu, ffn2_w_down, norm_final):
    bsz, seqlen, _ = x_prompt.shape
    n_seq, n_tok, _ = x_sample.shape
    n_dt = SSM_HEADS
    c_dt = 3 * D_MODEL + D_SSM + D_XBC

    w_gu1, w_dn1 = ffn1_w_gu[0].astype(BF16), ffn1_w_down[0].astype(BF16)
    w_gu2, w_dn2 = ffn2_w_gu[0].astype(BF16), ffn2_w_down[0].astype(BF16)
    w_proj = (jnp.pad(w_in[0, :, :c_dt].astype(BF16), ((0, 0), (0, LANES))),
              jnp.pad(w_in[0, :, c_dt:c_dt + n_dt].astype(BF16), ((0, 0), (0, LANES - n_dt))),
              w_in[0, :, c_dt + n_dt:].astype(BF16))
    wa, wb, wo = w_a_out[0].astype(BF16), w_b_out[0].astype(BF16), w_o[0].astype(BF16)
    pad_heads = lambda v: jnp.pad(v[0], (0, LANES - n_dt))[None]
    p = {
        "conv_a_w": conv_a_w[0], "ssm_conv_w": ssm_conv_w[0], "ssm_conv_b": ssm_conv_b[0][None],
        "dt_bias": pad_heads(dt_bias), "a_log": pad_heads(a_log),
        "d_skip": jnp.repeat(d_skip[0], SSM_HEAD_DIM)[None], "ssm_norm_w": ssm_norm_w[0][None],
    }
    n1, nm, n2, nf = norm_ffn1[0][None], norm_mix[0][None], norm_ffn2[0][None], norm_final[None]
    eh = lax.broadcasted_iota(jnp.int32, (LANES, D_SSM), 0)
    ec = lax.broadcasted_iota(jnp.int32, (LANES, D_SSM), 1)
    expand = (ec // SSM_HEAD_DIM == eh).astype(BF16)

    def ffn1(x):
        return _ffn(x, n1, w_gu1, w_dn1, nf, tm=_row_tile(x.shape[0], 512), tf=D_FF // 2,
                    final_norm=False)

    def post(ya, yb, sga, sgb, h):
        t = h.shape[0]
        h = _outproj(ya, yb, sga, sgb, h, wa, wb, wo, tm=_row_tile(t, 512))
        return _ffn(h, n2, w_gu2, w_dn2, nf, tm=_row_tile(t, 512), tf=D_FF // 2, final_norm=True)

    zeros = lambda *s: jnp.zeros(s, F32)
    h_m = ffn1(jnp.pad(meta_tokens, ((0, CHUNK - N_META), (0, 0))))
    _, zs, xc, dt, _, _, ca_m, cx_m = _inproj_conv(
        h_m, nm, w_proj, p, zeros(SUBLANES, D_MODEL), zeros(SUBLANES, D_XBC),
        tm=CHUNK, seq_tiles=1, valid_len=N_META)
    _, hT_m, _ = _mixer(zs, xc, dt, p, expand, zeros(SSM_GROUPS, SSM_STATE, GROUP_W),
                        batch=1, seqlen=CHUNK, tb=CHUNK, valid_len=N_META)

    h_p = ffn1(x_prompt.reshape(bsz * seqlen, D_MODEL))
    tm = _row_tile(seqlen, 512)
    ya, zs, xc, dt, sga, sgb, ca_p, cx_p = _inproj_conv(
        h_p, nm, w_proj, p, ca_m[0], cx_m[0], tm=tm, seq_tiles=seqlen // tm, valid_len=tm)
    tb = _row_tile(seqlen, 512)
    yb, _, h_out_p = _mixer(zs, xc, dt, p, expand, hT_m[0], batch=bsz, seqlen=seqlen, tb=tb,
                            valid_len=tb)
    y_prompt = post(ya, yb, sga, sgb, h_p).reshape(bsz, seqlen, D_MODEL)
    prompt_conv_a = ca_p[None, :, SUBLANES - (CONV_A_TAPS - 1):, :]
    prompt_ssm_conv = cx_p[None, :, SUBLANES - (CONV_X_TAPS - 1):, :]
    prompt_ssm = h_out_p.reshape(1, bsz, SSM_HEADS, SSM_HEAD_DIM, SSM_STATE)

    x_s = jnp.swapaxes(x_sample, 0, 1).reshape(n_tok * n_seq, D_MODEL)
    h_s = ffn1(x_s)
    ab, ach, zs, xbc, dt, sga, sgb = _inproj(h_s, nm, w_proj, tm=_row_tile(n_tok * n_seq, 256), act_dtype=F32)
    tsf = lambda a: a.reshape(n_tok, n_seq, a.shape[-1])
    hh = lax.broadcasted_iota(jnp.int32, (D_BC, LANES), 1)
    kk = lax.broadcasted_iota(jnp.int32, (D_BC, LANES), 0)
    gsum = ((hh < SSM_HEADS) & (kk // SSM_STATE == hh // HEADS_PER_GROUP)).astype(BF16)
    ya, yb, sa_new, sx_new, sample_ssm = _decode_mixer(
        tsf(ab), tsf(ach), tsf(zs), tsf(xbc), tsf(dt),
        jnp.swapaxes(state_conv_a[0], 0, 1), jnp.swapaxes(state_ssm_conv[0], 0, 1), state_ssm[0],
        p, gsum, expand, n_tok=n_tok, n_seq=n_seq)
    flat = lambda a: a.reshape(n_tok * n_seq, a.shape[-1])
    y_s = post(flat(ya), flat(yb), sga, sgb, h_s)
    y_sample = jnp.swapaxes(y_s.reshape(n_tok, n_seq, D_MODEL), 0, 1)
    sample_conv_a = jnp.swapaxes(sa_new, 0, 1)[None]
    sample_ssm_conv = jnp.swapaxes(sx_new, 0, 1)[None]

    return (y_prompt, y_sample, prompt_conv_a, prompt_ssm_conv, prompt_ssm,
            sample_conv_a, sample_ssm_conv, sample_ssm[None])
```

```python
import functools

import jax
import jax.numpy as jnp
from jax import lax
from jax.experimental import pallas as pl
from jax.experimental.pallas import tpu as pltpu

D_MODEL = 1024
D_FF = 2816
D_SSM = 2048
SSM_HEADS = 32
SSM_HEAD_DIM = 64
SSM_GROUPS = 4
HEADS_PER_GROUP = SSM_HEADS // SSM_GROUPS
SSM_STATE = 128
D_BC = SSM_GROUPS * SSM_STATE
D_XBC = D_SSM + 2 * D_BC
CONV_A_TAPS = 3
CONV_X_TAPS = 4
N_META = 16
EPS = 1e-6

LANES = 128
SUBLANES = 8
CHUNK = 128
VMEM_LIMIT = 56 * 1024 * 1024

F32 = jnp.float32
BF16 = jnp.bfloat16


def _dot(a, b):
    return jnp.dot(a, b, preferred_element_type=F32)


def _rms(x, w):
    return x * lax.rsqrt(jnp.mean(x * x, axis=-1, keepdims=True) + EPS) * w


def _silu(x):
    return x * jax.nn.sigmoid(x)


def _softplus(x):
    return jnp.maximum(x, 0.0) + jnp.log1p(jnp.exp(-jnp.abs(x)))


def _split3(x):
    hi = x.astype(BF16)
    r = x - hi.astype(F32)
    mid = r.astype(BF16)
    lo = (r - mid.astype(F32)).astype(BF16)
    return hi, mid, lo


def _split2(x):
    hi = x.astype(BF16)
    lo = (x - hi.astype(F32)).astype(BF16)
    return hi, lo


def _resident(shape):
    return pl.BlockSpec(shape, lambda *_: (0,) * len(shape), pipeline_mode=pl.Buffered(1))


def _ffn_kernel(x_ref, nw_ref, wg_ref, wu_ref, wd_ref, fw_ref, o_ref, *, final_norm):
    x = x_ref[...]
    xn = _rms(x, nw_ref[...]).astype(BF16)
    act = (_silu(_dot(xn, wg_ref[...])) * _dot(xn, wu_ref[...])).astype(BF16)
    h = x + 0.5 * _dot(act, wd_ref[...])
    if final_norm:
        h = _rms(h, fw_ref[...])
    o_ref[...] = h


def _ffn(x, nw, w_gu, w_down, fw, *, tm, final_norm):
    t = x.shape[0]
    once = pl.Buffered(1)
    return pl.pallas_call(
        functools.partial(_ffn_kernel, final_norm=final_norm),
        grid=(t // tm,),
        in_specs=[
            pl.BlockSpec((tm, D_MODEL), lambda i: (i, 0)),
            _resident((1, D_MODEL)),
            pl.BlockSpec((D_MODEL, D_FF), lambda i: (0, 0), pipeline_mode=once),
            pl.BlockSpec((D_MODEL, D_FF), lambda i: (0, 1), pipeline_mode=once),
            _resident((D_FF, D_MODEL)),
            _resident((1, D_MODEL)),
        ],
        out_specs=pl.BlockSpec((tm, D_MODEL), lambda i: (i, 0)),
        out_shape=jax.ShapeDtypeStruct((t, D_MODEL), F32),
        compiler_params=pltpu.CompilerParams(
            dimension_semantics=("parallel",), vmem_limit_bytes=VMEM_LIMIT),
        name="ffn",
    )(x, nw, w_gu, w_gu, w_down, fw)


C_DT = 3 * D_MODEL + D_SSM + D_XBC
C_GATE = C_DT + LANES


def _proj_weight(w_refs, c0, width):
    w_main, w_dt, w_gate = w_refs
    if c0 < C_DT:
        return w_main[:, c0:c0 + width]
    if c0 < C_GATE:
        return w_dt[:, c0 - C_DT:c0 - C_DT + width]
    return w_gate[:, c0 - C_GATE:c0 - C_GATE + width]


def _proj_weight_specs():
    return [_resident((D_MODEL, C_DT + LANES)), _resident((D_MODEL, LANES)), _resident((D_MODEL, 2 * D_MODEL))]


def _inproj_kernel(h_ref, nw_ref, wm_ref, wdt_ref, wg_ref,
                   ab_ref, ach_ref, zs_ref, xbc_ref, dt_ref, sga_ref, sgb_ref):
    u = _rms(h_ref[...], nw_ref[...]).astype(BF16)

    def proj(c0, width):
        return _dot(u, _proj_weight((wm_ref, wdt_ref, wg_ref), c0, width))

    ab_ref[...] = proj(0, D_MODEL).astype(ab_ref.dtype)
    ach_ref[...] = (proj(D_MODEL, D_MODEL) * proj(2 * D_MODEL, D_MODEL)).astype(ach_ref.dtype)
    c0 = 3 * D_MODEL
    for k in range(D_SSM // D_MODEL):
        zs_ref[:, k * D_MODEL:(k + 1) * D_MODEL] = _silu(proj(c0 + k * D_MODEL, D_MODEL)).astype(zs_ref.dtype)
    c0 += D_SSM
    for k in range(D_XBC // D_MODEL):
        xbc_ref[:, k * D_MODEL:(k + 1) * D_MODEL] = proj(c0 + k * D_MODEL, D_MODEL).astype(xbc_ref.dtype)
    c0 += D_XBC
    dt_ref[...] = proj(c0, LANES)
    c0 += LANES
    sga_ref[...] = jax.nn.sigmoid(proj(c0, D_MODEL)).astype(sga_ref.dtype)
    sgb_ref[...] = jax.nn.sigmoid(proj(c0 + D_MODEL, D_MODEL)).astype(sgb_ref.dtype)


def _inproj(h, nw, w_packed, *, tm, act_dtype):
    t = h.shape[0]
    widths = (D_MODEL, D_MODEL, D_SSM, D_XBC, LANES, D_MODEL, D_MODEL)
    dtypes = (act_dtype, act_dtype, act_dtype, act_dtype, F32, act_dtype, act_dtype)
    return pl.pallas_call(
        _inproj_kernel,
        grid=(t // tm,),
        in_specs=[pl.BlockSpec((tm, D_MODEL), lambda i: (i, 0)), _resident((1, D_MODEL))]
        + _proj_weight_specs(),
        out_specs=[pl.BlockSpec((tm, w), lambda i: (i, 0)) for w in widths],
        out_shape=[jax.ShapeDtypeStruct((t, w), d) for w, d in zip(widths, dtypes)],
        compiler_params=pltpu.CompilerParams(
            dimension_semantics=("parallel",), vmem_limit_bytes=VMEM_LIMIT),
        name="inproj",
    )(h, nw, *w_packed)


COL_BLOCK = 512


def _inproj_conv_kernel(h_ref, nw_ref, wm_ref, wdt_ref, wg_ref, caw_ref, cxw_ref, cxb_ref,
                        ca0_ref, cx0_ref,
                        ya_ref, zs_ref, xc_ref, dt_ref, sga_ref, sgb_ref, cao_ref, cxo_ref,
                        ca_carry, cx_carry, *, tm, seq_tiles, valid_len):
    i = pl.program_id(0)

    @pl.when(i % seq_tiles == 0)
    def _():
        ca_carry[...] = ca0_ref[...]
        cx_carry[...] = cx0_ref[...]

    u = _rms(h_ref[...], nw_ref[...]).astype(BF16)

    def proj(c0, width=COL_BLOCK):
        return _dot(u, _proj_weight((wm_ref, wdt_ref, wg_ref), c0, width))

    n8 = tm // SUBLANES
    sublane = lax.broadcasted_iota(jnp.int32, (n8, SUBLANES, COL_BLOCK), 1)

    def causal_conv(x, carry_ref, w_ref, taps, cols):
        cur = x.reshape(n8, SUBLANES, COL_BLOCK)
        prev = jnp.concatenate([carry_ref[:, cols], x[:tm - SUBLANES]], axis=0).reshape(cur.shape)
        acc = w_ref[taps - 1:taps, cols] * x
        for d in range(1, taps):
            merged = jnp.where(sublane < SUBLANES - d, cur, prev)
            acc += w_ref[taps - 1 - d:taps - d, cols] * pltpu.roll(merged, d, axis=1).reshape(x.shape)
        carry_ref[:, cols] = x[valid_len - SUBLANES:valid_len]
        return acc

    c_z, c_x, c_dt = 3 * D_MODEL, 3 * D_MODEL + D_SSM, 3 * D_MODEL + D_SSM + D_XBC
    c_g = c_dt + LANES

    def tail_a(k, cols, prj):
        conv = causal_conv(prj[1] * prj[2], ca_carry, caw_ref, CONV_A_TAPS, cols)
        ya_ref[:, cols] = (prj[0] * conv).astype(ya_ref.dtype)

    def tail_z(k, cols, prj):
        zs_ref[:, cols] = _silu(prj[0]).astype(zs_ref.dtype)

    def tail_x(k, cols, prj):
        conv = causal_conv(prj[0], cx_carry, cxw_ref, CONV_X_TAPS, cols)
        xc_ref[:, cols] = _silu(conv + cxb_ref[:, cols]).astype(xc_ref.dtype)

    def tail_g(k, cols, prj):
        sga_ref[:, cols] = jax.nn.sigmoid(prj[0]).astype(sga_ref.dtype)
        sgb_ref[:, cols] = jax.nn.sigmoid(prj[1]).astype(sgb_ref.dtype)

    def blocks(tail, width, starts):
        return [(tail, k, [c + k * COL_BLOCK for c in starts]) for k in range(width // COL_BLOCK)]

    light = blocks(tail_z, D_SSM, [c_z])
    for n, (blk_a, blk_g) in enumerate(zip(blocks(tail_a, D_MODEL, [0, D_MODEL, 2 * D_MODEL]),
                                           blocks(tail_g, D_MODEL, [c_g, c_g + D_MODEL]))):
        light.insert(3 * n, blk_a)
        light.insert(3 * n + 2, blk_g)
    heavy = blocks(tail_x, D_XBC, [c_x])
    order = []
    for n in range(max(len(light), len(heavy))):
        order += light[n:n + 1] + heavy[n:n + 1]
    for tail, k, starts in order:
        tail(k, slice(k * COL_BLOCK, (k + 1) * COL_BLOCK), [proj(c) for c in starts])
    dt_ref[...] = proj(c_dt, LANES)
    cao_ref[0] = ca_carry[...]
    cxo_ref[0] = cx_carry[...]


def _inproj_conv(h, nw, w_packed, p, ca0, cx0, *, tm, seq_tiles, valid_len):
    t = h.shape[0]
    n_seq = t // (tm * seq_tiles)
    widths = (D_MODEL, D_SSM, D_XBC, LANES, D_MODEL, D_MODEL)
    dtypes = (BF16, BF16, BF16, F32, BF16, BF16)
    tail = lambda w: pl.BlockSpec((1, SUBLANES, w), lambda i: (i // seq_tiles, 0, 0))
    return pl.pallas_call(
        functools.partial(_inproj_conv_kernel, tm=tm, seq_tiles=seq_tiles, valid_len=valid_len),
        grid=(t // tm,),
        in_specs=[pl.BlockSpec((tm, D_MODEL), lambda i: (i, 0)), _resident((1, D_MODEL))]
        + _proj_weight_specs()
        + [_resident((CONV_A_TAPS, D_MODEL)), _resident((CONV_X_TAPS, D_XBC)),
           _resident((1, D_XBC)),
           _resident((SUBLANES, D_MODEL)), _resident((SUBLANES, D_XBC))],
        out_specs=[pl.BlockSpec((tm, w), lambda i: (i, 0)) for w in widths]
        + [tail(D_MODEL), tail(D_XBC)],
        out_shape=[jax.ShapeDtypeStruct((t, w), d) for w, d in zip(widths, dtypes)]
        + [jax.ShapeDtypeStruct((n_seq, SUBLANES, D_MODEL), F32),
           jax.ShapeDtypeStruct((n_seq, SUBLANES, D_XBC), F32)],
        scratch_shapes=[pltpu.VMEM((SUBLANES, D_MODEL), F32), pltpu.VMEM((SUBLANES, D_XBC), F32)],
        compiler_params=pltpu.CompilerParams(
            dimension_semantics=("arbitrary",), vmem_limit_bytes=VMEM_LIMIT),
        name="inproj_conv",
    )(h, nw, *w_packed, p["conv_a_w"], p["ssm_conv_w"], p["ssm_conv_b"], ca0, cx0)


def _outproj_kernel(ya_ref, yb_ref, sga_ref, sgb_ref, h_ref, wa_ref, wb_ref, wo_ref, o_ref):
    y_a = _dot(ya_ref[...].astype(BF16), wa_ref[...])
    y_b = _dot(yb_ref[...].astype(BF16), wb_ref[...])
    merged = sga_ref[...].astype(F32) * y_a + sgb_ref[...].astype(F32) * y_b
    o_ref[...] = h_ref[...] + _dot(merged.astype(BF16), wo_ref[...])


def _outproj(ya, yb, sga, sgb, h, wa, wb, wo, *, tm):
    t = h.shape[0]
    row = lambda w: pl.BlockSpec((tm, w), lambda i: (i, 0))
    return pl.pallas_call(
        _outproj_kernel,
        grid=(t // tm,),
        in_specs=[row(D_MODEL), row(D_SSM), row(D_MODEL), row(D_MODEL), row(D_MODEL),
                  _resident((D_MODEL, D_MODEL)), _resident((D_SSM, D_MODEL)),
                  _resident((D_MODEL, D_MODEL))],
        out_specs=row(D_MODEL),
        out_shape=jax.ShapeDtypeStruct((t, D_MODEL), F32),
        compiler_params=pltpu.CompilerParams(
            dimension_semantics=("parallel",), vmem_limit_bytes=VMEM_LIMIT),
        name="outproj",
    )(ya, yb, sga, sgb, h, wa, wb, wo)


def _gated_group_norm(y, xs, zs, dskip, normw):
    y = (y + dskip * xs) * zs
    gw = D_SSM // SSM_GROUPS
    parts = []
    for g in range(SSM_GROUPS):
        yg = y[:, g * gw:(g + 1) * gw]
        parts.append(yg * lax.rsqrt(jnp.mean(yg * yg, axis=-1, keepdims=True) + EPS))
    return jnp.concatenate(parts, axis=-1) * normw


GROUP_W = HEADS_PER_GROUP * SSM_HEAD_DIM
LOG2E = 1.4426950408889634


def _mixer_kernel(zs_ref, xc_ref, dt_ref, dtb_ref, alog_ref, dskip_ref, normw_ref, expand_ref,
                  h0_ref, yb_ref, hTo_ref, ho_ref, hT, ybuf, *, tb, valid_len):
    i = pl.program_id(1)
    q = CHUNK

    @pl.when(i == 0)
    def _():
        hT[...] = h0_ref[...]

    a_neg = -jnp.exp(alog_ref[...])
    rows = lax.broadcasted_iota(jnp.int32, (q, q), 0)
    cols_i = lax.broadcasted_iota(jnp.int32, (q, q), 1)
    causal = rows >= cols_i
    tri = causal.astype(BF16)
    first_head = lax.broadcasted_iota(jnp.int32, (q, LANES), 1) < SSM_HEAD_DIM

    def chunk_body(c, carry):
        r0 = pl.multiple_of(c * q, q)
        tok = pl.ds(r0, q)
        dt = _softplus(dt_ref[tok, :] + dtb_ref[...])
        if valid_len < tb:
            t_idx = r0 + lax.broadcasted_iota(jnp.int32, (q, LANES), 0)
            dt = jnp.where(t_idx < valid_len, dt, 0.0)
        hi, mid, lo = _split3(dt * a_neg)
        acs3 = _dot(tri, jnp.concatenate([hi, mid, lo], axis=1))
        acs = acs3[:, :LANES] + acs3[:, LANES:2 * LANES] + acs3[:, 2 * LANES:]
        acs2 = acs * LOG2E
        row_t = (acs2 - jnp.log2(dt)).T
        acs_end = acs2[q - 1:q, :]
        cd_hi, cd_lo = _split2(jnp.broadcast_to(jnp.exp2(acs_end), (SUBLANES, LANES)))
        stack = jnp.concatenate(
            [jnp.exp2(acs2).astype(BF16), (jnp.exp2(acs_end - acs2) * dt).astype(BF16),
             cd_hi, cd_lo], axis=0)

        for g in range(SSM_GROUPS):
            gcols = slice(g * GROUP_W, (g + 1) * GROUP_W)
            ex = _dot(stack, expand_ref[:, gcols])
            state_decay = ex[2 * q:2 * q + 1] + ex[2 * q + SUBLANES:2 * q + SUBLANES + 1]
            b_t = xc_ref[tok, D_SSM + g * SSM_STATE:D_SSM + (g + 1) * SSM_STATE].astype(F32).T.astype(BF16)
            c_bf = xc_ref[tok, D_SSM + D_BC + g * SSM_STATE:D_SSM + D_BC + (g + 1) * SSM_STATE]
            cb = _dot(c_bf, b_t)
            h_g = hT[g]
            y_g = ex[0:q] * _dot(c_bf, h_g.astype(BF16))
            xw_g = xc_ref[tok, gcols] * ex[q:2 * q].astype(BF16)
            hT[g] = h_g * state_decay + _dot(b_t, xw_g)
            for jp in range(HEADS_PER_GROUP // 2):
                pair = g * (HEADS_PER_GROUP // 2) + jp
                lanes = slice(pair * LANES, (pair + 1) * LANES)
                x_pair = xc_ref[tok, lanes]
                zero = jnp.zeros_like(x_pair)
                x2 = jnp.concatenate([jnp.where(first_head, x_pair, zero),
                                      jnp.where(first_head, zero, x_pair)], axis=0)
                m2 = []
                for e in range(2):
                    hd = 2 * pair + e
                    acs_col = jnp.broadcast_to(acs2[:, hd:hd + 1], (q, q))
                    decay = jnp.exp2(jnp.where(causal, acs_col - row_t[hd:hd + 1, :], -jnp.inf))
                    m2.append((cb * decay).astype(BF16))
                ybuf[:, lanes] = y_g[:, jp * LANES:(jp + 1) * LANES] + _dot(
                    jnp.concatenate(m2, axis=1), x2)

        yn = _gated_group_norm(ybuf[...], xc_ref[tok, :D_SSM].astype(F32),
                               zs_ref[tok, :].astype(F32), dskip_ref[...], normw_ref[...])
        yb_ref[tok, :] = yn.astype(yb_ref.dtype)
        return carry

    lax.fori_loop(0, tb // q, chunk_body, 0)

    @pl.when(i == pl.num_programs(1) - 1)
    def _():
        hTo_ref[0] = hT[...]
        for g in range(SSM_GROUPS):
            ho_ref[0, g] = hT[g].T


def _mixer(zs, xc, dt, p, expand, h0, *, batch, seqlen, tb, valid_len):
    nt = seqlen // tb
    row = lambda w: pl.BlockSpec((tb, w), lambda b, i: (b * nt + i, 0))
    full = lambda shape: pl.BlockSpec(shape, lambda b, i: (0,) * len(shape))
    per_b = lambda shape: pl.BlockSpec((1,) + shape, lambda b, i: (b,) + (0,) * len(shape))
    return pl.pallas_call(
        functools.partial(_mixer_kernel, tb=tb, valid_len=valid_len),
        grid=(batch, nt),
        in_specs=[row(D_SSM), row(D_XBC), row(LANES),
                  full((1, LANES)), full((1, LANES)), full((1, D_SSM)), full((1, D_SSM)),
                  full((LANES, D_SSM)), full((SSM_GROUPS, SSM_STATE, GROUP_W))],
        out_specs=[row(D_SSM), per_b((SSM_GROUPS, SSM_STATE, GROUP_W)),
                   per_b((SSM_GROUPS, GROUP_W, SSM_STATE))],
        out_shape=[jax.ShapeDtypeStruct((batch * seqlen, D_SSM), BF16),
                   jax.ShapeDtypeStruct((batch, SSM_GROUPS, SSM_STATE, GROUP_W), F32),
                   jax.ShapeDtypeStruct((batch, SSM_GROUPS, GROUP_W, SSM_STATE), F32)],
        scratch_shapes=[pltpu.VMEM((SSM_GROUPS, SSM_STATE, GROUP_W), F32),
                        pltpu.VMEM((CHUNK, D_SSM), F32)],
        compiler_params=pltpu.CompilerParams(
            dimension_semantics=("arbitrary", "arbitrary"), vmem_limit_bytes=VMEM_LIMIT),
        name="mixer",
    )(zs, xc, dt, p["dt_bias"], p["a_log"], p["d_skip"], p["ssm_norm_w"], expand, h0)


SEQ_BLOCK = SUBLANES


def _decode_mixer_kernel(ab_ref, ach_ref, zs_ref, xbc_ref, dt_ref, sa_ref, sx_ref, h0_ref,
                         caw_ref, cxw_ref, cxb_ref, dtb_ref, alog_ref, dskip_ref, normw_ref,
                         gsum_ref, expand_ref,
                         ya_ref, yb_ref, sao_ref, sxo_ref, ho_ref,
                         xc_buf, bc_buf, os_buf, xw_buf, cd_buf, y_buf, *, n_tok):
    sb = SEQ_BLOCK

    xp = [sa_ref[k] for k in range(CONV_A_TAPS - 1)] + [ach_ref[t] for t in range(n_tok)]
    for t in range(n_tok):
        conv = caw_ref[0:1, :] * xp[t]
        for k in range(1, CONV_A_TAPS):
            conv += caw_ref[k:k + 1, :] * xp[t + k]
        ya_ref[t] = ab_ref[t] * conv
    for k in range(CONV_A_TAPS - 1):
        sao_ref[k] = xp[n_tok + k]

    xp = [sx_ref[k] for k in range(CONV_X_TAPS - 1)] + [xbc_ref[t] for t in range(n_tok)]
    for t in range(n_tok):
        conv = cxw_ref[0:1, :] * xp[t]
        for k in range(1, CONV_X_TAPS):
            conv += cxw_ref[k:k + 1, :] * xp[t + k]
        xc_buf[t * sb:(t + 1) * sb, :] = _silu(conv + cxb_ref[...])
    for k in range(CONV_X_TAPS - 1):
        sxo_ref[k] = xp[n_tok + k]

    a_neg = -jnp.exp(alog_ref[...])
    dts, acs = [], []
    for t in range(n_tok):
        dts.append(_softplus(dt_ref[t] + dtb_ref[...]))
        acs.append(dts[t] * a_neg + (acs[t - 1] if t else 0.0))

    pairs = [(qi, si) for qi in range(n_tok) for si in range(qi + 1)]
    prods = [xc_buf[qi * sb:(qi + 1) * sb, D_SSM + D_BC:] * xc_buf[si * sb:(si + 1) * sb, D_SSM:D_SSM + D_BC]
             for qi, si in pairs]
    p_hi, p_lo = _split2(jnp.concatenate(prods, axis=0))
    cbh = _dot(p_hi, gsum_ref[...]) + _dot(p_lo, gsum_ref[...])
    coef = [jnp.exp(acs[t]) for t in range(n_tok)]
    coef += [cbh[k * sb:(k + 1) * sb] * jnp.exp(acs[qi] - acs[si]) * dts[si]
             for k, (qi, si) in enumerate(pairs)]
    coef += [jnp.exp(acs[n_tok - 1] - acs[s]) * dts[s] for s in range(n_tok)]
    coef.append(jnp.exp(acs[n_tok - 1]))
    c_hi, c_lo = _split2(jnp.concatenate(coef, axis=0))
    coef_x = _dot(c_hi, expand_ref[...]) + _dot(c_lo, expand_ref[...])
    slab = lambda k: coef_x[k * sb:(k + 1) * sb]

    def store_tiles(buf, r0, val):
        for lt in range(val.shape[1] // LANES):
            buf[lt, r0:r0 + val.shape[0], :] = val[:, lt * LANES:(lt + 1) * LANES]

    def load_rows(buf, lt0, n_lt, rows):
        return jnp.concatenate([buf[lt0 + j, rows, :] for j in range(n_lt)], axis=1)

    store_tiles(bc_buf, 0, xc_buf[:, D_SSM:])
    store_tiles(os_buf, 0, coef_x[0:n_tok * sb])
    diag = {pr: slab(n_tok + k) for k, pr in enumerate(pairs)}
    k0 = n_tok + len(pairs)
    for s in range(n_tok):
        store_tiles(xw_buf, s * sb, slab(k0 + s) * xc_buf[s * sb:(s + 1) * sb, :D_SSM])
    cd_hi, cd_lo = _split2(slab(k0 + n_tok))
    store_tiles(cd_buf, 0, cd_hi.astype(F32))
    store_tiles(cd_buf, sb, cd_lo.astype(F32))

    for t in range(n_tok):
        acc = diag[(t, 0)] * xc_buf[0:sb, :D_SSM]
        for s in range(1, t + 1):
            acc += diag[(t, s)] * xc_buf[s * sb:(s + 1) * sb, :D_SSM]
        store_tiles(y_buf, t * sb, acc)

    gt = GROUP_W // LANES
    k_rows = 2 * SUBLANES
    krow = lax.broadcasted_iota(jnp.int32, (k_rows, SSM_STATE), 0)
    ones_rows = jnp.where((krow >= n_tok) & (krow < n_tok + 2), 1.0, 0.0).astype(BF16)
    pad_lhs = jnp.zeros((k_rows - n_tok - 2, GROUP_W), F32)
    pad_rhs = jnp.zeros((k_rows - n_tok, SSM_STATE), F32)

    def seq_body(b, carry):
        tok_rows = pl.ds(b, n_tok, stride=sb)
        for g in range(SSM_GROUPS):
            b_bg = bc_buf[g, tok_rows, :]
            c_bg = bc_buf[SSM_GROUPS + g, tok_rows, :]
            heads = pl.ds(g * HEADS_PER_GROUP, HEADS_PER_GROUP)
            h_bg = h0_ref[b, heads].reshape(GROUP_W, SSM_STATE)
            y_off = lax.dot_general(c_bg.astype(BF16), h_bg.astype(BF16),
                                    (((1,), (1,)), ((), ())), preferred_element_type=F32)
            y_off = y_off * load_rows(os_buf, g * gt, gt, tok_rows)
            for j in range(gt):
                y_buf[g * gt + j, tok_rows, :] = (y_buf[g * gt + j, tok_rows, :]
                                                  + y_off[:, j * LANES:(j + 1) * LANES])
            lhs = jnp.concatenate(
                [load_rows(xw_buf, g * gt, gt, tok_rows),
                 load_rows(cd_buf, g * gt, gt, pl.ds(b, 2, stride=sb)), pad_lhs],
                axis=0).astype(BF16)
            rhs = jnp.concatenate(
                [jnp.concatenate([b_bg, pad_rhs], axis=0).astype(BF16), ones_rows], axis=1)
            upd = lax.dot_general(lhs, rhs, (((0,), (0,)), ((), ())), preferred_element_type=F32)
            new = h_bg * upd[:, SSM_STATE:] + upd[:, :SSM_STATE]
            ho_ref[b, heads] = new.reshape(HEADS_PER_GROUP, SSM_HEAD_DIM, SSM_STATE)
        return carry

    lax.fori_loop(0, sb, seq_body, 0, unroll=2)

    xs_all = xc_buf[:, :D_SSM]
    zs_all = jnp.concatenate([zs_ref[t] for t in range(n_tok)], axis=0)
    y_all = load_rows(y_buf, 0, D_SSM // LANES, slice(None))
    yn = _gated_group_norm(y_all, xs_all, zs_all, dskip_ref[...], normw_ref[...])
    for t in range(n_tok):
        yb_ref[t] = yn[t * sb:(t + 1) * sb]


def _decode_mixer(ab, ach, zs, xbc, dt, sa, sx, h0, p, gsum, expand, *, n_tok, n_seq):
    sb = SEQ_BLOCK
    tok = lambda w: pl.BlockSpec((n_tok, sb, w), lambda i: (0, i, 0))
    st = lambda k, w: pl.BlockSpec((k, sb, w), lambda i: (0, i, 0))
    hblk = pl.BlockSpec((sb, SSM_HEADS, SSM_HEAD_DIM, SSM_STATE), lambda i: (i, 0, 0, 0))
    full = lambda shape: pl.BlockSpec(shape, lambda i: (0,) * len(shape))
    return pl.pallas_call(
        functools.partial(_decode_mixer_kernel, n_tok=n_tok),
        grid=(n_seq // sb,),
        in_specs=[tok(D_MODEL), tok(D_MODEL), tok(D_SSM), tok(D_XBC), tok(LANES),
                  st(CONV_A_TAPS - 1, D_MODEL), st(CONV_X_TAPS - 1, D_XBC), hblk,
                  full((CONV_A_TAPS, D_MODEL)), full((CONV_X_TAPS, D_XBC)), full((1, D_XBC)),
                  full((1, LANES)), full((1, LANES)), full((1, D_SSM)), full((1, D_SSM)),
                  full((D_BC, LANES)), full((LANES, D_SSM))],
        out_specs=[tok(D_MODEL), tok(D_SSM), st(CONV_A_TAPS - 1, D_MODEL),
                   st(CONV_X_TAPS - 1, D_XBC), hblk],
        out_shape=[jax.ShapeDtypeStruct((n_tok, n_seq, D_MODEL), F32),
                   jax.ShapeDtypeStruct((n_tok, n_seq, D_SSM), F32),
                   jax.ShapeDtypeStruct((CONV_A_TAPS - 1, n_seq, D_MODEL), F32),
                   jax.ShapeDtypeStruct((CONV_X_TAPS - 1, n_seq, D_XBC), F32),
                   jax.ShapeDtypeStruct((n_seq, SSM_HEADS, SSM_HEAD_DIM, SSM_STATE), F32)],
        scratch_shapes=[pltpu.VMEM((n_tok * sb, D_XBC), F32),
                        pltpu.VMEM((2 * D_BC // LANES, n_tok * sb, LANES), F32),
                        pltpu.VMEM((D_SSM // LANES, n_tok * sb, LANES), F32),
                        pltpu.VMEM((D_SSM // LANES, n_tok * sb, LANES), F32),
                        pltpu.VMEM((D_SSM // LANES, 2 * sb, LANES), F32),
                        pltpu.VMEM((D_SSM // LANES, n_tok * sb, LANES), F32)],
        compiler_params=pltpu.CompilerParams(
            dimension_semantics=("parallel",), vmem_limit_bytes=VMEM_LIMIT),
        name="decode_mixer",
    )(ab, ach, zs, xbc, dt, sa, sx, h0, p["conv_a_w"], p["ssm_conv_w"], p["ssm_conv_b"],
      p["dt_bias"], p["a_log"], p["d_skip"], p["ssm_norm_w"], gsum, expand)


def _row_tile(t, cap):
    if t <= cap:
        return t
    best = LANES
    for cand in range(LANES, cap + 1, LANES):
        if t % cand == 0:
            best = cand
    return best


def kernel(x_prompt, x_sample, state_conv_a, state_ssm_conv, state_ssm, meta_tokens, norm_ffn1, ffn1_w_gu, ffn1_w_down, norm_mix, w_in, conv_a_w, w_a_out, ssm_conv_w, ssm_conv_b, dt_bias, a_log, d_skip, ssm_norm_w, w_b_out, w_o, norm_ffn2, ffn2_w_gu, ffn2_w_down, norm_final):
    bsz, seqlen, _ = x_prompt.shape
    n_seq, n_tok, _ = x_sample.shape
    n_dt = SSM_HEADS
    c_dt = 3 * D_MODEL + D_SSM + D_XBC

    w_gu1, w_dn1 = ffn1_w_gu[0].astype(BF16), ffn1_w_down[0].astype(BF16)
    w_gu2, w_dn2 = ffn2_w_gu[0].astype(BF16), ffn2_w_down[0].astype(BF16)
    w_proj = (jnp.pad(w_in[0, :, :c_dt].astype(BF16), ((0, 0), (0, LANES))),
              jnp.pad(w_in[0, :, c_dt:c_dt + n_dt].astype(BF16), ((0, 0), (0, LANES - n_dt))),
              w_in[0, :, c_dt + n_dt:].astype(BF16))
    wa, wb, wo = w_a_out[0].astype(BF16), w_b_out[0].astype(BF16), w_o[0].astype(BF16)
    pad_heads = lambda v: jnp.pad(v[0], (0, LANES - n_dt))[None]
    p = {
        "conv_a_w": conv_a_w[0], "ssm_conv_w": ssm_conv_w[0], "ssm_conv_b": ssm_conv_b[0][None],
        "dt_bias": pad_heads(dt_bias), "a_log": pad_heads(a_log),
        "d_skip": jnp.repeat(d_skip[0], SSM_HEAD_DIM)[None], "ssm_norm_w": ssm_norm_w[0][None],
    }
    n1, nm, n2, nf = norm_ffn1[0][None], norm_mix[0][None], norm_ffn2[0][None], norm_final[None]
    eh = lax.broadcasted_iota(jnp.int32, (LANES, D_SSM), 0)
    ec = lax.broadcasted_iota(jnp.int32, (LANES, D_SSM), 1)
    expand = (ec // SSM_HEAD_DIM == eh).astype(BF16)

    def ffn1(x):
        return _ffn(x, n1, w_gu1, w_dn1, nf, tm=_row_tile(x.shape[0], 512), final_norm=False)

    def post(ya, yb, sga, sgb, h):
        t = h.shape[0]
        h = _outproj(ya, yb, sga, sgb, h, wa, wb, wo, tm=_row_tile(t, 512))
        return _ffn(h, n2, w_gu2, w_dn2, nf, tm=_row_tile(t, 512), final_norm=True)

    zeros = lambda *s: jnp.zeros(s, F32)
    h_m = ffn1(jnp.pad(meta_tokens, ((0, CHUNK - N_META), (0, 0))))
    _, zs, xc, dt, _, _, ca_m, cx_m = _inproj_conv(
        h_m, nm, w_proj, p, zeros(SUBLANES, D_MODEL), zeros(SUBLANES, D_XBC),
        tm=CHUNK, seq_tiles=1, valid_len=N_META)
    _, hT_m, _ = _mixer(zs, xc, dt, p, expand, zeros(SSM_GROUPS, SSM_STATE, GROUP_W),
                        batch=1, seqlen=CHUNK, tb=CHUNK, valid_len=N_META)

    h_p = ffn1(x_prompt.reshape(bsz * seqlen, D_MODEL))
    tm = _row_tile(seqlen, 512)
    ya, zs, xc, dt, sga, sgb, ca_p, cx_p = _inproj_conv(
        h_p, nm, w_proj, p, ca_m[0], cx_m[0], tm=tm, seq_tiles=seqlen // tm, valid_len=tm)
    tb = _row_tile(seqlen, 512)
    yb, _, h_out_p = _mixer(zs, xc, dt, p, expand, hT_m[0], batch=bsz, seqlen=seqlen, tb=tb,
                            valid_len=tb)
    y_prompt = post(ya, yb, sga, sgb, h_p).reshape(bsz, seqlen, D_MODEL)
    prompt_conv_a = ca_p[None, :, SUBLANES - (CONV_A_TAPS - 1):, :]
    prompt_ssm_conv = cx_p[None, :, SUBLANES - (CONV_X_TAPS - 1):, :]
    prompt_ssm = h_out_p.reshape(1, bsz, SSM_HEADS, SSM_HEAD_DIM, SSM_STATE)

    x_s = jnp.swapaxes(x_sample, 0, 1).reshape(n_tok * n_seq, D_MODEL)
    h_s = ffn1(x_s)
    ab, ach, zs, xbc, dt, sga, sgb = _inproj(h_s, nm, w_proj, tm=_row_tile(n_tok * n_seq, 256), act_dtype=F32)
    tsf = lambda a: a.reshape(n_tok, n_seq, a.shape[-1])
    hh = lax.broadcasted_iota(jnp.int32, (D_BC, LANES), 1)
    kk = lax.broadcasted_iota(jnp.int32, (D_BC, LANES), 0)
    gsum = ((hh < SSM_HEADS) & (kk // SSM_STATE == hh // HEADS_PER_GROUP)).astype(BF16)
    ya, yb, sa_new, sx_new, sample_ssm = _decode_mixer(
        tsf(ab), tsf(ach), tsf(zs), tsf(xbc), tsf(dt),
        jnp.swapaxes(state_conv_a[0], 0, 1), jnp.swapaxes(state_ssm_conv[0], 0, 1), state_ssm[0],
        p, gsum, expand, n_tok=n_tok, n_seq=n_seq)
    flat = lambda a: a.reshape(n_tok * n_seq, a.shape[-1])
    y_s = post(flat(ya), flat(yb), sga, sgb, h_s)
    y_sample = jnp.swapaxes(y_s.reshape(n_tok, n_seq, D_MODEL), 0, 1)
    sample_conv_a = jnp.swapaxes(sa_new, 0, 1)[None]
    sample_ssm_conv = jnp.swapaxes(sx_new, 0, 1)[None]

    return (y_prompt, y_sample, prompt_conv_a, prompt_ssm_conv, prompt_ssm,
            sample_conv_a, sample_ssm_conv, sample_ssm[None])
```

```python
import functools

import jax
import jax.numpy as jnp
from jax import lax
from jax.experimental import pallas as pl
from jax.experimental.pallas import tpu as pltpu

D_MODEL = 1024
D_FF = 2816
D_SSM = 2048
SSM_HEADS = 32
SSM_HEAD_DIM = 64
SSM_GROUPS = 4
HEADS_PER_GROUP = SSM_HEADS // SSM_GROUPS
SSM_STATE = 128
D_BC = SSM_GROUPS * SSM_STATE
D_XBC = D_SSM + 2 * D_BC
CONV_A_TAPS = 3
CONV_X_TAPS = 4
N_META = 16
EPS = 1e-6

LANES = 128
SUBLANES = 8
CHUNK = 128
VMEM_LIMIT = 56 * 1024 * 1024

F32 = jnp.float32
BF16 = jnp.bfloat16


def _dot(a, b):
    return jnp.dot(a, b, preferred_element_type=F32)


def _rms(x, w):
    return x * lax.rsqrt(jnp.mean(x * x, axis=-1, keepdims=True) + EPS) * w


def _silu(x):
    return x * jax.nn.sigmoid(x)


def _softplus(x):
    return jnp.maximum(x, 0.0) + jnp.log1p(jnp.exp(-jnp.abs(x)))


def _split3(x):
    hi = x.astype(BF16)
    r = x - hi.astype(F32)
    mid = r.astype(BF16)
    lo = (r - mid.astype(F32)).astype(BF16)
    return hi, mid, lo


def _split2(x):
    hi = x.astype(BF16)
    lo = (x - hi.astype(F32)).astype(BF16)
    return hi, lo


def _resident(shape):
    return pl.BlockSpec(shape, lambda *_: (0,) * len(shape), pipeline_mode=pl.Buffered(1))


def _ffn_kernel(x_ref, nw_ref, wg_ref, wu_ref, wd_ref, fw_ref, o_ref, *, final_norm):
    x = x_ref[...]
    xn = _rms(x, nw_ref[...]).astype(BF16)
    act = (_silu(_dot(xn, wg_ref[...])) * _dot(xn, wu_ref[...])).astype(BF16)
    h = x + 0.5 * _dot(act, wd_ref[...])
    if final_norm:
        h = _rms(h, fw_ref[...])
    o_ref[...] = h


def _ffn(x, nw, w_gu, w_down, fw, *, tm, final_norm):
    t = x.shape[0]
    once = pl.Buffered(1)
    return pl.pallas_call(
        functools.partial(_ffn_kernel, final_norm=final_norm),
        grid=(t // tm,),
        in_specs=[
            pl.BlockSpec((tm, D_MODEL), lambda i: (i, 0)),
            _resident((1, D_MODEL)),
            pl.BlockSpec((D_MODEL, D_FF), lambda i: (0, 0), pipeline_mode=once),
            pl.BlockSpec((D_MODEL, D_FF), lambda i: (0, 1), pipeline_mode=once),
            _resident((D_FF, D_MODEL)),
            _resident((1, D_MODEL)),
        ],
        out_specs=pl.BlockSpec((tm, D_MODEL), lambda i: (i, 0)),
        out_shape=jax.ShapeDtypeStruct((t, D_MODEL), F32),
        compiler_params=pltpu.CompilerParams(
            dimension_semantics=("parallel",), vmem_limit_bytes=VMEM_LIMIT),
        name="ffn",
    )(x, nw, w_gu, w_gu, w_down, fw)


C_DT = 3 * D_MODEL + D_SSM + D_XBC
C_GATE = C_DT + LANES


def _proj_weight(w_refs, c0, width):
    w_main, w_dt, w_gate = w_refs
    if c0 < C_DT:
        return w_main[:, c0:c0 + width]
    if c0 < C_GATE:
        return w_dt[:, c0 - C_DT:c0 - C_DT + width]
    return w_gate[:, c0 - C_GATE:c0 - C_GATE + width]


def _proj_weight_specs():
    return [_resident((D_MODEL, C_DT + LANES)), _resident((D_MODEL, LANES)), _resident((D_MODEL, 2 * D_MODEL))]


def _inproj_kernel(h_ref, nw_ref, wm_ref, wdt_ref, wg_ref,
                   ab_ref, ach_ref, zs_ref, xbc_ref, dt_ref, sga_ref, sgb_ref):
    u = _rms(h_ref[...], nw_ref[...]).astype(BF16)

    def proj(c0, width):
        return _dot(u, _proj_weight((wm_ref, wdt_ref, wg_ref), c0, width))

    ab_ref[...] = proj(0, D_MODEL).astype(ab_ref.dtype)
    ach_ref[...] = (proj(D_MODEL, D_MODEL) * proj(2 * D_MODEL, D_MODEL)).astype(ach_ref.dtype)
    c0 = 3 * D_MODEL
    for k in range(D_SSM // D_MODEL):
        zs_ref[:, k * D_MODEL:(k + 1) * D_MODEL] = _silu(proj(c0 + k * D_MODEL, D_MODEL)).astype(zs_ref.dtype)
    c0 += D_SSM
    for k in range(D_XBC // D_MODEL):
        xbc_ref[:, k * D_MODEL:(k + 1) * D_MODEL] = proj(c0 + k * D_MODEL, D_MODEL).astype(xbc_ref.dtype)
    c0 += D_XBC
    dt_ref[...] = proj(c0, LANES)
    c0 += LANES
    sga_ref[...] = jax.nn.sigmoid(proj(c0, D_MODEL)).astype(sga_ref.dtype)
    sgb_ref[...] = jax.nn.sigmoid(proj(c0 + D_MODEL, D_MODEL)).astype(sgb_ref.dtype)


def _inproj(h, nw, w_packed, *, tm, act_dtype):
    t = h.shape[0]
    widths = (D_MODEL, D_MODEL, D_SSM, D_XBC, LANES, D_MODEL, D_MODEL)
    dtypes = (act_dtype, act_dtype, act_dtype, act_dtype, F32, act_dtype, act_dtype)
    return pl.pallas_call(
        _inproj_kernel,
        grid=(t // tm,),
        in_specs=[pl.BlockSpec((tm, D_MODEL), lambda i: (i, 0)), _resident((1, D_MODEL))]
        + _proj_weight_specs(),
        out_specs=[pl.BlockSpec((tm, w), lambda i: (i, 0)) for w in widths],
        out_shape=[jax.ShapeDtypeStruct((t, w), d) for w, d in zip(widths, dtypes)],
        compiler_params=pltpu.CompilerParams(
            dimension_semantics=("parallel",), vmem_limit_bytes=VMEM_LIMIT),
        name="inproj",
    )(h, nw, *w_packed)


COL_BLOCK = 512


def _inproj_conv_kernel(h_ref, nw_ref, wm_ref, wdt_ref, wg_ref, caw_ref, cxw_ref, cxb_ref,
                        ca0_ref, cx0_ref,
                        ya_ref, zs_ref, xc_ref, dt_ref, sga_ref, sgb_ref, cao_ref, cxo_ref,
                        ca_carry, cx_carry, *stages, tm, seq_tiles, valid_len):
    i = pl.program_id(0)

    @pl.when(i % seq_tiles == 0)
    def _():
        ca_carry[...] = ca0_ref[...]
        cx_carry[...] = cx0_ref[...]

    u = _rms(h_ref[...], nw_ref[...]).astype(BF16)

    def proj(c0, width=COL_BLOCK):
        return _dot(u, _proj_weight((wm_ref, wdt_ref, wg_ref), c0, width))

    n8 = tm // SUBLANES
    sublane = lax.broadcasted_iota(jnp.int32, (n8, SUBLANES, COL_BLOCK), 1)

    def causal_conv(x, carry_ref, w_ref, taps, cols):
        cur = x.reshape(n8, SUBLANES, COL_BLOCK)
        prev = jnp.concatenate([carry_ref[:, cols], x[:tm - SUBLANES]], axis=0).reshape(cur.shape)
        acc = w_ref[taps - 1:taps, cols] * x
        for d in range(1, taps):
            merged = jnp.where(sublane < SUBLANES - d, cur, prev)
            acc += w_ref[taps - 1 - d:taps - d, cols] * pltpu.roll(merged, d, axis=1).reshape(x.shape)
        carry_ref[:, cols] = x[valid_len - SUBLANES:valid_len]
        return acc

    c_z, c_x, c_dt = 3 * D_MODEL, 3 * D_MODEL + D_SSM, 3 * D_MODEL + D_SSM + D_XBC
    c_g = c_dt + LANES

    def tail_a(k, cols, prj):
        conv = causal_conv(prj[1] * prj[2], ca_carry, caw_ref, CONV_A_TAPS, cols)
        ya_ref[:, cols] = (prj[0] * conv).astype(ya_ref.dtype)

    def tail_z(k, cols, prj):
        zs_ref[:, cols] = _silu(prj[0]).astype(zs_ref.dtype)

    def tail_x(k, cols, prj):
        conv = causal_conv(prj[0], cx_carry, cxw_ref, CONV_X_TAPS, cols)
        xc_ref[:, cols] = _silu(conv + cxb_ref[:, cols]).astype(xc_ref.dtype)

    def tail_g(k, cols, prj):
        sga_ref[:, cols] = jax.nn.sigmoid(prj[0]).astype(sga_ref.dtype)
        sgb_ref[:, cols] = jax.nn.sigmoid(prj[1]).astype(sgb_ref.dtype)

    def blocks(tail, width, starts):
        return [(tail, k, [c + k * COL_BLOCK for c in starts]) for k in range(width // COL_BLOCK)]

    light = blocks(tail_z, D_SSM, [c_z])
    for n, (blk_a, blk_g) in enumerate(zip(blocks(tail_a, D_MODEL, [0, D_MODEL, 2 * D_MODEL]),
                                           blocks(tail_g, D_MODEL, [c_g, c_g + D_MODEL]))):
        light.insert(3 * n, blk_a)
        light.insert(3 * n + 2, blk_g)
    heavy = blocks(tail_x, D_XBC, [c_x])
    order = []
    for n in range(max(len(light), len(heavy))):
        order += heavy[n:n + 1] + light[n:n + 1]
    park = jnp.minimum(i, 0)
    for n, (tail, k, starts) in enumerate(order):
        buf = stages[n % 2]
        for j, c in enumerate(starts):
            buf[park, j] = proj(c)
        tail(k, slice(k * COL_BLOCK, (k + 1) * COL_BLOCK), [buf[park, j] for j in range(len(starts))])
    dt_ref[...] = proj(c_dt, LANES)
    cao_ref[0] = ca_carry[...]
    cxo_ref[0] = cx_carry[...]


def _inproj_conv(h, nw, w_packed, p, ca0, cx0, *, tm, seq_tiles, valid_len):
    t = h.shape[0]
    n_seq = t // (tm * seq_tiles)
    widths = (D_MODEL, D_SSM, D_XBC, LANES, D_MODEL, D_MODEL)
    dtypes = (BF16, BF16, BF16, F32, BF16, BF16)
    tail = lambda w: pl.BlockSpec((1, SUBLANES, w), lambda i: (i // seq_tiles, 0, 0))
    return pl.pallas_call(
        functools.partial(_inproj_conv_kernel, tm=tm, seq_tiles=seq_tiles, valid_len=valid_len),
        grid=(t // tm,),
        in_specs=[pl.BlockSpec((tm, D_MODEL), lambda i: (i, 0)), _resident((1, D_MODEL))]
        + _proj_weight_specs()
        + [_resident((CONV_A_TAPS, D_MODEL)), _resident((CONV_X_TAPS, D_XBC)),
           _resident((1, D_XBC)),
           _resident((SUBLANES, D_MODEL)), _resident((SUBLANES, D_XBC))],
        out_specs=[pl.BlockSpec((tm, w), lambda i: (i, 0)) for w in widths]
        + [tail(D_MODEL), tail(D_XBC)],
        out_shape=[jax.ShapeDtypeStruct((t, w), d) for w, d in zip(widths, dtypes)]
        + [jax.ShapeDtypeStruct((n_seq, SUBLANES, D_MODEL), F32),
           jax.ShapeDtypeStruct((n_seq, SUBLANES, D_XBC), F32)],
        scratch_shapes=[pltpu.VMEM((SUBLANES, D_MODEL), F32), pltpu.VMEM((SUBLANES, D_XBC), F32),
                        pltpu.VMEM((1, 3, tm, COL_BLOCK), F32), pltpu.VMEM((1, 3, tm, COL_BLOCK), F32)],
        compiler_params=pltpu.CompilerParams(
            dimension_semantics=("arbitrary",), vmem_limit_bytes=VMEM_LIMIT),
        name="inproj_conv",
    )(h, nw, *w_packed, p["conv_a_w"], p["ssm_conv_w"], p["ssm_conv_b"], ca0, cx0)


def _outproj_kernel(ya_ref, yb_ref, sga_ref, sgb_ref, h_ref, wa_ref, wb_ref, wo_ref, o_ref):
    y_a = _dot(ya_ref[...].astype(BF16), wa_ref[...])
    y_b = _dot(yb_ref[...].astype(BF16), wb_ref[...])
    merged = sga_ref[...].astype(F32) * y_a + sgb_ref[...].astype(F32) * y_b
    o_ref[...] = h_ref[...] + _dot(merged.astype(BF16), wo_ref[...])


def _outproj(ya, yb, sga, sgb, h, wa, wb, wo, *, tm):
    t = h.shape[0]
    row = lambda w: pl.BlockSpec((tm, w), lambda i: (i, 0))
    return pl.pallas_call(
        _outproj_kernel,
        grid=(t // tm,),
        in_specs=[row(D_MODEL), row(D_SSM), row(D_MODEL), row(D_MODEL), row(D_MODEL),
                  _resident((D_MODEL, D_MODEL)), _resident((D_SSM, D_MODEL)),
                  _resident((D_MODEL, D_MODEL))],
        out_specs=row(D_MODEL),
        out_shape=jax.ShapeDtypeStruct((t, D_MODEL), F32),
        compiler_params=pltpu.CompilerParams(
            dimension_semantics=("parallel",), vmem_limit_bytes=VMEM_LIMIT),
        name="outproj",
    )(ya, yb, sga, sgb, h, wa, wb, wo)


def _gated_group_norm(y, xs, zs, dskip, normw):
    y = (y + dskip * xs) * zs
    gw = D_SSM // SSM_GROUPS
    parts = []
    for g in range(SSM_GROUPS):
        yg = y[:, g * gw:(g + 1) * gw]
        parts.append(yg * lax.rsqrt(jnp.mean(yg * yg, axis=-1, keepdims=True) + EPS))
    return jnp.concatenate(parts, axis=-1) * normw


GROUP_W = HEADS_PER_GROUP * SSM_HEAD_DIM
LOG2E = 1.4426950408889634


def _mixer_kernel(zs_ref, xc_ref, dt_ref, dtb_ref, alog_ref, dskip_ref, normw_ref, expand_ref,
                  h0_ref, yb_ref, hTo_ref, ho_ref, hT, ybuf, *, tb, valid_len):
    i = pl.program_id(1)
    q = CHUNK

    @pl.when(i == 0)
    def _():
        hT[...] = h0_ref[...]

    a_neg = -jnp.exp(alog_ref[...])
    rows = lax.broadcasted_iota(jnp.int32, (q, q), 0)
    cols_i = lax.broadcasted_iota(jnp.int32, (q, q), 1)
    causal = rows >= cols_i
    tri = causal.astype(BF16)
    first_head = lax.broadcasted_iota(jnp.int32, (q, LANES), 1) < SSM_HEAD_DIM

    def chunk_body(c, carry):
        r0 = pl.multiple_of(c * q, q)
        tok = pl.ds(r0, q)
        dt = _softplus(dt_ref[tok, :] + dtb_ref[...])
        if valid_len < tb:
            t_idx = r0 + lax.broadcasted_iota(jnp.int32, (q, LANES), 0)
            dt = jnp.where(t_idx < valid_len, dt, 0.0)
        hi, mid, lo = _split3(dt * a_neg)
        acs3 = _dot(tri, jnp.concatenate([hi, mid, lo], axis=1))
        acs = acs3[:, :LANES] + acs3[:, LANES:2 * LANES] + acs3[:, 2 * LANES:]
        acs2 = acs * LOG2E
        row_t = (acs2 - jnp.log2(dt)).T
        acs_end = acs2[q - 1:q, :]
        cd_hi, cd_lo = _split2(jnp.broadcast_to(jnp.exp2(acs_end), (SUBLANES, LANES)))
        stack = jnp.concatenate(
            [jnp.exp2(acs2).astype(BF16), (jnp.exp2(acs_end - acs2) * dt).astype(BF16),
             cd_hi, cd_lo], axis=0)

        for g in range(SSM_GROUPS):
            gcols = slice(g * GROUP_W, (g + 1) * GROUP_W)
            ex = _dot(stack, expand_ref[:, gcols])
            state_decay = ex[2 * q:2 * q + 1] + ex[2 * q + SUBLANES:2 * q + SUBLANES + 1]
            b_t = xc_ref[tok, D_SSM + g * SSM_STATE:D_SSM + (g + 1) * SSM_STATE].astype(F32).T.astype(BF16)
            c_bf = xc_ref[tok, D_SSM + D_BC + g * SSM_STATE:D_SSM + D_BC + (g + 1) * SSM_STATE]
            cb = _dot(c_bf, b_t)
            h_g = hT[g]
            y_g = ex[0:q] * _dot(c_bf, h_g.astype(BF16))
            xw_g = xc_ref[tok, gcols] * ex[q:2 * q].astype(BF16)
            hT[g] = h_g * state_decay + _dot(b_t, xw_g)
            for jp in range(HEADS_PER_GROUP // 2):
                pair = g * (HEADS_PER_GROUP // 2) + jp
                lanes = slice(pair * LANES, (pair + 1) * LANES)
                x_pair = xc_ref[tok, lanes]
                zero = jnp.zeros_like(x_pair)
                x2 = jnp.concatenate([jnp.where(first_head, x_pair, zero),
                                      jnp.where(first_head, zero, x_pair)], axis=0)
                m2 = []
                for e in range(2):
                    hd = 2 * pair + e
                    acs_col = jnp.broadcast_to(acs2[:, hd:hd + 1], (q, q))
                    decay = jnp.exp2(jnp.where(causal, acs_col - row_t[hd:hd + 1, :], -jnp.inf))
                    m2.append((cb * decay).astype(BF16))
                ybuf[:, lanes] = y_g[:, jp * LANES:(jp + 1) * LANES] + _dot(
                    jnp.concatenate(m2, axis=1), x2)

        yn = _gated_group_norm(ybuf[...], xc_ref[tok, :D_SSM].astype(F32),
                               zs_ref[tok, :].astype(F32), dskip_ref[...], normw_ref[...])
        yb_ref[tok, :] = yn.astype(yb_ref.dtype)
        return carry

    lax.fori_loop(0, tb // q, chunk_body, 0)

    @pl.when(i == pl.num_programs(1) - 1)
    def _():
        hTo_ref[0] = hT[...]
        for g in range(SSM_GROUPS):
            ho_ref[0, g] = hT[g].T


def _mixer(zs, xc, dt, p, expand, h0, *, batch, seqlen, tb, valid_len):
    nt = seqlen // tb
    row = lambda w: pl.BlockSpec((tb, w), lambda b, i: (b * nt + i, 0))
    full = lambda shape: pl.BlockSpec(shape, lambda b, i: (0,) * len(shape))
    per_b = lambda shape: pl.BlockSpec((1,) + shape, lambda b, i: (b,) + (0,) * len(shape))
    return pl.pallas_call(
        functools.partial(_mixer_kernel, tb=tb, valid_len=valid_len),
        grid=(batch, nt),
        in_specs=[row(D_SSM), row(D_XBC), row(LANES),
                  full((1, LANES)), full((1, LANES)), full((1, D_SSM)), full((1, D_SSM)),
                  full((LANES, D_SSM)), full((SSM_GROUPS, SSM_STATE, GROUP_W))],
        out_specs=[row(D_SSM), per_b((SSM_GROUPS, SSM_STATE, GROUP_W)),
                   per_b((SSM_GROUPS, GROUP_W, SSM_STATE))],
        out_shape=[jax.ShapeDtypeStruct((batch * seqlen, D_SSM), BF16),
                   jax.ShapeDtypeStruct((batch, SSM_GROUPS, SSM_STATE, GROUP_W), F32),
                   jax.ShapeDtypeStruct((batch, SSM_GROUPS, GROUP_W, SSM_STATE), F32)],
        scratch_shapes=[pltpu.VMEM((SSM_GROUPS, SSM_STATE, GROUP_W), F32),
                        pltpu.VMEM((CHUNK, D_SSM), F32)],
        compiler_params=pltpu.CompilerParams(
            dimension_semantics=("arbitrary", "arbitrary"), vmem_limit_bytes=VMEM_LIMIT),
        name="mixer",
    )(zs, xc, dt, p["dt_bias"], p["a_log"], p["d_skip"], p["ssm_norm_w"], expand, h0)


SEQ_BLOCK = SUBLANES


def _decode_mixer_kernel(ab_ref, ach_ref, zs_ref, xbc_ref, dt_ref, sa_ref, sx_ref, h0_ref,
                         caw_ref, cxw_ref, cxb_ref, dtb_ref, alog_ref, dskip_ref, normw_ref,
                         gsum_ref, expand_ref,
                         ya_ref, yb_ref, sao_ref, sxo_ref, ho_ref,
                         xc_buf, bc_buf, os_buf, xw_buf, cd_buf, y_buf, *, n_tok):
    sb = SEQ_BLOCK

    xp = [sa_ref[k] for k in range(CONV_A_TAPS - 1)] + [ach_ref[t] for t in range(n_tok)]
    for t in range(n_tok):
        conv = caw_ref[0:1, :] * xp[t]
        for k in range(1, CONV_A_TAPS):
            conv += caw_ref[k:k + 1, :] * xp[t + k]
        ya_ref[t] = ab_ref[t] * conv
    for k in range(CONV_A_TAPS - 1):
        sao_ref[k] = xp[n_tok + k]

    xp = [sx_ref[k] for k in range(CONV_X_TAPS - 1)] + [xbc_ref[t] for t in range(n_tok)]
    for t in range(n_tok):
        conv = cxw_ref[0:1, :] * xp[t]
        for k in range(1, CONV_X_TAPS):
            conv += cxw_ref[k:k + 1, :] * xp[t + k]
        xc_buf[t * sb:(t + 1) * sb, :] = _silu(conv + cxb_ref[...])
    for k in range(CONV_X_TAPS - 1):
        sxo_ref[k] = xp[n_tok + k]

    a_neg = -jnp.exp(alog_ref[...])
    dts, acs = [], []
    for t in range(n_tok):
        dts.append(_softplus(dt_ref[t] + dtb_ref[...]))
        acs.append(dts[t] * a_neg + (acs[t - 1] if t else 0.0))

    pairs = [(qi, si) for qi in range(n_tok) for si in range(qi + 1)]
    prods = [xc_buf[qi * sb:(qi + 1) * sb, D_SSM + D_BC:] * xc_buf[si * sb:(si + 1) * sb, D_SSM:D_SSM + D_BC]
             for qi, si in pairs]
    p_hi, p_lo = _split2(jnp.concatenate(prods, axis=0))
    cbh = _dot(p_hi, gsum_ref[...]) + _dot(p_lo, gsum_ref[...])
    coef = [jnp.exp(acs[t]) for t in range(n_tok)]
    coef += [cbh[k * sb:(k + 1) * sb] * jnp.exp(acs[qi] - acs[si]) * dts[si]
             for k, (qi, si) in enumerate(pairs)]
    coef += [jnp.exp(acs[n_tok - 1] - acs[s]) * dts[s] for s in range(n_tok)]
    coef.append(jnp.exp(acs[n_tok - 1]))
    c_hi, c_lo = _split2(jnp.concatenate(coef, axis=0))
    coef_x = _dot(c_hi, expand_ref[...]) + _dot(c_lo, expand_ref[...])
    slab = lambda k: coef_x[k * sb:(k + 1) * sb]

    def store_tiles(buf, r0, val):
        for lt in range(val.shape[1] // LANES):
            buf[lt, r0:r0 + val.shape[0], :] = val[:, lt * LANES:(lt + 1) * LANES]

    def load_rows(buf, lt0, n_lt, rows):
        return jnp.concatenate([buf[lt0 + j, rows, :] for j in range(n_lt)], axis=1)

    store_tiles(bc_buf, 0, xc_buf[:, D_SSM:])
    store_tiles(os_buf, 0, coef_x[0:n_tok * sb])
    diag = {pr: slab(n_tok + k) for k, pr in enumerate(pairs)}
    k0 = n_tok + len(pairs)
    for s in range(n_tok):
        store_tiles(xw_buf, s * sb, slab(k0 + s) * xc_buf[s * sb:(s + 1) * sb, :D_SSM])
    cd_hi, cd_lo = _split2(slab(k0 + n_tok))
    store_tiles(cd_buf, 0, cd_hi.astype(F32))
    store_tiles(cd_buf, sb, cd_lo.astype(F32))

    for t in range(n_tok):
        acc = diag[(t, 0)] * xc_buf[0:sb, :D_SSM]
        for s in range(1, t + 1):
            acc += diag[(t, s)] * xc_buf[s * sb:(s + 1) * sb, :D_SSM]
        store_tiles(y_buf, t * sb, acc)

    gt = GROUP_W // LANES
    k_rows = 2 * SUBLANES
    krow = lax.broadcasted_iota(jnp.int32, (k_rows, SSM_STATE), 0)
    ones_rows = jnp.where((krow >= n_tok) & (krow < n_tok + 2), 1.0, 0.0).astype(BF16)
    pad_lhs = jnp.zeros((k_rows - n_tok - 2, GROUP_W), F32)
    pad_rhs = jnp.zeros((k_rows - n_tok, SSM_STATE), F32)

    def seq_body(b, carry):
        tok_rows = pl.ds(b, n_tok, stride=sb)
        for g in range(SSM_GROUPS):
            b_bg = bc_buf[g, tok_rows, :]
            c_bg = bc_buf[SSM_GROUPS + g, tok_rows, :]
            heads = pl.ds(g * HEADS_PER_GROUP, HEADS_PER_GROUP)
            h_bg = h0_ref[b, heads].reshape(GROUP_W, SSM_STATE)
            y_off = lax.dot_general(c_bg.astype(BF16), h_bg.astype(BF16),
                                    (((1,), (1,)), ((), ())), preferred_element_type=F32)
            y_off = y_off * load_rows(os_buf, g * gt, gt, tok_rows)
            for j in range(gt):
                y_buf[g * gt + j, tok_rows, :] = (y_buf[g * gt + j, tok_rows, :]
                                                  + y_off[:, j * LANES:(j + 1) * LANES])
            lhs = jnp.concatenate(
                [load_rows(xw_buf, g * gt, gt, tok_rows),
                 load_rows(cd_buf, g * gt, gt, pl.ds(b, 2, stride=sb)), pad_lhs],
                axis=0).astype(BF16)
            rhs = jnp.concatenate(
                [jnp.concatenate([b_bg, pad_rhs], axis=0).astype(BF16), ones_rows], axis=1)
            upd = lax.dot_general(lhs, rhs, (((0,), (0,)), ((), ())), preferred_element_type=F32)
            new = h_bg * upd[:, SSM_STATE:] + upd[:, :SSM_STATE]
            ho_ref[b, heads] = new.reshape(HEADS_PER_GROUP, SSM_HEAD_DIM, SSM_STATE)
        return carry

    lax.fori_loop(0, sb, seq_body, 0, unroll=2)

    xs_all = xc_buf[:, :D_SSM]
    zs_all = jnp.concatenate([zs_ref[t] for t in range(n_tok)], axis=0)
    y_all = load_rows(y_buf, 0, D_SSM // LANES, slice(None))
    yn = _gated_group_norm(y_all, xs_all, zs_all, dskip_ref[...], normw_ref[...])
    for t in range(n_tok):
        yb_ref[t] = yn[t * sb:(t + 1) * sb]


def _decode_mixer(ab, ach, zs, xbc, dt, sa, sx, h0, p, gsum, expand, *, n_tok, n_seq):
    sb = SEQ_BLOCK
    tok = lambda w: pl.BlockSpec((n_tok, sb, w), lambda i: (0, i, 0))
    st = lambda k, w: pl.BlockSpec((k, sb, w), lambda i: (0, i, 0))
    hblk = pl.BlockSpec((sb, SSM_HEADS, SSM_HEAD_DIM, SSM_STATE), lambda i: (i, 0, 0, 0))
    full = lambda shape: pl.BlockSpec(shape, lambda i: (0,) * len(shape))
    return pl.pallas_call(
        functools.partial(_decode_mixer_kernel, n_tok=n_tok),
        grid=(n_seq // sb,),
        in_specs=[tok(D_MODEL), tok(D_MODEL), tok(D_SSM), tok(D_XBC), tok(LANES),
                  st(CONV_A_TAPS - 1, D_MODEL), st(CONV_X_TAPS - 1, D_XBC), hblk,
                  full((CONV_A_TAPS, D_MODEL)), full((CONV_X_TAPS, D_XBC)), full((1, D_XBC)),
                  full((1, LANES)), full((1, LANES)), full((1, D_SSM)), full((1, D_SSM)),
                  full((D_BC, LANES)), full((LANES, D_SSM))],
        out_specs=[tok(D_MODEL), tok(D_SSM), st(CONV_A_TAPS - 1, D_MODEL),
                   st(CONV_X_TAPS - 1, D_XBC), hblk],
        out_shape=[jax.ShapeDtypeStruct((n_tok, n_seq, D_MODEL), F32),
                   jax.ShapeDtypeStruct((n_tok, n_seq, D_SSM), F32),
                   jax.ShapeDtypeStruct((CONV_A_TAPS - 1, n_seq, D_MODEL), F32),
                   jax.ShapeDtypeStruct((CONV_X_TAPS - 1, n_seq, D_XBC), F32),
                   jax.ShapeDtypeStruct((n_seq, SSM_HEADS, SSM_HEAD_DIM, SSM_STATE), F32)],
        scratch_shapes=[pltpu.VMEM((n_tok * sb, D_XBC), F32),
                        pltpu.VMEM((2 * D_BC // LANES, n_tok * sb, LANES), F32),
                        pltpu.VMEM((D_SSM // LANES, n_tok * sb, LANES), F32),
                        pltpu.VMEM((D_SSM // LANES, n_tok * sb, LANES), F32),
                        pltpu.VMEM((D_SSM // LANES, 2 * sb, LANES), F32),
                        pltpu.VMEM((D_SSM // LANES, n_tok * sb, LANES), F32)],
        compiler_params=pltpu.CompilerParams(
            dimension_semantics=("parallel",), vmem_limit_bytes=VMEM_LIMIT),
        name="decode_mixer",
    )(ab, ach, zs, xbc, dt, sa, sx, h0, p["conv_a_w"], p["ssm_conv_w"], p["ssm_conv_b"],
      p["dt_bias"], p["a_log"], p["d_skip"], p["ssm_norm_w"], gsum, expand)


def _row_tile(t, cap):
    if t <= cap:
        return t
    best = LANES
    for cand in range(LANES, cap + 1, LANES):
        if t % cand == 0:
            best = cand
    return best


def kernel(x_prompt, x_sample, state_conv_a, state_ssm_conv, state_ssm, meta_tokens, norm_ffn1, ffn1_w_gu, ffn1_w_down, norm_mix, w_in, conv_a_w, w_a_out, ssm_conv_w, ssm_conv_b, dt_bias, a_log, d_skip, ssm_norm_w, w_b_out, w_o, norm_ffn2, ffn2_w_gu, ffn2_w_down, norm_final):
    bsz, seqlen, _ = x_prompt.shape
    n_seq, n_tok, _ = x_sample.shape
    n_dt = SSM_HEADS
    c_dt = 3 * D_MODEL + D_SSM + D_XBC

    w_gu1, w_dn1 = ffn1_w_gu[0].astype(BF16), ffn1_w_down[0].astype(BF16)
    w_gu2, w_dn2 = ffn2_w_gu[0].astype(BF16), ffn2_w_down[0].astype(BF16)
    w_proj = (jnp.pad(w_in[0, :, :c_dt].astype(BF16), ((0, 0), (0, LANES))),
              jnp.pad(w_in[0, :, c_dt:c_dt + n_dt].astype(BF16), ((0, 0), (0, LANES - n_dt))),
              w_in[0, :, c_dt + n_dt:].astype(BF16))
    wa, wb, wo = w_a_out[0].astype(BF16), w_b_out[0].astype(BF16), w_o[0].astype(BF16)
    pad_heads = lambda v: jnp.pad(v[0], (0, LANES - n_dt))[None]
    p = {
        "conv_a_w": conv_a_w[0], "ssm_conv_w": ssm_conv_w[0], "ssm_conv_b": ssm_conv_b[0][None],
        "dt_bias": pad_heads(dt_bias), "a_log": pad_heads(a_log),
        "d_skip": jnp.repeat(d_skip[0], SSM_HEAD_DIM)[None], "ssm_norm_w": ssm_norm_w[0][None],
    }
    n1, nm, n2, nf = norm_ffn1[0][None], norm_mix[0][None], norm_ffn2[0][None], norm_final[None]
    eh = lax.broadcasted_iota(jnp.int32, (LANES, D_SSM), 0)
    ec = lax.broadcasted_iota(jnp.int32, (LANES, D_SSM), 1)
    expand = (ec // SSM_HEAD_DIM == eh).astype(BF16)

    def ffn1(x):
        return _ffn(x, n1, w_gu1, w_dn1, nf, tm=_row_tile(x.shape[0], 512), final_norm=False)

    def post(ya, yb, sga, sgb, h):
        t = h.shape[0]
        h = _outproj(ya, yb, sga, sgb, h, wa, wb, wo, tm=_row_tile(t, 512))
        return _ffn(h, n2, w_gu2, w_dn2, nf, tm=_row_tile(t, 512), final_norm=True)

    zeros = lambda *s: jnp.zeros(s, F32)
    h_m = ffn1(jnp.pad(meta_tokens, ((0, CHUNK - N_META), (0, 0))))
    _, zs, xc, dt, _, _, ca_m, cx_m = _inproj_conv(
        h_m, nm, w_proj, p, zeros(SUBLANES, D_MODEL), zeros(SUBLANES, D_XBC),
        tm=CHUNK, seq_tiles=1, valid_len=N_META)
    _, hT_m, _ = _mixer(zs, xc, dt, p, expand, zeros(SSM_GROUPS, SSM_STATE, GROUP_W),
                        batch=1, seqlen=CHUNK, tb=CHUNK, valid_len=N_META)

    h_p = ffn1(x_prompt.reshape(bsz * seqlen, D_MODEL))
    tm = _row_tile(seqlen, 512)
    ya, zs, xc, dt, sga, sgb, ca_p, cx_p = _inproj_conv(
        h_p, nm, w_proj, p, ca_m[0], cx_m[0], tm=tm, seq_tiles=seqlen // tm, valid_len=tm)
    tb = _row_tile(seqlen, 512)
    yb, _, h_out_p = _mixer(zs, xc, dt, p, expand, hT_m[0], batch=bsz, seqlen=seqlen, tb=tb,
                            valid_len=tb)
    y_prompt = post(ya, yb, sga, sgb, h_p).reshape(bsz, seqlen, D_MODEL)
    prompt_conv_a = ca_p[None, :, SUBLANES - (CONV_A_TAPS - 1):, :]
    prompt_ssm_conv = cx_p[None, :, SUBLANES - (CONV_X_TAPS - 1):, :]
    prompt_ssm = h_out_p.reshape(1, bsz, SSM_HEADS, SSM_HEAD_DIM, SSM_STATE)

    x_s = jnp.swapaxes(x_sample, 0, 1).reshape(n_tok * n_seq, D_MODEL)
    h_s = ffn1(x_s)
    ab, ach, zs, xbc, dt, sga, sgb = _inproj(h_s, nm, w_proj, tm=_row_tile(n_tok * n_seq, 256), act_dtype=F32)
    tsf = lambda a: a.reshape(n_tok, n_seq, a.shape[-1])
    hh = lax.broadcasted_iota(jnp.int32, (D_BC, LANES), 1)
    kk = lax.broadcasted_iota(jnp.int32, (D_BC, LANES), 0)
    gsum = ((hh < SSM_HEADS) & (kk // SSM_STATE == hh // HEADS_PER_GROUP)).astype(BF16)
    ya, yb, sa_new, sx_new, sample_ssm = _decode_mixer(
        tsf(ab), tsf(ach), tsf(zs), tsf(xbc), tsf(dt),
        jnp.swapaxes(state_conv_a[0], 0, 1), jnp.swapaxes(state_ssm_conv[0], 0, 1), state_ssm[0],
        p, gsum, expand, n_tok=n_tok, n_seq=n_seq)
    flat = lambda a: a.reshape(n_tok * n_seq, a.shape[-1])
    y_s = post(flat(ya), flat(yb), sga, sgb, h_s)
    y_sample = jnp.swapaxes(y_s.reshape(n_tok, n_seq, D_MODEL), 0, 1)
    sample_conv_a = jnp.swapaxes(sa_new, 0, 1)[None]
    sample_ssm_conv = jnp.swapaxes(sx_new, 0, 1)[None]

    return (y_prompt, y_sample, prompt_conv_a, prompt_ssm_conv, prompt_ssm,
            sample_conv_a, sample_ssm_conv, sample_ssm[None])
```

```python
import functools

import jax
import jax.numpy as jnp
from jax import lax
from jax.experimental import pallas as pl
from jax.experimental.pallas import tpu as pltpu

D_MODEL = 1024
D_FF = 2816
D_SSM = 2048
SSM_HEADS = 32
SSM_HEAD_DIM = 64
SSM_GROUPS = 4
HEADS_PER_GROUP = SSM_HEADS // SSM_GROUPS
SSM_STATE = 128
D_BC = SSM_GROUPS * SSM_STATE
D_XBC = D_SSM + 2 * D_BC
CONV_A_TAPS = 3
CONV_X_TAPS = 4
N_META = 16
EPS = 1e-6

LANES = 128
SUBLANES = 8
CHUNK = 128
VMEM_LIMIT = 56 * 1024 * 1024

F32 = jnp.float32
BF16 = jnp.bfloat16


def _dot(a, b):
    return jnp.dot(a, b, preferred_element_type=F32)


def _rms(x, w):
    return x * lax.rsqrt(jnp.mean(x * x, axis=-1, keepdims=True) + EPS) * w


def _silu(x):
    return x * jax.nn.sigmoid(x)


def _softplus(x):
    return jnp.maximum(x, 0.0) + jnp.log1p(jnp.exp(-jnp.abs(x)))


def _split3(x):
    hi = x.astype(BF16)
    r = x - hi.astype(F32)
    mid = r.astype(BF16)
    lo = (r - mid.astype(F32)).astype(BF16)
    return hi, mid, lo


def _split2(x):
    hi = x.astype(BF16)
    lo = (x - hi.astype(F32)).astype(BF16)
    return hi, lo


def _resident(shape):
    return pl.BlockSpec(shape, lambda *_: (0,) * len(shape), pipeline_mode=pl.Buffered(1))


def _ffn_kernel(x_ref, nw_ref, wg_ref, wu_ref, wd_ref, fw_ref, o_ref, *, final_norm):
    x = x_ref[...]
    xn = _rms(x, nw_ref[...]).astype(BF16)
    act = (_silu(_dot(xn, wg_ref[...])) * _dot(xn, wu_ref[...])).astype(BF16)
    h = x + 0.5 * _dot(act, wd_ref[...])
    if final_norm:
        h = _rms(h, fw_ref[...])
    o_ref[...] = h


def _ffn(x, nw, w_gu, w_down, fw, *, tm, final_norm):
    t = x.shape[0]
    once = pl.Buffered(1)
    return pl.pallas_call(
        functools.partial(_ffn_kernel, final_norm=final_norm),
        grid=(t // tm,),
        in_specs=[
            pl.BlockSpec((tm, D_MODEL), lambda i: (i, 0)),
            _resident((1, D_MODEL)),
            pl.BlockSpec((D_MODEL, D_FF), lambda i: (0, 0), pipeline_mode=once),
            pl.BlockSpec((D_MODEL, D_FF), lambda i: (0, 1), pipeline_mode=once),
            _resident((D_FF, D_MODEL)),
            _resident((1, D_MODEL)),
        ],
        out_specs=pl.BlockSpec((tm, D_MODEL), lambda i: (i, 0)),
        out_shape=jax.ShapeDtypeStruct((t, D_MODEL), F32),
        compiler_params=pltpu.CompilerParams(
            dimension_semantics=("parallel",), vmem_limit_bytes=VMEM_LIMIT),
        name="ffn",
    )(x, nw, w_gu, w_gu, w_down, fw)


C_DT = 3 * D_MODEL + D_SSM + D_XBC
C_GATE = C_DT + LANES


def _proj_weight(w_refs, c0, width):
    w_main, w_dt, w_gate = w_refs
    if c0 < C_DT:
        return w_main[:, c0:c0 + width]
    if c0 < C_GATE:
        return w_dt[:, c0 - C_DT:c0 - C_DT + width]
    return w_gate[:, c0 - C_GATE:c0 - C_GATE + width]


def _proj_weight_specs():
    return [_resident((D_MODEL, C_DT + LANES)), _resident((D_MODEL, LANES)), _resident((D_MODEL, 2 * D_MODEL))]


def _inproj_kernel(h_ref, nw_ref, wm_ref, wdt_ref, wg_ref,
                   ab_ref, ach_ref, zs_ref, xbc_ref, dt_ref, sga_ref, sgb_ref):
    u = _rms(h_ref[...], nw_ref[...]).astype(BF16)

    def proj(c0, width):
        return _dot(u, _proj_weight((wm_ref, wdt_ref, wg_ref), c0, width))

    ab_ref[...] = proj(0, D_MODEL).astype(ab_ref.dtype)
    ach_ref[...] = (proj(D_MODEL, D_MODEL) * proj(2 * D_MODEL, D_MODEL)).astype(ach_ref.dtype)
    c0 = 3 * D_MODEL
    for k in range(D_SSM // D_MODEL):
        zs_ref[:, k * D_MODEL:(k + 1) * D_MODEL] = _silu(proj(c0 + k * D_MODEL, D_MODEL)).astype(zs_ref.dtype)
    c0 += D_SSM
    for k in range(D_XBC // D_MODEL):
        xbc_ref[:, k * D_MODEL:(k + 1) * D_MODEL] = proj(c0 + k * D_MODEL, D_MODEL).astype(xbc_ref.dtype)
    c0 += D_XBC
    dt_ref[...] = proj(c0, LANES)
    c0 += LANES
    sga_ref[...] = jax.nn.sigmoid(proj(c0, D_MODEL)).astype(sga_ref.dtype)
    sgb_ref[...] = jax.nn.sigmoid(proj(c0 + D_MODEL, D_MODEL)).astype(sgb_ref.dtype)


def _inproj(h, nw, w_packed, *, tm, act_dtype):
    t = h.shape[0]
    widths = (D_MODEL, D_MODEL, D_SSM, D_XBC, LANES, D_MODEL, D_MODEL)
    dtypes = (act_dtype, act_dtype, act_dtype, act_dtype, F32, act_dtype, act_dtype)
    return pl.pallas_call(
        _inproj_kernel,
        grid=(t // tm,),
        in_specs=[pl.BlockSpec((tm, D_MODEL), lambda i: (i, 0)), _resident((1, D_MODEL))]
        + _proj_weight_specs(),
        out_specs=[pl.BlockSpec((tm, w), lambda i: (i, 0)) for w in widths],
        out_shape=[jax.ShapeDtypeStruct((t, w), d) for w, d in zip(widths, dtypes)],
        compiler_params=pltpu.CompilerParams(
            dimension_semantics=("parallel",), vmem_limit_bytes=VMEM_LIMIT),
        name="inproj",
    )(h, nw, *w_packed)


COL_BLOCK = 512


def _inproj_conv_kernel(h_ref, nw_ref, wm_ref, wdt_ref, wg_ref, caw_ref, cxw_ref, cxb_ref,
                        ca0_ref, cx0_ref,
                        ya_ref, zs_ref, xc_ref, dt_ref, sga_ref, sgb_ref, cao_ref, cxo_ref,
                        ca_carry, cx_carry, *stages, tm, seq_tiles, valid_len):
    i = pl.program_id(0)

    @pl.when(i % seq_tiles == 0)
    def _():
        ca_carry[...] = ca0_ref[...]
        cx_carry[...] = cx0_ref[...]

    u = _rms(h_ref[...], nw_ref[...]).astype(BF16)

    def proj(c0, width=COL_BLOCK):
        return _dot(u, _proj_weight((wm_ref, wdt_ref, wg_ref), c0, width))

    n8 = tm // SUBLANES
    sublane = lax.broadcasted_iota(jnp.int32, (n8, SUBLANES, COL_BLOCK), 1)

    def causal_conv(x, carry_ref, w_ref, taps, cols):
        cur = x.reshape(n8, SUBLANES, COL_BLOCK)
        prev = jnp.concatenate([carry_ref[:, cols], x[:tm - SUBLANES]], axis=0).reshape(cur.shape)
        acc = w_ref[taps - 1:taps, cols] * x
        for d in range(1, taps):
            merged = jnp.where(sublane < SUBLANES - d, cur, prev)
            acc += w_ref[taps - 1 - d:taps - d, cols] * pltpu.roll(merged, d, axis=1).reshape(x.shape)
        carry_ref[:, cols] = x[valid_len - SUBLANES:valid_len]
        return acc

    c_z, c_x, c_dt = 3 * D_MODEL, 3 * D_MODEL + D_SSM, 3 * D_MODEL + D_SSM + D_XBC
    c_g = c_dt + LANES

    def tail_a(k, cols, prj):
        conv = causal_conv(prj[1] * prj[2], ca_carry, caw_ref, CONV_A_TAPS, cols)
        ya_ref[:, cols] = (prj[0] * conv).astype(ya_ref.dtype)

    def tail_z(k, cols, prj):
        zs_ref[:, cols] = _silu(prj[0]).astype(zs_ref.dtype)

    def tail_x(k, cols, prj):
        conv = causal_conv(prj[0], cx_carry, cxw_ref, CONV_X_TAPS, cols)
        xc_ref[:, cols] = _silu(conv + cxb_ref[:, cols]).astype(xc_ref.dtype)

    def tail_g(k, cols, prj):
        sga_ref[:, cols] = jax.nn.sigmoid(prj[0]).astype(sga_ref.dtype)
        sgb_ref[:, cols] = jax.nn.sigmoid(prj[1]).astype(sgb_ref.dtype)

    def blocks(tail, width, starts):
        return [(tail, k, [c + k * COL_BLOCK for c in starts]) for k in range(width // COL_BLOCK)]

    light = blocks(tail_z, D_SSM, [c_z])
    for n, (blk_a, blk_g) in enumerate(zip(blocks(tail_a, D_MODEL, [0, D_MODEL, 2 * D_MODEL]),
                                           blocks(tail_g, D_MODEL, [c_g, c_g + D_MODEL]))):
        light.insert(3 * n, blk_a)
        light.insert(3 * n + 2, blk_g)
    heavy = blocks(tail_x, D_XBC, [c_x])
    order = []
    for n in range(max(len(light), len(heavy))):
        order += heavy[n:n + 1] + light[n:n + 1]
    park = jnp.minimum(i, 0)
    for n, (tail, k, starts) in enumerate(order):
        buf = stages[n % 2]
        for j, c in enumerate(starts):
            buf[park, j] = proj(c)
        tail(k, slice(k * COL_BLOCK, (k + 1) * COL_BLOCK), [buf[park, j] for j in range(len(starts))])
    dt_ref[...] = proj(c_dt, LANES)
    cao_ref[0] = ca_carry[...]
    cxo_ref[0] = cx_carry[...]


def _inproj_conv(h, nw, w_packed, p, ca0, cx0, *, tm, seq_tiles, valid_len):
    t = h.shape[0]
    n_seq = t // (tm * seq_tiles)
    widths = (D_MODEL, D_SSM, D_XBC, LANES, D_MODEL, D_MODEL)
    dtypes = (BF16, BF16, BF16, F32, BF16, BF16)
    tail = lambda w: pl.BlockSpec((1, SUBLANES, w), lambda i: (i // seq_tiles, 0, 0))
    return pl.pallas_call(
        functools.partial(_inproj_conv_kernel, tm=tm, seq_tiles=seq_tiles, valid_len=valid_len),
        grid=(t // tm,),
        in_specs=[pl.BlockSpec((tm, D_MODEL), lambda i: (i, 0)), _resident((1, D_MODEL))]
        + _proj_weight_specs()
        + [_resident((CONV_A_TAPS, D_MODEL)), _resident((CONV_X_TAPS, D_XBC)),
           _resident((1, D_XBC)),
           _resident((SUBLANES, D_MODEL)), _resident((SUBLANES, D_XBC))],
        out_specs=[pl.BlockSpec((tm, w), lambda i: (i, 0)) for w in widths]
        + [tail(D_MODEL), tail(D_XBC)],
        out_shape=[jax.ShapeDtypeStruct((t, w), d) for w, d in zip(widths, dtypes)]
        + [jax.ShapeDtypeStruct((n_seq, SUBLANES, D_MODEL), F32),
           jax.ShapeDtypeStruct((n_seq, SUBLANES, D_XBC), F32)],
        scratch_shapes=[pltpu.VMEM((SUBLANES, D_MODEL), F32), pltpu.VMEM((SUBLANES, D_XBC), F32),
                        pltpu.VMEM((1, 3, tm, COL_BLOCK), F32), pltpu.VMEM((1, 3, tm, COL_BLOCK), F32)],
        compiler_params=pltpu.CompilerParams(
            dimension_semantics=("arbitrary",), vmem_limit_bytes=VMEM_LIMIT),
        name="inproj_conv",
    )(h, nw, *w_packed, p["conv_a_w"], p["ssm_conv_w"], p["ssm_conv_b"], ca0, cx0)


def _outproj_kernel(ya_ref, yb_ref, sga_ref, sgb_ref, h_ref, wa_ref, wb_ref, wo_ref, o_ref):
    y_a = _dot(ya_ref[...].astype(BF16), wa_ref[...])
    y_b = _dot(yb_ref[...].astype(BF16), wb_ref[...])
    merged = sga_ref[...].astype(F32) * y_a + sgb_ref[...].astype(F32) * y_b
    o_ref[...] = h_ref[...] + _dot(merged.astype(BF16), wo_ref[...])


def _outproj(ya, yb, sga, sgb, h, wa, wb, wo, *, tm):
    t = h.shape[0]
    row = lambda w: pl.BlockSpec((tm, w), lambda i: (i, 0))
    return pl.pallas_call(
        _outproj_kernel,
        grid=(t // tm,),
        in_specs=[row(D_MODEL), row(D_SSM), row(D_MODEL), row(D_MODEL), row(D_MODEL),
                  _resident((D_MODEL, D_MODEL)), _resident((D_SSM, D_MODEL)),
                  _resident((D_MODEL, D_MODEL))],
        out_specs=row(D_MODEL),
        out_shape=jax.ShapeDtypeStruct((t, D_MODEL), F32),
        compiler_params=pltpu.CompilerParams(
            dimension_semantics=("parallel",), vmem_limit_bytes=VMEM_LIMIT),
        name="outproj",
    )(ya, yb, sga, sgb, h, wa, wb, wo)


def _gated_group_norm(y, xs, zs, dskip, normw):
    y = (y + dskip * xs) * zs
    gw = D_SSM // SSM_GROUPS
    parts = []
    for g in range(SSM_GROUPS):
        yg = y[:, g * gw:(g + 1) * gw]
        parts.append(yg * lax.rsqrt(jnp.mean(yg * yg, axis=-1, keepdims=True) + EPS))
    return jnp.concatenate(parts, axis=-1) * normw


GROUP_W = HEADS_PER_GROUP * SSM_HEAD_DIM
LOG2E = 1.4426950408889634


def _ssd_masks():
    q = CHUNK
    rows = lax.broadcasted_iota(jnp.int32, (q, q), 0)
    cols = lax.broadcasted_iota(jnp.int32, (q, q), 1)
    causal = rows >= cols
    first_head = lax.broadcasted_iota(jnp.int32, (q, LANES), 1) < SSM_HEAD_DIM
    return causal, causal.astype(BF16), first_head


def _ssd_chunk(tok, r0, masks, zs_ref, xc_ref, dt_ref, dtb_ref, alog_ref, dskip_ref, normw_ref,
               expand_ref, hT, ybuf, valid_len):
    q = CHUNK
    causal, tri, first_head = masks
    a_neg = -jnp.exp(alog_ref[...])
    dt = _softplus(dt_ref[tok, :] + dtb_ref[...])
    if valid_len is not None:
        t_idx = r0 + lax.broadcasted_iota(jnp.int32, (q, LANES), 0)
        dt = jnp.where(t_idx < valid_len, dt, 0.0)
    hi, mid, lo = _split3(dt * a_neg)
    acs3 = _dot(tri, jnp.concatenate([hi, mid, lo], axis=1))
    acs = acs3[:, :LANES] + acs3[:, LANES:2 * LANES] + acs3[:, 2 * LANES:]
    acs2 = acs * LOG2E
    row_t = (acs2 - jnp.log2(dt)).T
    acs_end = acs2[q - 1:q, :]
    cd_hi, cd_lo = _split2(jnp.broadcast_to(jnp.exp2(acs_end), (SUBLANES, LANES)))
    stack = jnp.concatenate(
        [jnp.exp2(acs2).astype(BF16), (jnp.exp2(acs_end - acs2) * dt).astype(BF16),
         cd_hi, cd_lo], axis=0)

    for g in range(SSM_GROUPS):
        gcols = slice(g * GROUP_W, (g + 1) * GROUP_W)
        ex = _dot(stack, expand_ref[:, gcols])
        state_decay = ex[2 * q:2 * q + 1] + ex[2 * q + SUBLANES:2 * q + SUBLANES + 1]
        b_t = xc_ref[tok, D_SSM + g * SSM_STATE:D_SSM + (g + 1) * SSM_STATE].astype(F32).T.astype(BF16)
        c_bf = xc_ref[tok, D_SSM + D_BC + g * SSM_STATE:D_SSM + D_BC + (g + 1) * SSM_STATE]
        cb = _dot(c_bf, b_t)
        h_g = hT[g]
        y_g = ex[0:q] * _dot(c_bf, h_g.astype(BF16))
        xw_g = xc_ref[tok, gcols] * ex[q:2 * q].astype(BF16)
        hT[g] = h_g * state_decay + _dot(b_t, xw_g)
        for jp in range(HEADS_PER_GROUP // 2):
            pair = g * (HEADS_PER_GROUP // 2) + jp
            lanes = slice(pair * LANES, (pair + 1) * LANES)
            x_pair = xc_ref[tok, lanes]
            zero = jnp.zeros_like(x_pair)
            x2 = jnp.concatenate([jnp.where(first_head, x_pair, zero),
                                  jnp.where(first_head, zero, x_pair)], axis=0)
            m2 = []
            for e in range(2):
                hd = 2 * pair + e
                acs_col = jnp.broadcast_to(acs2[:, hd:hd + 1], (q, q))
                decay = jnp.exp2(jnp.where(causal, acs_col - row_t[hd:hd + 1, :], -jnp.inf))
                m2.append((cb * decay).astype(BF16))
            ybuf[:, lanes] = y_g[:, jp * LANES:(jp + 1) * LANES] + _dot(
                jnp.concatenate(m2, axis=1), x2)

    return _gated_group_norm(ybuf[...], xc_ref[tok, :D_SSM].astype(F32),
                             zs_ref[tok, :].astype(F32), dskip_ref[...], normw_ref[...])


def _write_final_state(hT, hTo_ref, ho_ref):
    hTo_ref[0] = hT[...]
    for g in range(SSM_GROUPS):
        ho_ref[0, g] = hT[g].T


def _mixer_kernel(zs_ref, xc_ref, dt_ref, dtb_ref, alog_ref, dskip_ref, normw_ref, expand_ref,
                  h0_ref, yb_ref, hTo_ref, ho_ref, hT, ybuf, *, tb, valid_len):
    i = pl.program_id(1)

    @pl.when(i == 0)
    def _():
        hT[...] = h0_ref[...]

    masks = _ssd_masks()

    def chunk_body(c, carry):
        r0 = pl.multiple_of(c * CHUNK, CHUNK)
        tok = pl.ds(r0, CHUNK)
        yn = _ssd_chunk(tok, r0, masks, zs_ref, xc_ref, dt_ref, dtb_ref, alog_ref, dskip_ref,
                        normw_ref, expand_ref, hT, ybuf, valid_len if valid_len < tb else None)
        yb_ref[tok, :] = yn.astype(yb_ref.dtype)
        return carry

    lax.fori_loop(0, tb // CHUNK, chunk_body, 0)

    @pl.when(i == pl.num_programs(1) - 1)
    def _():
        _write_final_state(hT, hTo_ref, ho_ref)


HALF = 2 * CHUNK


def _mixer_outproj_kernel(zs_ref, xc_ref, dt_ref, ya_ref, sga_ref, sgb_ref, h_ref,
                          dtb_ref, alog_ref, dskip_ref, normw_ref, expand_ref, h0_ref,
                          wa_ref, wb_ref, wo_ref,
                          o_ref, hTo_ref, ho_ref, hT, ybuf, yb_scr, ya_scr, *, tb):
    i = pl.program_id(1)

    @pl.when(i == 0)
    def _():
        hT[...] = h0_ref[...]

    masks = _ssd_masks()
    ya_scr[...] = _dot(ya_ref[...], wa_ref[...])

    def project(rows):
        y_b = _dot(yb_scr[rows, :], wb_ref[...])
        merged = (sga_ref[rows, :].astype(F32) * ya_scr[rows, :]
                  + sgb_ref[rows, :].astype(F32) * y_b)
        o_ref[rows, :] = h_ref[rows, :] + _dot(merged.astype(BF16), wo_ref[...])

    for c in range(tb // CHUNK):
        r0 = c * CHUNK
        tok = slice(r0, r0 + CHUNK)
        yn = _ssd_chunk(tok, r0, masks, zs_ref, xc_ref, dt_ref, dtb_ref, alog_ref, dskip_ref,
                        normw_ref, expand_ref, hT, ybuf, None)
        yb_scr[tok, :] = yn.astype(BF16)
        if (r0 + CHUNK) % HALF == 0:
            project(slice(r0 + CHUNK - HALF, r0 + CHUNK))

    @pl.when(i == pl.num_programs(1) - 1)
    def _():
        _write_final_state(hT, hTo_ref, ho_ref)


def _mixer_outproj(zs, xc, dt, ya, sga, sgb, h, p, expand, h0, wa, wb, wo, *, batch, seqlen, tb):
    nt = seqlen // tb
    row = lambda w: pl.BlockSpec((tb, w), lambda b, i: (b * nt + i, 0))
    full = lambda shape: pl.BlockSpec(shape, lambda b, i: (0,) * len(shape))
    once = lambda shape: pl.BlockSpec(shape, lambda b, i: (0,) * len(shape), pipeline_mode=pl.Buffered(1))
    per_b = lambda shape: pl.BlockSpec((1,) + shape, lambda b, i: (b,) + (0,) * len(shape))
    return pl.pallas_call(
        functools.partial(_mixer_outproj_kernel, tb=tb),
        grid=(batch, nt),
        in_specs=[row(D_SSM), row(D_XBC), row(LANES), row(D_MODEL), row(D_MODEL), row(D_MODEL),
                  row(D_MODEL),
                  full((1, LANES)), full((1, LANES)), full((1, D_SSM)), full((1, D_SSM)),
                  once((LANES, D_SSM)), full((SSM_GROUPS, SSM_STATE, GROUP_W)),
                  once((D_MODEL, D_MODEL)), once((D_SSM, D_MODEL)), once((D_MODEL, D_MODEL))],
        out_specs=[row(D_MODEL), per_b((SSM_GROUPS, SSM_STATE, GROUP_W)),
                   per_b((SSM_GROUPS, GROUP_W, SSM_STATE))],
        out_shape=[jax.ShapeDtypeStruct((batch * seqlen, D_MODEL), F32),
                   jax.ShapeDtypeStruct((batch, SSM_GROUPS, SSM_STATE, GROUP_W), F32),
                   jax.ShapeDtypeStruct((batch, SSM_GROUPS, GROUP_W, SSM_STATE), F32)],
        scratch_shapes=[pltpu.VMEM((SSM_GROUPS, SSM_STATE, GROUP_W), F32),
                        pltpu.VMEM((CHUNK, D_SSM), F32),
                        pltpu.VMEM((tb, D_SSM), BF16),
                        pltpu.VMEM((tb, D_MODEL), F32)],
        compiler_params=pltpu.CompilerParams(
            dimension_semantics=("arbitrary", "arbitrary"), vmem_limit_bytes=VMEM_LIMIT),
        name="mixer_outproj",
    )(zs, xc, dt, ya, sga, sgb, h, p["dt_bias"], p["a_log"], p["d_skip"], p["ssm_norm_w"], expand,
      h0, wa, wb, wo)


def _mixer(zs, xc, dt, p, expand, h0, *, batch, seqlen, tb, valid_len):
    nt = seqlen // tb
    row = lambda w: pl.BlockSpec((tb, w), lambda b, i: (b * nt + i, 0))
    full = lambda shape: pl.BlockSpec(shape, lambda b, i: (0,) * len(shape))
    per_b = lambda shape: pl.BlockSpec((1,) + shape, lambda b, i: (b,) + (0,) * len(shape))
    return pl.pallas_call(
        functools.partial(_mixer_kernel, tb=tb, valid_len=valid_len),
        grid=(batch, nt),
        in_specs=[row(D_SSM), row(D_XBC), row(LANES),
                  full((1, LANES)), full((1, LANES)), full((1, D_SSM)), full((1, D_SSM)),
                  full((LANES, D_SSM)), full((SSM_GROUPS, SSM_STATE, GROUP_W))],
        out_specs=[row(D_SSM), per_b((SSM_GROUPS, SSM_STATE, GROUP_W)),
                   per_b((SSM_GROUPS, GROUP_W, SSM_STATE))],
        out_shape=[jax.ShapeDtypeStruct((batch * seqlen, D_SSM), BF16),
                   jax.ShapeDtypeStruct((batch, SSM_GROUPS, SSM_STATE, GROUP_W), F32),
                   jax.ShapeDtypeStruct((batch, SSM_GROUPS, GROUP_W, SSM_STATE), F32)],
        scratch_shapes=[pltpu.VMEM((SSM_GROUPS, SSM_STATE, GROUP_W), F32),
                        pltpu.VMEM((CHUNK, D_SSM), F32)],
        compiler_params=pltpu.CompilerParams(
            dimension_semantics=("arbitrary", "arbitrary"), vmem_limit_bytes=VMEM_LIMIT),
        name="mixer",
    )(zs, xc, dt, p["dt_bias"], p["a_log"], p["d_skip"], p["ssm_norm_w"], expand, h0)


SEQ_BLOCK = SUBLANES


def _decode_mixer_kernel(ab_ref, ach_ref, zs_ref, xbc_ref, dt_ref, sa_ref, sx_ref, h0_ref,
                         caw_ref, cxw_ref, cxb_ref, dtb_ref, alog_ref, dskip_ref, normw_ref,
                         gsum_ref, expand_ref,
                         ya_ref, yb_ref, sao_ref, sxo_ref, ho_ref,
                         xc_buf, bc_buf, os_buf, xw_buf, cd_buf, y_buf, *, n_tok):
    sb = SEQ_BLOCK

    xp = [sa_ref[k] for k in range(CONV_A_TAPS - 1)] + [ach_ref[t] for t in range(n_tok)]
    for t in range(n_tok):
        conv = caw_ref[0:1, :] * xp[t]
        for k in range(1, CONV_A_TAPS):
            conv += caw_ref[k:k + 1, :] * xp[t + k]
        ya_ref[t] = ab_ref[t] * conv
    for k in range(CONV_A_TAPS - 1):
        sao_ref[k] = xp[n_tok + k]

    xp = [sx_ref[k] for k in range(CONV_X_TAPS - 1)] + [xbc_ref[t] for t in range(n_tok)]
    for t in range(n_tok):
        conv = cxw_ref[0:1, :] * xp[t]
        for k in range(1, CONV_X_TAPS):
            conv += cxw_ref[k:k + 1, :] * xp[t + k]
        xc_buf[t * sb:(t + 1) * sb, :] = _silu(conv + cxb_ref[...])
    for k in range(CONV_X_TAPS - 1):
        sxo_ref[k] = xp[n_tok + k]

    a_neg = -jnp.exp(alog_ref[...])
    dts, acs = [], []
    for t in range(n_tok):
        dts.append(_softplus(dt_ref[t] + dtb_ref[...]))
        acs.append(dts[t] * a_neg + (acs[t - 1] if t else 0.0))

    pairs = [(qi, si) for qi in range(n_tok) for si in range(qi + 1)]
    prods = [xc_buf[qi * sb:(qi + 1) * sb, D_SSM + D_BC:] * xc_buf[si * sb:(si + 1) * sb, D_SSM:D_SSM + D_BC]
             for qi, si in pairs]
    p_hi, p_lo = _split2(jnp.concatenate(prods, axis=0))
    cbh = _dot(p_hi, gsum_ref[...]) + _dot(p_lo, gsum_ref[...])
    coef = [jnp.exp(acs[t]) for t in range(n_tok)]
    coef += [cbh[k * sb:(k + 1) * sb] * jnp.exp(acs[qi] - acs[si]) * dts[si]
             for k, (qi, si) in enumerate(pairs)]
    coef += [jnp.exp(acs[n_tok - 1] - acs[s]) * dts[s] for s in range(n_tok)]
    coef.append(jnp.exp(acs[n_tok - 1]))
    c_hi, c_lo = _split2(jnp.concatenate(coef, axis=0))
    coef_x = _dot(c_hi, expand_ref[...]) + _dot(c_lo, expand_ref[...])
    slab = lambda k: coef_x[k * sb:(k + 1) * sb]

    def store_tiles(buf, r0, val):
        for lt in range(val.shape[1] // LANES):
            buf[lt, r0:r0 + val.shape[0], :] = val[:, lt * LANES:(lt + 1) * LANES]

    def load_rows(buf, lt0, n_lt, rows):
        return jnp.concatenate([buf[lt0 + j, rows, :] for j in range(n_lt)], axis=1)

    store_tiles(bc_buf, 0, xc_buf[:, D_SSM:])
    store_tiles(os_buf, 0, coef_x[0:n_tok * sb])
    diag = {pr: slab(n_tok + k) for k, pr in enumerate(pairs)}
    k0 = n_tok + len(pairs)
    for s in range(n_tok):
        store_tiles(xw_buf, s * sb, slab(k0 + s) * xc_buf[s * sb:(s + 1) * sb, :D_SSM])
    cd_hi, cd_lo = _split2(slab(k0 + n_tok))
    store_tiles(cd_buf, 0, cd_hi.astype(F32))
    store_tiles(cd_buf, sb, cd_lo.astype(F32))

    for t in range(n_tok):
        acc = diag[(t, 0)] * xc_buf[0:sb, :D_SSM]
        for s in range(1, t + 1):
            acc += diag[(t, s)] * xc_buf[s * sb:(s + 1) * sb, :D_SSM]
        store_tiles(y_buf, t * sb, acc)

    gt = GROUP_W // LANES
    k_rows = 2 * SUBLANES
    krow = lax.broadcasted_iota(jnp.int32, (k_rows, SSM_STATE), 0)
    ones_rows = jnp.where((krow >= n_tok) & (krow < n_tok + 2), 1.0, 0.0).astype(BF16)
    pad_lhs = jnp.zeros((k_rows - n_tok - 2, GROUP_W), F32)
    pad_rhs = jnp.zeros((k_rows - n_tok, SSM_STATE), F32)

    def seq_body(b, carry):
        tok_rows = pl.ds(b, n_tok, stride=sb)
        for g in range(SSM_GROUPS):
            b_bg = bc_buf[g, tok_rows, :]
            c_bg = bc_buf[SSM_GROUPS + g, tok_rows, :]
            heads = pl.ds(g * HEADS_PER_GROUP, HEADS_PER_GROUP)
            h_bg = h0_ref[b, heads].reshape(GROUP_W, SSM_STATE)
            y_off = lax.dot_general(c_bg.astype(BF16), h_bg.astype(BF16),
                                    (((1,), (1,)), ((), ())), preferred_element_type=F32)
            y_off = y_off * load_rows(os_buf, g * gt, gt, tok_rows)
            for j in range(gt):
                y_buf[g * gt + j, tok_rows, :] = (y_buf[g * gt + j, tok_rows, :]
                                                  + y_off[:, j * LANES:(j + 1) * LANES])
            lhs = jnp.concatenate(
                [load_rows(xw_buf, g * gt, gt, tok_rows),
                 load_rows(cd_buf, g * gt, gt, pl.ds(b, 2, stride=sb)), pad_lhs],
                axis=0).astype(BF16)
            rhs = jnp.concatenate(
                [jnp.concatenate([b_bg, pad_rhs], axis=0).astype(BF16), ones_rows], axis=1)
            upd = lax.dot_general(lhs, rhs, (((0,), (0,)), ((), ())), preferred_element_type=F32)
            new = h_bg * upd[:, SSM_STATE:] + upd[:, :SSM_STATE]
            ho_ref[b, heads] = new.reshape(HEADS_PER_GROUP, SSM_HEAD_DIM, SSM_STATE)
        return carry

    lax.fori_loop(0, sb, seq_body, 0, unroll=2)

    xs_all = xc_buf[:, :D_SSM]
    zs_all = jnp.concatenate([zs_ref[t] for t in range(n_tok)], axis=0)
    y_all = load_rows(y_buf, 0, D_SSM // LANES, slice(None))
    yn = _gated_group_norm(y_all, xs_all, zs_all, dskip_ref[...], normw_ref[...])
    for t in range(n_tok):
        yb_ref[t] = yn[t * sb:(t + 1) * sb]


def _decode_mixer(ab, ach, zs, xbc, dt, sa, sx, h0, p, gsum, expand, *, n_tok, n_seq):
    sb = SEQ_BLOCK
    tok = lambda w: pl.BlockSpec((n_tok, sb, w), lambda i: (0, i, 0))
    st = lambda k, w: pl.BlockSpec((k, sb, w), lambda i: (0, i, 0))
    hblk = pl.BlockSpec((sb, SSM_HEADS, SSM_HEAD_DIM, SSM_STATE), lambda i: (i, 0, 0, 0))
    full = lambda shape: pl.BlockSpec(shape, lambda i: (0,) * len(shape))
    return pl.pallas_call(
        functools.partial(_decode_mixer_kernel, n_tok=n_tok),
        grid=(n_seq // sb,),
        in_specs=[tok(D_MODEL), tok(D_MODEL), tok(D_SSM), tok(D_XBC), tok(LANES),
                  st(CONV_A_TAPS - 1, D_MODEL), st(CONV_X_TAPS - 1, D_XBC), hblk,
                  full((CONV_A_TAPS, D_MODEL)), full((CONV_X_TAPS, D_XBC)), full((1, D_XBC)),
                  full((1, LANES)), full((1, LANES)), full((1, D_SSM)), full((1, D_SSM)),
                  full((D_BC, LANES)), full((LANES, D_SSM))],
        out_specs=[tok(D_MODEL), tok(D_SSM), st(CONV_A_TAPS - 1, D_MODEL),
                   st(CONV_X_TAPS - 1, D_XBC), hblk],
        out_shape=[jax.ShapeDtypeStruct((n_tok, n_seq, D_MODEL), F32),
                   jax.ShapeDtypeStruct((n_tok, n_seq, D_SSM), F32),
                   jax.ShapeDtypeStruct((CONV_A_TAPS - 1, n_seq, D_MODEL), F32),
                   jax.ShapeDtypeStruct((CONV_X_TAPS - 1, n_seq, D_XBC), F32),
                   jax.ShapeDtypeStruct((n_seq, SSM_HEADS, SSM_HEAD_DIM, SSM_STATE), F32)],
        scratch_shapes=[pltpu.VMEM((n_tok * sb, D_XBC), F32),
                        pltpu.VMEM((2 * D_BC // LANES, n_tok * sb, LANES), F32),
                        pltpu.VMEM((D_SSM // LANES, n_tok * sb, LANES), F32),
                        pltpu.VMEM((D_SSM // LANES, n_tok * sb, LANES), F32),
                        pltpu.VMEM((D_SSM // LANES, 2 * sb, LANES), F32),
                        pltpu.VMEM((D_SSM // LANES, n_tok * sb, LANES), F32)],
        compiler_params=pltpu.CompilerParams(
            dimension_semantics=("parallel",), vmem_limit_bytes=VMEM_LIMIT),
        name="decode_mixer",
    )(ab, ach, zs, xbc, dt, sa, sx, h0, p["conv_a_w"], p["ssm_conv_w"], p["ssm_conv_b"],
      p["dt_bias"], p["a_log"], p["d_skip"], p["ssm_norm_w"], gsum, expand)


def _row_tile(t, cap):
    if t <= cap:
        return t
    best = LANES
    for cand in range(LANES, cap + 1, LANES):
        if t % cand == 0:
            best = cand
    return best


def kernel(x_prompt, x_sample, state_conv_a, state_ssm_conv, state_ssm, meta_tokens, norm_ffn1, ffn1_w_gu, ffn1_w_down, norm_mix, w_in, conv_a_w, w_a_out, ssm_conv_w, ssm_conv_b, dt_bias, a_log, d_skip, ssm_norm_w, w_b_out, w_o, norm_ffn2, ffn2_w_gu, ffn2_w_down, norm_final):
    bsz, seqlen, _ = x_prompt.shape
    n_seq, n_tok, _ = x_sample.shape
    n_dt = SSM_HEADS
    c_dt = 3 * D_MODEL + D_SSM + D_XBC

    w_gu1, w_dn1 = ffn1_w_gu[0].astype(BF16), ffn1_w_down[0].astype(BF16)
    w_gu2, w_dn2 = ffn2_w_gu[0].astype(BF16), ffn2_w_down[0].astype(BF16)
    w_proj = (jnp.pad(w_in[0, :, :c_dt].astype(BF16), ((0, 0), (0, LANES))),
              jnp.pad(w_in[0, :, c_dt:c_dt + n_dt].astype(BF16), ((0, 0), (0, LANES - n_dt))),
              w_in[0, :, c_dt + n_dt:].astype(BF16))
    wa, wb, wo = w_a_out[0].astype(BF16), w_b_out[0].astype(BF16), w_o[0].astype(BF16)
    pad_heads = lambda v: jnp.pad(v[0], (0, LANES - n_dt))[None]
    p = {
        "conv_a_w": conv_a_w[0], "ssm_conv_w": ssm_conv_w[0], "ssm_conv_b": ssm_conv_b[0][None],
        "dt_bias": pad_heads(dt_bias), "a_log": pad_heads(a_log),
        "d_skip": jnp.repeat(d_skip[0], SSM_HEAD_DIM)[None], "ssm_norm_w": ssm_norm_w[0][None],
    }
    n1, nm, n2, nf = norm_ffn1[0][None], norm_mix[0][None], norm_ffn2[0][None], norm_final[None]
    eh = lax.broadcasted_iota(jnp.int32, (LANES, D_SSM), 0)
    ec = lax.broadcasted_iota(jnp.int32, (LANES, D_SSM), 1)
    expand = (ec // SSM_HEAD_DIM == eh).astype(BF16)

    def ffn1(x):
        return _ffn(x, n1, w_gu1, w_dn1, nf, tm=_row_tile(x.shape[0], 512), final_norm=False)

    def ffn2(h):
        return _ffn(h, n2, w_gu2, w_dn2, nf, tm=_row_tile(h.shape[0], 512), final_norm=True)

    zeros = lambda *s: jnp.zeros(s, F32)
    h_m = ffn1(jnp.pad(meta_tokens, ((0, CHUNK - N_META), (0, 0))))
    _, zs, xc, dt, _, _, ca_m, cx_m = _inproj_conv(
        h_m, nm, w_proj, p, zeros(SUBLANES, D_MODEL), zeros(SUBLANES, D_XBC),
        tm=CHUNK, seq_tiles=1, valid_len=N_META)
    _, hT_m, _ = _mixer(zs, xc, dt, p, expand, zeros(SSM_GROUPS, SSM_STATE, GROUP_W),
                        batch=1, seqlen=CHUNK, tb=CHUNK, valid_len=N_META)

    h_p = ffn1(x_prompt.reshape(bsz * seqlen, D_MODEL))
    tm = _row_tile(seqlen, 512)
    ya, zs, xc, dt, sga, sgb, ca_p, cx_p = _inproj_conv(
        h_p, nm, w_proj, p, ca_m[0], cx_m[0], tm=tm, seq_tiles=seqlen // tm, valid_len=tm)
    tb = _row_tile(seqlen, 512)
    h2_p, _, h_out_p = _mixer_outproj(zs, xc, dt, ya, sga, sgb, h_p, p, expand, hT_m[0], wa, wb, wo,
                                      batch=bsz, seqlen=seqlen, tb=tb)
    y_prompt = ffn2(h2_p).reshape(bsz, seqlen, D_MODEL)
    prompt_conv_a = ca_p[None, :, SUBLANES - (CONV_A_TAPS - 1):, :]
    prompt_ssm_conv = cx_p[None, :, SUBLANES - (CONV_X_TAPS - 1):, :]
    prompt_ssm = h_out_p.reshape(1, bsz, SSM_HEADS, SSM_HEAD_DIM, SSM_STATE)

    x_s = jnp.swapaxes(x_sample, 0, 1).reshape(n_tok * n_seq, D_MODEL)
    h_s = ffn1(x_s)
    ab, ach, zs, xbc, dt, sga, sgb = _inproj(h_s, nm, w_proj, tm=_row_tile(n_tok * n_seq, 256), act_dtype=F32)
    tsf = lambda a: a.reshape(n_tok, n_seq, a.shape[-1])
    hh = lax.broadcasted_iota(jnp.int32, (D_BC, LANES), 1)
    kk = lax.broadcasted_iota(jnp.int32, (D_BC, LANES), 0)
    gsum = ((hh < SSM_HEADS) & (kk // SSM_STATE == hh // HEADS_PER_GROUP)).astype(BF16)
    ya, yb, sa_new, sx_new, sample_ssm = _decode_mixer(
        tsf(ab), tsf(ach), tsf(zs), tsf(xbc), tsf(dt),
        jnp.swapaxes(state_conv_a[0], 0, 1), jnp.swapaxes(state_ssm_conv[0], 0, 1), state_ssm[0],
        p, gsum, expand, n_tok=n_tok, n_seq=n_seq)
    flat = lambda a: a.reshape(n_tok * n_seq, a.shape[-1])
    y_s = ffn2(_outproj(flat(ya), flat(yb), sga, sgb, h_s, wa, wb, wo, tm=_row_tile(n_tok * n_seq, 512)))
    y_sample = jnp.swapaxes(y_s.reshape(n_tok, n_seq, D_MODEL), 0, 1)
    sample_conv_a = jnp.swapaxes(sa_new, 0, 1)[None]
    sample_ssm_conv = jnp.swapaxes(sx_new, 0, 1)[None]

    return (y_prompt, y_sample, prompt_conv_a, prompt_ssm_conv, prompt_ssm,
            sample_conv_a, sample_ssm_conv, sample_ssm[None])
```

```python
import functools

import jax
import jax.numpy as jnp
from jax import lax
from jax.experimental import pallas as pl
from jax.experimental.pallas import tpu as pltpu

D_MODEL = 1024
D_FF = 2816
D_SSM = 2048
SSM_HEADS = 32
SSM_HEAD_DIM = 64
SSM_GROUPS = 4
HEADS_PER_GROUP = SSM_HEADS // SSM_GROUPS
SSM_STATE = 128
D_BC = SSM_GROUPS * SSM_STATE
D_XBC = D_SSM + 2 * D_BC
CONV_A_TAPS = 3
CONV_X_TAPS = 4
N_META = 16
EPS = 1e-6

LANES = 128
SUBLANES = 8
CHUNK = 128
VMEM_LIMIT = 56 * 1024 * 1024

F32 = jnp.float32
BF16 = jnp.bfloat16


def _dot(a, b):
    return jnp.dot(a, b, preferred_element_type=F32)


def _rms(x, w):
    return x * lax.rsqrt(jnp.mean(x * x, axis=-1, keepdims=True) + EPS) * w


def _silu(x):
    return x * jax.nn.sigmoid(x)


def _softplus(x):
    return jnp.maximum(x, 0.0) + jnp.log1p(jnp.exp(-jnp.abs(x)))


def _split3(x):
    hi = x.astype(BF16)
    r = x - hi.astype(F32)
    mid = r.astype(BF16)
    lo = (r - mid.astype(F32)).astype(BF16)
    return hi, mid, lo


def _split2(x):
    hi = x.astype(BF16)
    lo = (x - hi.astype(F32)).astype(BF16)
    return hi, lo


def _resident(shape):
    return pl.BlockSpec(shape, lambda *_: (0,) * len(shape), pipeline_mode=pl.Buffered(1))


def _ffn_kernel(x_ref, nw_ref, wg_ref, wu_ref, wd_ref, fw_ref, o_ref, *, final_norm):
    x = x_ref[...]
    xn = _rms(x, nw_ref[...]).astype(BF16)
    act = (_silu(_dot(xn, wg_ref[...])) * _dot(xn, wu_ref[...])).astype(BF16)
    h = x + 0.5 * _dot(act, wd_ref[...])
    if final_norm:
        h = _rms(h, fw_ref[...])
    o_ref[...] = h


def _ffn(x, nw, w_gu, w_down, fw, *, tm, final_norm):
    t = x.shape[0]
    once = pl.Buffered(1)
    return pl.pallas_call(
        functools.partial(_ffn_kernel, final_norm=final_norm),
        grid=(t // tm,),
        in_specs=[
            pl.BlockSpec((tm, D_MODEL), lambda i: (i, 0)),
            _resident((1, D_MODEL)),
            pl.BlockSpec((D_MODEL, D_FF), lambda i: (0, 0), pipeline_mode=once),
            pl.BlockSpec((D_MODEL, D_FF), lambda i: (0, 1), pipeline_mode=once),
            _resident((D_FF, D_MODEL)),
            _resident((1, D_MODEL)),
        ],
        out_specs=pl.BlockSpec((tm, D_MODEL), lambda i: (i, 0)),
        out_shape=jax.ShapeDtypeStruct((t, D_MODEL), F32),
        compiler_params=pltpu.CompilerParams(
            dimension_semantics=("parallel",), vmem_limit_bytes=VMEM_LIMIT),
        name="ffn",
    )(x, nw, w_gu, w_gu, w_down, fw)


C_DT = 3 * D_MODEL + D_SSM + D_XBC
C_GATE = C_DT + LANES


def _proj_weight(w_refs, c0, width):
    w_main, w_dt, w_gate = w_refs
    if c0 < C_DT:
        return w_main[:, c0:c0 + width]
    if c0 < C_GATE:
        return w_dt[:, c0 - C_DT:c0 - C_DT + width]
    return w_gate[:, c0 - C_GATE:c0 - C_GATE + width]


def _proj_weight_specs():
    return [_resident((D_MODEL, C_DT + LANES)), _resident((D_MODEL, LANES)), _resident((D_MODEL, 2 * D_MODEL))]


def _stage_w_in_kernel(w_ref, wm_ref, wdt_ref):
    w = w_ref[0]
    wm_ref[:, :C_DT] = w[:, :C_DT].astype(BF16)
    wm_ref[:, C_DT:] = jnp.zeros((w.shape[0], LANES), BF16)
    lane = lax.broadcasted_iota(jnp.int32, (w.shape[0], LANES), 1)
    wdt_ref[...] = jnp.where(lane < SSM_HEADS, w[:, C_DT:], 0.0).astype(BF16)


def _stage_w_in(w_in):
    rows = LANES
    return pl.pallas_call(
        _stage_w_in_kernel,
        grid=(D_MODEL // rows,),
        in_specs=[pl.BlockSpec((1, rows, C_DT + LANES), lambda i: (0, i, 0))],
        out_specs=[pl.BlockSpec((rows, C_DT + LANES), lambda i: (i, 0)),
                   pl.BlockSpec((rows, LANES), lambda i: (i, 0))],
        out_shape=[jax.ShapeDtypeStruct((D_MODEL, C_DT + LANES), BF16),
                   jax.ShapeDtypeStruct((D_MODEL, LANES), BF16)],
        compiler_params=pltpu.CompilerParams(
            dimension_semantics=("parallel",), vmem_limit_bytes=VMEM_LIMIT),
        name="stage_w_in",
    )(w_in)


def _inproj_kernel(h_ref, nw_ref, wm_ref, wdt_ref, wg_ref,
                   ab_ref, ach_ref, zs_ref, xbc_ref, dt_ref, sga_ref, sgb_ref):
    u = _rms(h_ref[...], nw_ref[...]).astype(BF16)

    def proj(c0, width):
        return _dot(u, _proj_weight((wm_ref, wdt_ref, wg_ref), c0, width))

    ab_ref[...] = proj(0, D_MODEL).astype(ab_ref.dtype)
    ach_ref[...] = (proj(D_MODEL, D_MODEL) * proj(2 * D_MODEL, D_MODEL)).astype(ach_ref.dtype)
    c0 = 3 * D_MODEL
    for k in range(D_SSM // D_MODEL):
        zs_ref[:, k * D_MODEL:(k + 1) * D_MODEL] = _silu(proj(c0 + k * D_MODEL, D_MODEL)).astype(zs_ref.dtype)
    c0 += D_SSM
    for k in range(D_XBC // D_MODEL):
        xbc_ref[:, k * D_MODEL:(k + 1) * D_MODEL] = proj(c0 + k * D_MODEL, D_MODEL).astype(xbc_ref.dtype)
    c0 += D_XBC
    dt_ref[...] = proj(c0, LANES)
    c0 += LANES
    sga_ref[...] = jax.nn.sigmoid(proj(c0, D_MODEL)).astype(sga_ref.dtype)
    sgb_ref[...] = jax.nn.sigmoid(proj(c0 + D_MODEL, D_MODEL)).astype(sgb_ref.dtype)


def _inproj(h, nw, w_packed, *, tm, act_dtype):
    t = h.shape[0]
    widths = (D_MODEL, D_MODEL, D_SSM, D_XBC, LANES, D_MODEL, D_MODEL)
    dtypes = (act_dtype, act_dtype, act_dtype, act_dtype, F32, act_dtype, act_dtype)
    return pl.pallas_call(
        _inproj_kernel,
        grid=(t // tm,),
        in_specs=[pl.BlockSpec((tm, D_MODEL), lambda i: (i, 0)), _resident((1, D_MODEL))]
        + _proj_weight_specs(),
        out_specs=[pl.BlockSpec((tm, w), lambda i: (i, 0)) for w in widths],
        out_shape=[jax.ShapeDtypeStruct((t, w), d) for w, d in zip(widths, dtypes)],
        compiler_params=pltpu.CompilerParams(
            dimension_semantics=("parallel",), vmem_limit_bytes=VMEM_LIMIT),
        name="inproj",
    )(h, nw, *w_packed)


COL_BLOCK = 512


def _inproj_conv_kernel(h_ref, nw_ref, wm_ref, wdt_ref, wg_ref, caw_ref, cxw_ref, cxb_ref,
                        ca0_ref, cx0_ref,
                        ya_ref, zs_ref, xc_ref, dt_ref, sga_ref, sgb_ref, cao_ref, cxo_ref,
                        ca_carry, cx_carry, *stages, tm, seq_tiles, valid_len):
    i = pl.program_id(0)

    @pl.when(i % seq_tiles == 0)
    def _():
        ca_carry[...] = ca0_ref[...]
        cx_carry[...] = cx0_ref[...]

    u = _rms(h_ref[...], nw_ref[...]).astype(BF16)

    def proj(c0, width=COL_BLOCK):
        return _dot(u, _proj_weight((wm_ref, wdt_ref, wg_ref), c0, width))

    n8 = tm // SUBLANES
    sublane = lax.broadcasted_iota(jnp.int32, (n8, SUBLANES, COL_BLOCK), 1)

    def causal_conv(x, carry_ref, w_ref, taps, cols):
        cur = x.reshape(n8, SUBLANES, COL_BLOCK)
        prev = jnp.concatenate([carry_ref[:, cols], x[:tm - SUBLANES]], axis=0).reshape(cur.shape)
        acc = w_ref[taps - 1:taps, cols] * x
        for d in range(1, taps):
            merged = jnp.where(sublane < SUBLANES - d, cur, prev)
            acc += w_ref[taps - 1 - d:taps - d, cols] * pltpu.roll(merged, d, axis=1).reshape(x.shape)
        carry_ref[:, cols] = x[valid_len - SUBLANES:valid_len]
        return acc

    c_z, c_x, c_dt = 3 * D_MODEL, 3 * D_MODEL + D_SSM, 3 * D_MODEL + D_SSM + D_XBC
    c_g = c_dt + LANES

    def tail_a(k, cols, prj):
        conv = causal_conv(prj[1] * prj[2], ca_carry, caw_ref, CONV_A_TAPS, cols)
        ya_ref[:, cols] = (prj[0] * conv).astype(ya_ref.dtype)

    def tail_z(k, cols, prj):
        zs_ref[:, cols] = _silu(prj[0]).astype(zs_ref.dtype)

    def tail_x(k, cols, prj):
        conv = causal_conv(prj[0], cx_carry, cxw_ref, CONV_X_TAPS, cols)
        xc_ref[:, cols] = _silu(conv + cxb_ref[:, cols]).astype(xc_ref.dtype)

    def tail_g(k, cols, prj):
        sga_ref[:, cols] = jax.nn.sigmoid(prj[0]).astype(sga_ref.dtype)
        sgb_ref[:, cols] = jax.nn.sigmoid(prj[1]).astype(sgb_ref.dtype)

    def blocks(tail, width, starts):
        return [(tail, k, [c + k * COL_BLOCK for c in starts]) for k in range(width // COL_BLOCK)]

    light = blocks(tail_z, D_SSM, [c_z])
    for n, (blk_a, blk_g) in enumerate(zip(blocks(tail_a, D_MODEL, [0, D_MODEL, 2 * D_MODEL]),
                                           blocks(tail_g, D_MODEL, [c_g, c_g + D_MODEL]))):
        light.insert(3 * n, blk_a)
        light.insert(3 * n + 2, blk_g)
    heavy = blocks(tail_x, D_XBC, [c_x])
    order = []
    for n in range(max(len(light), len(heavy))):
        order += heavy[n:n + 1] + light[n:n + 1]
    park = jnp.minimum(i, 0)
    for n, (tail, k, starts) in enumerate(order):
        buf = stages[n % 2]
        for j, c in enumerate(starts):
            buf[park, j] = proj(c)
        tail(k, slice(k * COL_BLOCK, (k + 1) * COL_BLOCK), [buf[park, j] for j in range(len(starts))])
    dt_ref[...] = proj(c_dt, LANES)
    cao_ref[0] = ca_carry[...]
    cxo_ref[0] = cx_carry[...]


def _inproj_conv(h, nw, w_packed, p, ca0, cx0, *, tm, seq_tiles, valid_len):
    t = h.shape[0]
    n_seq = t // (tm * seq_tiles)
    widths = (D_MODEL, D_SSM, D_XBC, LANES, D_MODEL, D_MODEL)
    dtypes = (BF16, BF16, BF16, F32, BF16, BF16)
    tail = lambda w: pl.BlockSpec((1, SUBLANES, w), lambda i: (i // seq_tiles, 0, 0))
    return pl.pallas_call(
        functools.partial(_inproj_conv_kernel, tm=tm, seq_tiles=seq_tiles, valid_len=valid_len),
        grid=(t // tm,),
        in_specs=[pl.BlockSpec((tm, D_MODEL), lambda i: (i, 0)), _resident((1, D_MODEL))]
        + _proj_weight_specs()
        + [_resident((CONV_A_TAPS, D_MODEL)), _resident((CONV_X_TAPS, D_XBC)),
           _resident((1, D_XBC)),
           _resident((SUBLANES, D_MODEL)), _resident((SUBLANES, D_XBC))],
        out_specs=[pl.BlockSpec((tm, w), lambda i: (i, 0)) for w in widths]
        + [tail(D_MODEL), tail(D_XBC)],
        out_shape=[jax.ShapeDtypeStruct((t, w), d) for w, d in zip(widths, dtypes)]
        + [jax.ShapeDtypeStruct((n_seq, SUBLANES, D_MODEL), F32),
           jax.ShapeDtypeStruct((n_seq, SUBLANES, D_XBC), F32)],
        scratch_shapes=[pltpu.VMEM((SUBLANES, D_MODEL), F32), pltpu.VMEM((SUBLANES, D_XBC), F32),
                        pltpu.VMEM((1, 3, tm, COL_BLOCK), F32), pltpu.VMEM((1, 3, tm, COL_BLOCK), F32)],
        compiler_params=pltpu.CompilerParams(
            dimension_semantics=("arbitrary",), vmem_limit_bytes=VMEM_LIMIT),
        name="inproj_conv",
    )(h, nw, *w_packed, p["conv_a_w"], p["ssm_conv_w"], p["ssm_conv_b"], ca0, cx0)


def _outproj_kernel(ya_ref, yb_ref, sga_ref, sgb_ref, h_ref, wa_ref, wb_ref, wo_ref, o_ref):
    y_a = _dot(ya_ref[...].astype(BF16), wa_ref[...])
    y_b = _dot(yb_ref[...].astype(BF16), wb_ref[...])
    merged = sga_ref[...].astype(F32) * y_a + sgb_ref[...].astype(F32) * y_b
    o_ref[...] = h_ref[...] + _dot(merged.astype(BF16), wo_ref[...])


def _outproj(ya, yb, sga, sgb, h, wa, wb, wo, *, tm):
    t = h.shape[0]
    row = lambda w: pl.BlockSpec((tm, w), lambda i: (i, 0))
    return pl.pallas_call(
        _outproj_kernel,
        grid=(t // tm,),
        in_specs=[row(D_MODEL), row(D_SSM), row(D_MODEL), row(D_MODEL), row(D_MODEL),
                  _resident((D_MODEL, D_MODEL)), _resident((D_SSM, D_MODEL)),
                  _resident((D_MODEL, D_MODEL))],
        out_specs=row(D_MODEL),
        out_shape=jax.ShapeDtypeStruct((t, D_MODEL), F32),
        compiler_params=pltpu.CompilerParams(
            dimension_semantics=("parallel",), vmem_limit_bytes=VMEM_LIMIT),
        name="outproj",
    )(ya, yb, sga, sgb, h, wa, wb, wo)


def _gated_group_norm(y, xs, zs, dskip, normw):
    y = (y + dskip * xs) * zs
    gw = D_SSM // SSM_GROUPS
    parts = []
    for g in range(SSM_GROUPS):
        yg = y[:, g * gw:(g + 1) * gw]
        parts.append(yg * lax.rsqrt(jnp.mean(yg * yg, axis=-1, keepdims=True) + EPS))
    return jnp.concatenate(parts, axis=-1) * normw


GROUP_W = HEADS_PER_GROUP * SSM_HEAD_DIM
LOG2E = 1.4426950408889634


def _ssd_masks():
    q = CHUNK
    rows = lax.broadcasted_iota(jnp.int32, (q, q), 0)
    cols = lax.broadcasted_iota(jnp.int32, (q, q), 1)
    causal = rows >= cols
    first_head = lax.broadcasted_iota(jnp.int32, (q, LANES), 1) < SSM_HEAD_DIM
    return causal, causal.astype(BF16), first_head


def _ssd_chunk(tok, r0, masks, zs_ref, xc_ref, dt_ref, dtb_ref, alog_ref, dskip_ref, normw_ref,
               expand_ref, hT, ybuf, valid_len):
    q = CHUNK
    causal, tri, first_head = masks
    a_neg = -jnp.exp(alog_ref[...])
    dt = _softplus(dt_ref[tok, :] + dtb_ref[...])
    if valid_len is not None:
        t_idx = r0 + lax.broadcasted_iota(jnp.int32, (q, LANES), 0)
        dt = jnp.where(t_idx < valid_len, dt, 0.0)
    hi, mid, lo = _split3(dt * a_neg)
    acs3 = _dot(tri, jnp.concatenate([hi, mid, lo], axis=1))
    acs = acs3[:, :LANES] + acs3[:, LANES:2 * LANES] + acs3[:, 2 * LANES:]
    acs2 = acs * LOG2E
    row_t = (acs2 - jnp.log2(dt)).T
    acs_end = acs2[q - 1:q, :]
    cd_hi, cd_lo = _split2(jnp.broadcast_to(jnp.exp2(acs_end), (SUBLANES, LANES)))
    stack = jnp.concatenate(
        [jnp.exp2(acs2).astype(BF16), (jnp.exp2(acs_end - acs2) * dt).astype(BF16),
         cd_hi, cd_lo], axis=0)

    for g in range(SSM_GROUPS):
        gcols = slice(g * GROUP_W, (g + 1) * GROUP_W)
        ex = _dot(stack, expand_ref[:, gcols])
        state_decay = ex[2 * q:2 * q + 1] + ex[2 * q + SUBLANES:2 * q + SUBLANES + 1]
        b_t = xc_ref[tok, D_SSM + g * SSM_STATE:D_SSM + (g + 1) * SSM_STATE].astype(F32).T.astype(BF16)
        c_bf = xc_ref[tok, D_SSM + D_BC + g * SSM_STATE:D_SSM + D_BC + (g + 1) * SSM_STATE]
        cb = _dot(c_bf, b_t)
        h_g = hT[g]
        y_g = ex[0:q] * _dot(c_bf, h_g.astype(BF16))
        xw_g = xc_ref[tok, gcols] * ex[q:2 * q].astype(BF16)
        hT[g] = h_g * state_decay + _dot(b_t, xw_g)
        for jp in range(HEADS_PER_GROUP // 2):
            pair = g * (HEADS_PER_GROUP // 2) + jp
            lanes = slice(pair * LANES, (pair + 1) * LANES)
            x_pair = xc_ref[tok, lanes]
            zero = jnp.zeros_like(x_pair)
            x2 = jnp.concatenate([jnp.where(first_head, x_pair, zero),
                                  jnp.where(first_head, zero, x_pair)], axis=0)
            m2 = []
            for e in range(2):
                hd = 2 * pair + e
                acs_col = jnp.broadcast_to(acs2[:, hd:hd + 1], (q, q))
                decay = jnp.exp2(jnp.where(causal, acs_col - row_t[hd:hd + 1, :], -jnp.inf))
                m2.append((cb * decay).astype(BF16))
            ybuf[:, lanes] = y_g[:, jp * LANES:(jp + 1) * LANES] + _dot(
                jnp.concatenate(m2, axis=1), x2)

    return _gated_group_norm(ybuf[...], xc_ref[tok, :D_SSM].astype(F32),
                             zs_ref[tok, :].astype(F32), dskip_ref[...], normw_ref[...])


def _write_final_state(hT, hTo_ref, ho_ref):
    hTo_ref[0] = hT[...]
    for g in range(SSM_GROUPS):
        ho_ref[0, g] = hT[g].T


def _mixer_kernel(zs_ref, xc_ref, dt_ref, dtb_ref, alog_ref, dskip_ref, normw_ref, expand_ref,
                  h0_ref, yb_ref, hTo_ref, ho_ref, hT, ybuf, *, tb, valid_len):
    i = pl.program_id(1)

    @pl.when(i == 0)
    def _():
        hT[...] = h0_ref[...]

    masks = _ssd_masks()

    def chunk_body(c, carry):
        r0 = pl.multiple_of(c * CHUNK, CHUNK)
        tok = pl.ds(r0, CHUNK)
        yn = _ssd_chunk(tok, r0, masks, zs_ref, xc_ref, dt_ref, dtb_ref, alog_ref, dskip_ref,
                        normw_ref, expand_ref, hT, ybuf, valid_len if valid_len < tb else None)
        yb_ref[tok, :] = yn.astype(yb_ref.dtype)
        return carry

    lax.fori_loop(0, tb // CHUNK, chunk_body, 0)

    @pl.when(i == pl.num_programs(1) - 1)
    def _():
        _write_final_state(hT, hTo_ref, ho_ref)


HALF = 2 * CHUNK


def _mixer_outproj_kernel(zs_ref, xc_ref, dt_ref, ya_ref, sga_ref, sgb_ref, h_ref,
                          dtb_ref, alog_ref, dskip_ref, normw_ref, expand_ref, h0_ref,
                          wa_ref, wb_ref, wo_ref,
                          o_ref, hTo_ref, ho_ref, hT, ybuf, yb_scr, ya_scr, *, tb):
    i = pl.program_id(1)

    @pl.when(i == 0)
    def _():
        hT[...] = h0_ref[...]

    masks = _ssd_masks()
    ya_scr[...] = _dot(ya_ref[...], wa_ref[...])

    def project(rows):
        y_b = _dot(yb_scr[rows, :], wb_ref[...])
        merged = (sga_ref[rows, :].astype(F32) * ya_scr[rows, :]
                  + sgb_ref[rows, :].astype(F32) * y_b)
        o_ref[rows, :] = h_ref[rows, :] + _dot(merged.astype(BF16), wo_ref[...])

    for c in range(tb // CHUNK):
        r0 = c * CHUNK
        tok = slice(r0, r0 + CHUNK)
        yn = _ssd_chunk(tok, r0, masks, zs_ref, xc_ref, dt_ref, dtb_ref, alog_ref, dskip_ref,
                        normw_ref, expand_ref, hT, ybuf, None)
        yb_scr[tok, :] = yn.astype(BF16)
        if (r0 + CHUNK) % HALF == 0:
            project(slice(r0 + CHUNK - HALF, r0 + CHUNK))

    @pl.when(i == pl.num_programs(1) - 1)
    def _():
        _write_final_state(hT, hTo_ref, ho_ref)


def _mixer_outproj(zs, xc, dt, ya, sga, sgb, h, p, expand, h0, wa, wb, wo, *, batch, seqlen, tb):
    nt = seqlen // tb
    row = lambda w: pl.BlockSpec((tb, w), lambda b, i: (b * nt + i, 0))
    full = lambda shape: pl.BlockSpec(shape, lambda b, i: (0,) * len(shape))
    once = lambda shape: pl.BlockSpec(shape, lambda b, i: (0,) * len(shape), pipeline_mode=pl.Buffered(1))
    per_b = lambda shape: pl.BlockSpec((1,) + shape, lambda b, i: (b,) + (0,) * len(shape))
    return pl.pallas_call(
        functools.partial(_mixer_outproj_kernel, tb=tb),
        grid=(batch, nt),
        in_specs=[row(D_SSM), row(D_XBC), row(LANES), row(D_MODEL), row(D_MODEL), row(D_MODEL),
                  row(D_MODEL),
                  full((1, LANES)), full((1, LANES)), full((1, D_SSM)), full((1, D_SSM)),
                  once((LANES, D_SSM)), full((SSM_GROUPS, SSM_STATE, GROUP_W)),
                  once((D_MODEL, D_MODEL)), once((D_SSM, D_MODEL)), once((D_MODEL, D_MODEL))],
        out_specs=[row(D_MODEL), per_b((SSM_GROUPS, SSM_STATE, GROUP_W)),
                   per_b((SSM_GROUPS, GROUP_W, SSM_STATE))],
        out_shape=[jax.ShapeDtypeStruct((batch * seqlen, D_MODEL), F32),
                   jax.ShapeDtypeStruct((batch, SSM_GROUPS, SSM_STATE, GROUP_W), F32),
                   jax.ShapeDtypeStruct((batch, SSM_GROUPS, GROUP_W, SSM_STATE), F32)],
        scratch_shapes=[pltpu.VMEM((SSM_GROUPS, SSM_STATE, GROUP_W), F32),
                        pltpu.VMEM((CHUNK, D_SSM), F32),
                        pltpu.VMEM((tb, D_SSM), BF16),
                        pltpu.VMEM((tb, D_MODEL), F32)],
        compiler_params=pltpu.CompilerParams(
            dimension_semantics=("arbitrary", "arbitrary"), vmem_limit_bytes=VMEM_LIMIT),
        name="mixer_outproj",
    )(zs, xc, dt, ya, sga, sgb, h, p["dt_bias"], p["a_log"], p["d_skip"], p["ssm_norm_w"], expand,
      h0, wa, wb, wo)


def _mixer(zs, xc, dt, p, expand, h0, *, batch, seqlen, tb, valid_len):
    nt = seqlen // tb
    row = lambda w: pl.BlockSpec((tb, w), lambda b, i: (b * nt + i, 0))
    full = lambda shape: pl.BlockSpec(shape, lambda b, i: (0,) * len(shape))
    per_b = lambda shape: pl.BlockSpec((1,) + shape, lambda b, i: (b,) + (0,) * len(shape))
    return pl.pallas_call(
        functools.partial(_mixer_kernel, tb=tb, valid_len=valid_len),
        grid=(batch, nt),
        in_specs=[row(D_SSM), row(D_XBC), row(LANES),
                  full((1, LANES)), full((1, LANES)), full((1, D_SSM)), full((1, D_SSM)),
                  full((LANES, D_SSM)), full((SSM_GROUPS, SSM_STATE, GROUP_W))],
        out_specs=[row(D_SSM), per_b((SSM_GROUPS, SSM_STATE, GROUP_W)),
                   per_b((SSM_GROUPS, GROUP_W, SSM_STATE))],
        out_shape=[jax.ShapeDtypeStruct((batch * seqlen, D_SSM), BF16),
                   jax.ShapeDtypeStruct((batch, SSM_GROUPS, SSM_STATE, GROUP_W), F32),
                   jax.ShapeDtypeStruct((batch, SSM_GROUPS, GROUP_W, SSM_STATE), F32)],
        scratch_shapes=[pltpu.VMEM((SSM_GROUPS, SSM_STATE, GROUP_W), F32),
                        pltpu.VMEM((CHUNK, D_SSM), F32)],
        compiler_params=pltpu.CompilerParams(
            dimension_semantics=("arbitrary", "arbitrary"), vmem_limit_bytes=VMEM_LIMIT),
        name="mixer",
    )(zs, xc, dt, p["dt_bias"], p["a_log"], p["d_skip"], p["ssm_norm_w"], expand, h0)


SEQ_BLOCK = SUBLANES


def _decode_mixer_kernel(ab_ref, ach_ref, zs_ref, xbc_ref, dt_ref, sa_ref, sx_ref, h0_ref,
                         caw_ref, cxw_ref, cxb_ref, dtb_ref, alog_ref, dskip_ref, normw_ref,
                         gsum_ref, expand_ref,
                         ya_ref, yb_ref, sao_ref, sxo_ref, ho_ref,
                         xc_buf, bc_buf, os_buf, xw_buf, cd_buf, y_buf, *, n_tok):
    sb = SEQ_BLOCK

    xp = [sa_ref[k] for k in range(CONV_A_TAPS - 1)] + [ach_ref[t] for t in range(n_tok)]
    for t in range(n_tok):
        conv = caw_ref[0:1, :] * xp[t]
        for k in range(1, CONV_A_TAPS):
            conv += caw_ref[k:k + 1, :] * xp[t + k]
        ya_ref[t] = ab_ref[t] * conv
    for k in range(CONV_A_TAPS - 1):
        sao_ref[k] = xp[n_tok + k]

    xp = [sx_ref[k] for k in range(CONV_X_TAPS - 1)] + [xbc_ref[t] for t in range(n_tok)]
    for t in range(n_tok):
        conv = cxw_ref[0:1, :] * xp[t]
        for k in range(1, CONV_X_TAPS):
            conv += cxw_ref[k:k + 1, :] * xp[t + k]
        xc_buf[t * sb:(t + 1) * sb, :] = _silu(conv + cxb_ref[...])
    for k in range(CONV_X_TAPS - 1):
        sxo_ref[k] = xp[n_tok + k]

    a_neg = -jnp.exp(alog_ref[...])
    dts, acs = [], []
    for t in range(n_tok):
        dts.append(_softplus(dt_ref[t] + dtb_ref[...]))
        acs.append(dts[t] * a_neg + (acs[t - 1] if t else 0.0))

    pairs = [(qi, si) for qi in range(n_tok) for si in range(qi + 1)]
    prods = [xc_buf[qi * sb:(qi + 1) * sb, D_SSM + D_BC:] * xc_buf[si * sb:(si + 1) * sb, D_SSM:D_SSM + D_BC]
             for qi, si in pairs]
    p_hi, p_lo = _split2(jnp.concatenate(prods, axis=0))
    cbh = _dot(p_hi, gsum_ref[...]) + _dot(p_lo, gsum_ref[...])
    coef = [jnp.exp(acs[t]) for t in range(n_tok)]
    coef += [cbh[k * sb:(k + 1) * sb] * jnp.exp(acs[qi] - acs[si]) * dts[si]
             for k, (qi, si) in enumerate(pairs)]
    coef += [jnp.exp(acs[n_tok - 1] - acs[s]) * dts[s] for s in range(n_tok)]
    coef.append(jnp.exp(acs[n_tok - 1]))
    c_hi, c_lo = _split2(jnp.concatenate(coef, axis=0))
    coef_x = _dot(c_hi, expand_ref[...]) + _dot(c_lo, expand_ref[...])
    slab = lambda k: coef_x[k * sb:(k + 1) * sb]

    def store_tiles(buf, r0, val):
        for lt in range(val.shape[1] // LANES):
            buf[lt, r0:r0 + val.shape[0], :] = val[:, lt * LANES:(lt + 1) * LANES]

    def load_rows(buf, lt0, n_lt, rows):
        return jnp.concatenate([buf[lt0 + j, rows, :] for j in range(n_lt)], axis=1)

    store_tiles(bc_buf, 0, xc_buf[:, D_SSM:])
    store_tiles(os_buf, 0, coef_x[0:n_tok * sb])
    diag = {pr: slab(n_tok + k) for k, pr in enumerate(pairs)}
    k0 = n_tok + len(pairs)
    for s in range(n_tok):
        store_tiles(xw_buf, s * sb, slab(k0 + s) * xc_buf[s * sb:(s + 1) * sb, :D_SSM])
    cd_hi, cd_lo = _split2(slab(k0 + n_tok))
    store_tiles(cd_buf, 0, cd_hi.astype(F32))
    store_tiles(cd_buf, sb, cd_lo.astype(F32))

    for t in range(n_tok):
        acc = diag[(t, 0)] * xc_buf[0:sb, :D_SSM]
        for s in range(1, t + 1):
            acc += diag[(t, s)] * xc_buf[s * sb:(s + 1) * sb, :D_SSM]
        store_tiles(y_buf, t * sb, acc)

    gt = GROUP_W // LANES
    k_rows = 2 * SUBLANES
    krow = lax.broadcasted_iota(jnp.int32, (k_rows, SSM_STATE), 0)
    ones_rows = jnp.where((krow >= n_tok) & (krow < n_tok + 2), 1.0, 0.0).astype(BF16)
    pad_lhs = jnp.zeros((k_rows - n_tok - 2, GROUP_W), F32)
    pad_rhs = jnp.zeros((k_rows - n_tok, SSM_STATE), F32)

    def seq_body(b, carry):
        tok_rows = pl.ds(b, n_tok, stride=sb)
        for g in range(SSM_GROUPS):
            b_bg = bc_buf[g, tok_rows, :]
            c_bg = bc_buf[SSM_GROUPS + g, tok_rows, :]
            heads = pl.ds(g * HEADS_PER_GROUP, HEADS_PER_GROUP)
            h_bg = h0_ref[b, heads].reshape(GROUP_W, SSM_STATE)
            y_off = lax.dot_general(c_bg.astype(BF16), h_bg.astype(BF16),
                                    (((1,), (1,)), ((), ())), preferred_element_type=F32)
            y_off = y_off * load_rows(os_buf, g * gt, gt, tok_rows)
            for j in range(gt):
                y_buf[g * gt + j, tok_rows, :] = (y_buf[g * gt + j, tok_rows, :]
                                                  + y_off[:, j * LANES:(j + 1) * LANES])
            lhs = jnp.concatenate(
                [load_rows(xw_buf, g * gt, gt, tok_rows),
                 load_rows(cd_buf, g * gt, gt, pl.ds(b, 2, stride=sb)), pad_lhs],
                axis=0).astype(BF16)
            rhs = jnp.concatenate(
                [jnp.concatenate([b_bg, pad_rhs], axis=0).astype(BF16), ones_rows], axis=1)
            upd = lax.dot_general(lhs, rhs, (((0,), (0,)), ((), ())), preferred_element_type=F32)
            new = h_bg * upd[:, SSM_STATE:] + upd[:, :SSM_STATE]
            ho_ref[b, heads] = new.reshape(HEADS_PER_GROUP, SSM_HEAD_DIM, SSM_STATE)
        return carry

    lax.fori_loop(0, sb, seq_body, 0, unroll=2)

    xs_all = xc_buf[:, :D_SSM]
    zs_all = jnp.concatenate([zs_ref[t] for t in range(n_tok)], axis=0)
    y_all = load_rows(y_buf, 0, D_SSM // LANES, slice(None))
    yn = _gated_group_norm(y_all, xs_all, zs_all, dskip_ref[...], normw_ref[...])
    for t in range(n_tok):
        yb_ref[t] = yn[t * sb:(t + 1) * sb]


def _decode_mixer(ab, ach, zs, xbc, dt, sa, sx, h0, p, gsum, expand, *, n_tok, n_seq):
    sb = SEQ_BLOCK
    tok = lambda w: pl.BlockSpec((n_tok, sb, w), lambda i: (0, i, 0))
    st = lambda k, w: pl.BlockSpec((k, sb, w), lambda i: (0, i, 0))
    hblk = pl.BlockSpec((sb, SSM_HEADS, SSM_HEAD_DIM, SSM_STATE), lambda i: (i, 0, 0, 0))
    full = lambda shape: pl.BlockSpec(shape, lambda i: (0,) * len(shape))
    return pl.pallas_call(
        functools.partial(_decode_mixer_kernel, n_tok=n_tok),
        grid=(n_seq // sb,),
        in_specs=[tok(D_MODEL), tok(D_MODEL), tok(D_SSM), tok(D_XBC), tok(LANES),
                  st(CONV_A_TAPS - 1, D_MODEL), st(CONV_X_TAPS - 1, D_XBC), hblk,
                  full((CONV_A_TAPS, D_MODEL)), full((CONV_X_TAPS, D_XBC)), full((1, D_XBC)),
                  full((1, LANES)), full((1, LANES)), full((1, D_SSM)), full((1, D_SSM)),
                  full((D_BC, LANES)), full((LANES, D_SSM))],
        out_specs=[tok(D_MODEL), tok(D_SSM), st(CONV_A_TAPS - 1, D_MODEL),
                   st(CONV_X_TAPS - 1, D_XBC), hblk],
        out_shape=[jax.ShapeDtypeStruct((n_tok, n_seq, D_MODEL), F32),
                   jax.ShapeDtypeStruct((n_tok, n_seq, D_SSM), F32),
                   jax.ShapeDtypeStruct((CONV_A_TAPS - 1, n_seq, D_MODEL), F32),
                   jax.ShapeDtypeStruct((CONV_X_TAPS - 1, n_seq, D_XBC), F32),
                   jax.ShapeDtypeStruct((n_seq, SSM_HEADS, SSM_HEAD_DIM, SSM_STATE), F32)],
        scratch_shapes=[pltpu.VMEM((n_tok * sb, D_XBC), F32),
                        pltpu.VMEM((2 * D_BC // LANES, n_tok * sb, LANES), F32),
                        pltpu.VMEM((D_SSM // LANES, n_tok * sb, LANES), F32),
                        pltpu.VMEM((D_SSM // LANES, n_tok * sb, LANES), F32),
                        pltpu.VMEM((D_SSM // LANES, 2 * sb, LANES), F32),
                        pltpu.VMEM((D_SSM // LANES, n_tok * sb, LANES), F32)],
        compiler_params=pltpu.CompilerParams(
            dimension_semantics=("parallel",), vmem_limit_bytes=VMEM_LIMIT),
        name="decode_mixer",
    )(ab, ach, zs, xbc, dt, sa, sx, h0, p["conv_a_w"], p["ssm_conv_w"], p["ssm_conv_b"],
      p["dt_bias"], p["a_log"], p["d_skip"], p["ssm_norm_w"], gsum, expand)


def _row_tile(t, cap):
    if t <= cap:
        return t
    best = LANES
    for cand in range(LANES, cap + 1, LANES):
        if t % cand == 0:
            best = cand
    return best


def kernel(x_prompt, x_sample, state_conv_a, state_ssm_conv, state_ssm, meta_tokens, norm_ffn1, ffn1_w_gu, ffn1_w_down, norm_mix, w_in, conv_a_w, w_a_out, ssm_conv_w, ssm_conv_b, dt_bias, a_log, d_skip, ssm_norm_w, w_b_out, w_o, norm_ffn2, ffn2_w_gu, ffn2_w_down, norm_final):
    bsz, seqlen, _ = x_prompt.shape
    n_seq, n_tok, _ = x_sample.shape
    n_dt = SSM_HEADS
    c_dt = 3 * D_MODEL + D_SSM + D_XBC

    w_gu1, w_dn1 = ffn1_w_gu[0].astype(BF16), ffn1_w_down[0].astype(BF16)
    w_gu2, w_dn2 = ffn2_w_gu[0].astype(BF16), ffn2_w_down[0].astype(BF16)
    w_proj = (*_stage_w_in(w_in), w_in[0, :, c_dt + n_dt:].astype(BF16))
    wa, wb, wo = w_a_out[0].astype(BF16), w_b_out[0].astype(BF16), w_o[0].astype(BF16)
    pad_heads = lambda v: jnp.pad(v[0], (0, LANES - n_dt))[None]
    p = {
        "conv_a_w": conv_a_w[0], "ssm_conv_w": ssm_conv_w[0], "ssm_conv_b": ssm_conv_b[0][None],
        "dt_bias": pad_heads(dt_bias), "a_log": pad_heads(a_log),
        "d_skip": jnp.repeat(d_skip[0], SSM_HEAD_DIM)[None], "ssm_norm_w": ssm_norm_w[0][None],
    }
    n1, nm, n2, nf = norm_ffn1[0][None], norm_mix[0][None], norm_ffn2[0][None], norm_final[None]
    eh = lax.broadcasted_iota(jnp.int32, (LANES, D_SSM), 0)
    ec = lax.broadcasted_iota(jnp.int32, (LANES, D_SSM), 1)
    expand = (ec // SSM_HEAD_DIM == eh).astype(BF16)

    def ffn1(x):
        return _ffn(x, n1, w_gu1, w_dn1, nf, tm=_row_tile(x.shape[0], 512), final_norm=False)

    def ffn2(h):
        return _ffn(h, n2, w_gu2, w_dn2, nf, tm=_row_tile(h.shape[0], 512), final_norm=True)

    zeros = lambda *s: jnp.zeros(s, F32)
    h_m = ffn1(jnp.pad(meta_tokens, ((0, CHUNK - N_META), (0, 0))))
    _, zs, xc, dt, _, _, ca_m, cx_m = _inproj_conv(
        h_m, nm, w_proj, p, zeros(SUBLANES, D_MODEL), zeros(SUBLANES, D_XBC),
        tm=CHUNK, seq_tiles=1, valid_len=N_META)
    _, hT_m, _ = _mixer(zs, xc, dt, p, expand, zeros(SSM_GROUPS, SSM_STATE, GROUP_W),
                        batch=1, seqlen=CHUNK, tb=CHUNK, valid_len=N_META)

    h_p = ffn1(x_prompt.reshape(bsz * seqlen, D_MODEL))
    tm = _row_tile(seqlen, 512)
    ya, zs, xc, dt, sga, sgb, ca_p, cx_p = _inproj_conv(
        h_p, nm, w_proj, p, ca_m[0], cx_m[0], tm=tm, seq_tiles=seqlen // tm, valid_len=tm)
    tb = _row_tile(seqlen, 512)
    h2_p, _, h_out_p = _mixer_outproj(zs, xc, dt, ya, sga, sgb, h_p, p, expand, hT_m[0], wa, wb, wo,
                                      batch=bsz, seqlen=seqlen, tb=tb)
    y_prompt = ffn2(h2_p).reshape(bsz, seqlen, D_MODEL)
    prompt_conv_a = ca_p[None, :, SUBLANES - (CONV_A_TAPS - 1):, :]
    prompt_ssm_conv = cx_p[None, :, SUBLANES - (CONV_X_TAPS - 1):, :]
    prompt_ssm = h_out_p.reshape(1, bsz, SSM_HEADS, SSM_HEAD_DIM, SSM_STATE)

    x_s = jnp.swapaxes(x_sample, 0, 1).reshape(n_tok * n_seq, D_MODEL)
    h_s = ffn1(x_s)
    ab, ach, zs, xbc, dt, sga, sgb = _inproj(h_s, nm, w_proj, tm=_row_tile(n_tok * n_seq, 256), act_dtype=F32)
    tsf = lambda a: a.reshape(n_tok, n_seq, a.shape[-1])
    hh = lax.broadcasted_iota(jnp.int32, (D_BC, LANES), 1)
    kk = lax.broadcasted_iota(jnp.int32, (D_BC, LANES), 0)
    gsum = ((hh < SSM_HEADS) & (kk // SSM_STATE == hh // HEADS_PER_GROUP)).astype(BF16)
    ya, yb, sa_new, sx_new, sample_ssm = _decode_mixer(
        tsf(ab), tsf(ach), tsf(zs), tsf(xbc), tsf(dt),
        jnp.swapaxes(state_conv_a[0], 0, 1), jnp.swapaxes(state_ssm_conv[0], 0, 1), state_ssm[0],
        p, gsum, expand, n_tok=n_tok, n_seq=n_seq)
    flat = lambda a: a.reshape(n_tok * n_seq, a.shape[-1])
    y_s = ffn2(_outproj(flat(ya), flat(yb), sga, sgb, h_s, wa, wb, wo, tm=_row_tile(n_tok * n_seq, 512)))
    y_sample = jnp.swapaxes(y_s.reshape(n_tok, n_seq, D_MODEL), 0, 1)
    sample_conv_a = jnp.swapaxes(sa_new, 0, 1)[None]
    sample_ssm_conv = jnp.swapaxes(sx_new, 0, 1)[None]

    return (y_prompt, y_sample, prompt_conv_a, prompt_ssm_conv, prompt_ssm,
            sample_conv_a, sample_ssm_conv, sample_ssm[None])
```

```python
import functools

import jax
import jax.numpy as jnp
from jax import lax
from jax.experimental import pallas as pl
from jax.experimental.pallas import tpu as pltpu

D_MODEL = 1024
D_FF = 2816
D_SSM = 2048
SSM_HEADS = 32
SSM_HEAD_DIM = 64
SSM_GROUPS = 4
HEADS_PER_GROUP = SSM_HEADS // SSM_GROUPS
SSM_STATE = 128
D_BC = SSM_GROUPS * SSM_STATE
D_XBC = D_SSM + 2 * D_BC
CONV_A_TAPS = 3
CONV_X_TAPS = 4
N_META = 16
EPS = 1e-6

LANES = 128
SUBLANES = 8
CHUNK = 128
VMEM_LIMIT = 56 * 1024 * 1024

F32 = jnp.float32
BF16 = jnp.bfloat16


def _dot(a, b):
    return jnp.dot(a, b, preferred_element_type=F32)


def _rms(x, w):
    return x * lax.rsqrt(jnp.mean(x * x, axis=-1, keepdims=True) + EPS) * w


def _silu(x):
    return x * jax.nn.sigmoid(x)


def _softplus(x):
    return jnp.maximum(x, 0.0) + jnp.log1p(jnp.exp(-jnp.abs(x)))


def _split3(x):
    hi = x.astype(BF16)
    r = x - hi.astype(F32)
    mid = r.astype(BF16)
    lo = (r - mid.astype(F32)).astype(BF16)
    return hi, mid, lo


def _split2(x):
    hi = x.astype(BF16)
    lo = (x - hi.astype(F32)).astype(BF16)
    return hi, lo


def _resident(shape):
    return pl.BlockSpec(shape, lambda *_: (0,) * len(shape), pipeline_mode=pl.Buffered(1))


def _ffn_kernel(x_ref, nw_ref, wg_ref, wu_ref, wd_ref, fw_ref, o_ref, *, final_norm):
    x = x_ref[...]
    xn = _rms(x, nw_ref[...]).astype(BF16)
    act = (_silu(_dot(xn, wg_ref[...])) * _dot(xn, wu_ref[...])).astype(BF16)
    h = x + 0.5 * _dot(act, wd_ref[...])
    if final_norm:
        h = _rms(h, fw_ref[...])
    o_ref[...] = h


def _ffn(x, nw, w_gu, w_down, fw, *, tm, final_norm):
    t = x.shape[0]
    once = pl.Buffered(1)
    return pl.pallas_call(
        functools.partial(_ffn_kernel, final_norm=final_norm),
        grid=(t // tm,),
        in_specs=[
            pl.BlockSpec((tm, D_MODEL), lambda i: (i, 0)),
            _resident((1, D_MODEL)),
            pl.BlockSpec((D_MODEL, D_FF), lambda i: (0, 0), pipeline_mode=once),
            pl.BlockSpec((D_MODEL, D_FF), lambda i: (0, 1), pipeline_mode=once),
            _resident((D_FF, D_MODEL)),
            _resident((1, D_MODEL)),
        ],
        out_specs=pl.BlockSpec((tm, D_MODEL), lambda i: (i, 0)),
        out_shape=jax.ShapeDtypeStruct((t, D_MODEL), F32),
        compiler_params=pltpu.CompilerParams(
            dimension_semantics=("parallel",), vmem_limit_bytes=VMEM_LIMIT),
        name="ffn",
    )(x, nw, w_gu, w_gu, w_down, fw)


C_DT = 3 * D_MODEL + D_SSM + D_XBC
C_GATE = C_DT + LANES


def _project(u, w_refs, c0, width):
    w_main, w_dt, w_gate = w_refs
    if c0 < C_DT:
        w = w_main[c0:c0 + width, :]
    elif c0 < C_GATE:
        w = w_dt[c0 - C_DT:c0 - C_DT + width, :]
    else:
        w = w_gate[c0 - C_GATE:c0 - C_GATE + width, :]
    return lax.dot_general(u, w, (((1,), (1,)), ((), ())), preferred_element_type=F32)


def _proj_weight_specs():
    once = pl.Buffered(1)
    return [pl.BlockSpec((C_DT, D_MODEL), lambda *_: (0, 0), pipeline_mode=once),
            pl.BlockSpec((LANES, D_MODEL), lambda *_: (C_DT // LANES, 0), pipeline_mode=once),
            pl.BlockSpec((pl.Element(2 * D_MODEL), pl.Element(D_MODEL)),
                         lambda *_: (C_DT + SSM_HEADS, 0), pipeline_mode=once)]


def _inproj_kernel(h_ref, nw_ref, wm_ref, wdt_ref, wg_ref,
                   ab_ref, ach_ref, zs_ref, xbc_ref, dt_ref, sga_ref, sgb_ref):
    u = _rms(h_ref[...], nw_ref[...]).astype(BF16)

    def proj(c0, width):
        return _project(u, (wm_ref, wdt_ref, wg_ref), c0, width)

    ab_ref[...] = proj(0, D_MODEL).astype(ab_ref.dtype)
    ach_ref[...] = (proj(D_MODEL, D_MODEL) * proj(2 * D_MODEL, D_MODEL)).astype(ach_ref.dtype)
    c0 = 3 * D_MODEL
    for k in range(D_SSM // D_MODEL):
        zs_ref[:, k * D_MODEL:(k + 1) * D_MODEL] = _silu(proj(c0 + k * D_MODEL, D_MODEL)).astype(zs_ref.dtype)
    c0 += D_SSM
    for k in range(D_XBC // D_MODEL):
        xbc_ref[:, k * D_MODEL:(k + 1) * D_MODEL] = proj(c0 + k * D_MODEL, D_MODEL).astype(xbc_ref.dtype)
    c0 += D_XBC
    dt_ref[...] = proj(c0, LANES)
    c0 += LANES
    sga_ref[...] = jax.nn.sigmoid(proj(c0, D_MODEL)).astype(sga_ref.dtype)
    sgb_ref[...] = jax.nn.sigmoid(proj(c0 + D_MODEL, D_MODEL)).astype(sgb_ref.dtype)


def _inproj(h, nw, w_packed, *, tm, act_dtype):
    t = h.shape[0]
    widths = (D_MODEL, D_MODEL, D_SSM, D_XBC, LANES, D_MODEL, D_MODEL)
    dtypes = (act_dtype, act_dtype, act_dtype, act_dtype, F32, act_dtype, act_dtype)
    return pl.pallas_call(
        _inproj_kernel,
        grid=(t // tm,),
        in_specs=[pl.BlockSpec((tm, D_MODEL), lambda i: (i, 0)), _resident((1, D_MODEL))]
        + _proj_weight_specs(),
        out_specs=[pl.BlockSpec((tm, w), lambda i: (i, 0)) for w in widths],
        out_shape=[jax.ShapeDtypeStruct((t, w), d) for w, d in zip(widths, dtypes)],
        compiler_params=pltpu.CompilerParams(
            dimension_semantics=("parallel",), vmem_limit_bytes=VMEM_LIMIT),
        name="inproj",
    )(h, nw, *w_packed)


COL_BLOCK = 512


def _inproj_conv_kernel(h_ref, nw_ref, wm_ref, wdt_ref, wg_ref, caw_ref, cxw_ref, cxb_ref,
                        ca0_ref, cx0_ref,
                        ya_ref, zs_ref, xc_ref, dt_ref, sga_ref, sgb_ref, cao_ref, cxo_ref,
                        ca_carry, cx_carry, *stages, tm, seq_tiles, valid_len):
    i = pl.program_id(0)

    @pl.when(i % seq_tiles == 0)
    def _():
        ca_carry[...] = ca0_ref[...]
        cx_carry[...] = cx0_ref[...]

    u = _rms(h_ref[...], nw_ref[...]).astype(BF16)

    def proj(c0, width=COL_BLOCK):
        return _project(u, (wm_ref, wdt_ref, wg_ref), c0, width)

    n8 = tm // SUBLANES
    sublane = lax.broadcasted_iota(jnp.int32, (n8, SUBLANES, COL_BLOCK), 1)

    def causal_conv(x, carry_ref, w_ref, taps, cols):
        cur = x.reshape(n8, SUBLANES, COL_BLOCK)
        prev = jnp.concatenate([carry_ref[:, cols], x[:tm - SUBLANES]], axis=0).reshape(cur.shape)
        acc = w_ref[taps - 1:taps, cols] * x
        for d in range(1, taps):
            merged = jnp.where(sublane < SUBLANES - d, cur, prev)
            acc += w_ref[taps - 1 - d:taps - d, cols] * pltpu.roll(merged, d, axis=1).reshape(x.shape)
        carry_ref[:, cols] = x[valid_len - SUBLANES:valid_len]
        return acc

    c_z, c_x, c_dt = 3 * D_MODEL, 3 * D_MODEL + D_SSM, 3 * D_MODEL + D_SSM + D_XBC
    c_g = c_dt + LANES

    def tail_a(k, cols, prj):
        conv = causal_conv(prj[1] * prj[2], ca_carry, caw_ref, CONV_A_TAPS, cols)
        ya_ref[:, cols] = (prj[0] * conv).astype(ya_ref.dtype)

    def tail_z(k, cols, prj):
        zs_ref[:, cols] = _silu(prj[0]).astype(zs_ref.dtype)

    def tail_x(k, cols, prj):
        conv = causal_conv(prj[0], cx_carry, cxw_ref, CONV_X_TAPS, cols)
        xc_ref[:, cols] = _silu(conv + cxb_ref[:, cols]).astype(xc_ref.dtype)

    def tail_g(k, cols, prj):
        sga_ref[:, cols] = jax.nn.sigmoid(prj[0]).astype(sga_ref.dtype)
        sgb_ref[:, cols] = jax.nn.sigmoid(prj[1]).astype(sgb_ref.dtype)

    def blocks(tail, width, starts):
        return [(tail, k, [c + k * COL_BLOCK for c in starts]) for k in range(width // COL_BLOCK)]

    light = blocks(tail_z, D_SSM, [c_z])
    for n, (blk_a, blk_g) in enumerate(zip(blocks(tail_a, D_MODEL, [0, D_MODEL, 2 * D_MODEL]),
                                           blocks(tail_g, D_MODEL, [c_g, c_g + D_MODEL]))):
        light.insert(3 * n, blk_a)
        light.insert(3 * n + 2, blk_g)
    heavy = blocks(tail_x, D_XBC, [c_x])
    order = []
    for n in range(max(len(light), len(heavy))):
        order += heavy[n:n + 1] + light[n:n + 1]
    park = jnp.minimum(i, 0)
    for n, (tail, k, starts) in enumerate(order):
        buf = stages[n % 2]
        for j, c in enumerate(starts):
            buf[park, j] = proj(c)
        tail(k, slice(k * COL_BLOCK, (k + 1) * COL_BLOCK), [buf[park, j] for j in range(len(starts))])
    dt_ref[...] = proj(c_dt, LANES)
    cao_ref[0] = ca_carry[...]
    cxo_ref[0] = cx_carry[...]


def _inproj_conv(h, nw, w_packed, p, ca0, cx0, *, tm, seq_tiles, valid_len):
    t = h.shape[0]
    n_seq = t // (tm * seq_tiles)
    widths = (D_MODEL, D_SSM, D_XBC, LANES, D_MODEL, D_MODEL)
    dtypes = (BF16, BF16, BF16, F32, BF16, BF16)
    tail = lambda w: pl.BlockSpec((1, SUBLANES, w), lambda i: (i // seq_tiles, 0, 0))
    return pl.pallas_call(
        functools.partial(_inproj_conv_kernel, tm=tm, seq_tiles=seq_tiles, valid_len=valid_len),
        grid=(t // tm,),
        in_specs=[pl.BlockSpec((tm, D_MODEL), lambda i: (i, 0)), _resident((1, D_MODEL))]
        + _proj_weight_specs()
        + [_resident((CONV_A_TAPS, D_MODEL)), _resident((CONV_X_TAPS, D_XBC)),
           _resident((1, D_XBC)),
           _resident((SUBLANES, D_MODEL)), _resident((SUBLANES, D_XBC))],
        out_specs=[pl.BlockSpec((tm, w), lambda i: (i, 0)) for w in widths]
        + [tail(D_MODEL), tail(D_XBC)],
        out_shape=[jax.ShapeDtypeStruct((t, w), d) for w, d in zip(widths, dtypes)]
        + [jax.ShapeDtypeStruct((n_seq, SUBLANES, D_MODEL), F32),
           jax.ShapeDtypeStruct((n_seq, SUBLANES, D_XBC), F32)],
        scratch_shapes=[pltpu.VMEM((SUBLANES, D_MODEL), F32), pltpu.VMEM((SUBLANES, D_XBC), F32),
                        pltpu.VMEM((1, 3, tm, COL_BLOCK), F32), pltpu.VMEM((1, 3, tm, COL_BLOCK), F32)],
        compiler_params=pltpu.CompilerParams(
            dimension_semantics=("arbitrary",), vmem_limit_bytes=VMEM_LIMIT),
        name="inproj_conv",
    )(h, nw, *w_packed, p["conv_a_w"], p["ssm_conv_w"], p["ssm_conv_b"], ca0, cx0)


def _outproj_kernel(ya_ref, yb_ref, sga_ref, sgb_ref, h_ref, wa_ref, wb_ref, wo_ref, o_ref):
    y_a = _dot(ya_ref[...].astype(BF16), wa_ref[...])
    y_b = _dot(yb_ref[...].astype(BF16), wb_ref[...])
    merged = sga_ref[...].astype(F32) * y_a + sgb_ref[...].astype(F32) * y_b
    o_ref[...] = h_ref[...] + _dot(merged.astype(BF16), wo_ref[...])


def _outproj(ya, yb, sga, sgb, h, wa, wb, wo, *, tm):
    t = h.shape[0]
    row = lambda w: pl.BlockSpec((tm, w), lambda i: (i, 0))
    return pl.pallas_call(
        _outproj_kernel,
        grid=(t // tm,),
        in_specs=[row(D_MODEL), row(D_SSM), row(D_MODEL), row(D_MODEL), row(D_MODEL),
                  _resident((D_MODEL, D_MODEL)), _resident((D_SSM, D_MODEL)),
                  _resident((D_MODEL, D_MODEL))],
        out_specs=row(D_MODEL),
        out_shape=jax.ShapeDtypeStruct((t, D_MODEL), F32),
        compiler_params=pltpu.CompilerParams(
            dimension_semantics=("parallel",), vmem_limit_bytes=VMEM_LIMIT),
        name="outproj",
    )(ya, yb, sga, sgb, h, wa, wb, wo)


def _gated_group_norm(y, xs, zs, dskip, normw):
    y = (y + dskip * xs) * zs
    gw = D_SSM // SSM_GROUPS
    parts = []
    for g in range(SSM_GROUPS):
        yg = y[:, g * gw:(g + 1) * gw]
        parts.append(yg * lax.rsqrt(jnp.mean(yg * yg, axis=-1, keepdims=True) + EPS))
    return jnp.concatenate(parts, axis=-1) * normw


GROUP_W = HEADS_PER_GROUP * SSM_HEAD_DIM
LOG2E = 1.4426950408889634


def _ssd_masks():
    q = CHUNK
    rows = lax.broadcasted_iota(jnp.int32, (q, q), 0)
    cols = lax.broadcasted_iota(jnp.int32, (q, q), 1)
    causal = rows >= cols
    first_head = lax.broadcasted_iota(jnp.int32, (q, LANES), 1) < SSM_HEAD_DIM
    return causal, causal.astype(BF16), first_head


def _ssd_chunk(tok, r0, masks, zs_ref, xc_ref, dt_ref, dtb_ref, alog_ref, dskip_ref, normw_ref,
               expand_ref, hT, ybuf, valid_len):
    q = CHUNK
    causal, tri, first_head = masks
    a_neg = -jnp.exp(alog_ref[...])
    dt = _softplus(dt_ref[tok, :] + dtb_ref[...])
    if valid_len is not None:
        t_idx = r0 + lax.broadcasted_iota(jnp.int32, (q, LANES), 0)
        dt = jnp.where(t_idx < valid_len, dt, 0.0)
    hi, mid, lo = _split3(dt * a_neg)
    acs3 = _dot(tri, jnp.concatenate([hi, mid, lo], axis=1))
    acs = acs3[:, :LANES] + acs3[:, LANES:2 * LANES] + acs3[:, 2 * LANES:]
    acs2 = acs * LOG2E
    row_t = (acs2 - jnp.log2(dt)).T
    acs_end = acs2[q - 1:q, :]
    cd_hi, cd_lo = _split2(jnp.broadcast_to(jnp.exp2(acs_end), (SUBLANES, LANES)))
    stack = jnp.concatenate(
        [jnp.exp2(acs2).astype(BF16), (jnp.exp2(acs_end - acs2) * dt).astype(BF16),
         cd_hi, cd_lo], axis=0)

    for g in range(SSM_GROUPS):
        gcols = slice(g * GROUP_W, (g + 1) * GROUP_W)
        ex = _dot(stack, expand_ref[:, gcols])
        state_decay = ex[2 * q:2 * q + 1] + ex[2 * q + SUBLANES:2 * q + SUBLANES + 1]
        b_t = xc_ref[tok, D_SSM + g * SSM_STATE:D_SSM + (g + 1) * SSM_STATE].astype(F32).T.astype(BF16)
        c_bf = xc_ref[tok, D_SSM + D_BC + g * SSM_STATE:D_SSM + D_BC + (g + 1) * SSM_STATE]
        cb = _dot(c_bf, b_t)
        h_g = hT[g]
        y_g = ex[0:q] * _dot(c_bf, h_g.astype(BF16))
        xw_g = xc_ref[tok, gcols] * ex[q:2 * q].astype(BF16)
        hT[g] = h_g * state_decay + _dot(b_t, xw_g)
        for jp in range(HEADS_PER_GROUP // 2):
            pair = g * (HEADS_PER_GROUP // 2) + jp
            lanes = slice(pair * LANES, (pair + 1) * LANES)
            x_pair = xc_ref[tok, lanes]
            zero = jnp.zeros_like(x_pair)
            x2 = jnp.concatenate([jnp.where(first_head, x_pair, zero),
                                  jnp.where(first_head, zero, x_pair)], axis=0)
            m2 = []
            for e in range(2):
                hd = 2 * pair + e
                acs_col = jnp.broadcast_to(acs2[:, hd:hd + 1], (q, q))
                decay = jnp.exp2(jnp.where(causal, acs_col - row_t[hd:hd + 1, :], -jnp.inf))
                m2.append((cb * decay).astype(BF16))
            ybuf[:, lanes] = y_g[:, jp * LANES:(jp + 1) * LANES] + _dot(
                jnp.concatenate(m2, axis=1), x2)

    return _gated_group_norm(ybuf[...], xc_ref[tok, :D_SSM].astype(F32),
                             zs_ref[tok, :].astype(F32), dskip_ref[...], normw_ref[...])


def _write_final_state(hT, hTo_ref, ho_ref):
    hTo_ref[0] = hT[...]
    for g in range(SSM_GROUPS):
        ho_ref[0, g] = hT[g].T


def _mixer_kernel(zs_ref, xc_ref, dt_ref, dtb_ref, alog_ref, dskip_ref, normw_ref, expand_ref,
                  h0_ref, yb_ref, hTo_ref, ho_ref, hT, ybuf, *, tb, valid_len):
    i = pl.program_id(1)

    @pl.when(i == 0)
    def _():
        hT[...] = h0_ref[...]

    masks = _ssd_masks()

    def chunk_body(c, carry):
        r0 = pl.multiple_of(c * CHUNK, CHUNK)
        tok = pl.ds(r0, CHUNK)
        yn = _ssd_chunk(tok, r0, masks, zs_ref, xc_ref, dt_ref, dtb_ref, alog_ref, dskip_ref,
                        normw_ref, expand_ref, hT, ybuf, valid_len if valid_len < tb else None)
        yb_ref[tok, :] = yn.astype(yb_ref.dtype)
        return carry

    lax.fori_loop(0, tb // CHUNK, chunk_body, 0)

    @pl.when(i == pl.num_programs(1) - 1)
    def _():
        _write_final_state(hT, hTo_ref, ho_ref)


HALF = 2 * CHUNK


def _mixer_outproj_kernel(zs_ref, xc_ref, dt_ref, ya_ref, sga_ref, sgb_ref, h_ref,
                          dtb_ref, alog_ref, dskip_ref, normw_ref, expand_ref, h0_ref,
                          wa_ref, wb_ref, wo_ref,
                          o_ref, hTo_ref, ho_ref, hT, ybuf, yb_scr, ya_scr, *, tb):
    i = pl.program_id(1)

    @pl.when(i == 0)
    def _():
        hT[...] = h0_ref[...]

    masks = _ssd_masks()
    ya_scr[...] = _dot(ya_ref[...], wa_ref[...])

    def project(rows):
        y_b = _dot(yb_scr[rows, :], wb_ref[...])
        merged = (sga_ref[rows, :].astype(F32) * ya_scr[rows, :]
                  + sgb_ref[rows, :].astype(F32) * y_b)
        o_ref[rows, :] = h_ref[rows, :] + _dot(merged.astype(BF16), wo_ref[...])

    for c in range(tb // CHUNK):
        r0 = c * CHUNK
        tok = slice(r0, r0 + CHUNK)
        yn = _ssd_chunk(tok, r0, masks, zs_ref, xc_ref, dt_ref, dtb_ref, alog_ref, dskip_ref,
                        normw_ref, expand_ref, hT, ybuf, None)
        yb_scr[tok, :] = yn.astype(BF16)
        if (r0 + CHUNK) % HALF == 0:
            project(slice(r0 + CHUNK - HALF, r0 + CHUNK))

    @pl.when(i == pl.num_programs(1) - 1)
    def _():
        _write_final_state(hT, hTo_ref, ho_ref)


def _mixer_outproj(zs, xc, dt, ya, sga, sgb, h, p, expand, h0, wa, wb, wo, *, batch, seqlen, tb):
    nt = seqlen // tb
    row = lambda w: pl.BlockSpec((tb, w), lambda b, i: (b * nt + i, 0))
    full = lambda shape: pl.BlockSpec(shape, lambda b, i: (0,) * len(shape))
    once = lambda shape: pl.BlockSpec(shape, lambda b, i: (0,) * len(shape), pipeline_mode=pl.Buffered(1))
    per_b = lambda shape: pl.BlockSpec((1,) + shape, lambda b, i: (b,) + (0,) * len(shape))
    return pl.pallas_call(
        functools.partial(_mixer_outproj_kernel, tb=tb),
        grid=(batch, nt),
        in_specs=[row(D_SSM), row(D_XBC), row(LANES), row(D_MODEL), row(D_MODEL), row(D_MODEL),
                  row(D_MODEL),
                  full((1, LANES)), full((1, LANES)), full((1, D_SSM)), full((1, D_SSM)),
                  once((LANES, D_SSM)), full((SSM_GROUPS, SSM_STATE, GROUP_W)),
                  once((D_MODEL, D_MODEL)), once((D_SSM, D_MODEL)), once((D_MODEL, D_MODEL))],
        out_specs=[row(D_MODEL), per_b((SSM_GROUPS, SSM_STATE, GROUP_W)),
                   per_b((SSM_GROUPS, GROUP_W, SSM_STATE))],
        out_shape=[jax.ShapeDtypeStruct((batch * seqlen, D_MODEL), F32),
                   jax.ShapeDtypeStruct((batch, SSM_GROUPS, SSM_STATE, GROUP_W), F32),
                   jax.ShapeDtypeStruct((batch, SSM_GROUPS, GROUP_W, SSM_STATE), F32)],
        scratch_shapes=[pltpu.VMEM((SSM_GROUPS, SSM_STATE, GROUP_W), F32),
                        pltpu.VMEM((CHUNK, D_SSM), F32),
                        pltpu.VMEM((tb, D_SSM), BF16),
                        pltpu.VMEM((tb, D_MODEL), F32)],
        compiler_params=pltpu.CompilerParams(
            dimension_semantics=("arbitrary", "arbitrary"), vmem_limit_bytes=VMEM_LIMIT),
        name="mixer_outproj",
    )(zs, xc, dt, ya, sga, sgb, h, p["dt_bias"], p["a_log"], p["d_skip"], p["ssm_norm_w"], expand,
      h0, wa, wb, wo)


def _mixer(zs, xc, dt, p, expand, h0, *, batch, seqlen, tb, valid_len):
    nt = seqlen // tb
    row = lambda w: pl.BlockSpec((tb, w), lambda b, i: (b * nt + i, 0))
    full = lambda shape: pl.BlockSpec(shape, lambda b, i: (0,) * len(shape))
    per_b = lambda shape: pl.BlockSpec((1,) + shape, lambda b, i: (b,) + (0,) * len(shape))
    return pl.pallas_call(
        functools.partial(_mixer_kernel, tb=tb, valid_len=valid_len),
        grid=(batch, nt),
        in_specs=[row(D_SSM), row(D_XBC), row(LANES),
                  full((1, LANES)), full((1, LANES)), full((1, D_SSM)), full((1, D_SSM)),
                  full((LANES, D_SSM)), full((SSM_GROUPS, SSM_STATE, GROUP_W))],
        out_specs=[row(D_SSM), per_b((SSM_GROUPS, SSM_STATE, GROUP_W)),
                   per_b((SSM_GROUPS, GROUP_W, SSM_STATE))],
        out_shape=[jax.ShapeDtypeStruct((batch * seqlen, D_SSM), BF16),
                   jax.ShapeDtypeStruct((batch, SSM_GROUPS, SSM_STATE, GROUP_W), F32),
                   jax.ShapeDtypeStruct((batch, SSM_GROUPS, GROUP_W, SSM_STATE), F32)],
        scratch_shapes=[pltpu.VMEM((SSM_GROUPS, SSM_STATE, GROUP_W), F32),
                        pltpu.VMEM((CHUNK, D_SSM), F32)],
        compiler_params=pltpu.CompilerParams(
            dimension_semantics=("arbitrary", "arbitrary"), vmem_limit_bytes=VMEM_LIMIT),
        name="mixer",
    )(zs, xc, dt, p["dt_bias"], p["a_log"], p["d_skip"], p["ssm_norm_w"], expand, h0)


SEQ_BLOCK = SUBLANES


def _decode_mixer_kernel(ab_ref, ach_ref, zs_ref, xbc_ref, dt_ref, sa_ref, sx_ref, h0_ref,
                         caw_ref, cxw_ref, cxb_ref, dtb_ref, alog_ref, dskip_ref, normw_ref,
                         gsum_ref, expand_ref,
                         ya_ref, yb_ref, sao_ref, sxo_ref, ho_ref,
                         xc_buf, bc_buf, os_buf, xw_buf, cd_buf, y_buf, *, n_tok):
    sb = SEQ_BLOCK

    xp = [sa_ref[k] for k in range(CONV_A_TAPS - 1)] + [ach_ref[t] for t in range(n_tok)]
    for t in range(n_tok):
        conv = caw_ref[0:1, :] * xp[t]
        for k in range(1, CONV_A_TAPS):
            conv += caw_ref[k:k + 1, :] * xp[t + k]
        ya_ref[t] = ab_ref[t] * conv
    for k in range(CONV_A_TAPS - 1):
        sao_ref[k] = xp[n_tok + k]

    xp = [sx_ref[k] for k in range(CONV_X_TAPS - 1)] + [xbc_ref[t] for t in range(n_tok)]
    for t in range(n_tok):
        conv = cxw_ref[0:1, :] * xp[t]
        for k in range(1, CONV_X_TAPS):
            conv += cxw_ref[k:k + 1, :] * xp[t + k]
        xc_buf[t * sb:(t + 1) * sb, :] = _silu(conv + cxb_ref[...])
    for k in range(CONV_X_TAPS - 1):
        sxo_ref[k] = xp[n_tok + k]

    a_neg = -jnp.exp(alog_ref[...])
    dts, acs = [], []
    for t in range(n_tok):
        dts.append(_softplus(dt_ref[t] + dtb_ref[...]))
        acs.append(dts[t] * a_neg + (acs[t - 1] if t else 0.0))

    pairs = [(qi, si) for qi in range(n_tok) for si in range(qi + 1)]
    prods = [xc_buf[qi * sb:(qi + 1) * sb, D_SSM + D_BC:] * xc_buf[si * sb:(si + 1) * sb, D_SSM:D_SSM + D_BC]
             for qi, si in pairs]
    p_hi, p_lo = _split2(jnp.concatenate(prods, axis=0))
    cbh = _dot(p_hi, gsum_ref[...]) + _dot(p_lo, gsum_ref[...])
    coef = [jnp.exp(acs[t]) for t in range(n_tok)]
    coef += [cbh[k * sb:(k + 1) * sb] * jnp.exp(acs[qi] - acs[si]) * dts[si]
             for k, (qi, si) in enumerate(pairs)]
    coef += [jnp.exp(acs[n_tok - 1] - acs[s]) * dts[s] for s in range(n_tok)]
    coef.append(jnp.exp(acs[n_tok - 1]))
    c_hi, c_lo = _split2(jnp.concatenate(coef, axis=0))
    coef_x = _dot(c_hi, expand_ref[...]) + _dot(c_lo, expand_ref[...])
    slab = lambda k: coef_x[k * sb:(k + 1) * sb]

    def store_tiles(buf, r0, val):
        for lt in range(val.shape[1] // LANES):
            buf[lt, r0:r0 + val.shape[0], :] = val[:, lt * LANES:(lt + 1) * LANES]

    def load_rows(buf, lt0, n_lt, rows):
        return jnp.concatenate([buf[lt0 + j, rows, :] for j in range(n_lt)], axis=1)

    store_tiles(bc_buf, 0, xc_buf[:, D_SSM:])
    store_tiles(os_buf, 0, coef_x[0:n_tok * sb])
    diag = {pr: slab(n_tok + k) for k, pr in enumerate(pairs)}
    k0 = n_tok + len(pairs)
    for s in range(n_tok):
        store_tiles(xw_buf, s * sb, slab(k0 + s) * xc_buf[s * sb:(s + 1) * sb, :D_SSM])
    cd_hi, cd_lo = _split2(slab(k0 + n_tok))
    store_tiles(cd_buf, 0, cd_hi.astype(F32))
    store_tiles(cd_buf, sb, cd_lo.astype(F32))

    for t in range(n_tok):
        acc = diag[(t, 0)] * xc_buf[0:sb, :D_SSM]
        for s in range(1, t + 1):
            acc += diag[(t, s)] * xc_buf[s * sb:(s + 1) * sb, :D_SSM]
        store_tiles(y_buf, t * sb, acc)

    gt = GROUP_W // LANES
    k_rows = 2 * SUBLANES
    krow = lax.broadcasted_iota(jnp.int32, (k_rows, SSM_STATE), 0)
    ones_rows = jnp.where((krow >= n_tok) & (krow < n_tok + 2), 1.0, 0.0).astype(BF16)
    pad_lhs = jnp.zeros((k_rows - n_tok - 2, GROUP_W), F32)
    pad_rhs = jnp.zeros((k_rows - n_tok, SSM_STATE), F32)

    def seq_body(b, carry):
        tok_rows = pl.ds(b, n_tok, stride=sb)
        for g in range(SSM_GROUPS):
            b_bg = bc_buf[g, tok_rows, :]
            c_bg = bc_buf[SSM_GROUPS + g, tok_rows, :]
            heads = pl.ds(g * HEADS_PER_GROUP, HEADS_PER_GROUP)
            h_bg = h0_ref[b, heads].reshape(GROUP_W, SSM_STATE)
            y_off = lax.dot_general(c_bg.astype(BF16), h_bg.astype(BF16),
                                    (((1,), (1,)), ((), ())), preferred_element_type=F32)
            y_off = y_off * load_rows(os_buf, g * gt, gt, tok_rows)
            for j in range(gt):
                y_buf[g * gt + j, tok_rows, :] = (y_buf[g * gt + j, tok_rows, :]
                                                  + y_off[:, j * LANES:(j + 1) * LANES])
            lhs = jnp.concatenate(
                [load_rows(xw_buf, g * gt, gt, tok_rows),
                 load_rows(cd_buf, g * gt, gt, pl.ds(b, 2, stride=sb)), pad_lhs],
                axis=0).astype(BF16)
            rhs = jnp.concatenate(
                [jnp.concatenate([b_bg, pad_rhs], axis=0).astype(BF16), ones_rows], axis=1)
            upd = lax.dot_general(lhs, rhs, (((0,), (0,)), ((), ())), preferred_element_type=F32)
            new = h_bg * upd[:, SSM_STATE:] + upd[:, :SSM_STATE]
            ho_ref[b, heads] = new.reshape(HEADS_PER_GROUP, SSM_HEAD_DIM, SSM_STATE)
        return carry

    lax.fori_loop(0, sb, seq_body, 0, unroll=2)

    xs_all = xc_buf[:, :D_SSM]
    zs_all = jnp.concatenate([zs_ref[t] for t in range(n_tok)], axis=0)
    y_all = load_rows(y_buf, 0, D_SSM // LANES, slice(None))
    yn = _gated_group_norm(y_all, xs_all, zs_all, dskip_ref[...], normw_ref[...])
    for t in range(n_tok):
        yb_ref[t] = yn[t * sb:(t + 1) * sb]


def _decode_mixer(ab, ach, zs, xbc, dt, sa, sx, h0, p, gsum, expand, *, n_tok, n_seq):
    sb = SEQ_BLOCK
    tok = lambda w: pl.BlockSpec((n_tok, sb, w), lambda i: (0, i, 0))
    st = lambda k, w: pl.BlockSpec((k, sb, w), lambda i: (0, i, 0))
    hblk = pl.BlockSpec((sb, SSM_HEADS, SSM_HEAD_DIM, SSM_STATE), lambda i: (i, 0, 0, 0))
    full = lambda shape: pl.BlockSpec(shape, lambda i: (0,) * len(shape))
    return pl.pallas_call(
        functools.partial(_decode_mixer_kernel, n_tok=n_tok),
        grid=(n_seq // sb,),
        in_specs=[tok(D_MODEL), tok(D_MODEL), tok(D_SSM), tok(D_XBC), tok(LANES),
                  st(CONV_A_TAPS - 1, D_MODEL), st(CONV_X_TAPS - 1, D_XBC), hblk,
                  full((CONV_A_TAPS, D_MODEL)), full((CONV_X_TAPS, D_XBC)), full((1, D_XBC)),
                  full((1, LANES)), full((1, LANES)), full((1, D_SSM)), full((1, D_SSM)),
                  full((D_BC, LANES)), full((LANES, D_SSM))],
        out_specs=[tok(D_MODEL), tok(D_SSM), st(CONV_A_TAPS - 1, D_MODEL),
                   st(CONV_X_TAPS - 1, D_XBC), hblk],
        out_shape=[jax.ShapeDtypeStruct((n_tok, n_seq, D_MODEL), F32),
                   jax.ShapeDtypeStruct((n_tok, n_seq, D_SSM), F32),
                   jax.ShapeDtypeStruct((CONV_A_TAPS - 1, n_seq, D_MODEL), F32),
                   jax.ShapeDtypeStruct((CONV_X_TAPS - 1, n_seq, D_XBC), F32),
                   jax.ShapeDtypeStruct((n_seq, SSM_HEADS, SSM_HEAD_DIM, SSM_STATE), F32)],
        scratch_shapes=[pltpu.VMEM((n_tok * sb, D_XBC), F32),
                        pltpu.VMEM((2 * D_BC // LANES, n_tok * sb, LANES), F32),
                        pltpu.VMEM((D_SSM // LANES, n_tok * sb, LANES), F32),
                        pltpu.VMEM((D_SSM // LANES, n_tok * sb, LANES), F32),
                        pltpu.VMEM((D_SSM // LANES, 2 * sb, LANES), F32),
                        pltpu.VMEM((D_SSM // LANES, n_tok * sb, LANES), F32)],
        compiler_params=pltpu.CompilerParams(
            dimension_semantics=("parallel",), vmem_limit_bytes=VMEM_LIMIT),
        name="decode_mixer",
    )(ab, ach, zs, xbc, dt, sa, sx, h0, p["conv_a_w"], p["ssm_conv_w"], p["ssm_conv_b"],
      p["dt_bias"], p["a_log"], p["d_skip"], p["ssm_norm_w"], gsum, expand)


def _row_tile(t, cap):
    if t <= cap:
        return t
    best = LANES
    for cand in range(LANES, cap + 1, LANES):
        if t % cand == 0:
            best = cand
    return best


def kernel(x_prompt, x_sample, state_conv_a, state_ssm_conv, state_ssm, meta_tokens, norm_ffn1, ffn1_w_gu, ffn1_w_down, norm_mix, w_in, conv_a_w, w_a_out, ssm_conv_w, ssm_conv_b, dt_bias, a_log, d_skip, ssm_norm_w, w_b_out, w_o, norm_ffn2, ffn2_w_gu, ffn2_w_down, norm_final):
    bsz, seqlen, _ = x_prompt.shape
    n_seq, n_tok, _ = x_sample.shape
    n_dt = SSM_HEADS
    c_dt = 3 * D_MODEL + D_SSM + D_XBC

    w_gu1, w_dn1 = ffn1_w_gu[0].astype(BF16), ffn1_w_down[0].astype(BF16)
    w_gu2, w_dn2 = ffn2_w_gu[0].astype(BF16), ffn2_w_down[0].astype(BF16)
    w_proj = (jnp.swapaxes(w_in[0], 0, 1).astype(BF16),) * 3
    wa, wb, wo = w_a_out[0].astype(BF16), w_b_out[0].astype(BF16), w_o[0].astype(BF16)
    pad_heads = lambda v: jnp.pad(v[0], (0, LANES - n_dt))[None]
    p = {
        "conv_a_w": conv_a_w[0], "ssm_conv_w": ssm_conv_w[0], "ssm_conv_b": ssm_conv_b[0][None],
        "dt_bias": pad_heads(dt_bias), "a_log": pad_heads(a_log),
        "d_skip": jnp.repeat(d_skip[0], SSM_HEAD_DIM)[None], "ssm_norm_w": ssm_norm_w[0][None],
    }
    n1, nm, n2, nf = norm_ffn1[0][None], norm_mix[0][None], norm_ffn2[0][None], norm_final[None]
    eh = lax.broadcasted_iota(jnp.int32, (LANES, D_SSM), 0)
    ec = lax.broadcasted_iota(jnp.int32, (LANES, D_SSM), 1)
    expand = (ec // SSM_HEAD_DIM == eh).astype(BF16)

    def ffn1(x):
        return _ffn(x, n1, w_gu1, w_dn1, nf, tm=_row_tile(x.shape[0], 512), final_norm=False)

    def ffn2(h):
        return _ffn(h, n2, w_gu2, w_dn2, nf, tm=_row_tile(h.shape[0], 512), final_norm=True)

    zeros = lambda *s: jnp.zeros(s, F32)
    h_m = ffn1(jnp.pad(meta_tokens, ((0, CHUNK - N_META), (0, 0))))
    _, zs, xc, dt, _, _, ca_m, cx_m = _inproj_conv(
        h_m, nm, w_proj, p, zeros(SUBLANES, D_MODEL), zeros(SUBLANES, D_XBC),
        tm=CHUNK, seq_tiles=1, valid_len=N_META)
    _, hT_m, _ = _mixer(zs, xc, dt, p, expand, zeros(SSM_GROUPS, SSM_STATE, GROUP_W),
                        batch=1, seqlen=CHUNK, tb=CHUNK, valid_len=N_META)

    h_p = ffn1(x_prompt.reshape(bsz * seqlen, D_MODEL))
    tm = _row_tile(seqlen, 512)
    ya, zs, xc, dt, sga, sgb, ca_p, cx_p = _inproj_conv(
        h_p, nm, w_proj, p, ca_m[0], cx_m[0], tm=tm, seq_tiles=seqlen // tm, valid_len=tm)
    tb = _row_tile(seqlen, 512)
    h2_p, _, h_out_p = _mixer_outproj(zs, xc, dt, ya, sga, sgb, h_p, p, expand, hT_m[0], wa, wb, wo,
                                      batch=bsz, seqlen=seqlen, tb=tb)
    y_prompt = ffn2(h2_p).reshape(bsz, seqlen, D_MODEL)
    prompt_conv_a = ca_p[None, :, SUBLANES - (CONV_A_TAPS - 1):, :]
    prompt_ssm_conv = cx_p[None, :, SUBLANES - (CONV_X_TAPS - 1):, :]
    prompt_ssm = h_out_p.reshape(1, bsz, SSM_HEADS, SSM_HEAD_DIM, SSM_STATE)

    x_s = jnp.swapaxes(x_sample, 0, 1).reshape(n_tok * n_seq, D_MODEL)
    h_s = ffn1(x_s)
    ab, ach, zs, xbc, dt, sga, sgb = _inproj(h_s, nm, w_proj, tm=_row_tile(n_tok * n_seq, 256), act_dtype=F32)
    tsf = lambda a: a.reshape(n_tok, n_seq, a.shape[-1])
    hh = lax.broadcasted_iota(jnp.int32, (D_BC, LANES), 1)
    kk = lax.broadcasted_iota(jnp.int32, (D_BC, LANES), 0)
    gsum = ((hh < SSM_HEADS) & (kk // SSM_STATE == hh // HEADS_PER_GROUP)).astype(BF16)
    ya, yb, sa_new, sx_new, sample_ssm = _decode_mixer(
        tsf(ab), tsf(ach), tsf(zs), tsf(xbc), tsf(dt),
        jnp.swapaxes(state_conv_a[0], 0, 1), jnp.swapaxes(state_ssm_conv[0], 0, 1), state_ssm[0],
        p, gsum, expand, n_tok=n_tok, n_seq=n_seq)
    flat = lambda a: a.reshape(n_tok * n_seq, a.shape[-1])
    y_s = ffn2(_outproj(flat(ya), flat(yb), sga, sgb, h_s, wa, wb, wo, tm=_row_tile(n_tok * n_seq, 512)))
    y_sample = jnp.swapaxes(y_s.reshape(n_tok, n_seq, D_MODEL), 0, 1)
    sample_conv_a = jnp.swapaxes(sa_new, 0, 1)[None]
    sample_ssm_conv = jnp.swapaxes(sx_new, 0, 1)[None]

    return (y_prompt, y_sample, prompt_conv_a, prompt_ssm_conv, prompt_ssm,
            sample_conv_a, sample_ssm_conv, sample_ssm[None])
```

```python
import functools

import jax
import jax.numpy as jnp
from jax import lax
from jax.experimental import pallas as pl
from jax.experimental.pallas import tpu as pltpu

D_MODEL = 1024
D_FF = 2816
D_SSM = 2048
SSM_HEADS = 32
SSM_HEAD_DIM = 64
SSM_GROUPS = 4
HEADS_PER_GROUP = SSM_HEADS // SSM_GROUPS
SSM_STATE = 128
D_BC = SSM_GROUPS * SSM_STATE
D_XBC = D_SSM + 2 * D_BC
CONV_A_TAPS = 3
CONV_X_TAPS = 4
N_META = 16
EPS = 1e-6

LANES = 128
SUBLANES = 8
CHUNK = 128
VMEM_LIMIT = 56 * 1024 * 1024

F32 = jnp.float32
BF16 = jnp.bfloat16


def _dot(a, b):
    return jnp.dot(a, b, preferred_element_type=F32)


def _rms(x, w):
    return x * lax.rsqrt(jnp.mean(x * x, axis=-1, keepdims=True) + EPS) * w


def _silu(x):
    return x * jax.nn.sigmoid(x)


def _softplus(x):
    return jnp.maximum(x, 0.0) + jnp.log1p(jnp.exp(-jnp.abs(x)))


def _split3(x):
    hi = x.astype(BF16)
    r = x - hi.astype(F32)
    mid = r.astype(BF16)
    lo = (r - mid.astype(F32)).astype(BF16)
    return hi, mid, lo


def _split2(x):
    hi = x.astype(BF16)
    lo = (x - hi.astype(F32)).astype(BF16)
    return hi, lo


def _resident(shape):
    return pl.BlockSpec(shape, lambda *_: (0,) * len(shape), pipeline_mode=pl.Buffered(1))


def _ffn_kernel(x_ref, nw_ref, wg_ref, wu_ref, wd_ref, fw_ref, o_ref, *, final_norm):
    x = x_ref[...]
    xn = _rms(x, nw_ref[...]).astype(BF16)
    act = (_silu(_dot(xn, wg_ref[...])) * _dot(xn, wu_ref[...])).astype(BF16)
    h = x + 0.5 * _dot(act, wd_ref[...])
    if final_norm:
        h = _rms(h, fw_ref[...])
    o_ref[...] = h


def _ffn(x, nw, w_gu, w_down, fw, *, tm, final_norm):
    t = x.shape[0]
    once = pl.Buffered(1)
    return pl.pallas_call(
        functools.partial(_ffn_kernel, final_norm=final_norm),
        grid=(t // tm,),
        in_specs=[
            pl.BlockSpec((tm, D_MODEL), lambda i: (i, 0)),
            _resident((1, D_MODEL)),
            pl.BlockSpec((D_MODEL, D_FF), lambda i: (0, 0), pipeline_mode=once),
            pl.BlockSpec((D_MODEL, D_FF), lambda i: (0, 1), pipeline_mode=once),
            _resident((D_FF, D_MODEL)),
            _resident((1, D_MODEL)),
        ],
        out_specs=pl.BlockSpec((tm, D_MODEL), lambda i: (i, 0)),
        out_shape=jax.ShapeDtypeStruct((t, D_MODEL), F32),
        compiler_params=pltpu.CompilerParams(
            dimension_semantics=("parallel",), vmem_limit_bytes=VMEM_LIMIT),
        name="ffn",
    )(x, nw, w_gu, w_gu, w_down, fw)


C_DT = 3 * D_MODEL + D_SSM + D_XBC
C_GATE = C_DT + LANES


def _project(u, w_refs, c0, width):
    w_main, w_dt, w_gate = w_refs
    if c0 < C_DT:
        w = w_main[c0:c0 + width, :]
    elif c0 < C_GATE:
        w = w_dt[c0 - C_DT:c0 - C_DT + width, :]
    else:
        w = w_gate[c0 - C_GATE:c0 - C_GATE + width, :]
    return lax.dot_general(u, w, (((1,), (1,)), ((), ())), preferred_element_type=F32)


def _proj_weight_specs():
    once = pl.Buffered(1)
    return [pl.BlockSpec((C_DT, D_MODEL), lambda *_: (0, 0), pipeline_mode=once),
            pl.BlockSpec((LANES, D_MODEL), lambda *_: (C_DT // LANES, 0), pipeline_mode=once),
            pl.BlockSpec((pl.Element(2 * D_MODEL), pl.Element(D_MODEL)),
                         lambda *_: (C_DT + SSM_HEADS, 0), pipeline_mode=once)]


def _inproj_kernel(h_ref, nw_ref, wm_ref, wdt_ref, wg_ref,
                   ab_ref, ach_ref, zs_ref, xbc_ref, dt_ref, sga_ref, sgb_ref):
    u = _rms(h_ref[...], nw_ref[...]).astype(BF16)

    def proj(c0, width):
        return _project(u, (wm_ref, wdt_ref, wg_ref), c0, width)

    ab_ref[...] = proj(0, D_MODEL).astype(ab_ref.dtype)
    ach_ref[...] = (proj(D_MODEL, D_MODEL) * proj(2 * D_MODEL, D_MODEL)).astype(ach_ref.dtype)
    c0 = 3 * D_MODEL
    for k in range(D_SSM // D_MODEL):
        zs_ref[:, k * D_MODEL:(k + 1) * D_MODEL] = _silu(proj(c0 + k * D_MODEL, D_MODEL)).astype(zs_ref.dtype)
    c0 += D_SSM
    for k in range(D_XBC // D_MODEL):
        xbc_ref[:, k * D_MODEL:(k + 1) * D_MODEL] = proj(c0 + k * D_MODEL, D_MODEL).astype(xbc_ref.dtype)
    c0 += D_XBC
    dt_ref[...] = proj(c0, LANES)
    c0 += LANES
    sga_ref[...] = jax.nn.sigmoid(proj(c0, D_MODEL)).astype(sga_ref.dtype)
    sgb_ref[...] = jax.nn.sigmoid(proj(c0 + D_MODEL, D_MODEL)).astype(sgb_ref.dtype)


def _inproj(h, nw, w_packed, *, tm, act_dtype):
    t = h.shape[0]
    widths = (D_MODEL, D_MODEL, D_SSM, D_XBC, LANES, D_MODEL, D_MODEL)
    dtypes = (act_dtype, act_dtype, act_dtype, act_dtype, F32, act_dtype, act_dtype)
    return pl.pallas_call(
        _inproj_kernel,
        grid=(t // tm,),
        in_specs=[pl.BlockSpec((tm, D_MODEL), lambda i: (i, 0)), _resident((1, D_MODEL))]
        + _proj_weight_specs(),
        out_specs=[pl.BlockSpec((tm, w), lambda i: (i, 0)) for w in widths],
        out_shape=[jax.ShapeDtypeStruct((t, w), d) for w, d in zip(widths, dtypes)],
        compiler_params=pltpu.CompilerParams(
            dimension_semantics=("parallel",), vmem_limit_bytes=VMEM_LIMIT),
        name="inproj",
    )(h, nw, *w_packed)


COL_BLOCK = 512
N_PARK = 8


def _inproj_conv_kernel(h_ref, nw_ref, wm_ref, wdt_ref, wg_ref, caw_ref, cxw_ref, cxb_ref,
                        ca0_ref, cx0_ref,
                        ya_ref, zs_ref, xc_ref, dt_ref, sga_ref, sgb_ref, cao_ref, cxo_ref,
                        ca_carry, cx_carry, *stages, tm, seq_tiles, valid_len):
    i = pl.program_id(0)

    @pl.when(i % seq_tiles == 0)
    def _():
        ca_carry[...] = ca0_ref[...]
        cx_carry[...] = cx0_ref[...]

    u = _rms(h_ref[...], nw_ref[...]).astype(BF16)

    def proj(c0, width=COL_BLOCK):
        return _project(u, (wm_ref, wdt_ref, wg_ref), c0, width)

    n8 = tm // SUBLANES
    sublane = lax.broadcasted_iota(jnp.int32, (n8, SUBLANES, COL_BLOCK), 1)

    def causal_conv(x, carry_ref, w_ref, taps, cols):
        cur = x.reshape(n8, SUBLANES, COL_BLOCK)
        prev = jnp.concatenate([carry_ref[:, cols], x[:tm - SUBLANES]], axis=0).reshape(cur.shape)
        acc = w_ref[taps - 1:taps, cols] * x
        for d in range(1, taps):
            merged = jnp.where(sublane < SUBLANES - d, cur, prev)
            acc += w_ref[taps - 1 - d:taps - d, cols] * pltpu.roll(merged, d, axis=1).reshape(x.shape)
        carry_ref[:, cols] = x[valid_len - SUBLANES:valid_len]
        return acc

    c_z, c_x, c_dt = 3 * D_MODEL, 3 * D_MODEL + D_SSM, 3 * D_MODEL + D_SSM + D_XBC
    c_g = c_dt + LANES

    def tail_a(k, cols, prj):
        conv = causal_conv(prj[1] * prj[2], ca_carry, caw_ref, CONV_A_TAPS, cols)
        ya_ref[:, cols] = (prj[0] * conv).astype(ya_ref.dtype)

    def tail_z(k, cols, prj):
        zs_ref[:, cols] = _silu(prj[0]).astype(zs_ref.dtype)

    def tail_x(k, cols, prj):
        conv = causal_conv(prj[0], cx_carry, cxw_ref, CONV_X_TAPS, cols)
        xc_ref[:, cols] = _silu(conv + cxb_ref[:, cols]).astype(xc_ref.dtype)

    def tail_g(k, cols, prj):
        sga_ref[:, cols] = jax.nn.sigmoid(prj[0]).astype(sga_ref.dtype)
        sgb_ref[:, cols] = jax.nn.sigmoid(prj[1]).astype(sgb_ref.dtype)

    def blocks(tail, width, starts):
        return [(tail, k, [c + k * COL_BLOCK for c in starts]) for k in range(width // COL_BLOCK)]

    light = blocks(tail_z, D_SSM, [c_z])
    for n, (blk_a, blk_g) in enumerate(zip(blocks(tail_a, D_MODEL, [0, D_MODEL, 2 * D_MODEL]),
                                           blocks(tail_g, D_MODEL, [c_g, c_g + D_MODEL]))):
        light.insert(3 * n, blk_a)
        light.insert(3 * n + 2, blk_g)
    heavy = blocks(tail_x, D_XBC, [c_x])
    order = heavy + light
    park = jnp.minimum(i, 0)
    slot = 0
    for tail, k, starts in order:
        parked = []
        for c in starts:
            stages[slot % len(stages)][park] = proj(c)
            parked.append(stages[slot % len(stages)])
            slot += 1
        tail(k, slice(k * COL_BLOCK, (k + 1) * COL_BLOCK), [s[park] for s in parked])
    dt_ref[...] = proj(c_dt, LANES)
    cao_ref[0] = ca_carry[...]
    cxo_ref[0] = cx_carry[...]


def _inproj_conv(h, nw, w_packed, p, ca0, cx0, *, tm, seq_tiles, valid_len):
    t = h.shape[0]
    n_seq = t // (tm * seq_tiles)
    widths = (D_MODEL, D_SSM, D_XBC, LANES, D_MODEL, D_MODEL)
    dtypes = (BF16, BF16, BF16, F32, BF16, BF16)
    tail = lambda w: pl.BlockSpec((1, SUBLANES, w), lambda i: (i // seq_tiles, 0, 0))
    return pl.pallas_call(
        functools.partial(_inproj_conv_kernel, tm=tm, seq_tiles=seq_tiles, valid_len=valid_len),
        grid=(t // tm,),
        in_specs=[pl.BlockSpec((tm, D_MODEL), lambda i: (i, 0)), _resident((1, D_MODEL))]
        + _proj_weight_specs()
        + [_resident((CONV_A_TAPS, D_MODEL)), _resident((CONV_X_TAPS, D_XBC)),
           _resident((1, D_XBC)),
           _resident((SUBLANES, D_MODEL)), _resident((SUBLANES, D_XBC))],
        out_specs=[pl.BlockSpec((tm, w), lambda i: (i, 0)) for w in widths]
        + [tail(D_MODEL), tail(D_XBC)],
        out_shape=[jax.ShapeDtypeStruct((t, w), d) for w, d in zip(widths, dtypes)]
        + [jax.ShapeDtypeStruct((n_seq, SUBLANES, D_MODEL), F32),
           jax.ShapeDtypeStruct((n_seq, SUBLANES, D_XBC), F32)],
        scratch_shapes=[pltpu.VMEM((SUBLANES, D_MODEL), F32), pltpu.VMEM((SUBLANES, D_XBC), F32),
                        ] + [pltpu.VMEM((1, tm, COL_BLOCK), F32)] * N_PARK,
        compiler_params=pltpu.CompilerParams(
            dimension_semantics=("arbitrary",), vmem_limit_bytes=VMEM_LIMIT),
        name="inproj_conv",
    )(h, nw, *w_packed, p["conv_a_w"], p["ssm_conv_w"], p["ssm_conv_b"], ca0, cx0)


def _outproj_kernel(ya_ref, yb_ref, sga_ref, sgb_ref, h_ref, wa_ref, wb_ref, wo_ref, o_ref):
    y_a = _dot(ya_ref[...].astype(BF16), wa_ref[...])
    y_b = _dot(yb_ref[...].astype(BF16), wb_ref[...])
    merged = sga_ref[...].astype(F32) * y_a + sgb_ref[...].astype(F32) * y_b
    o_ref[...] = h_ref[...] + _dot(merged.astype(BF16), wo_ref[...])


def _outproj(ya, yb, sga, sgb, h, wa, wb, wo, *, tm):
    t = h.shape[0]
    row = lambda w: pl.BlockSpec((tm, w), lambda i: (i, 0))
    return pl.pallas_call(
        _outproj_kernel,
        grid=(t // tm,),
        in_specs=[row(D_MODEL), row(D_SSM), row(D_MODEL), row(D_MODEL), row(D_MODEL),
                  _resident((D_MODEL, D_MODEL)), _resident((D_SSM, D_MODEL)),
                  _resident((D_MODEL, D_MODEL))],
        out_specs=row(D_MODEL),
        out_shape=jax.ShapeDtypeStruct((t, D_MODEL), F32),
        compiler_params=pltpu.CompilerParams(
            dimension_semantics=("parallel",), vmem_limit_bytes=VMEM_LIMIT),
        name="outproj",
    )(ya, yb, sga, sgb, h, wa, wb, wo)


def _gated_group_norm(y, xs, zs, dskip, normw):
    y = (y + dskip * xs) * zs
    gw = D_SSM // SSM_GROUPS
    parts = []
    for g in range(SSM_GROUPS):
        yg = y[:, g * gw:(g + 1) * gw]
        parts.append(yg * lax.rsqrt(jnp.mean(yg * yg, axis=-1, keepdims=True) + EPS))
    return jnp.concatenate(parts, axis=-1) * normw


GROUP_W = HEADS_PER_GROUP * SSM_HEAD_DIM
LOG2E = 1.4426950408889634


def _ssd_masks():
    q = CHUNK
    rows = lax.broadcasted_iota(jnp.int32, (q, q), 0)
    cols = lax.broadcasted_iota(jnp.int32, (q, q), 1)
    causal = rows >= cols
    first_head = lax.broadcasted_iota(jnp.int32, (q, LANES), 1) < SSM_HEAD_DIM
    return causal, causal.astype(BF16), first_head


def _ssd_chunk(tok, r0, masks, zs_ref, xc_ref, dt_ref, dtb_ref, alog_ref, dskip_ref, normw_ref,
               expand_ref, hT, ybuf, valid_len):
    q = CHUNK
    causal, tri, first_head = masks
    a_neg = -jnp.exp(alog_ref[...])
    dt = _softplus(dt_ref[tok, :] + dtb_ref[...])
    if valid_len is not None:
        t_idx = r0 + lax.broadcasted_iota(jnp.int32, (q, LANES), 0)
        dt = jnp.where(t_idx < valid_len, dt, 0.0)
    hi, mid, lo = _split3(dt * a_neg)
    acs3 = _dot(tri, jnp.concatenate([hi, mid, lo], axis=1))
    acs = acs3[:, :LANES] + acs3[:, LANES:2 * LANES] + acs3[:, 2 * LANES:]
    acs2 = acs * LOG2E
    row_t = (acs2 - jnp.log2(dt)).T
    acs_end = acs2[q - 1:q, :]
    cd_hi, cd_lo = _split2(jnp.broadcast_to(jnp.exp2(acs_end), (SUBLANES, LANES)))
    stack = jnp.concatenate(
        [jnp.exp2(acs2).astype(BF16), (jnp.exp2(acs_end - acs2) * dt).astype(BF16),
         cd_hi, cd_lo], axis=0)

    for g in range(SSM_GROUPS):
        gcols = slice(g * GROUP_W, (g + 1) * GROUP_W)
        ex = _dot(stack, expand_ref[:, gcols])
        state_decay = ex[2 * q:2 * q + 1] + ex[2 * q + SUBLANES:2 * q + SUBLANES + 1]
        b_t = xc_ref[tok, D_SSM + g * SSM_STATE:D_SSM + (g + 1) * SSM_STATE].astype(F32).T.astype(BF16)
        c_bf = xc_ref[tok, D_SSM + D_BC + g * SSM_STATE:D_SSM + D_BC + (g + 1) * SSM_STATE]
        cb = _dot(c_bf, b_t)
        h_g = hT[g]
        y_g = ex[0:q] * _dot(c_bf, h_g.astype(BF16))
        xw_g = xc_ref[tok, gcols] * ex[q:2 * q].astype(BF16)
        hT[g] = h_g * state_decay + _dot(b_t, xw_g)
        for jp in range(HEADS_PER_GROUP // 2):
            pair = g * (HEADS_PER_GROUP // 2) + jp
            lanes = slice(pair * LANES, (pair + 1) * LANES)
            x_pair = xc_ref[tok, lanes]
            zero = jnp.zeros_like(x_pair)
            x2 = jnp.concatenate([jnp.where(first_head, x_pair, zero),
                                  jnp.where(first_head, zero, x_pair)], axis=0)
            m2 = []
            for e in range(2):
                hd = 2 * pair + e
                acs_col = jnp.broadcast_to(acs2[:, hd:hd + 1], (q, q))
                decay = jnp.exp2(jnp.where(causal, acs_col - row_t[hd:hd + 1, :], -jnp.inf))
                m2.append((cb * decay).astype(BF16))
            ybuf[:, lanes] = y_g[:, jp * LANES:(jp + 1) * LANES] + _dot(
                jnp.concatenate(m2, axis=1), x2)

    return _gated_group_norm(ybuf[...], xc_ref[tok, :D_SSM].astype(F32),
                             zs_ref[tok, :].astype(F32), dskip_ref[...], normw_ref[...])


def _write_final_state(hT, hTo_ref, ho_ref):
    hTo_ref[0] = hT[...]
    for g in range(SSM_GROUPS):
        ho_ref[0, g] = hT[g].T


def _mixer_kernel(zs_ref, xc_ref, dt_ref, dtb_ref, alog_ref, dskip_ref, normw_ref, expand_ref,
                  h0_ref, yb_ref, hTo_ref, ho_ref, hT, ybuf, *, tb, valid_len):
    i = pl.program_id(1)

    @pl.when(i == 0)
    def _():
        hT[...] = h0_ref[...]

    masks = _ssd_masks()

    def chunk_body(c, carry):
        r0 = pl.multiple_of(c * CHUNK, CHUNK)
        tok = pl.ds(r0, CHUNK)
        yn = _ssd_chunk(tok, r0, masks, zs_ref, xc_ref, dt_ref, dtb_ref, alog_ref, dskip_ref,
                        normw_ref, expand_ref, hT, ybuf, valid_len if valid_len < tb else None)
        yb_ref[tok, :] = yn.astype(yb_ref.dtype)
        return carry

    lax.fori_loop(0, tb // CHUNK, chunk_body, 0)

    @pl.when(i == pl.num_programs(1) - 1)
    def _():
        _write_final_state(hT, hTo_ref, ho_ref)


HALF = 2 * CHUNK


def _mixer_outproj_kernel(zs_ref, xc_ref, dt_ref, ya_ref, sga_ref, sgb_ref, h_ref,
                          dtb_ref, alog_ref, dskip_ref, normw_ref, expand_ref, h0_ref,
                          wa_ref, wb_ref, wo_ref,
                          o_ref, hTo_ref, ho_ref, hT, ybuf, yb_scr, ya_scr, *, tb):
    i = pl.program_id(1)

    @pl.when(i == 0)
    def _():
        hT[...] = h0_ref[...]

    masks = _ssd_masks()
    ya_scr[...] = _dot(ya_ref[...], wa_ref[...])

    def project(rows):
        y_b = _dot(yb_scr[rows, :], wb_ref[...])
        merged = (sga_ref[rows, :].astype(F32) * ya_scr[rows, :]
                  + sgb_ref[rows, :].astype(F32) * y_b)
        o_ref[rows, :] = h_ref[rows, :] + _dot(merged.astype(BF16), wo_ref[...])

    for c in range(tb // CHUNK):
        r0 = c * CHUNK
        tok = slice(r0, r0 + CHUNK)
        yn = _ssd_chunk(tok, r0, masks, zs_ref, xc_ref, dt_ref, dtb_ref, alog_ref, dskip_ref,
                        normw_ref, expand_ref, hT, ybuf, None)
        yb_scr[tok, :] = yn.astype(BF16)
        if (r0 + CHUNK) % HALF == 0:
            project(slice(r0 + CHUNK - HALF, r0 + CHUNK))

    @pl.when(i == pl.num_programs(1) - 1)
    def _():
        _write_final_state(hT, hTo_ref, ho_ref)


def _mixer_outproj(zs, xc, dt, ya, sga, sgb, h, p, expand, h0, wa, wb, wo, *, batch, seqlen, tb):
    nt = seqlen // tb
    row = lambda w: pl.BlockSpec((tb, w), lambda b, i: (b * nt + i, 0))
    full = lambda shape: pl.BlockSpec(shape, lambda b, i: (0,) * len(shape))
    once = lambda shape: pl.BlockSpec(shape, lambda b, i: (0,) * len(shape), pipeline_mode=pl.Buffered(1))
    per_b = lambda shape: pl.BlockSpec((1,) + shape, lambda b, i: (b,) + (0,) * len(shape))
    return pl.pallas_call(
        functools.partial(_mixer_outproj_kernel, tb=tb),
        grid=(batch, nt),
        in_specs=[row(D_SSM), row(D_XBC), row(LANES), row(D_MODEL), row(D_MODEL), row(D_MODEL),
                  row(D_MODEL),
                  full((1, LANES)), full((1, LANES)), full((1, D_SSM)), full((1, D_SSM)),
                  once((LANES, D_SSM)), full((SSM_GROUPS, SSM_STATE, GROUP_W)),
                  once((D_MODEL, D_MODEL)), once((D_SSM, D_MODEL)), once((D_MODEL, D_MODEL))],
        out_specs=[row(D_MODEL), per_b((SSM_GROUPS, SSM_STATE, GROUP_W)),
                   per_b((SSM_GROUPS, GROUP_W, SSM_STATE))],
        out_shape=[jax.ShapeDtypeStruct((batch * seqlen, D_MODEL), F32),
                   jax.ShapeDtypeStruct((batch, SSM_GROUPS, SSM_STATE, GROUP_W), F32),
                   jax.ShapeDtypeStruct((batch, SSM_GROUPS, GROUP_W, SSM_STATE), F32)],
        scratch_shapes=[pltpu.VMEM((SSM_GROUPS, SSM_STATE, GROUP_W), F32),
                        pltpu.VMEM((CHUNK, D_SSM), F32),
                        pltpu.VMEM((tb, D_SSM), BF16),
                        pltpu.VMEM((tb, D_MODEL), F32)],
        compiler_params=pltpu.CompilerParams(
            dimension_semantics=("arbitrary", "arbitrary"), vmem_limit_bytes=VMEM_LIMIT),
        name="mixer_outproj",
    )(zs, xc, dt, ya, sga, sgb, h, p["dt_bias"], p["a_log"], p["d_skip"], p["ssm_norm_w"], expand,
      h0, wa, wb, wo)


def _mixer(zs, xc, dt, p, expand, h0, *, batch, seqlen, tb, valid_len):
    nt = seqlen // tb
    row = lambda w: pl.BlockSpec((tb, w), lambda b, i: (b * nt + i, 0))
    full = lambda shape: pl.BlockSpec(shape, lambda b, i: (0,) * len(shape))
    per_b = lambda shape: pl.BlockSpec((1,) + shape, lambda b, i: (b,) + (0,) * len(shape))
    return pl.pallas_call(
        functools.partial(_mixer_kernel, tb=tb, valid_len=valid_len),
        grid=(batch, nt),
        in_specs=[row(D_SSM), row(D_XBC), row(LANES),
                  full((1, LANES)), full((1, LANES)), full((1, D_SSM)), full((1, D_SSM)),
                  full((LANES, D_SSM)), full((SSM_GROUPS, SSM_STATE, GROUP_W))],
        out_specs=[row(D_SSM), per_b((SSM_GROUPS, SSM_STATE, GROUP_W)),
                   per_b((SSM_GROUPS, GROUP_W, SSM_STATE))],
        out_shape=[jax.ShapeDtypeStruct((batch * seqlen, D_SSM), BF16),
                   jax.ShapeDtypeStruct((batch, SSM_GROUPS, SSM_STATE, GROUP_W), F32),
                   jax.ShapeDtypeStruct((batch, SSM_GROUPS, GROUP_W, SSM_STATE), F32)],
        scratch_shapes=[pltpu.VMEM((SSM_GROUPS, SSM_STATE, GROUP_W), F32),
                        pltpu.VMEM((CHUNK, D_SSM), F32)],
        compiler_params=pltpu.CompilerParams(
            dimension_semantics=("arbitrary", "arbitrary"), vmem_limit_bytes=VMEM_LIMIT),
        name="mixer",
    )(zs, xc, dt, p["dt_bias"], p["a_log"], p["d_skip"], p["ssm_norm_w"], expand, h0)


SEQ_BLOCK = SUBLANES


def _decode_mixer_kernel(ab_ref, ach_ref, zs_ref, xbc_ref, dt_ref, sa_ref, sx_ref, h0_ref,
                         caw_ref, cxw_ref, cxb_ref, dtb_ref, alog_ref, dskip_ref, normw_ref,
                         gsum_ref, expand_ref,
                         ya_ref, yb_ref, sao_ref, sxo_ref, ho_ref,
                         xc_buf, bc_buf, os_buf, xw_buf, cd_buf, y_buf, *, n_tok):
    sb = SEQ_BLOCK

    xp = [sa_ref[k] for k in range(CONV_A_TAPS - 1)] + [ach_ref[t] for t in range(n_tok)]
    for t in range(n_tok):
        conv = caw_ref[0:1, :] * xp[t]
        for k in range(1, CONV_A_TAPS):
            conv += caw_ref[k:k + 1, :] * xp[t + k]
        ya_ref[t] = ab_ref[t] * conv
    for k in range(CONV_A_TAPS - 1):
        sao_ref[k] = xp[n_tok + k]

    xp = [sx_ref[k] for k in range(CONV_X_TAPS - 1)] + [xbc_ref[t] for t in range(n_tok)]
    for t in range(n_tok):
        conv = cxw_ref[0:1, :] * xp[t]
        for k in range(1, CONV_X_TAPS):
            conv += cxw_ref[k:k + 1, :] * xp[t + k]
        xc_buf[t * sb:(t + 1) * sb, :] = _silu(conv + cxb_ref[...])
    for k in range(CONV_X_TAPS - 1):
        sxo_ref[k] = xp[n_tok + k]

    a_neg = -jnp.exp(alog_ref[...])
    dts, acs = [], []
    for t in range(n_tok):
        dts.append(_softplus(dt_ref[t] + dtb_ref[...]))
        acs.append(dts[t] * a_neg + (acs[t - 1] if t else 0.0))

    pairs = [(qi, si) for qi in range(n_tok) for si in range(qi + 1)]
    prods = [xc_buf[qi * sb:(qi + 1) * sb, D_SSM + D_BC:] * xc_buf[si * sb:(si + 1) * sb, D_SSM:D_SSM + D_BC]
             for qi, si in pairs]
    p_hi, p_lo = _split2(jnp.concatenate(prods, axis=0))
    cbh = _dot(p_hi, gsum_ref[...]) + _dot(p_lo, gsum_ref[...])
    coef = [jnp.exp(acs[t]) for t in range(n_tok)]
    coef += [cbh[k * sb:(k + 1) * sb] * jnp.exp(acs[qi] - acs[si]) * dts[si]
             for k, (qi, si) in enumerate(pairs)]
    coef += [jnp.exp(acs[n_tok - 1] - acs[s]) * dts[s] for s in range(n_tok)]
    coef.append(jnp.exp(acs[n_tok - 1]))
    c_hi, c_lo = _split2(jnp.concatenate(coef, axis=0))
    coef_x = _dot(c_hi, expand_ref[...]) + _dot(c_lo, expand_ref[...])
    slab = lambda k: coef_x[k * sb:(k + 1) * sb]

    def store_tiles(buf, r0, val):
        for lt in range(val.shape[1] // LANES):
            buf[lt, r0:r0 + val.shape[0], :] = val[:, lt * LANES:(lt + 1) * LANES]

    def load_rows(buf, lt0, n_lt, rows):
        return jnp.concatenate([buf[lt0 + j, rows, :] for j in range(n_lt)], axis=1)

    store_tiles(bc_buf, 0, xc_buf[:, D_SSM:])
    store_tiles(os_buf, 0, coef_x[0:n_tok * sb])
    diag = {pr: slab(n_tok + k) for k, pr in enumerate(pairs)}
    k0 = n_tok + len(pairs)
    for s in range(n_tok):
        store_tiles(xw_buf, s * sb, slab(k0 + s) * xc_buf[s * sb:(s + 1) * sb, :D_SSM])
    cd_hi, cd_lo = _split2(slab(k0 + n_tok))
    store_tiles(cd_buf, 0, cd_hi.astype(F32))
    store_tiles(cd_buf, sb, cd_lo.astype(F32))

    for t in range(n_tok):
        acc = diag[(t, 0)] * xc_buf[0:sb, :D_SSM]
        for s in range(1, t + 1):
            acc += diag[(t, s)] * xc_buf[s * sb:(s + 1) * sb, :D_SSM]
        store_tiles(y_buf, t * sb, acc)

    gt = GROUP_W // LANES
    k_rows = 2 * SUBLANES
    krow = lax.broadcasted_iota(jnp.int32, (k_rows, SSM_STATE), 0)
    ones_rows = jnp.where((krow >= n_tok) & (krow < n_tok + 2), 1.0, 0.0).astype(BF16)
    pad_lhs = jnp.zeros((k_rows - n_tok - 2, GROUP_W), F32)
    pad_rhs = jnp.zeros((k_rows - n_tok, SSM_STATE), F32)

    def seq_body(b, carry):
        tok_rows = pl.ds(b, n_tok, stride=sb)
        for g in range(SSM_GROUPS):
            b_bg = bc_buf[g, tok_rows, :]
            c_bg = bc_buf[SSM_GROUPS + g, tok_rows, :]
            heads = pl.ds(g * HEADS_PER_GROUP, HEADS_PER_GROUP)
            h_bg = h0_ref[b, heads].reshape(GROUP_W, SSM_STATE)
            y_off = lax.dot_general(c_bg.astype(BF16), h_bg.astype(BF16),
                                    (((1,), (1,)), ((), ())), preferred_element_type=F32)
            y_off = y_off * load_rows(os_buf, g * gt, gt, tok_rows)
            for j in range(gt):
                y_buf[g * gt + j, tok_rows, :] = (y_buf[g * gt + j, tok_rows, :]
                                                  + y_off[:, j * LANES:(j + 1) * LANES])
            lhs = jnp.concatenate(
                [load_rows(xw_buf, g * gt, gt, tok_rows),
                 load_rows(cd_buf, g * gt, gt, pl.ds(b, 2, stride=sb)), pad_lhs],
                axis=0).astype(BF16)
            rhs = jnp.concatenate(
                [jnp.concatenate([b_bg, pad_rhs], axis=0).astype(BF16), ones_rows], axis=1)
            upd = lax.dot_general(lhs, rhs, (((0,), (0,)), ((), ())), preferred_element_type=F32)
            new = h_bg * upd[:, SSM_STATE:] + upd[:, :SSM_STATE]
            ho_ref[b, heads] = new.reshape(HEADS_PER_GROUP, SSM_HEAD_DIM, SSM_STATE)
        return carry

    lax.fori_loop(0, sb, seq_body, 0, unroll=2)

    xs_all = xc_buf[:, :D_SSM]
    zs_all = jnp.concatenate([zs_ref[t] for t in range(n_tok)], axis=0)
    y_all = load_rows(y_buf, 0, D_SSM // LANES, slice(None))
    yn = _gated_group_norm(y_all, xs_all, zs_all, dskip_ref[...], normw_ref[...])
    for t in range(n_tok):
        yb_ref[t] = yn[t * sb:(t + 1) * sb]


def _decode_mixer(ab, ach, zs, xbc, dt, sa, sx, h0, p, gsum, expand, *, n_tok, n_seq):
    sb = SEQ_BLOCK
    tok = lambda w: pl.BlockSpec((n_tok, sb, w), lambda i: (0, i, 0))
    st = lambda k, w: pl.BlockSpec((k, sb, w), lambda i: (0, i, 0))
    hblk = pl.BlockSpec((sb, SSM_HEADS, SSM_HEAD_DIM, SSM_STATE), lambda i: (i, 0, 0, 0))
    full = lambda shape: pl.BlockSpec(shape, lambda i: (0,) * len(shape))
    return pl.pallas_call(
        functools.partial(_decode_mixer_kernel, n_tok=n_tok),
        grid=(n_seq // sb,),
        in_specs=[tok(D_MODEL), tok(D_MODEL), tok(D_SSM), tok(D_XBC), tok(LANES),
                  st(CONV_A_TAPS - 1, D_MODEL), st(CONV_X_TAPS - 1, D_XBC), hblk,
                  full((CONV_A_TAPS, D_MODEL)), full((CONV_X_TAPS, D_XBC)), full((1, D_XBC)),
                  full((1, LANES)), full((1, LANES)), full((1, D_SSM)), full((1, D_SSM)),
                  full((D_BC, LANES)), full((LANES, D_SSM))],
        out_specs=[tok(D_MODEL), tok(D_SSM), st(CONV_A_TAPS - 1, D_MODEL),
                   st(CONV_X_TAPS - 1, D_XBC), hblk],
        out_shape=[jax.ShapeDtypeStruct((n_tok, n_seq, D_MODEL), F32),
                   jax.ShapeDtypeStruct((n_tok, n_seq, D_SSM), F32),
                   jax.ShapeDtypeStruct((CONV_A_TAPS - 1, n_seq, D_MODEL), F32),
                   jax.ShapeDtypeStruct((CONV_X_TAPS - 1, n_seq, D_XBC), F32),
                   jax.ShapeDtypeStruct((n_seq, SSM_HEADS, SSM_HEAD_DIM, SSM_STATE), F32)],
        scratch_shapes=[pltpu.VMEM((n_tok * sb, D_XBC), F32),
                        pltpu.VMEM((2 * D_BC // LANES, n_tok * sb, LANES), F32),
                        pltpu.VMEM((D_SSM // LANES, n_tok * sb, LANES), F32),
                        pltpu.VMEM((D_SSM // LANES, n_tok * sb, LANES), F32),
                        pltpu.VMEM((D_SSM // LANES, 2 * sb, LANES), F32),
                        pltpu.VMEM((D_SSM // LANES, n_tok * sb, LANES), F32)],
        compiler_params=pltpu.CompilerParams(
            dimension_semantics=("parallel",), vmem_limit_bytes=VMEM_LIMIT),
        name="decode_mixer",
    )(ab, ach, zs, xbc, dt, sa, sx, h0, p["conv_a_w"], p["ssm_conv_w"], p["ssm_conv_b"],
      p["dt_bias"], p["a_log"], p["d_skip"], p["ssm_norm_w"], gsum, expand)


def _row_tile(t, cap):
    if t <= cap:
        return t
    best = LANES
    for cand in range(LANES, cap + 1, LANES):
        if t % cand == 0:
            best = cand
    return best


def kernel(x_prompt, x_sample, state_conv_a, state_ssm_conv, state_ssm, meta_tokens, norm_ffn1, ffn1_w_gu, ffn1_w_down, norm_mix, w_in, conv_a_w, w_a_out, ssm_conv_w, ssm_conv_b, dt_bias, a_log, d_skip, ssm_norm_w, w_b_out, w_o, norm_ffn2, ffn2_w_gu, ffn2_w_down, norm_final):
    bsz, seqlen, _ = x_prompt.shape
    n_seq, n_tok, _ = x_sample.shape
    n_dt = SSM_HEADS
    c_dt = 3 * D_MODEL + D_SSM + D_XBC

    w_gu1, w_dn1 = ffn1_w_gu[0].astype(BF16), ffn1_w_down[0].astype(BF16)
    w_gu2, w_dn2 = ffn2_w_gu[0].astype(BF16), ffn2_w_down[0].astype(BF16)
    w_proj = (jnp.swapaxes(w_in[0], 0, 1).astype(BF16),) * 3
    wa, wb, wo = w_a_out[0].astype(BF16), w_b_out[0].astype(BF16), w_o[0].astype(BF16)
    pad_heads = lambda v: jnp.pad(v[0], (0, LANES - n_dt))[None]
    p = {
        "conv_a_w": conv_a_w[0], "ssm_conv_w": ssm_conv_w[0], "ssm_conv_b": ssm_conv_b[0][None],
        "dt_bias": pad_heads(dt_bias), "a_log": pad_heads(a_log),
        "d_skip": jnp.repeat(d_skip[0], SSM_HEAD_DIM)[None], "ssm_norm_w": ssm_norm_w[0][None],
    }
    n1, nm, n2, nf = norm_ffn1[0][None], norm_mix[0][None], norm_ffn2[0][None], norm_final[None]
    eh = lax.broadcasted_iota(jnp.int32, (LANES, D_SSM), 0)
    ec = lax.broadcasted_iota(jnp.int32, (LANES, D_SSM), 1)
    expand = (ec // SSM_HEAD_DIM == eh).astype(BF16)

    def ffn1(x):
        return _ffn(x, n1, w_gu1, w_dn1, nf, tm=_row_tile(x.shape[0], 512), final_norm=False)

    def ffn2(h):
        return _ffn(h, n2, w_gu2, w_dn2, nf, tm=_row_tile(h.shape[0], 512), final_norm=True)

    zeros = lambda *s: jnp.zeros(s, F32)
    h_m = ffn1(jnp.pad(meta_tokens, ((0, CHUNK - N_META), (0, 0))))
    _, zs, xc, dt, _, _, ca_m, cx_m = _inproj_conv(
        h_m, nm, w_proj, p, zeros(SUBLANES, D_MODEL), zeros(SUBLANES, D_XBC),
        tm=CHUNK, seq_tiles=1, valid_len=N_META)
    _, hT_m, _ = _mixer(zs, xc, dt, p, expand, zeros(SSM_GROUPS, SSM_STATE, GROUP_W),
                        batch=1, seqlen=CHUNK, tb=CHUNK, valid_len=N_META)

    h_p = ffn1(x_prompt.reshape(bsz * seqlen, D_MODEL))
    tm = _row_tile(seqlen, 512)
    ya, zs, xc, dt, sga, sgb, ca_p, cx_p = _inproj_conv(
        h_p, nm, w_proj, p, ca_m[0], cx_m[0], tm=tm, seq_tiles=seqlen // tm, valid_len=tm)
    tb = _row_tile(seqlen, 512)
    h2_p, _, h_out_p = _mixer_outproj(zs, xc, dt, ya, sga, sgb, h_p, p, expand, hT_m[0], wa, wb, wo,
                                      batch=bsz, seqlen=seqlen, tb=tb)
    y_prompt = ffn2(h2_p).reshape(bsz, seqlen, D_MODEL)
    prompt_conv_a = ca_p[None, :, SUBLANES - (CONV_A_TAPS - 1):, :]
    prompt_ssm_conv = cx_p[None, :, SUBLANES - (CONV_X_TAPS - 1):, :]
    prompt_ssm = h_out_p.reshape(1, bsz, SSM_HEADS, SSM_HEAD_DIM, SSM_STATE)

    x_s = jnp.swapaxes(x_sample, 0, 1).reshape(n_tok * n_seq, D_MODEL)
    h_s = ffn1(x_s)
    ab, ach, zs, xbc, dt, sga, sgb = _inproj(h_s, nm, w_proj, tm=_row_tile(n_tok * n_seq, 256), act_dtype=F32)
    tsf = lambda a: a.reshape(n_tok, n_seq, a.shape[-1])
    hh = lax.broadcasted_iota(jnp.int32, (D_BC, LANES), 1)
    kk = lax.broadcasted_iota(jnp.int32, (D_BC, LANES), 0)
    gsum = ((hh < SSM_HEADS) & (kk // SSM_STATE == hh // HEADS_PER_GROUP)).astype(BF16)
    ya, yb, sa_new, sx_new, sample_ssm = _decode_mixer(
        tsf(ab), tsf(ach), tsf(zs), tsf(xbc), tsf(dt),
        jnp.swapaxes(state_conv_a[0], 0, 1), jnp.swapaxes(state_ssm_conv[0], 0, 1), state_ssm[0],
        p, gsum, expand, n_tok=n_tok, n_seq=n_seq)
    flat = lambda a: a.reshape(n_tok * n_seq, a.shape[-1])
    y_s = ffn2(_outproj(flat(ya), flat(yb), sga, sgb, h_s, wa, wb, wo, tm=_row_tile(n_tok * n_seq, 512)))
    y_sample = jnp.swapaxes(y_s.reshape(n_tok, n_seq, D_MODEL), 0, 1)
    sample_conv_a = jnp.swapaxes(sa_new, 0, 1)[None]
    sample_ssm_conv = jnp.swapaxes(sx_new, 0, 1)[None]

    return (y_prompt, y_sample, prompt_conv_a, prompt_ssm_conv, prompt_ssm,
            sample_conv_a, sample_ssm_conv, sample_ssm[None])
```

```python
import functools

import jax
import jax.numpy as jnp
from jax import lax
from jax.experimental import pallas as pl
from jax.experimental.pallas import tpu as pltpu

D_MODEL = 1024
D_FF = 2816
D_SSM = 2048
SSM_HEADS = 32
SSM_HEAD_DIM = 64
SSM_GROUPS = 4
HEADS_PER_GROUP = SSM_HEADS // SSM_GROUPS
SSM_STATE = 128
D_BC = SSM_GROUPS * SSM_STATE
D_XBC = D_SSM + 2 * D_BC
CONV_A_TAPS = 3
CONV_X_TAPS = 4
N_META = 16
EPS = 1e-6

LANES = 128
SUBLANES = 8
CHUNK = 128
VMEM_LIMIT = 56 * 1024 * 1024

F32 = jnp.float32
BF16 = jnp.bfloat16


def _dot(a, b):
    return jnp.dot(a, b, preferred_element_type=F32)


def _rms(x, w):
    return x * lax.rsqrt(jnp.mean(x * x, axis=-1, keepdims=True) + EPS) * w


def _silu(x):
    return x * jax.nn.sigmoid(x)


def _softplus(x):
    return jnp.maximum(x, 0.0) + jnp.log1p(jnp.exp(-jnp.abs(x)))


def _split3(x):
    hi = x.astype(BF16)
    r = x - hi.astype(F32)
    mid = r.astype(BF16)
    lo = (r - mid.astype(F32)).astype(BF16)
    return hi, mid, lo


def _split2(x):
    hi = x.astype(BF16)
    lo = (x - hi.astype(F32)).astype(BF16)
    return hi, lo


def _resident(shape):
    return pl.BlockSpec(shape, lambda *_: (0,) * len(shape), pipeline_mode=pl.Buffered(1))


def _ffn_kernel(xprev_ref, xnext_ref, nw_ref, wg_ref, wu_ref, wd_ref, fw_ref, o_ref,
                xn_buf, acc_park, *, final_norm):
    i = pl.program_id(0)
    n_tiles = pl.num_programs(0) - 1

    def finish_previous():
        h = xprev_ref[...] + 0.5 * acc_park[...]
        if final_norm:
            h = _rms(h, fw_ref[...])
        o_ref[...] = h

    @pl.when(i == 0)
    def _():
        xn_buf[0] = _rms(xprev_ref[...], nw_ref[...]).astype(BF16)
        acc_park[...] = jnp.zeros_like(acc_park)

    @pl.when(i < n_tiles)
    def _():
        slot = i % 2
        finish_previous()
        xn = xn_buf[slot]
        act = (_silu(_dot(xn, wg_ref[...])) * _dot(xn, wu_ref[...])).astype(BF16)
        down = _dot(act, wd_ref[...])
        xn_buf[1 - slot] = _rms(xnext_ref[...], nw_ref[...]).astype(BF16)
        acc_park[...] = down

    @pl.when(i == n_tiles)
    def _():
        finish_previous()


def _ffn(x, nw, w_gu, w_down, fw, *, tm, final_norm):
    t = x.shape[0]
    n = t // tm
    once = pl.Buffered(1)
    return pl.pallas_call(
        functools.partial(_ffn_kernel, final_norm=final_norm),
        grid=(n + 1,),
        in_specs=[
            pl.BlockSpec((tm, D_MODEL), lambda i: (jnp.maximum(i - 1, 0), 0)),
            pl.BlockSpec((tm, D_MODEL), lambda i: (jnp.minimum(i + 1, n - 1), 0)),
            _resident((1, D_MODEL)),
            pl.BlockSpec((D_MODEL, D_FF), lambda i: (0, 0), pipeline_mode=once),
            pl.BlockSpec((D_MODEL, D_FF), lambda i: (0, 1), pipeline_mode=once),
            _resident((D_FF, D_MODEL)),
            _resident((1, D_MODEL)),
        ],
        out_specs=pl.BlockSpec((tm, D_MODEL), lambda i: (jnp.maximum(i - 1, 0), 0)),
        out_shape=jax.ShapeDtypeStruct((t, D_MODEL), F32),
        scratch_shapes=[pltpu.VMEM((2, tm, D_MODEL), BF16), pltpu.VMEM((tm, D_MODEL), F32)],
        compiler_params=pltpu.CompilerParams(
            dimension_semantics=("arbitrary",), vmem_limit_bytes=VMEM_LIMIT),
        name="ffn",
    )(x, x, nw, w_gu, w_gu, w_down, fw)


C_DT = 3 * D_MODEL + D_SSM + D_XBC
C_GATE = C_DT + LANES


def _project(u, w_refs, c0, width):
    w_main, w_dt, w_gate = w_refs
    if c0 < C_DT:
        w = w_main[c0:c0 + width, :]
    elif c0 < C_GATE:
        w = w_dt[c0 - C_DT:c0 - C_DT + width, :]
    else:
        w = w_gate[c0 - C_GATE:c0 - C_GATE + width, :]
    return lax.dot_general(u, w, (((1,), (1,)), ((), ())), preferred_element_type=F32)


def _proj_weight_specs():
    once = pl.Buffered(1)
    return [pl.BlockSpec((C_DT, D_MODEL), lambda *_: (0, 0), pipeline_mode=once),
            pl.BlockSpec((LANES, D_MODEL), lambda *_: (C_DT // LANES, 0), pipeline_mode=once),
            pl.BlockSpec((pl.Element(2 * D_MODEL), pl.Element(D_MODEL)),
                         lambda *_: (C_DT + SSM_HEADS, 0), pipeline_mode=once)]


def _inproj_kernel(h_ref, nw_ref, wm_ref, wdt_ref, wg_ref,
                   ab_ref, ach_ref, zs_ref, xbc_ref, dt_ref, sga_ref, sgb_ref):
    u = _rms(h_ref[...], nw_ref[...]).astype(BF16)

    def proj(c0, width):
        return _project(u, (wm_ref, wdt_ref, wg_ref), c0, width)

    ab_ref[...] = proj(0, D_MODEL).astype(ab_ref.dtype)
    ach_ref[...] = (proj(D_MODEL, D_MODEL) * proj(2 * D_MODEL, D_MODEL)).astype(ach_ref.dtype)
    c0 = 3 * D_MODEL
    for k in range(D_SSM // D_MODEL):
        zs_ref[:, k * D_MODEL:(k + 1) * D_MODEL] = _silu(proj(c0 + k * D_MODEL, D_MODEL)).astype(zs_ref.dtype)
    c0 += D_SSM
    for k in range(D_XBC // D_MODEL):
        xbc_ref[:, k * D_MODEL:(k + 1) * D_MODEL] = proj(c0 + k * D_MODEL, D_MODEL).astype(xbc_ref.dtype)
    c0 += D_XBC
    dt_ref[...] = proj(c0, LANES)
    c0 += LANES
    sga_ref[...] = jax.nn.sigmoid(proj(c0, D_MODEL)).astype(sga_ref.dtype)
    sgb_ref[...] = jax.nn.sigmoid(proj(c0 + D_MODEL, D_MODEL)).astype(sgb_ref.dtype)


def _inproj(h, nw, w_packed, *, tm, act_dtype):
    t = h.shape[0]
    widths = (D_MODEL, D_MODEL, D_SSM, D_XBC, LANES, D_MODEL, D_MODEL)
    dtypes = (act_dtype, act_dtype, act_dtype, act_dtype, F32, act_dtype, act_dtype)
    return pl.pallas_call(
        _inproj_kernel,
        grid=(t // tm,),
        in_specs=[pl.BlockSpec((tm, D_MODEL), lambda i: (i, 0)), _resident((1, D_MODEL))]
        + _proj_weight_specs(),
        out_specs=[pl.BlockSpec((tm, w), lambda i: (i, 0)) for w in widths],
        out_shape=[jax.ShapeDtypeStruct((t, w), d) for w, d in zip(widths, dtypes)],
        compiler_params=pltpu.CompilerParams(
            dimension_semantics=("parallel",), vmem_limit_bytes=VMEM_LIMIT),
        name="inproj",
    )(h, nw, *w_packed)


COL_BLOCK = 512
N_PARK = 8


def _inproj_conv_kernel(h_ref, nw_ref, wm_ref, wdt_ref, wg_ref, caw_ref, cxw_ref, cxb_ref,
                        ca0_ref, cx0_ref,
                        ya_ref, zs_ref, xc_ref, dt_ref, sga_ref, sgb_ref, cao_ref, cxo_ref,
                        ca_carry, cx_carry, *stages, tm, seq_tiles, valid_len):
    i = pl.program_id(0)

    @pl.when(i % seq_tiles == 0)
    def _():
        ca_carry[...] = ca0_ref[...]
        cx_carry[...] = cx0_ref[...]

    u = _rms(h_ref[...], nw_ref[...]).astype(BF16)

    def proj(c0, width=COL_BLOCK):
        return _project(u, (wm_ref, wdt_ref, wg_ref), c0, width)

    n8 = tm // SUBLANES
    sublane = lax.broadcasted_iota(jnp.int32, (n8, SUBLANES, COL_BLOCK), 1)

    def causal_conv(x, carry_ref, w_ref, taps, cols):
        cur = x.reshape(n8, SUBLANES, COL_BLOCK)
        prev = jnp.concatenate([carry_ref[:, cols], x[:tm - SUBLANES]], axis=0).reshape(cur.shape)
        acc = w_ref[taps - 1:taps, cols] * x
        for d in range(1, taps):
            merged = jnp.where(sublane < SUBLANES - d, cur, prev)
            acc += w_ref[taps - 1 - d:taps - d, cols] * pltpu.roll(merged, d, axis=1).reshape(x.shape)
        carry_ref[:, cols] = x[valid_len - SUBLANES:valid_len]
        return acc

    c_z, c_x, c_dt = 3 * D_MODEL, 3 * D_MODEL + D_SSM, 3 * D_MODEL + D_SSM + D_XBC
    c_g = c_dt + LANES

    def tail_a(k, cols, prj):
        conv = causal_conv(prj[1] * prj[2], ca_carry, caw_ref, CONV_A_TAPS, cols)
        ya_ref[:, cols] = (prj[0] * conv).astype(ya_ref.dtype)

    def tail_z(k, cols, prj):
        zs_ref[:, cols] = _silu(prj[0]).astype(zs_ref.dtype)

    def tail_x(k, cols, prj):
        conv = causal_conv(prj[0], cx_carry, cxw_ref, CONV_X_TAPS, cols)
        xc_ref[:, cols] = _silu(conv + cxb_ref[:, cols]).astype(xc_ref.dtype)

    def tail_g(k, cols, prj):
        sga_ref[:, cols] = jax.nn.sigmoid(prj[0]).astype(sga_ref.dtype)
        sgb_ref[:, cols] = jax.nn.sigmoid(prj[1]).astype(sgb_ref.dtype)

    def blocks(tail, width, starts):
        return [(tail, k, [c + k * COL_BLOCK for c in starts]) for k in range(width // COL_BLOCK)]

    light = blocks(tail_z, D_SSM, [c_z])
    for n, (blk_a, blk_g) in enumerate(zip(blocks(tail_a, D_MODEL, [0, D_MODEL, 2 * D_MODEL]),
                                           blocks(tail_g, D_MODEL, [c_g, c_g + D_MODEL]))):
        light.insert(3 * n, blk_a)
        light.insert(3 * n + 2, blk_g)
    heavy = blocks(tail_x, D_XBC, [c_x])
    order = heavy + light
    park = jnp.minimum(i, 0)
    slot = 0
    for tail, k, starts in order:
        parked = []
        for c in starts:
            stages[slot % len(stages)][park] = proj(c)
            parked.append(stages[slot % len(stages)])
            slot += 1
        tail(k, slice(k * COL_BLOCK, (k + 1) * COL_BLOCK), [s[park] for s in parked])
    dt_ref[...] = proj(c_dt, LANES)
    cao_ref[0] = ca_carry[...]
    cxo_ref[0] = cx_carry[...]


def _inproj_conv(h, nw, w_packed, p, ca0, cx0, *, tm, seq_tiles, valid_len):
    t = h.shape[0]
    n_seq = t // (tm * seq_tiles)
    widths = (D_MODEL, D_SSM, D_XBC, LANES, D_MODEL, D_MODEL)
    dtypes = (BF16, BF16, BF16, F32, BF16, BF16)
    tail = lambda w: pl.BlockSpec((1, SUBLANES, w), lambda i: (i // seq_tiles, 0, 0))
    return pl.pallas_call(
        functools.partial(_inproj_conv_kernel, tm=tm, seq_tiles=seq_tiles, valid_len=valid_len),
        grid=(t // tm,),
        in_specs=[pl.BlockSpec((tm, D_MODEL), lambda i: (i, 0)), _resident((1, D_MODEL))]
        + _proj_weight_specs()
        + [_resident((CONV_A_TAPS, D_MODEL)), _resident((CONV_X_TAPS, D_XBC)),
           _resident((1, D_XBC)),
           _resident((SUBLANES, D_MODEL)), _resident((SUBLANES, D_XBC))],
        out_specs=[pl.BlockSpec((tm, w), lambda i: (i, 0)) for w in widths]
        + [tail(D_MODEL), tail(D_XBC)],
        out_shape=[jax.ShapeDtypeStruct((t, w), d) for w, d in zip(widths, dtypes)]
        + [jax.ShapeDtypeStruct((n_seq, SUBLANES, D_MODEL), F32),
           jax.ShapeDtypeStruct((n_seq, SUBLANES, D_XBC), F32)],
        scratch_shapes=[pltpu.VMEM((SUBLANES, D_MODEL), F32), pltpu.VMEM((SUBLANES, D_XBC), F32),
                        ] + [pltpu.VMEM((1, tm, COL_BLOCK), F32)] * N_PARK,
        compiler_params=pltpu.CompilerParams(
            dimension_semantics=("arbitrary",), vmem_limit_bytes=VMEM_LIMIT),
        name="inproj_conv",
    )(h, nw, *w_packed, p["conv_a_w"], p["ssm_conv_w"], p["ssm_conv_b"], ca0, cx0)


def _outproj_kernel(ya_ref, yb_ref, sga_ref, sgb_ref, h_ref, wa_ref, wb_ref, wo_ref, o_ref):
    y_a = _dot(ya_ref[...].astype(BF16), wa_ref[...])
    y_b = _dot(yb_ref[...].astype(BF16), wb_ref[...])
    merged = sga_ref[...].astype(F32) * y_a + sgb_ref[...].astype(F32) * y_b
    o_ref[...] = h_ref[...] + _dot(merged.astype(BF16), wo_ref[...])


def _outproj(ya, yb, sga, sgb, h, wa, wb, wo, *, tm):
    t = h.shape[0]
    row = lambda w: pl.BlockSpec((tm, w), lambda i: (i, 0))
    return pl.pallas_call(
        _outproj_kernel,
        grid=(t // tm,),
        in_specs=[row(D_MODEL), row(D_SSM), row(D_MODEL), row(D_MODEL), row(D_MODEL),
                  _resident((D_MODEL, D_MODEL)), _resident((D_SSM, D_MODEL)),
                  _resident((D_MODEL, D_MODEL))],
        out_specs=row(D_MODEL),
        out_shape=jax.ShapeDtypeStruct((t, D_MODEL), F32),
        compiler_params=pltpu.CompilerParams(
            dimension_semantics=("parallel",), vmem_limit_bytes=VMEM_LIMIT),
        name="outproj",
    )(ya, yb, sga, sgb, h, wa, wb, wo)


def _gated_group_norm(y, xs, zs, dskip, normw):
    y = (y + dskip * xs) * zs
    gw = D_SSM // SSM_GROUPS
    parts = []
    for g in range(SSM_GROUPS):
        yg = y[:, g * gw:(g + 1) * gw]
        parts.append(yg * lax.rsqrt(jnp.mean(yg * yg, axis=-1, keepdims=True) + EPS))
    return jnp.concatenate(parts, axis=-1) * normw


GROUP_W = HEADS_PER_GROUP * SSM_HEAD_DIM
LOG2E = 1.4426950408889634


def _ssd_masks():
    q = CHUNK
    rows = lax.broadcasted_iota(jnp.int32, (q, q), 0)
    cols = lax.broadcasted_iota(jnp.int32, (q, q), 1)
    causal = rows >= cols
    first_head = lax.broadcasted_iota(jnp.int32, (q, LANES), 1) < SSM_HEAD_DIM
    return causal, causal.astype(BF16), first_head


def _ssd_chunk(tok, r0, masks, zs_ref, xc_ref, dt_ref, dtb_ref, alog_ref, dskip_ref, normw_ref,
               expand_ref, hT, ybuf, valid_len):
    q = CHUNK
    causal, tri, first_head = masks
    a_neg = -jnp.exp(alog_ref[...])
    dt = _softplus(dt_ref[tok, :] + dtb_ref[...])
    if valid_len is not None:
        t_idx = r0 + lax.broadcasted_iota(jnp.int32, (q, LANES), 0)
        dt = jnp.where(t_idx < valid_len, dt, 0.0)
    hi, mid, lo = _split3(dt * a_neg)
    acs3 = _dot(tri, jnp.concatenate([hi, mid, lo], axis=1))
    acs = acs3[:, :LANES] + acs3[:, LANES:2 * LANES] + acs3[:, 2 * LANES:]
    acs2 = acs * LOG2E
    row_t = (acs2 - jnp.log2(dt)).T
    acs_end = acs2[q - 1:q, :]
    cd_hi, cd_lo = _split2(jnp.broadcast_to(jnp.exp2(acs_end), (SUBLANES, LANES)))
    stack = jnp.concatenate(
        [jnp.exp2(acs2).astype(BF16), (jnp.exp2(acs_end - acs2) * dt).astype(BF16),
         cd_hi, cd_lo], axis=0)

    for g in range(SSM_GROUPS):
        gcols = slice(g * GROUP_W, (g + 1) * GROUP_W)
        ex = _dot(stack, expand_ref[:, gcols])
        state_decay = ex[2 * q:2 * q + 1] + ex[2 * q + SUBLANES:2 * q + SUBLANES + 1]
        b_t = xc_ref[tok, D_SSM + g * SSM_STATE:D_SSM + (g + 1) * SSM_STATE].astype(F32).T.astype(BF16)
        c_bf = xc_ref[tok, D_SSM + D_BC + g * SSM_STATE:D_SSM + D_BC + (g + 1) * SSM_STATE]
        cb = _dot(c_bf, b_t)
        h_g = hT[g]
        y_g = ex[0:q] * _dot(c_bf, h_g.astype(BF16))
        xw_g = xc_ref[tok, gcols] * ex[q:2 * q].astype(BF16)
        hT[g] = h_g * state_decay + _dot(b_t, xw_g)
        for jp in range(HEADS_PER_GROUP // 2):
            pair = g * (HEADS_PER_GROUP // 2) + jp
            lanes = slice(pair * LANES, (pair + 1) * LANES)
            x_pair = xc_ref[tok, lanes]
            zero = jnp.zeros_like(x_pair)
            x2 = jnp.concatenate([jnp.where(first_head, x_pair, zero),
                                  jnp.where(first_head, zero, x_pair)], axis=0)
            m2 = []
            for e in range(2):
                hd = 2 * pair + e
                acs_col = jnp.broadcast_to(acs2[:, hd:hd + 1], (q, q))
                decay = jnp.exp2(jnp.where(causal, acs_col - row_t[hd:hd + 1, :], -jnp.inf))
                m2.append((cb * decay).astype(BF16))
            ybuf[:, lanes] = y_g[:, jp * LANES:(jp + 1) * LANES] + _dot(
                jnp.concatenate(m2, axis=1), x2)

    return _gated_group_norm(ybuf[...], xc_ref[tok, :D_SSM].astype(F32),
                             zs_ref[tok, :].astype(F32), dskip_ref[...], normw_ref[...])


def _write_final_state(hT, hTo_ref, ho_ref):
    hTo_ref[0] = hT[...]
    for g in range(SSM_GROUPS):
        ho_ref[0, g] = hT[g].T


def _mixer_kernel(zs_ref, xc_ref, dt_ref, dtb_ref, alog_ref, dskip_ref, normw_ref, expand_ref,
                  h0_ref, yb_ref, hTo_ref, ho_ref, hT, ybuf, *, tb, valid_len):
    i = pl.program_id(1)

    @pl.when(i == 0)
    def _():
        hT[...] = h0_ref[...]

    masks = _ssd_masks()

    def chunk_body(c, carry):
        r0 = pl.multiple_of(c * CHUNK, CHUNK)
        tok = pl.ds(r0, CHUNK)
        yn = _ssd_chunk(tok, r0, masks, zs_ref, xc_ref, dt_ref, dtb_ref, alog_ref, dskip_ref,
                        normw_ref, expand_ref, hT, ybuf, valid_len if valid_len < tb else None)
        yb_ref[tok, :] = yn.astype(yb_ref.dtype)
        return carry

    lax.fori_loop(0, tb // CHUNK, chunk_body, 0)

    @pl.when(i == pl.num_programs(1) - 1)
    def _():
        _write_final_state(hT, hTo_ref, ho_ref)


HALF = 2 * CHUNK


def _mixer_outproj_kernel(zs_ref, xc_ref, dt_ref, ya_ref, sga_ref, sgb_ref, h_ref,
                          dtb_ref, alog_ref, dskip_ref, normw_ref, expand_ref, h0_ref,
                          wa_ref, wb_ref, wo_ref,
                          o_ref, hTo_ref, ho_ref, hT, ybuf, yb_scr, ya_scr, *, tb):
    i = pl.program_id(1)

    @pl.when(i == 0)
    def _():
        hT[...] = h0_ref[...]

    masks = _ssd_masks()
    ya_scr[...] = _dot(ya_ref[...], wa_ref[...])

    def project(rows):
        y_b = _dot(yb_scr[rows, :], wb_ref[...])
        merged = (sga_ref[rows, :].astype(F32) * ya_scr[rows, :]
                  + sgb_ref[rows, :].astype(F32) * y_b)
        o_ref[rows, :] = h_ref[rows, :] + _dot(merged.astype(BF16), wo_ref[...])

    for c in range(tb // CHUNK):
        r0 = c * CHUNK
        tok = slice(r0, r0 + CHUNK)
        yn = _ssd_chunk(tok, r0, masks, zs_ref, xc_ref, dt_ref, dtb_ref, alog_ref, dskip_ref,
                        normw_ref, expand_ref, hT, ybuf, None)
        yb_scr[tok, :] = yn.astype(BF16)
        if (r0 + CHUNK) % HALF == 0:
            project(slice(r0 + CHUNK - HALF, r0 + CHUNK))

    @pl.when(i == pl.num_programs(1) - 1)
    def _():
        _write_final_state(hT, hTo_ref, ho_ref)


def _mixer_outproj(zs, xc, dt, ya, sga, sgb, h, p, expand, h0, wa, wb, wo, *, batch, seqlen, tb):
    nt = seqlen // tb
    row = lambda w: pl.BlockSpec((tb, w), lambda b, i: (b * nt + i, 0))
    full = lambda shape: pl.BlockSpec(shape, lambda b, i: (0,) * len(shape))
    once = lambda shape: pl.BlockSpec(shape, lambda b, i: (0,) * len(shape), pipeline_mode=pl.Buffered(1))
    per_b = lambda shape: pl.BlockSpec((1,) + shape, lambda b, i: (b,) + (0,) * len(shape))
    return pl.pallas_call(
        functools.partial(_mixer_outproj_kernel, tb=tb),
        grid=(batch, nt),
        in_specs=[row(D_SSM), row(D_XBC), row(LANES), row(D_MODEL), row(D_MODEL), row(D_MODEL),
                  row(D_MODEL),
                  full((1, LANES)), full((1, LANES)), full((1, D_SSM)), full((1, D_SSM)),
                  once((LANES, D_SSM)), full((SSM_GROUPS, SSM_STATE, GROUP_W)),
                  once((D_MODEL, D_MODEL)), once((D_SSM, D_MODEL)), once((D_MODEL, D_MODEL))],
        out_specs=[row(D_MODEL), per_b((SSM_GROUPS, SSM_STATE, GROUP_W)),
                   per_b((SSM_GROUPS, GROUP_W, SSM_STATE))],
        out_shape=[jax.ShapeDtypeStruct((batch * seqlen, D_MODEL), F32),
                   jax.ShapeDtypeStruct((batch, SSM_GROUPS, SSM_STATE, GROUP_W), F32),
                   jax.ShapeDtypeStruct((batch, SSM_GROUPS, GROUP_W, SSM_STATE), F32)],
        scratch_shapes=[pltpu.VMEM((SSM_GROUPS, SSM_STATE, GROUP_W), F32),
                        pltpu.VMEM((CHUNK, D_SSM), F32),
                        pltpu.VMEM((tb, D_SSM), BF16),
                        pltpu.VMEM((tb, D_MODEL), F32)],
        compiler_params=pltpu.CompilerParams(
            dimension_semantics=("arbitrary", "arbitrary"), vmem_limit_bytes=VMEM_LIMIT),
        name="mixer_outproj",
    )(zs, xc, dt, ya, sga, sgb, h, p["dt_bias"], p["a_log"], p["d_skip"], p["ssm_norm_w"], expand,
      h0, wa, wb, wo)


def _mixer(zs, xc, dt, p, expand, h0, *, batch, seqlen, tb, valid_len):
    nt = seqlen // tb
    row = lambda w: pl.BlockSpec((tb, w), lambda b, i: (b * nt + i, 0))
    full = lambda shape: pl.BlockSpec(shape, lambda b, i: (0,) * len(shape))
    per_b = lambda shape: pl.BlockSpec((1,) + shape, lambda b, i: (b,) + (0,) * len(shape))
    return pl.pallas_call(
        functools.partial(_mixer_kernel, tb=tb, valid_len=valid_len),
        grid=(batch, nt),
        in_specs=[row(D_SSM), row(D_XBC), row(LANES),
                  full((1, LANES)), full((1, LANES)), full((1, D_SSM)), full((1, D_SSM)),
                  full((LANES, D_SSM)), full((SSM_GROUPS, SSM_STATE, GROUP_W))],
        out_specs=[row(D_SSM), per_b((SSM_GROUPS, SSM_STATE, GROUP_W)),
                   per_b((SSM_GROUPS, GROUP_W, SSM_STATE))],
        out_shape=[jax.ShapeDtypeStruct((batch * seqlen, D_SSM), BF16),
                   jax.ShapeDtypeStruct((batch, SSM_GROUPS, SSM_STATE, GROUP_W), F32),
                   jax.ShapeDtypeStruct((batch, SSM_GROUPS, GROUP_W, SSM_STATE), F32)],
        scratch_shapes=[pltpu.VMEM((SSM_GROUPS, SSM_STATE, GROUP_W), F32),
                        pltpu.VMEM((CHUNK, D_SSM), F32)],
        compiler_params=pltpu.CompilerParams(
            dimension_semantics=("arbitrary", "arbitrary"), vmem_limit_bytes=VMEM_LIMIT),
        name="mixer",
    )(zs, xc, dt, p["dt_bias"], p["a_log"], p["d_skip"], p["ssm_norm_w"], expand, h0)


SEQ_BLOCK = SUBLANES


def _decode_mixer_kernel(ab_ref, ach_ref, zs_ref, xbc_ref, dt_ref, sa_ref, sx_ref, h0_ref,
                         caw_ref, cxw_ref, cxb_ref, dtb_ref, alog_ref, dskip_ref, normw_ref,
                         gsum_ref, expand_ref,
                         ya_ref, yb_ref, sao_ref, sxo_ref, ho_ref,
                         xc_buf, bc_buf, os_buf, xw_buf, cd_buf, y_buf, *, n_tok):
    sb = SEQ_BLOCK

    xp = [sa_ref[k] for k in range(CONV_A_TAPS - 1)] + [ach_ref[t] for t in range(n_tok)]
    for t in range(n_tok):
        conv = caw_ref[0:1, :] * xp[t]
        for k in range(1, CONV_A_TAPS):
            conv += caw_ref[k:k + 1, :] * xp[t + k]
        ya_ref[t] = ab_ref[t] * conv
    for k in range(CONV_A_TAPS - 1):
        sao_ref[k] = xp[n_tok + k]

    xp = [sx_ref[k] for k in range(CONV_X_TAPS - 1)] + [xbc_ref[t] for t in range(n_tok)]
    for t in range(n_tok):
        conv = cxw_ref[0:1, :] * xp[t]
        for k in range(1, CONV_X_TAPS):
            conv += cxw_ref[k:k + 1, :] * xp[t + k]
        xc_buf[t * sb:(t + 1) * sb, :] = _silu(conv + cxb_ref[...])
    for k in range(CONV_X_TAPS - 1):
        sxo_ref[k] = xp[n_tok + k]

    a_neg = -jnp.exp(alog_ref[...])
    dts, acs = [], []
    for t in range(n_tok):
        dts.append(_softplus(dt_ref[t] + dtb_ref[...]))
        acs.append(dts[t] * a_neg + (acs[t - 1] if t else 0.0))

    pairs = [(qi, si) for qi in range(n_tok) for si in range(qi + 1)]
    prods = [xc_buf[qi * sb:(qi + 1) * sb, D_SSM + D_BC:] * xc_buf[si * sb:(si + 1) * sb, D_SSM:D_SSM + D_BC]
             for qi, si in pairs]
    p_hi, p_lo = _split2(jnp.concatenate(prods, axis=0))
    cbh = _dot(p_hi, gsum_ref[...]) + _dot(p_lo, gsum_ref[...])
    coef = [jnp.exp(acs[t]) for t in range(n_tok)]
    coef += [cbh[k * sb:(k + 1) * sb] * jnp.exp(acs[qi] - acs[si]) * dts[si]
             for k, (qi, si) in enumerate(pairs)]
    coef += [jnp.exp(acs[n_tok - 1] - acs[s]) * dts[s] for s in range(n_tok)]
    coef.append(jnp.exp(acs[n_tok - 1]))
    c_hi, c_lo = _split2(jnp.concatenate(coef, axis=0))
    coef_x = _dot(c_hi, expand_ref[...]) + _dot(c_lo, expand_ref[...])
    slab = lambda k: coef_x[k * sb:(k + 1) * sb]

    def store_tiles(buf, r0, val):
        for lt in range(val.shape[1] // LANES):
            buf[lt, r0:r0 + val.shape[0], :] = val[:, lt * LANES:(lt + 1) * LANES]

    def load_rows(buf, lt0, n_lt, rows):
        return jnp.concatenate([buf[lt0 + j, rows, :] for j in range(n_lt)], axis=1)

    store_tiles(bc_buf, 0, xc_buf[:, D_SSM:])
    store_tiles(os_buf, 0, coef_x[0:n_tok * sb])
    diag = {pr: slab(n_tok + k) for k, pr in enumerate(pairs)}
    k0 = n_tok + len(pairs)
    for s in range(n_tok):
        store_tiles(xw_buf, s * sb, slab(k0 + s) * xc_buf[s * sb:(s + 1) * sb, :D_SSM])
    cd_hi, cd_lo = _split2(slab(k0 + n_tok))
    store_tiles(cd_buf, 0, cd_hi.astype(F32))
    store_tiles(cd_buf, sb, cd_lo.astype(F32))

    for t in range(n_tok):
        acc = diag[(t, 0)] * xc_buf[0:sb, :D_SSM]
        for s in range(1, t + 1):
            acc += diag[(t, s)] * xc_buf[s * sb:(s + 1) * sb, :D_SSM]
        store_tiles(y_buf, t * sb, acc)

    gt = GROUP_W // LANES
    k_rows = 2 * SUBLANES
    krow = lax.broadcasted_iota(jnp.int32, (k_rows, SSM_STATE), 0)
    ones_rows = jnp.where((krow >= n_tok) & (krow < n_tok + 2), 1.0, 0.0).astype(BF16)
    pad_lhs = jnp.zeros((k_rows - n_tok - 2, GROUP_W), F32)
    pad_rhs = jnp.zeros((k_rows - n_tok, SSM_STATE), F32)

    def seq_body(b, carry):
        tok_rows = pl.ds(b, n_tok, stride=sb)
        for g in range(SSM_GROUPS):
            b_bg = bc_buf[g, tok_rows, :]
            c_bg = bc_buf[SSM_GROUPS + g, tok_rows, :]
            heads = pl.ds(g * HEADS_PER_GROUP, HEADS_PER_GROUP)
            h_bg = h0_ref[b, heads].reshape(GROUP_W, SSM_STATE)
            y_off = lax.dot_general(c_bg.astype(BF16), h_bg.astype(BF16),
                                    (((1,), (1,)), ((), ())), preferred_element_type=F32)
            y_off = y_off * load_rows(os_buf, g * gt, gt, tok_rows)
            for j in range(gt):
                y_buf[g * gt + j, tok_rows, :] = (y_buf[g * gt + j, tok_rows, :]
                                                  + y_off[:, j * LANES:(j + 1) * LANES])
            lhs = jnp.concatenate(
                [load_rows(xw_buf, g * gt, gt, tok_rows),
                 load_rows(cd_buf, g * gt, gt, pl.ds(b, 2, stride=sb)), pad_lhs],
                axis=0).astype(BF16)
            rhs = jnp.concatenate(
                [jnp.concatenate([b_bg, pad_rhs], axis=0).astype(BF16), ones_rows], axis=1)
            upd = lax.dot_general(lhs, rhs, (((0,), (0,)), ((), ())), preferred_element_type=F32)
            new = h_bg * upd[:, SSM_STATE:] + upd[:, :SSM_STATE]
            ho_ref[b, heads] = new.reshape(HEADS_PER_GROUP, SSM_HEAD_DIM, SSM_STATE)
        return carry

    lax.fori_loop(0, sb, seq_body, 0, unroll=2)

    xs_all = xc_buf[:, :D_SSM]
    zs_all = jnp.concatenate([zs_ref[t] for t in range(n_tok)], axis=0)
    y_all = load_rows(y_buf, 0, D_SSM // LANES, slice(None))
    yn = _gated_group_norm(y_all, xs_all, zs_all, dskip_ref[...], normw_ref[...])
    for t in range(n_tok):
        yb_ref[t] = yn[t * sb:(t + 1) * sb]


def _decode_mixer(ab, ach, zs, xbc, dt, sa, sx, h0, p, gsum, expand, *, n_tok, n_seq):
    sb = SEQ_BLOCK
    tok = lambda w: pl.BlockSpec((n_tok, sb, w), lambda i: (0, i, 0))
    st = lambda k, w: pl.BlockSpec((k, sb, w), lambda i: (0, i, 0))
    hblk = pl.BlockSpec((sb, SSM_HEADS, SSM_HEAD_DIM, SSM_STATE), lambda i: (i, 0, 0, 0))
    full = lambda shape: pl.BlockSpec(shape, lambda i: (0,) * len(shape))
    return pl.pallas_call(
        functools.partial(_decode_mixer_kernel, n_tok=n_tok),
        grid=(n_seq // sb,),
        in_specs=[tok(D_MODEL), tok(D_MODEL), tok(D_SSM), tok(D_XBC), tok(LANES),
                  st(CONV_A_TAPS - 1, D_MODEL), st(CONV_X_TAPS - 1, D_XBC), hblk,
                  full((CONV_A_TAPS, D_MODEL)), full((CONV_X_TAPS, D_XBC)), full((1, D_XBC)),
                  full((1, LANES)), full((1, LANES)), full((1, D_SSM)), full((1, D_SSM)),
                  full((D_BC, LANES)), full((LANES, D_SSM))],
        out_specs=[tok(D_MODEL), tok(D_SSM), st(CONV_A_TAPS - 1, D_MODEL),
                   st(CONV_X_TAPS - 1, D_XBC), hblk],
        out_shape=[jax.ShapeDtypeStruct((n_tok, n_seq, D_MODEL), F32),
                   jax.ShapeDtypeStruct((n_tok, n_seq, D_SSM), F32),
                   jax.ShapeDtypeStruct((CONV_A_TAPS - 1, n_seq, D_MODEL), F32),
                   jax.ShapeDtypeStruct((CONV_X_TAPS - 1, n_seq, D_XBC), F32),
                   jax.ShapeDtypeStruct((n_seq, SSM_HEADS, SSM_HEAD_DIM, SSM_STATE), F32)],
        scratch_shapes=[pltpu.VMEM((n_tok * sb, D_XBC), F32),
                        pltpu.VMEM((2 * D_BC // LANES, n_tok * sb, LANES), F32),
                        pltpu.VMEM((D_SSM // LANES, n_tok * sb, LANES), F32),
                        pltpu.VMEM((D_SSM // LANES, n_tok * sb, LANES), F32),
                        pltpu.VMEM((D_SSM // LANES, 2 * sb, LANES), F32),
                        pltpu.VMEM((D_SSM // LANES, n_tok * sb, LANES), F32)],
        compiler_params=pltpu.CompilerParams(
            dimension_semantics=("parallel",), vmem_limit_bytes=VMEM_LIMIT),
        name="decode_mixer",
    )(ab, ach, zs, xbc, dt, sa, sx, h0, p["conv_a_w"], p["ssm_conv_w"], p["ssm_conv_b"],
      p["dt_bias"], p["a_log"], p["d_skip"], p["ssm_norm_w"], gsum, expand)


def _row_tile(t, cap):
    if t <= cap:
        return t
    best = LANES
    for cand in range(LANES, cap + 1, LANES):
        if t % cand == 0:
            best = cand
    return best


def kernel(x_prompt, x_sample, state_conv_a, state_ssm_conv, state_ssm, meta_tokens, norm_ffn1, ffn1_w_gu, ffn1_w_down, norm_mix, w_in, conv_a_w, w_a_out, ssm_conv_w, ssm_conv_b, dt_bias, a_log, d_skip, ssm_norm_w, w_b_out, w_o, norm_ffn2, ffn2_w_gu, ffn2_w_down, norm_final):
    bsz, seqlen, _ = x_prompt.shape
    n_seq, n_tok, _ = x_sample.shape
    n_dt = SSM_HEADS
    c_dt = 3 * D_MODEL + D_SSM + D_XBC

    w_gu1, w_dn1 = ffn1_w_gu[0].astype(BF16), ffn1_w_down[0].astype(BF16)
    w_gu2, w_dn2 = ffn2_w_gu[0].astype(BF16), ffn2_w_down[0].astype(BF16)
    w_proj = (jnp.swapaxes(w_in[0], 0, 1).astype(BF16),) * 3
    wa, wb, wo = w_a_out[0].astype(BF16), w_b_out[0].astype(BF16), w_o[0].astype(BF16)
    pad_heads = lambda v: jnp.pad(v[0], (0, LANES - n_dt))[None]
    p = {
        "conv_a_w": conv_a_w[0], "ssm_conv_w": ssm_conv_w[0], "ssm_conv_b": ssm_conv_b[0][None],
        "dt_bias": pad_heads(dt_bias), "a_log": pad_heads(a_log),
        "d_skip": jnp.repeat(d_skip[0], SSM_HEAD_DIM)[None], "ssm_norm_w": ssm_norm_w[0][None],
    }
    n1, nm, n2, nf = norm_ffn1[0][None], norm_mix[0][None], norm_ffn2[0][None], norm_final[None]
    eh = lax.broadcasted_iota(jnp.int32, (LANES, D_SSM), 0)
    ec = lax.broadcasted_iota(jnp.int32, (LANES, D_SSM), 1)
    expand = (ec // SSM_HEAD_DIM == eh).astype(BF16)

    def ffn1(x):
        return _ffn(x, n1, w_gu1, w_dn1, nf, tm=_row_tile(x.shape[0], 512), final_norm=False)

    def ffn2(h):
        return _ffn(h, n2, w_gu2, w_dn2, nf, tm=_row_tile(h.shape[0], 512), final_norm=True)

    zeros = lambda *s: jnp.zeros(s, F32)
    h_m = ffn1(jnp.pad(meta_tokens, ((0, CHUNK - N_META), (0, 0))))
    _, zs, xc, dt, _, _, ca_m, cx_m = _inproj_conv(
        h_m, nm, w_proj, p, zeros(SUBLANES, D_MODEL), zeros(SUBLANES, D_XBC),
        tm=CHUNK, seq_tiles=1, valid_len=N_META)
    _, hT_m, _ = _mixer(zs, xc, dt, p, expand, zeros(SSM_GROUPS, SSM_STATE, GROUP_W),
                        batch=1, seqlen=CHUNK, tb=CHUNK, valid_len=N_META)

    h_p = ffn1(x_prompt.reshape(bsz * seqlen, D_MODEL))
    tm = _row_tile(seqlen, 512)
    ya, zs, xc, dt, sga, sgb, ca_p, cx_p = _inproj_conv(
        h_p, nm, w_proj, p, ca_m[0], cx_m[0], tm=tm, seq_tiles=seqlen // tm, valid_len=tm)
    tb = _row_tile(seqlen, 512)
    h2_p, _, h_out_p = _mixer_outproj(zs, xc, dt, ya, sga, sgb, h_p, p, expand, hT_m[0], wa, wb, wo,
                                      batch=bsz, seqlen=seqlen, tb=tb)
    y_prompt = ffn2(h2_p).reshape(bsz, seqlen, D_MODEL)
    prompt_conv_a = ca_p[None, :, SUBLANES - (CONV_A_TAPS - 1):, :]
    prompt_ssm_conv = cx_p[None, :, SUBLANES - (CONV_X_TAPS - 1):, :]
    prompt_ssm = h_out_p.reshape(1, bsz, SSM_HEADS, SSM_HEAD_DIM, SSM_STATE)

    x_s = jnp.swapaxes(x_sample, 0, 1).reshape(n_tok * n_seq, D_MODEL)
    h_s = ffn1(x_s)
    ab, ach, zs, xbc, dt, sga, sgb = _inproj(h_s, nm, w_proj, tm=_row_tile(n_tok * n_seq, 256), act_dtype=F32)
    tsf = lambda a: a.reshape(n_tok, n_seq, a.shape[-1])
    hh = lax.broadcasted_iota(jnp.int32, (D_BC, LANES), 1)
    kk = lax.broadcasted_iota(jnp.int32, (D_BC, LANES), 0)
    gsum = ((hh < SSM_HEADS) & (kk // SSM_STATE == hh // HEADS_PER_GROUP)).astype(BF16)
    ya, yb, sa_new, sx_new, sample_ssm = _decode_mixer(
        tsf(ab), tsf(ach), tsf(zs), tsf(xbc), tsf(dt),
        jnp.swapaxes(state_conv_a[0], 0, 1), jnp.swapaxes(state_ssm_conv[0], 0, 1), state_ssm[0],
        p, gsum, expand, n_tok=n_tok, n_seq=n_seq)
    flat = lambda a: a.reshape(n_tok * n_seq, a.shape[-1])
    y_s = ffn2(_outproj(flat(ya), flat(yb), sga, sgb, h_s, wa, wb, wo, tm=_row_tile(n_tok * n_seq, 512)))
    y_sample = jnp.swapaxes(y_s.reshape(n_tok, n_seq, D_MODEL), 0, 1)
    sample_conv_a = jnp.swapaxes(sa_new, 0, 1)[None]
    sample_ssm_conv = jnp.swapaxes(sx_new, 0, 1)[None]

    return (y_prompt, y_sample, prompt_conv_a, prompt_ssm_conv, prompt_ssm,
            sample_conv_a, sample_ssm_conv, sample_ssm[None])
```

```python
import functools

import jax
import jax.numpy as jnp
from jax import lax
from jax.experimental import pallas as pl
from jax.experimental.pallas import tpu as pltpu

D_MODEL = 1024
D_FF = 2816
D_SSM = 2048
SSM_HEADS = 32
SSM_HEAD_DIM = 64
SSM_GROUPS = 4
HEADS_PER_GROUP = SSM_HEADS // SSM_GROUPS
SSM_STATE = 128
D_BC = SSM_GROUPS * SSM_STATE
D_XBC = D_SSM + 2 * D_BC
CONV_A_TAPS = 3
CONV_X_TAPS = 4
N_META = 16
EPS = 1e-6

LANES = 128
SUBLANES = 8
CHUNK = 128
VMEM_LIMIT = 56 * 1024 * 1024

F32 = jnp.float32
BF16 = jnp.bfloat16


def _dot(a, b):
    return jnp.dot(a, b, preferred_element_type=F32)


def _rms(x, w):
    return x * lax.rsqrt(jnp.mean(x * x, axis=-1, keepdims=True) + EPS) * w


def _silu(x):
    return x * jax.nn.sigmoid(x)


def _softplus(x):
    return jnp.maximum(x, 0.0) + jnp.log1p(jnp.exp(-jnp.abs(x)))


def _split3(x):
    hi = x.astype(BF16)
    r = x - hi.astype(F32)
    mid = r.astype(BF16)
    lo = (r - mid.astype(F32)).astype(BF16)
    return hi, mid, lo


def _split2(x):
    hi = x.astype(BF16)
    lo = (x - hi.astype(F32)).astype(BF16)
    return hi, lo


def _resident(shape):
    return pl.BlockSpec(shape, lambda *_: (0,) * len(shape), pipeline_mode=pl.Buffered(1))


def _ffn_kernel(x_ref, nw_ref, wg_ref, wu_ref, wd_ref, fw_ref, o_ref, *, final_norm):
    x = x_ref[...]
    xn = _rms(x, nw_ref[...]).astype(BF16)
    act = (_silu(_dot(xn, wg_ref[...])) * _dot(xn, wu_ref[...])).astype(BF16)
    h = x + 0.5 * _dot(act, wd_ref[...])
    if final_norm:
        h = _rms(h, fw_ref[...])
    o_ref[...] = h


def _ffn(x, nw, w_gu, w_down, fw, *, tm, final_norm):
    t = x.shape[0]
    once = pl.Buffered(1)
    return pl.pallas_call(
        functools.partial(_ffn_kernel, final_norm=final_norm),
        grid=(t // tm,),
        in_specs=[
            pl.BlockSpec((tm, D_MODEL), lambda i: (i, 0)),
            _resident((1, D_MODEL)),
            pl.BlockSpec((D_MODEL, D_FF), lambda i: (0, 0), pipeline_mode=once),
            pl.BlockSpec((D_MODEL, D_FF), lambda i: (0, 1), pipeline_mode=once),
            _resident((D_FF, D_MODEL)),
            _resident((1, D_MODEL)),
        ],
        out_specs=pl.BlockSpec((tm, D_MODEL), lambda i: (i, 0)),
        out_shape=jax.ShapeDtypeStruct((t, D_MODEL), F32),
        compiler_params=pltpu.CompilerParams(
            dimension_semantics=("parallel",), vmem_limit_bytes=VMEM_LIMIT),
        name="ffn",
    )(x, nw, w_gu, w_gu, w_down, fw)


C_DT = 3 * D_MODEL + D_SSM + D_XBC
C_GATE = C_DT + LANES


def _project(u, w_refs, c0, width):
    w_main, w_dt, w_gate = w_refs
    if c0 < C_DT:
        w = w_main[c0:c0 + width, :]
    elif c0 < C_GATE:
        w = w_dt[c0 - C_DT:c0 - C_DT + width, :]
    else:
        w = w_gate[c0 - C_GATE:c0 - C_GATE + width, :]
    return lax.dot_general(u, w, (((1,), (1,)), ((), ())), preferred_element_type=F32)


def _proj_weight_specs():
    once = pl.Buffered(1)
    return [pl.BlockSpec((C_DT, D_MODEL), lambda *_: (0, 0), pipeline_mode=once),
            pl.BlockSpec((LANES, D_MODEL), lambda *_: (C_DT // LANES, 0), pipeline_mode=once),
            pl.BlockSpec((pl.Element(2 * D_MODEL), pl.Element(D_MODEL)),
                         lambda *_: (C_DT + SSM_HEADS, 0), pipeline_mode=once)]


def _inproj_kernel(h_ref, nw_ref, wm_ref, wdt_ref, wg_ref,
                   ab_ref, ach_ref, zs_ref, xbc_ref, dt_ref, sga_ref, sgb_ref):
    u = _rms(h_ref[...], nw_ref[...]).astype(BF16)

    def proj(c0, width):
        return _project(u, (wm_ref, wdt_ref, wg_ref), c0, width)

    ab_ref[...] = proj(0, D_MODEL).astype(ab_ref.dtype)
    ach_ref[...] = (proj(D_MODEL, D_MODEL) * proj(2 * D_MODEL, D_MODEL)).astype(ach_ref.dtype)
    c0 = 3 * D_MODEL
    for k in range(D_SSM // D_MODEL):
        zs_ref[:, k * D_MODEL:(k + 1) * D_MODEL] = _silu(proj(c0 + k * D_MODEL, D_MODEL)).astype(zs_ref.dtype)
    c0 += D_SSM
    for k in range(D_XBC // D_MODEL):
        xbc_ref[:, k * D_MODEL:(k + 1) * D_MODEL] = proj(c0 + k * D_MODEL, D_MODEL).astype(xbc_ref.dtype)
    c0 += D_XBC
    dt_ref[...] = proj(c0, LANES)
    c0 += LANES
    sga_ref[...] = jax.nn.sigmoid(proj(c0, D_MODEL)).astype(sga_ref.dtype)
    sgb_ref[...] = jax.nn.sigmoid(proj(c0 + D_MODEL, D_MODEL)).astype(sgb_ref.dtype)


def _inproj(h, nw, w_packed, *, tm, act_dtype):
    t = h.shape[0]
    widths = (D_MODEL, D_MODEL, D_SSM, D_XBC, LANES, D_MODEL, D_MODEL)
    dtypes = (act_dtype, act_dtype, act_dtype, act_dtype, F32, act_dtype, act_dtype)
    return pl.pallas_call(
        _inproj_kernel,
        grid=(t // tm,),
        in_specs=[pl.BlockSpec((tm, D_MODEL), lambda i: (i, 0)), _resident((1, D_MODEL))]
        + _proj_weight_specs(),
        out_specs=[pl.BlockSpec((tm, w), lambda i: (i, 0)) for w in widths],
        out_shape=[jax.ShapeDtypeStruct((t, w), d) for w, d in zip(widths, dtypes)],
        compiler_params=pltpu.CompilerParams(
            dimension_semantics=("parallel",), vmem_limit_bytes=VMEM_LIMIT),
        name="inproj",
    )(h, nw, *w_packed)


COL_BLOCK = 512
N_PARK = 8


def _inproj_conv_kernel(h_ref, nw_ref, wm_ref, wdt_ref, wg_ref, caw_ref, cxw_ref, cxb_ref,
                        ca0_ref, cx0_ref,
                        ya_ref, zs_ref, xc_ref, dt_ref, sga_ref, sgb_ref, cao_ref, cxo_ref,
                        ca_carry, cx_carry, *stages, tm, seq_tiles, valid_len):
    i = pl.program_id(0)

    @pl.when(i % seq_tiles == 0)
    def _():
        ca_carry[...] = ca0_ref[...]
        cx_carry[...] = cx0_ref[...]

    u = _rms(h_ref[...], nw_ref[...]).astype(BF16)

    def proj(c0, width=COL_BLOCK):
        return _project(u, (wm_ref, wdt_ref, wg_ref), c0, width)

    n8 = tm // SUBLANES
    sublane = lax.broadcasted_iota(jnp.int32, (n8, SUBLANES, COL_BLOCK), 1)

    def causal_conv(x, carry_ref, w_ref, taps, cols):
        cur = x.reshape(n8, SUBLANES, COL_BLOCK)
        prev = jnp.concatenate([carry_ref[:, cols], x[:tm - SUBLANES]], axis=0).reshape(cur.shape)
        acc = w_ref[taps - 1:taps, cols] * x
        for d in range(1, taps):
            merged = jnp.where(sublane < SUBLANES - d, cur, prev)
            acc += w_ref[taps - 1 - d:taps - d, cols] * pltpu.roll(merged, d, axis=1).reshape(x.shape)
        carry_ref[:, cols] = x[valid_len - SUBLANES:valid_len]
        return acc

    c_z, c_x, c_dt = 3 * D_MODEL, 3 * D_MODEL + D_SSM, 3 * D_MODEL + D_SSM + D_XBC
    c_g = c_dt + LANES

    def tail_a(k, cols, prj):
        conv = causal_conv(prj[1] * prj[2], ca_carry, caw_ref, CONV_A_TAPS, cols)
        ya_ref[:, cols] = (prj[0] * conv).astype(ya_ref.dtype)

    def tail_z(k, cols, prj):
        zs_ref[:, cols] = _silu(prj[0]).astype(zs_ref.dtype)

    def tail_x(k, cols, prj):
        conv = causal_conv(prj[0], cx_carry, cxw_ref, CONV_X_TAPS, cols)
        xc_ref[:, cols] = _silu(conv + cxb_ref[:, cols]).astype(xc_ref.dtype)

    def tail_g(k, cols, prj):
        sga_ref[:, cols] = jax.nn.sigmoid(prj[0]).astype(sga_ref.dtype)
        sgb_ref[:, cols] = jax.nn.sigmoid(prj[1]).astype(sgb_ref.dtype)

    def blocks(tail, width, starts):
        return [(tail, k, [c + k * COL_BLOCK for c in starts]) for k in range(width // COL_BLOCK)]

    light = blocks(tail_z, D_SSM, [c_z])
    for n, (blk_a, blk_g) in enumerate(zip(blocks(tail_a, D_MODEL, [0, D_MODEL, 2 * D_MODEL]),
                                           blocks(tail_g, D_MODEL, [c_g, c_g + D_MODEL]))):
        light.insert(3 * n, blk_a)
        light.insert(3 * n + 2, blk_g)
    heavy = blocks(tail_x, D_XBC, [c_x])
    order = heavy + light
    park = jnp.minimum(i, 0)
    slot = 0
    for tail, k, starts in order:
        parked = []
        for c in starts:
            stages[slot % len(stages)][park] = proj(c)
            parked.append(stages[slot % len(stages)])
            slot += 1
        tail(k, slice(k * COL_BLOCK, (k + 1) * COL_BLOCK), [s[park] for s in parked])
    dt_ref[...] = proj(c_dt, LANES)
    cao_ref[0] = ca_carry[...]
    cxo_ref[0] = cx_carry[...]


def _inproj_conv(h, nw, w_packed, p, ca0, cx0, *, tm, seq_tiles, valid_len):
    t = h.shape[0]
    n_seq = t // (tm * seq_tiles)
    widths = (D_MODEL, D_SSM, D_XBC, LANES, D_MODEL, D_MODEL)
    dtypes = (BF16, BF16, BF16, F32, BF16, BF16)
    tail = lambda w: pl.BlockSpec((1, SUBLANES, w), lambda i: (i // seq_tiles, 0, 0))
    return pl.pallas_call(
        functools.partial(_inproj_conv_kernel, tm=tm, seq_tiles=seq_tiles, valid_len=valid_len),
        grid=(t // tm,),
        in_specs=[pl.BlockSpec((tm, D_MODEL), lambda i: (i, 0)), _resident((1, D_MODEL))]
        + _proj_weight_specs()
        + [_resident((CONV_A_TAPS, D_MODEL)), _resident((CONV_X_TAPS, D_XBC)),
           _resident((1, D_XBC)),
           _resident((SUBLANES, D_MODEL)), _resident((SUBLANES, D_XBC))],
        out_specs=[pl.BlockSpec((tm, w), lambda i: (i, 0)) for w in widths]
        + [tail(D_MODEL), tail(D_XBC)],
        out_shape=[jax.ShapeDtypeStruct((t, w), d) for w, d in zip(widths, dtypes)]
        + [jax.ShapeDtypeStruct((n_seq, SUBLANES, D_MODEL), F32),
           jax.ShapeDtypeStruct((n_seq, SUBLANES, D_XBC), F32)],
        scratch_shapes=[pltpu.VMEM((SUBLANES, D_MODEL), F32), pltpu.VMEM((SUBLANES, D_XBC), F32),
                        ] + [pltpu.VMEM((1, tm, COL_BLOCK), F32)] * N_PARK,
        compiler_params=pltpu.CompilerParams(
            dimension_semantics=("arbitrary",), vmem_limit_bytes=VMEM_LIMIT),
        name="inproj_conv",
    )(h, nw, *w_packed, p["conv_a_w"], p["ssm_conv_w"], p["ssm_conv_b"], ca0, cx0)


def _outproj_kernel(ya_ref, yb_ref, sga_ref, sgb_ref, h_ref, wa_ref, wb_ref, wo_ref, o_ref):
    y_a = _dot(ya_ref[...].astype(BF16), wa_ref[...])
    y_b = _dot(yb_ref[...].astype(BF16), wb_ref[...])
    merged = sga_ref[...].astype(F32) * y_a + sgb_ref[...].astype(F32) * y_b
    o_ref[...] = h_ref[...] + _dot(merged.astype(BF16), wo_ref[...])


def _outproj(ya, yb, sga, sgb, h, wa, wb, wo, *, tm):
    t = h.shape[0]
    row = lambda w: pl.BlockSpec((tm, w), lambda i: (i, 0))
    return pl.pallas_call(
        _outproj_kernel,
        grid=(t // tm,),
        in_specs=[row(D_MODEL), row(D_SSM), row(D_MODEL), row(D_MODEL), row(D_MODEL),
                  _resident((D_MODEL, D_MODEL)), _resident((D_SSM, D_MODEL)),
                  _resident((D_MODEL, D_MODEL))],
        out_specs=row(D_MODEL),
        out_shape=jax.ShapeDtypeStruct((t, D_MODEL), F32),
        compiler_params=pltpu.CompilerParams(
            dimension_semantics=("parallel",), vmem_limit_bytes=VMEM_LIMIT),
        name="outproj",
    )(ya, yb, sga, sgb, h, wa, wb, wo)


def _gated_group_norm(y, xs, zs, dskip, normw):
    y = (y + dskip * xs) * zs
    gw = D_SSM // SSM_GROUPS
    parts = []
    for g in range(SSM_GROUPS):
        yg = y[:, g * gw:(g + 1) * gw]
        parts.append(yg * lax.rsqrt(jnp.mean(yg * yg, axis=-1, keepdims=True) + EPS))
    return jnp.concatenate(parts, axis=-1) * normw


GROUP_W = HEADS_PER_GROUP * SSM_HEAD_DIM
LOG2E = 1.4426950408889634


def _ssd_masks():
    q = CHUNK
    rows = lax.broadcasted_iota(jnp.int32, (q, q), 0)
    cols = lax.broadcasted_iota(jnp.int32, (q, q), 1)
    causal = rows >= cols
    first_head = lax.broadcasted_iota(jnp.int32, (q, LANES), 1) < SSM_HEAD_DIM
    return causal, causal.astype(BF16), first_head


def _ssd_chunk(tok, r0, masks, zs_ref, xc_ref, dt_ref, dtb_ref, alog_ref, dskip_ref, normw_ref,
               expand_ref, hT, ybuf, valid_len):
    q = CHUNK
    causal, tri, first_head = masks
    a_neg = -jnp.exp(alog_ref[...])
    dt = _softplus(dt_ref[tok, :] + dtb_ref[...])
    if valid_len is not None:
        t_idx = r0 + lax.broadcasted_iota(jnp.int32, (q, LANES), 0)
        dt = jnp.where(t_idx < valid_len, dt, 0.0)
    hi, mid, lo = _split3(dt * a_neg)
    acs3 = _dot(tri, jnp.concatenate([hi, mid, lo], axis=1))
    acs = acs3[:, :LANES] + acs3[:, LANES:2 * LANES] + acs3[:, 2 * LANES:]
    acs2 = acs * LOG2E
    row_t = (acs2 - jnp.log2(dt)).T
    acs_end = acs2[q - 1:q, :]
    cd_hi, cd_lo = _split2(jnp.broadcast_to(jnp.exp2(acs_end), (SUBLANES, LANES)))
    stack = jnp.concatenate(
        [jnp.exp2(acs2).astype(BF16), (jnp.exp2(acs_end - acs2) * dt).astype(BF16),
         cd_hi, cd_lo], axis=0)

    for g in range(SSM_GROUPS):
        gcols = slice(g * GROUP_W, (g + 1) * GROUP_W)
        ex = _dot(stack, expand_ref[:, gcols])
        state_decay = ex[2 * q:2 * q + 1] + ex[2 * q + SUBLANES:2 * q + SUBLANES + 1]
        b_t = xc_ref[tok, D_SSM + g * SSM_STATE:D_SSM + (g + 1) * SSM_STATE].astype(F32).T.astype(BF16)
        c_bf = xc_ref[tok, D_SSM + D_BC + g * SSM_STATE:D_SSM + D_BC + (g + 1) * SSM_STATE]
        cb = _dot(c_bf, b_t)
        h_g = hT[g]
        y_g = ex[0:q] * _dot(c_bf, h_g.astype(BF16))
        xw_g = xc_ref[tok, gcols] * ex[q:2 * q].astype(BF16)
        hT[g] = h_g * state_decay + _dot(b_t, xw_g)
        for jp in range(HEADS_PER_GROUP // 2):
            pair = g * (HEADS_PER_GROUP // 2) + jp
            lanes = slice(pair * LANES, (pair + 1) * LANES)
            x_pair = xc_ref[tok, lanes]
            zero = jnp.zeros_like(x_pair)
            x2 = jnp.concatenate([jnp.where(first_head, x_pair, zero),
                                  jnp.where(first_head, zero, x_pair)], axis=0)
            m2 = []
            for e in range(2):
                hd = 2 * pair + e
                acs_col = jnp.broadcast_to(acs2[:, hd:hd + 1], (q, q))
                decay = jnp.exp2(jnp.where(causal, acs_col - row_t[hd:hd + 1, :], -jnp.inf))
                m2.append((cb * decay).astype(BF16))
            ybuf[:, lanes] = y_g[:, jp * LANES:(jp + 1) * LANES] + _dot(
                jnp.concatenate(m2, axis=1), x2)

    return _gated_group_norm(ybuf[...], xc_ref[tok, :D_SSM].astype(F32),
                             zs_ref[tok, :].astype(F32), dskip_ref[...], normw_ref[...])


def _write_final_state(hT, hTo_ref, ho_ref):
    hTo_ref[0] = hT[...]
    for g in range(SSM_GROUPS):
        ho_ref[0, g] = hT[g].T


def _mixer_kernel(zs_ref, xc_ref, dt_ref, dtb_ref, alog_ref, dskip_ref, normw_ref, expand_ref,
                  h0_ref, yb_ref, hTo_ref, ho_ref, hT, ybuf, *, tb, valid_len):
    i = pl.program_id(1)

    @pl.when(i == 0)
    def _():
        hT[...] = h0_ref[...]

    masks = _ssd_masks()

    def chunk_body(c, carry):
        r0 = pl.multiple_of(c * CHUNK, CHUNK)
        tok = pl.ds(r0, CHUNK)
        yn = _ssd_chunk(tok, r0, masks, zs_ref, xc_ref, dt_ref, dtb_ref, alog_ref, dskip_ref,
                        normw_ref, expand_ref, hT, ybuf, valid_len if valid_len < tb else None)
        yb_ref[tok, :] = yn.astype(yb_ref.dtype)
        return carry

    lax.fori_loop(0, tb // CHUNK, chunk_body, 0)

    @pl.when(i == pl.num_programs(1) - 1)
    def _():
        _write_final_state(hT, hTo_ref, ho_ref)


HALF = 2 * CHUNK


def _mixer_outproj_kernel(zs_ref, xc_ref, dt_ref, ya_ref, sga_ref, sgb_ref, h_ref,
                          dtb_ref, alog_ref, dskip_ref, normw_ref, expand_ref, h0_ref,
                          wa_ref, wb_ref, wo_ref,
                          o_ref, hTo_ref, ho_ref, hT, ybuf, yb_scr, ya_scr, *, tb):
    i = pl.program_id(1)

    @pl.when(i == 0)
    def _():
        hT[...] = h0_ref[...]

    masks = _ssd_masks()
    ya_scr[...] = _dot(ya_ref[...], wa_ref[...])

    def project(rows):
        y_b = _dot(yb_scr[rows, :], wb_ref[...])
        merged = (sga_ref[rows, :].astype(F32) * ya_scr[rows, :]
                  + sgb_ref[rows, :].astype(F32) * y_b)
        o_ref[rows, :] = h_ref[rows, :] + _dot(merged.astype(BF16), wo_ref[...])

    for c in range(tb // CHUNK):
        r0 = c * CHUNK
        tok = slice(r0, r0 + CHUNK)
        yn = _ssd_chunk(tok, r0, masks, zs_ref, xc_ref, dt_ref, dtb_ref, alog_ref, dskip_ref,
                        normw_ref, expand_ref, hT, ybuf, None)
        yb_scr[tok, :] = yn.astype(BF16)
        if (r0 + CHUNK) % HALF == 0:
            project(slice(r0 + CHUNK - HALF, r0 + CHUNK))

    @pl.when(i == pl.num_programs(1) - 1)
    def _():
        _write_final_state(hT, hTo_ref, ho_ref)


def _mixer_outproj(zs, xc, dt, ya, sga, sgb, h, p, expand, h0, wa, wb, wo, *, batch, seqlen, tb):
    nt = seqlen // tb
    row = lambda w: pl.BlockSpec((tb, w), lambda b, i: (b * nt + i, 0))
    full = lambda shape: pl.BlockSpec(shape, lambda b, i: (0,) * len(shape))
    once = lambda shape: pl.BlockSpec(shape, lambda b, i: (0,) * len(shape), pipeline_mode=pl.Buffered(1))
    per_b = lambda shape: pl.BlockSpec((1,) + shape, lambda b, i: (b,) + (0,) * len(shape))
    return pl.pallas_call(
        functools.partial(_mixer_outproj_kernel, tb=tb),
        grid=(batch, nt),
        in_specs=[row(D_SSM), row(D_XBC), row(LANES), row(D_MODEL), row(D_MODEL), row(D_MODEL),
                  row(D_MODEL),
                  full((1, LANES)), full((1, LANES)), full((1, D_SSM)), full((1, D_SSM)),
                  once((LANES, D_SSM)), full((SSM_GROUPS, SSM_STATE, GROUP_W)),
                  once((D_MODEL, D_MODEL)), once((D_SSM, D_MODEL)), once((D_MODEL, D_MODEL))],
        out_specs=[row(D_MODEL), per_b((SSM_GROUPS, SSM_STATE, GROUP_W)),
                   per_b((SSM_GROUPS, GROUP_W, SSM_STATE))],
        out_shape=[jax.ShapeDtypeStruct((batch * seqlen, D_MODEL), F32),
                   jax.ShapeDtypeStruct((batch, SSM_GROUPS, SSM_STATE, GROUP_W), F32),
                   jax.ShapeDtypeStruct((batch, SSM_GROUPS, GROUP_W, SSM_STATE), F32)],
        scratch_shapes=[pltpu.VMEM((SSM_GROUPS, SSM_STATE, GROUP_W), F32),
                        pltpu.VMEM((CHUNK, D_SSM), F32),
                        pltpu.VMEM((tb, D_SSM), BF16),
                        pltpu.VMEM((tb, D_MODEL), F32)],
        compiler_params=pltpu.CompilerParams(
            dimension_semantics=("arbitrary", "arbitrary"), vmem_limit_bytes=VMEM_LIMIT),
        name="mixer_outproj",
    )(zs, xc, dt, ya, sga, sgb, h, p["dt_bias"], p["a_log"], p["d_skip"], p["ssm_norm_w"], expand,
      h0, wa, wb, wo)


def _mixer(zs, xc, dt, p, expand, h0, *, batch, seqlen, tb, valid_len):
    nt = seqlen // tb
    row = lambda w: pl.BlockSpec((tb, w), lambda b, i: (b * nt + i, 0))
    full = lambda shape: pl.BlockSpec(shape, lambda b, i: (0,) * len(shape))
    per_b = lambda shape: pl.BlockSpec((1,) + shape, lambda b, i: (b,) + (0,) * len(shape))
    return pl.pallas_call(
        functools.partial(_mixer_kernel, tb=tb, valid_len=valid_len),
        grid=(batch, nt),
        in_specs=[row(D_SSM), row(D_XBC), row(LANES),
                  full((1, LANES)), full((1, LANES)), full((1, D_SSM)), full((1, D_SSM)),
                  full((LANES, D_SSM)), full((SSM_GROUPS, SSM_STATE, GROUP_W))],
        out_specs=[row(D_SSM), per_b((SSM_GROUPS, SSM_STATE, GROUP_W)),
                   per_b((SSM_GROUPS, GROUP_W, SSM_STATE))],
        out_shape=[jax.ShapeDtypeStruct((batch * seqlen, D_SSM), BF16),
                   jax.ShapeDtypeStruct((batch, SSM_GROUPS, SSM_STATE, GROUP_W), F32),
                   jax.ShapeDtypeStruct((batch, SSM_GROUPS, GROUP_W, SSM_STATE), F32)],
        scratch_shapes=[pltpu.VMEM((SSM_GROUPS, SSM_STATE, GROUP_W), F32),
                        pltpu.VMEM((CHUNK, D_SSM), F32)],
        compiler_params=pltpu.CompilerParams(
            dimension_semantics=("arbitrary", "arbitrary"), vmem_limit_bytes=VMEM_LIMIT),
        name="mixer",
    )(zs, xc, dt, p["dt_bias"], p["a_log"], p["d_skip"], p["ssm_norm_w"], expand, h0)


SEQ_BLOCK = SUBLANES


def _decode_mixer_kernel(ab_ref, ach_ref, zs_ref, xbc_ref, dt_ref, sa_ref, sx_ref, h0_ref,
                         caw_ref, cxw_ref, cxb_ref, dtb_ref, alog_ref, dskip_ref, normw_ref,
                         gsum_ref, expand_ref,
                         ya_ref, yb_ref, sao_ref, sxo_ref, ho_ref,
                         xc_buf, bc_buf, os_buf, xw_buf, cd_buf, y_buf, *, n_tok):
    sb = SEQ_BLOCK

    xp = [sa_ref[k] for k in range(CONV_A_TAPS - 1)] + [ach_ref[t] for t in range(n_tok)]
    for t in range(n_tok):
        conv = caw_ref[0:1, :] * xp[t]
        for k in range(1, CONV_A_TAPS):
            conv += caw_ref[k:k + 1, :] * xp[t + k]
        ya_ref[t] = ab_ref[t] * conv
    for k in range(CONV_A_TAPS - 1):
        sao_ref[k] = xp[n_tok + k]

    xp = [sx_ref[k] for k in range(CONV_X_TAPS - 1)] + [xbc_ref[t] for t in range(n_tok)]
    for t in range(n_tok):
        conv = cxw_ref[0:1, :] * xp[t]
        for k in range(1, CONV_X_TAPS):
            conv += cxw_ref[k:k + 1, :] * xp[t + k]
        xc_buf[t * sb:(t + 1) * sb, :] = _silu(conv + cxb_ref[...])
    for k in range(CONV_X_TAPS - 1):
        sxo_ref[k] = xp[n_tok + k]

    a_neg = -jnp.exp(alog_ref[...])
    dts, acs = [], []
    for t in range(n_tok):
        dts.append(_softplus(dt_ref[t] + dtb_ref[...]))
        acs.append(dts[t] * a_neg + (acs[t - 1] if t else 0.0))

    pairs = [(qi, si) for qi in range(n_tok) for si in range(qi + 1)]
    prods = [xc_buf[qi * sb:(qi + 1) * sb, D_SSM + D_BC:] * xc_buf[si * sb:(si + 1) * sb, D_SSM:D_SSM + D_BC]
             for qi, si in pairs]
    p_hi, p_lo = _split2(jnp.concatenate(prods, axis=0))
    cbh = _dot(p_hi, gsum_ref[...]) + _dot(p_lo, gsum_ref[...])
    coef = [jnp.exp(acs[t]) for t in range(n_tok)]
    coef += [cbh[k * sb:(k + 1) * sb] * jnp.exp(acs[qi] - acs[si]) * dts[si]
             for k, (qi, si) in enumerate(pairs)]
    coef += [jnp.exp(acs[n_tok - 1] - acs[s]) * dts[s] for s in range(n_tok)]
    coef.append(jnp.exp(acs[n_tok - 1]))
    c_hi, c_lo = _split2(jnp.concatenate(coef, axis=0))
    coef_x = _dot(c_hi, expand_ref[...]) + _dot(c_lo, expand_ref[...])
    slab = lambda k: coef_x[k * sb:(k + 1) * sb]

    def store_tiles(buf, r0, val):
        for lt in range(val.shape[1] // LANES):
            buf[lt, r0:r0 + val.shape[0], :] = val[:, lt * LANES:(lt + 1) * LANES]

    def load_rows(buf, lt0, n_lt, rows):
        return jnp.concatenate([buf[lt0 + j, rows, :] for j in range(n_lt)], axis=1)

    store_tiles(bc_buf, 0, xc_buf[:, D_SSM:])
    store_tiles(os_buf, 0, coef_x[0:n_tok * sb])
    diag = {pr: slab(n_tok + k) for k, pr in enumerate(pairs)}
    k0 = n_tok + len(pairs)
    for s in range(n_tok):
        store_tiles(xw_buf, s * sb, slab(k0 + s) * xc_buf[s * sb:(s + 1) * sb, :D_SSM])
    cd_hi, cd_lo = _split2(slab(k0 + n_tok))
    store_tiles(cd_buf, 0, cd_hi.astype(F32))
    store_tiles(cd_buf, sb, cd_lo.astype(F32))

    for t in range(n_tok):
        acc = diag[(t, 0)] * xc_buf[0:sb, :D_SSM]
        for s in range(1, t + 1):
            acc += diag[(t, s)] * xc_buf[s * sb:(s + 1) * sb, :D_SSM]
        store_tiles(y_buf, t * sb, acc)

    gt = GROUP_W // LANES
    k_rows = 2 * SUBLANES
    krow = lax.broadcasted_iota(jnp.int32, (k_rows, SSM_STATE), 0)
    ones_rows = jnp.where((krow >= n_tok) & (krow < n_tok + 2), 1.0, 0.0).astype(BF16)
    pad_lhs = jnp.zeros((k_rows - n_tok - 2, GROUP_W), F32)
    pad_rhs = jnp.zeros((k_rows - n_tok, SSM_STATE), F32)

    def seq_body(b, carry):
        tok_rows = pl.ds(b, n_tok, stride=sb)
        for g in range(SSM_GROUPS):
            b_bg = bc_buf[g, tok_rows, :]
            c_bg = bc_buf[SSM_GROUPS + g, tok_rows, :]
            heads = pl.ds(g * HEADS_PER_GROUP, HEADS_PER_GROUP)
            h_bg = h0_ref[b, heads].reshape(GROUP_W, SSM_STATE)
            y_off = lax.dot_general(c_bg.astype(BF16), h_bg.astype(BF16),
                                    (((1,), (1,)), ((), ())), preferred_element_type=F32)
            y_off = y_off * load_rows(os_buf, g * gt, gt, tok_rows)
            for j in range(gt):
                y_buf[g * gt + j, tok_rows, :] = (y_buf[g * gt + j, tok_rows, :]
                                                  + y_off[:, j * LANES:(j + 1) * LANES])
            lhs = jnp.concatenate(
                [load_rows(xw_buf, g * gt, gt, tok_rows),
                 load_rows(cd_buf, g * gt, gt, pl.ds(b, 2, stride=sb)), pad_lhs],
                axis=0).astype(BF16)
            rhs = jnp.concatenate(
                [jnp.concatenate([b_bg, pad_rhs], axis=0).astype(BF16), ones_rows], axis=1)
            upd = lax.dot_general(lhs, rhs, (((0,), (0,)), ((), ())), preferred_element_type=F32)
            new = h_bg * upd[:, SSM_STATE:] + upd[:, :SSM_STATE]
            ho_ref[b, heads] = new.reshape(HEADS_PER_GROUP, SSM_HEAD_DIM, SSM_STATE)
        return carry

    lax.fori_loop(0, sb, seq_body, 0, unroll=2)

    xs_all = xc_buf[:, :D_SSM]
    zs_all = jnp.concatenate([zs_ref[t] for t in range(n_tok)], axis=0)
    y_all = load_rows(y_buf, 0, D_SSM // LANES, slice(None))
    yn = _gated_group_norm(y_all, xs_all, zs_all, dskip_ref[...], normw_ref[...])
    for t in range(n_tok):
        yb_ref[t] = yn[t * sb:(t + 1) * sb]


def _decode_mixer(ab, ach, zs, xbc, dt, sa, sx, h0, p, gsum, expand, *, n_tok, n_seq):
    sb = SEQ_BLOCK
    tok = lambda w: pl.BlockSpec((n_tok, sb, w), lambda i: (0, i, 0))
    st = lambda k, w: pl.BlockSpec((k, sb, w), lambda i: (0, i, 0))
    hblk = pl.BlockSpec((sb, SSM_HEADS, SSM_HEAD_DIM, SSM_STATE), lambda i: (i, 0, 0, 0))
    full = lambda shape: pl.BlockSpec(shape, lambda i: (0,) * len(shape))
    return pl.pallas_call(
        functools.partial(_decode_mixer_kernel, n_tok=n_tok),
        grid=(n_seq // sb,),
        in_specs=[tok(D_MODEL), tok(D_MODEL), tok(D_SSM), tok(D_XBC), tok(LANES),
                  st(CONV_A_TAPS - 1, D_MODEL), st(CONV_X_TAPS - 1, D_XBC), hblk,
                  full((CONV_A_TAPS, D_MODEL)), full((CONV_X_TAPS, D_XBC)), full((1, D_XBC)),
                  full((1, LANES)), full((1, LANES)), full((1, D_SSM)), full((1, D_SSM)),
                  full((D_BC, LANES)), full((LANES, D_SSM))],
        out_specs=[tok(D_MODEL), tok(D_SSM), st(CONV_A_TAPS - 1, D_MODEL),
                   st(CONV_X_TAPS - 1, D_XBC), hblk],
        out_shape=[jax.ShapeDtypeStruct((n_tok, n_seq, D_MODEL), F32),
                   jax.ShapeDtypeStruct((n_tok, n_seq, D_SSM), F32),
                   jax.ShapeDtypeStruct((CONV_A_TAPS - 1, n_seq, D_MODEL), F32),
                   jax.ShapeDtypeStruct((CONV_X_TAPS - 1, n_seq, D_XBC), F32),
                   jax.ShapeDtypeStruct((n_seq, SSM_HEADS, SSM_HEAD_DIM, SSM_STATE), F32)],
        scratch_shapes=[pltpu.VMEM((n_tok * sb, D_XBC), F32),
                        pltpu.VMEM((2 * D_BC // LANES, n_tok * sb, LANES), F32),
                        pltpu.VMEM((D_SSM // LANES, n_tok * sb, LANES), F32),
                        pltpu.VMEM((D_SSM // LANES, n_tok * sb, LANES), F32),
                        pltpu.VMEM((D_SSM // LANES, 2 * sb, LANES), F32),
                        pltpu.VMEM((D_SSM // LANES, n_tok * sb, LANES), F32)],
        compiler_params=pltpu.CompilerParams(
            dimension_semantics=("parallel",), vmem_limit_bytes=VMEM_LIMIT),
        name="decode_mixer",
    )(ab, ach, zs, xbc, dt, sa, sx, h0, p["conv_a_w"], p["ssm_conv_w"], p["ssm_conv_b"],
      p["dt_bias"], p["a_log"], p["d_skip"], p["ssm_norm_w"], gsum, expand)


def _row_tile(t, cap):
    if t <= cap:
        return t
    best = LANES
    for cand in range(LANES, cap + 1, LANES):
        if t % cand == 0:
            best = cand
    return best


def kernel(x_prompt, x_sample, state_conv_a, state_ssm_conv, state_ssm, meta_tokens, norm_ffn1, ffn1_w_gu, ffn1_w_down, norm_mix, w_in, conv_a_w, w_a_out, ssm_conv_w, ssm_conv_b, dt_bias, a_log, d_skip, ssm_norm_w, w_b_out, w_o, norm_ffn2, ffn2_w_gu, ffn2_w_down, norm_final):
    bsz, seqlen, _ = x_prompt.shape
    n_seq, n_tok, _ = x_sample.shape
    n_dt = SSM_HEADS
    c_dt = 3 * D_MODEL + D_SSM + D_XBC

    w_gu1, w_dn1 = ffn1_w_gu[0].astype(BF16), ffn1_w_down[0].astype(BF16)
    w_gu2, w_dn2 = ffn2_w_gu[0].astype(BF16), ffn2_w_down[0].astype(BF16)
    w_proj = (jnp.swapaxes(w_in[0], 0, 1).astype(BF16),) * 3
    wa, wb, wo = w_a_out[0].astype(BF16), w_b_out[0].astype(BF16), w_o[0].astype(BF16)
    pad_heads = lambda v: jnp.pad(v[0], (0, LANES - n_dt))[None]
    p = {
        "conv_a_w": conv_a_w[0], "ssm_conv_w": ssm_conv_w[0], "ssm_conv_b": ssm_conv_b[0][None],
        "dt_bias": pad_heads(dt_bias), "a_log": pad_heads(a_log),
        "d_skip": jnp.repeat(d_skip[0], SSM_HEAD_DIM)[None], "ssm_norm_w": ssm_norm_w[0][None],
    }
    n1, nm, n2, nf = norm_ffn1[0][None], norm_mix[0][None], norm_ffn2[0][None], norm_final[None]
    eh = lax.broadcasted_iota(jnp.int32, (LANES, D_SSM), 0)
    ec = lax.broadcasted_iota(jnp.int32, (LANES, D_SSM), 1)
    expand = (ec // SSM_HEAD_DIM == eh).astype(BF16)

    def ffn1(x, cap=512):
        return _ffn(x, n1, w_gu1, w_dn1, nf, tm=_row_tile(x.shape[0], cap), final_norm=False)

    def ffn2(h):
        return _ffn(h, n2, w_gu2, w_dn2, nf, tm=_row_tile(h.shape[0], 512), final_norm=True)

    zeros = lambda *s: jnp.zeros(s, F32)
    x_s = jnp.swapaxes(x_sample, 0, 1).reshape(n_tok * n_seq, D_MODEL)
    h_sm = ffn1(jnp.concatenate([x_s, jnp.pad(meta_tokens, ((0, CHUNK - N_META), (0, 0)))], axis=0),
                cap=n_tok * n_seq + CHUNK)
    h_s, h_m = h_sm[:n_tok * n_seq], h_sm[n_tok * n_seq:]
    _, zs, xc, dt, _, _, ca_m, cx_m = _inproj_conv(
        h_m, nm, w_proj, p, zeros(SUBLANES, D_MODEL), zeros(SUBLANES, D_XBC),
        tm=CHUNK, seq_tiles=1, valid_len=N_META)
    _, hT_m, _ = _mixer(zs, xc, dt, p, expand, zeros(SSM_GROUPS, SSM_STATE, GROUP_W),
                        batch=1, seqlen=CHUNK, tb=CHUNK, valid_len=N_META)

    h_p = ffn1(x_prompt.reshape(bsz * seqlen, D_MODEL))
    tm = _row_tile(seqlen, 512)
    ya, zs, xc, dt, sga, sgb, ca_p, cx_p = _inproj_conv(
        h_p, nm, w_proj, p, ca_m[0], cx_m[0], tm=tm, seq_tiles=seqlen // tm, valid_len=tm)
    tb = _row_tile(seqlen, 512)
    h2_p, _, h_out_p = _mixer_outproj(zs, xc, dt, ya, sga, sgb, h_p, p, expand, hT_m[0], wa, wb, wo,
                                      batch=bsz, seqlen=seqlen, tb=tb)
    y_prompt = ffn2(h2_p).reshape(bsz, seqlen, D_MODEL)
    prompt_conv_a = ca_p[None, :, SUBLANES - (CONV_A_TAPS - 1):, :]
    prompt_ssm_conv = cx_p[None, :, SUBLANES - (CONV_X_TAPS - 1):, :]
    prompt_ssm = h_out_p.reshape(1, bsz, SSM_HEADS, SSM_HEAD_DIM, SSM_STATE)

    ab, ach, zs, xbc, dt, sga, sgb = _inproj(h_s, nm, w_proj, tm=_row_tile(n_tok * n_seq, 256), act_dtype=F32)
    tsf = lambda a: a.reshape(n_tok, n_seq, a.shape[-1])
    hh = lax.broadcasted_iota(jnp.int32, (D_BC, LANES), 1)
    kk = lax.broadcasted_iota(jnp.int32, (D_BC, LANES), 0)
    gsum = ((hh < SSM_HEADS) & (kk // SSM_STATE == hh // HEADS_PER_GROUP)).astype(BF16)
    ya, yb, sa_new, sx_new, sample_ssm = _decode_mixer(
        tsf(ab), tsf(ach), tsf(zs), tsf(xbc), tsf(dt),
        jnp.swapaxes(state_conv_a[0], 0, 1), jnp.swapaxes(state_ssm_conv[0], 0, 1), state_ssm[0],
        p, gsum, expand, n_tok=n_tok, n_seq=n_seq)
    flat = lambda a: a.reshape(n_tok * n_seq, a.shape[-1])
    y_s = ffn2(_outproj(flat(ya), flat(yb), sga, sgb, h_s, wa, wb, wo, tm=_row_tile(n_tok * n_seq, 512)))
    y_sample = jnp.swapaxes(y_s.reshape(n_tok, n_seq, D_MODEL), 0, 1)
    sample_conv_a = jnp.swapaxes(sa_new, 0, 1)[None]
    sample_ssm_conv = jnp.swapaxes(sx_new, 0, 1)[None]

    return (y_prompt, y_sample, prompt_conv_a, prompt_ssm_conv, prompt_ssm,
            sample_conv_a, sample_ssm_conv, sample_ssm[None])
```

```python
import functools

import jax
import jax.numpy as jnp
from jax import lax
from jax.experimental import pallas as pl
from jax.experimental.pallas import tpu as pltpu

D_MODEL = 1024
D_FF = 2816
D_SSM = 2048
SSM_HEADS = 32
SSM_HEAD_DIM = 64
SSM_GROUPS = 4
HEADS_PER_GROUP = SSM_HEADS // SSM_GROUPS
SSM_STATE = 128
D_BC = SSM_GROUPS * SSM_STATE
D_XBC = D_SSM + 2 * D_BC
CONV_A_TAPS = 3
CONV_X_TAPS = 4
N_META = 16
EPS = 1e-6

LANES = 128
SUBLANES = 8
CHUNK = 128
VMEM_LIMIT = 56 * 1024 * 1024

F32 = jnp.float32
BF16 = jnp.bfloat16


def _dot(a, b):
    return jnp.dot(a, b, preferred_element_type=F32)


def _rms(x, w):
    return x * lax.rsqrt(jnp.mean(x * x, axis=-1, keepdims=True) + EPS) * w


def _silu(x):
    return x * jax.nn.sigmoid(x)


def _softplus(x):
    return jnp.maximum(x, 0.0) + jnp.log1p(jnp.exp(-jnp.abs(x)))


def _split3(x):
    hi = x.astype(BF16)
    r = x - hi.astype(F32)
    mid = r.astype(BF16)
    lo = (r - mid.astype(F32)).astype(BF16)
    return hi, mid, lo


def _split2(x):
    hi = x.astype(BF16)
    lo = (x - hi.astype(F32)).astype(BF16)
    return hi, lo


def _resident(shape):
    return pl.BlockSpec(shape, lambda *_: (0,) * len(shape), pipeline_mode=pl.Buffered(1))


def _ffn_kernel(x_ref, nw_ref, wg_ref, wu_ref, wd_ref, fw_ref, *rest, final_norm, cast_blocks):
    n_cast = len(cast_blocks)
    cast_in, o_ref, cast_out = rest[:n_cast], rest[n_cast], rest[n_cast + 1:]
    x = x_ref[...]
    xn = _rms(x, nw_ref[...]).astype(BF16)
    act = (_silu(_dot(xn, wg_ref[...])) * _dot(xn, wu_ref[...])).astype(BF16)
    h = x + 0.5 * _dot(act, wd_ref[...])
    if final_norm:
        h = _rms(h, fw_ref[...])
    o_ref[...] = h
    for src, dst, n_blocks in zip(cast_in, cast_out, cast_blocks):
        @pl.when(pl.program_id(0) < n_blocks)
        def _(src=src, dst=dst):
            dst[...] = src[...].astype(BF16)


def _ffn(x, nw, w_gu, w_down, fw, *, tm, final_norm, cast=()):
    t = x.shape[0]
    once = pl.Buffered(1)
    n_blocks = [pl.cdiv(w.shape[0], rows) for w, rows in cast]
    assert all(nb <= t // tm for nb in n_blocks)
    cast_specs = [pl.BlockSpec((rows, w.shape[1]), functools.partial(
        lambda i, last: (jnp.minimum(i, last), 0), last=nb - 1))
        for (w, rows), nb in zip(cast, n_blocks)]
    out = pl.pallas_call(
        functools.partial(_ffn_kernel, final_norm=final_norm, cast_blocks=tuple(n_blocks)),
        grid=(t // tm,),
        in_specs=[
            pl.BlockSpec((tm, D_MODEL), lambda i: (i, 0)),
            _resident((1, D_MODEL)),
            pl.BlockSpec((D_MODEL, D_FF), lambda i: (0, 0), pipeline_mode=once),
            pl.BlockSpec((D_MODEL, D_FF), lambda i: (0, 1), pipeline_mode=once),
            _resident((D_FF, D_MODEL)),
            _resident((1, D_MODEL)),
        ] + cast_specs,
        out_specs=[pl.BlockSpec((tm, D_MODEL), lambda i: (i, 0))] + cast_specs,
        out_shape=[jax.ShapeDtypeStruct((t, D_MODEL), F32)]
        + [jax.ShapeDtypeStruct(w.shape, BF16) for w, _ in cast],
        compiler_params=pltpu.CompilerParams(
            dimension_semantics=("arbitrary" if cast else "parallel",), vmem_limit_bytes=VMEM_LIMIT),
        name="ffn",
    )(x, nw, w_gu, w_gu, w_down, fw, *[w for w, _ in cast])
    return out if cast else out[0]


C_DT = 3 * D_MODEL + D_SSM + D_XBC
C_GATE = C_DT + LANES


def _project(u, w_refs, c0, width):
    w_main, w_dt, w_gate = w_refs
    if c0 < C_DT:
        w = w_main[c0:c0 + width, :]
    elif c0 < C_GATE:
        w = w_dt[c0 - C_DT:c0 - C_DT + width, :]
    else:
        w = w_gate[c0 - C_GATE:c0 - C_GATE + width, :]
    return lax.dot_general(u, w, (((1,), (1,)), ((), ())), preferred_element_type=F32)


def _proj_weight_specs():
    once = pl.Buffered(1)
    return [pl.BlockSpec((C_DT, D_MODEL), lambda *_: (0, 0), pipeline_mode=once),
            pl.BlockSpec((LANES, D_MODEL), lambda *_: (C_DT // LANES, 0), pipeline_mode=once),
            pl.BlockSpec((pl.Element(2 * D_MODEL), pl.Element(D_MODEL)),
                         lambda *_: (C_DT + SSM_HEADS, 0), pipeline_mode=once)]


def _inproj_kernel(h_ref, nw_ref, wm_ref, wdt_ref, wg_ref,
                   ab_ref, ach_ref, zs_ref, xbc_ref, dt_ref, sga_ref, sgb_ref):
    u = _rms(h_ref[...], nw_ref[...]).astype(BF16)

    def proj(c0, width):
        return _project(u, (wm_ref, wdt_ref, wg_ref), c0, width)

    ab_ref[...] = proj(0, D_MODEL).astype(ab_ref.dtype)
    ach_ref[...] = (proj(D_MODEL, D_MODEL) * proj(2 * D_MODEL, D_MODEL)).astype(ach_ref.dtype)
    c0 = 3 * D_MODEL
    for k in range(D_SSM // D_MODEL):
        zs_ref[:, k * D_MODEL:(k + 1) * D_MODEL] = _silu(proj(c0 + k * D_MODEL, D_MODEL)).astype(zs_ref.dtype)
    c0 += D_SSM
    for k in range(D_XBC // D_MODEL):
        xbc_ref[:, k * D_MODEL:(k + 1) * D_MODEL] = proj(c0 + k * D_MODEL, D_MODEL).astype(xbc_ref.dtype)
    c0 += D_XBC
    dt_ref[...] = proj(c0, LANES)
    c0 += LANES
    sga_ref[...] = jax.nn.sigmoid(proj(c0, D_MODEL)).astype(sga_ref.dtype)
    sgb_ref[...] = jax.nn.sigmoid(proj(c0 + D_MODEL, D_MODEL)).astype(sgb_ref.dtype)


def _inproj(h, nw, w_packed, *, tm, act_dtype):
    t = h.shape[0]
    widths = (D_MODEL, D_MODEL, D_SSM, D_XBC, LANES, D_MODEL, D_MODEL)
    dtypes = (act_dtype, act_dtype, act_dtype, act_dtype, F32, act_dtype, act_dtype)
    return pl.pallas_call(
        _inproj_kernel,
        grid=(t // tm,),
        in_specs=[pl.BlockSpec((tm, D_MODEL), lambda i: (i, 0)), _resident((1, D_MODEL))]
        + _proj_weight_specs(),
        out_specs=[pl.BlockSpec((tm, w), lambda i: (i, 0)) for w in widths],
        out_shape=[jax.ShapeDtypeStruct((t, w), d) for w, d in zip(widths, dtypes)],
        compiler_params=pltpu.CompilerParams(
            dimension_semantics=("parallel",), vmem_limit_bytes=VMEM_LIMIT),
        name="inproj",
    )(h, nw, *w_packed)


COL_BLOCK = 512
N_PARK = 8


def _inproj_conv_kernel(h_ref, nw_ref, wm_ref, wdt_ref, wg_ref, caw_ref, cxw_ref, cxb_ref,
                        ca0_ref, cx0_ref,
                        ya_ref, zs_ref, xc_ref, dt_ref, sga_ref, sgb_ref, cao_ref, cxo_ref,
                        ca_carry, cx_carry, *stages, tm, seq_tiles, valid_len):
    i = pl.program_id(0)

    @pl.when(i % seq_tiles == 0)
    def _():
        ca_carry[...] = ca0_ref[...]
        cx_carry[...] = cx0_ref[...]

    u = _rms(h_ref[...], nw_ref[...]).astype(BF16)

    def proj(c0, width=COL_BLOCK):
        return _project(u, (wm_ref, wdt_ref, wg_ref), c0, width)

    n8 = tm // SUBLANES
    sublane = lax.broadcasted_iota(jnp.int32, (n8, SUBLANES, COL_BLOCK), 1)

    def causal_conv(x, carry_ref, w_ref, taps, cols):
        cur = x.reshape(n8, SUBLANES, COL_BLOCK)
        prev = jnp.concatenate([carry_ref[:, cols], x[:tm - SUBLANES]], axis=0).reshape(cur.shape)
        acc = w_ref[taps - 1:taps, cols] * x
        for d in range(1, taps):
            merged = jnp.where(sublane < SUBLANES - d, cur, prev)
            acc += w_ref[taps - 1 - d:taps - d, cols] * pltpu.roll(merged, d, axis=1).reshape(x.shape)
        carry_ref[:, cols] = x[valid_len - SUBLANES:valid_len]
        return acc

    c_z, c_x, c_dt = 3 * D_MODEL, 3 * D_MODEL + D_SSM, 3 * D_MODEL + D_SSM + D_XBC
    c_g = c_dt + LANES

    def tail_a(k, cols, prj):
        conv = causal_conv(prj[1] * prj[2], ca_carry, caw_ref, CONV_A_TAPS, cols)
        ya_ref[:, cols] = (prj[0] * conv).astype(ya_ref.dtype)

    def tail_z(k, cols, prj):
        zs_ref[:, cols] = _silu(prj[0]).astype(zs_ref.dtype)

    def tail_x(k, cols, prj):
        conv = causal_conv(prj[0], cx_carry, cxw_ref, CONV_X_TAPS, cols)
        xc_ref[:, cols] = _silu(conv + cxb_ref[:, cols]).astype(xc_ref.dtype)

    def tail_g(k, cols, prj):
        sga_ref[:, cols] = jax.nn.sigmoid(prj[0]).astype(sga_ref.dtype)
        sgb_ref[:, cols] = jax.nn.sigmoid(prj[1]).astype(sgb_ref.dtype)

    def blocks(tail, width, starts):
        return [(tail, k, [c + k * COL_BLOCK for c in starts]) for k in range(width // COL_BLOCK)]

    light = blocks(tail_z, D_SSM, [c_z])
    for n, (blk_a, blk_g) in enumerate(zip(blocks(tail_a, D_MODEL, [0, D_MODEL, 2 * D_MODEL]),
                                           blocks(tail_g, D_MODEL, [c_g, c_g + D_MODEL]))):
        light.insert(3 * n, blk_a)
        light.insert(3 * n + 2, blk_g)
    heavy = blocks(tail_x, D_XBC, [c_x])
    order = heavy + light
    park = jnp.minimum(i, 0)
    slot = 0
    for tail, k, starts in order:
        parked = []
        for c in starts:
            stages[slot % len(stages)][park] = proj(c)
            parked.append(stages[slot % len(stages)])
            slot += 1
        tail(k, slice(k * COL_BLOCK, (k + 1) * COL_BLOCK), [s[park] for s in parked])
    dt_ref[...] = proj(c_dt, LANES)
    cao_ref[0] = ca_carry[...]
    cxo_ref[0] = cx_carry[...]


def _inproj_conv(h, nw, w_packed, p, ca0, cx0, *, tm, seq_tiles, valid_len):
    t = h.shape[0]
    n_seq = t // (tm * seq_tiles)
    widths = (D_MODEL, D_SSM, D_XBC, LANES, D_MODEL, D_MODEL)
    dtypes = (BF16, BF16, BF16, F32, BF16, BF16)
    tail = lambda w: pl.BlockSpec((1, SUBLANES, w), lambda i: (i // seq_tiles, 0, 0))
    return pl.pallas_call(
        functools.partial(_inproj_conv_kernel, tm=tm, seq_tiles=seq_tiles, valid_len=valid_len),
        grid=(t // tm,),
        in_specs=[pl.BlockSpec((tm, D_MODEL), lambda i: (i, 0)), _resident((1, D_MODEL))]
        + _proj_weight_specs()
        + [_resident((CONV_A_TAPS, D_MODEL)), _resident((CONV_X_TAPS, D_XBC)),
           _resident((1, D_XBC)),
           _resident((SUBLANES, D_MODEL)), _resident((SUBLANES, D_XBC))],
        out_specs=[pl.BlockSpec((tm, w), lambda i: (i, 0)) for w in widths]
        + [tail(D_MODEL), tail(D_XBC)],
        out_shape=[jax.ShapeDtypeStruct((t, w), d) for w, d in zip(widths, dtypes)]
        + [jax.ShapeDtypeStruct((n_seq, SUBLANES, D_MODEL), F32),
           jax.ShapeDtypeStruct((n_seq, SUBLANES, D_XBC), F32)],
        scratch_shapes=[pltpu.VMEM((SUBLANES, D_MODEL), F32), pltpu.VMEM((SUBLANES, D_XBC), F32),
                        ] + [pltpu.VMEM((1, tm, COL_BLOCK), F32)] * N_PARK,
        compiler_params=pltpu.CompilerParams(
            dimension_semantics=("arbitrary",), vmem_limit_bytes=VMEM_LIMIT),
        name="inproj_conv",
    )(h, nw, *w_packed, p["conv_a_w"], p["ssm_conv_w"], p["ssm_conv_b"], ca0, cx0)


def _outproj_kernel(ya_ref, yb_ref, sga_ref, sgb_ref, h_ref, wa_ref, wb_ref, wo_ref, o_ref):
    y_a = _dot(ya_ref[...].astype(BF16), wa_ref[...])
    y_b = _dot(yb_ref[...].astype(BF16), wb_ref[...])
    merged = sga_ref[...].astype(F32) * y_a + sgb_ref[...].astype(F32) * y_b
    o_ref[...] = h_ref[...] + _dot(merged.astype(BF16), wo_ref[...])


def _outproj(ya, yb, sga, sgb, h, wa, wb, wo, *, tm):
    t = h.shape[0]
    row = lambda w: pl.BlockSpec((tm, w), lambda i: (i, 0))
    return pl.pallas_call(
        _outproj_kernel,
        grid=(t // tm,),
        in_specs=[row(D_MODEL), row(D_SSM), row(D_MODEL), row(D_MODEL), row(D_MODEL),
                  _resident((D_MODEL, D_MODEL)), _resident((D_SSM, D_MODEL)),
                  _resident((D_MODEL, D_MODEL))],
        out_specs=row(D_MODEL),
        out_shape=jax.ShapeDtypeStruct((t, D_MODEL), F32),
        compiler_params=pltpu.CompilerParams(
            dimension_semantics=("parallel",), vmem_limit_bytes=VMEM_LIMIT),
        name="outproj",
    )(ya, yb, sga, sgb, h, wa, wb, wo)


def _gated_group_norm(y, xs, zs, dskip, normw):
    y = (y + dskip * xs) * zs
    gw = D_SSM // SSM_GROUPS
    parts = []
    for g in range(SSM_GROUPS):
        yg = y[:, g * gw:(g + 1) * gw]
        parts.append(yg * lax.rsqrt(jnp.mean(yg * yg, axis=-1, keepdims=True) + EPS))
    return jnp.concatenate(parts, axis=-1) * normw


GROUP_W = HEADS_PER_GROUP * SSM_HEAD_DIM
LOG2E = 1.4426950408889634


def _ssd_masks():
    q = CHUNK
    rows = lax.broadcasted_iota(jnp.int32, (q, q), 0)
    cols = lax.broadcasted_iota(jnp.int32, (q, q), 1)
    causal = rows >= cols
    first_head = lax.broadcasted_iota(jnp.int32, (q, LANES), 1) < SSM_HEAD_DIM
    return causal, causal.astype(BF16), first_head


def _ssd_chunk(tok, r0, masks, zs_ref, xc_ref, dt_ref, dtb_ref, alog_ref, dskip_ref, normw_ref,
               expand_ref, hT, ybuf, valid_len):
    q = CHUNK
    causal, tri, first_head = masks
    a_neg = -jnp.exp(alog_ref[...])
    dt = _softplus(dt_ref[tok, :] + dtb_ref[...])
    if valid_len is not None:
        t_idx = r0 + lax.broadcasted_iota(jnp.int32, (q, LANES), 0)
        dt = jnp.where(t_idx < valid_len, dt, 0.0)
    hi, mid, lo = _split3(dt * a_neg)
    acs3 = _dot(tri, jnp.concatenate([hi, mid, lo], axis=1))
    acs = acs3[:, :LANES] + acs3[:, LANES:2 * LANES] + acs3[:, 2 * LANES:]
    acs2 = acs * LOG2E
    row_t = (acs2 - jnp.log2(dt)).T
    acs_end = acs2[q - 1:q, :]
    cd_hi, cd_lo = _split2(jnp.broadcast_to(jnp.exp2(acs_end), (SUBLANES, LANES)))
    stack = jnp.concatenate(
        [jnp.exp2(acs2).astype(BF16), (jnp.exp2(acs_end - acs2) * dt).astype(BF16),
         cd_hi, cd_lo], axis=0)

    for g in range(SSM_GROUPS):
        gcols = slice(g * GROUP_W, (g + 1) * GROUP_W)
        ex = _dot(stack, expand_ref[:, gcols])
        state_decay = ex[2 * q:2 * q + 1] + ex[2 * q + SUBLANES:2 * q + SUBLANES + 1]
        b_t = xc_ref[tok, D_SSM + g * SSM_STATE:D_SSM + (g + 1) * SSM_STATE].astype(F32).T.astype(BF16)
        c_bf = xc_ref[tok, D_SSM + D_BC + g * SSM_STATE:D_SSM + D_BC + (g + 1) * SSM_STATE]
        cb = _dot(c_bf, b_t)
        h_g = hT[g]
        y_g = ex[0:q] * _dot(c_bf, h_g.astype(BF16))
        xw_g = xc_ref[tok, gcols] * ex[q:2 * q].astype(BF16)
        hT[g] = h_g * state_decay + _dot(b_t, xw_g)
        for jp in range(HEADS_PER_GROUP // 2):
            pair = g * (HEADS_PER_GROUP // 2) + jp
            lanes = slice(pair * LANES, (pair + 1) * LANES)
            x_pair = xc_ref[tok, lanes]
            zero = jnp.zeros_like(x_pair)
            x2 = jnp.concatenate([jnp.where(first_head, x_pair, zero),
                                  jnp.where(first_head, zero, x_pair)], axis=0)
            m2 = []
            for e in range(2):
                hd = 2 * pair + e
                acs_col = jnp.broadcast_to(acs2[:, hd:hd + 1], (q, q))
                decay = jnp.exp2(jnp.where(causal, acs_col - row_t[hd:hd + 1, :], -jnp.inf))
                m2.append((cb * decay).astype(BF16))
            ybuf[:, lanes] = y_g[:, jp * LANES:(jp + 1) * LANES] + _dot(
                jnp.concatenate(m2, axis=1), x2)

    return _gated_group_norm(ybuf[...], xc_ref[tok, :D_SSM].astype(F32),
                             zs_ref[tok, :].astype(F32), dskip_ref[...], normw_ref[...])


def _write_final_state(hT, hTo_ref, ho_ref):
    hTo_ref[0] = hT[...]
    for g in range(SSM_GROUPS):
        ho_ref[0, g] = hT[g].T


def _mixer_kernel(zs_ref, xc_ref, dt_ref, dtb_ref, alog_ref, dskip_ref, normw_ref, expand_ref,
                  h0_ref, yb_ref, hTo_ref, ho_ref, hT, ybuf, *, tb, valid_len):
    i = pl.program_id(1)

    @pl.when(i == 0)
    def _():
        hT[...] = h0_ref[...]

    masks = _ssd_masks()

    def chunk_body(c, carry):
        r0 = pl.multiple_of(c * CHUNK, CHUNK)
        tok = pl.ds(r0, CHUNK)
        yn = _ssd_chunk(tok, r0, masks, zs_ref, xc_ref, dt_ref, dtb_ref, alog_ref, dskip_ref,
                        normw_ref, expand_ref, hT, ybuf, valid_len if valid_len < tb else None)
        yb_ref[tok, :] = yn.astype(yb_ref.dtype)
        return carry

    lax.fori_loop(0, tb // CHUNK, chunk_body, 0)

    @pl.when(i == pl.num_programs(1) - 1)
    def _():
        _write_final_state(hT, hTo_ref, ho_ref)


HALF = 2 * CHUNK


def _mixer_outproj_kernel(zs_ref, xc_ref, dt_ref, ya_ref, sga_ref, sgb_ref, h_ref,
                          dtb_ref, alog_ref, dskip_ref, normw_ref, expand_ref, h0_ref,
                          wa_ref, wb_ref, wo_ref,
                          o_ref, hTo_ref, ho_ref, hT, ybuf, yb_scr, ya_scr, *, tb):
    i = pl.program_id(1)

    @pl.when(i == 0)
    def _():
        hT[...] = h0_ref[...]

    masks = _ssd_masks()
    ya_scr[...] = _dot(ya_ref[...], wa_ref[...])

    def project(rows):
        y_b = _dot(yb_scr[rows, :], wb_ref[...])
        merged = (sga_ref[rows, :].astype(F32) * ya_scr[rows, :]
                  + sgb_ref[rows, :].astype(F32) * y_b)
        o_ref[rows, :] = h_ref[rows, :] + _dot(merged.astype(BF16), wo_ref[...])

    for c in range(tb // CHUNK):
        r0 = c * CHUNK
        tok = slice(r0, r0 + CHUNK)
        yn = _ssd_chunk(tok, r0, masks, zs_ref, xc_ref, dt_ref, dtb_ref, alog_ref, dskip_ref,
                        normw_ref, expand_ref, hT, ybuf, None)
        yb_scr[tok, :] = yn.astype(BF16)
        if (r0 + CHUNK) % HALF == 0:
            project(slice(r0 + CHUNK - HALF, r0 + CHUNK))

    @pl.when(i == pl.num_programs(1) - 1)
    def _():
        _write_final_state(hT, hTo_ref, ho_ref)


def _mixer_outproj(zs, xc, dt, ya, sga, sgb, h, p, expand, h0, wa, wb, wo, *, batch, seqlen, tb):
    nt = seqlen // tb
    row = lambda w: pl.BlockSpec((tb, w), lambda b, i: (b * nt + i, 0))
    full = lambda shape: pl.BlockSpec(shape, lambda b, i: (0,) * len(shape))
    once = lambda shape: pl.BlockSpec(shape, lambda b, i: (0,) * len(shape), pipeline_mode=pl.Buffered(1))
    per_b = lambda shape: pl.BlockSpec((1,) + shape, lambda b, i: (b,) + (0,) * len(shape))
    return pl.pallas_call(
        functools.partial(_mixer_outproj_kernel, tb=tb),
        grid=(batch, nt),
        in_specs=[row(D_SSM), row(D_XBC), row(LANES), row(D_MODEL), row(D_MODEL), row(D_MODEL),
                  row(D_MODEL),
                  full((1, LANES)), full((1, LANES)), full((1, D_SSM)), full((1, D_SSM)),
                  once((LANES, D_SSM)), full((SSM_GROUPS, SSM_STATE, GROUP_W)),
                  once((D_MODEL, D_MODEL)), once((D_SSM, D_MODEL)), once((D_MODEL, D_MODEL))],
        out_specs=[row(D_MODEL), per_b((SSM_GROUPS, SSM_STATE, GROUP_W)),
                   per_b((SSM_GROUPS, GROUP_W, SSM_STATE))],
        out_shape=[jax.ShapeDtypeStruct((batch * seqlen, D_MODEL), F32),
                   jax.ShapeDtypeStruct((batch, SSM_GROUPS, SSM_STATE, GROUP_W), F32),
                   jax.ShapeDtypeStruct((batch, SSM_GROUPS, GROUP_W, SSM_STATE), F32)],
        scratch_shapes=[pltpu.VMEM((SSM_GROUPS, SSM_STATE, GROUP_W), F32),
                        pltpu.VMEM((CHUNK, D_SSM), F32),
                        pltpu.VMEM((tb, D_SSM), BF16),
                        pltpu.VMEM((tb, D_MODEL), F32)],
        compiler_params=pltpu.CompilerParams(
            dimension_semantics=("arbitrary", "arbitrary"), vmem_limit_bytes=VMEM_LIMIT),
        name="mixer_outproj",
    )(zs, xc, dt, ya, sga, sgb, h, p["dt_bias"], p["a_log"], p["d_skip"], p["ssm_norm_w"], expand,
      h0, wa, wb, wo)


def _mixer(zs, xc, dt, p, expand, h0, *, batch, seqlen, tb, valid_len):
    nt = seqlen // tb
    row = lambda w: pl.BlockSpec((tb, w), lambda b, i: (b * nt + i, 0))
    full = lambda shape: pl.BlockSpec(shape, lambda b, i: (0,) * len(shape))
    per_b = lambda shape: pl.BlockSpec((1,) + shape, lambda b, i: (b,) + (0,) * len(shape))
    return pl.pallas_call(
        functools.partial(_mixer_kernel, tb=tb, valid_len=valid_len),
        grid=(batch, nt),
        in_specs=[row(D_SSM), row(D_XBC), row(LANES),
                  full((1, LANES)), full((1, LANES)), full((1, D_SSM)), full((1, D_SSM)),
                  full((LANES, D_SSM)), full((SSM_GROUPS, SSM_STATE, GROUP_W))],
        out_specs=[row(D_SSM), per_b((SSM_GROUPS, SSM_STATE, GROUP_W)),
                   per_b((SSM_GROUPS, GROUP_W, SSM_STATE))],
        out_shape=[jax.ShapeDtypeStruct((batch * seqlen, D_SSM), BF16),
                   jax.ShapeDtypeStruct((batch, SSM_GROUPS, SSM_STATE, GROUP_W), F32),
                   jax.ShapeDtypeStruct((batch, SSM_GROUPS, GROUP_W, SSM_STATE), F32)],
        scratch_shapes=[pltpu.VMEM((SSM_GROUPS, SSM_STATE, GROUP_W), F32),
                        pltpu.VMEM((CHUNK, D_SSM), F32)],
        compiler_params=pltpu.CompilerParams(
            dimension_semantics=("arbitrary", "arbitrary"), vmem_limit_bytes=VMEM_LIMIT),
        name="mixer",
    )(zs, xc, dt, p["dt_bias"], p["a_log"], p["d_skip"], p["ssm_norm_w"], expand, h0)


SEQ_BLOCK = SUBLANES


def _decode_mixer_kernel(ab_ref, ach_ref, zs_ref, xbc_ref, dt_ref, sa_ref, sx_ref, h0_ref,
                         caw_ref, cxw_ref, cxb_ref, dtb_ref, alog_ref, dskip_ref, normw_ref,
                         gsum_ref, expand_ref,
                         ya_ref, yb_ref, sao_ref, sxo_ref, ho_ref,
                         xc_buf, bc_buf, os_buf, xw_buf, cd_buf, y_buf, *, n_tok):
    sb = SEQ_BLOCK

    xp = [sa_ref[k] for k in range(CONV_A_TAPS - 1)] + [ach_ref[t] for t in range(n_tok)]
    for t in range(n_tok):
        conv = caw_ref[0:1, :] * xp[t]
        for k in range(1, CONV_A_TAPS):
            conv += caw_ref[k:k + 1, :] * xp[t + k]
        ya_ref[t] = ab_ref[t] * conv
    for k in range(CONV_A_TAPS - 1):
        sao_ref[k] = xp[n_tok + k]

    xp = [sx_ref[k] for k in range(CONV_X_TAPS - 1)] + [xbc_ref[t] for t in range(n_tok)]
    for t in range(n_tok):
        conv = cxw_ref[0:1, :] * xp[t]
        for k in range(1, CONV_X_TAPS):
            conv += cxw_ref[k:k + 1, :] * xp[t + k]
        xc_buf[t * sb:(t + 1) * sb, :] = _silu(conv + cxb_ref[...])
    for k in range(CONV_X_TAPS - 1):
        sxo_ref[k] = xp[n_tok + k]

    a_neg = -jnp.exp(alog_ref[...])
    dts, acs = [], []
    for t in range(n_tok):
        dts.append(_softplus(dt_ref[t] + dtb_ref[...]))
        acs.append(dts[t] * a_neg + (acs[t - 1] if t else 0.0))

    pairs = [(qi, si) for qi in range(n_tok) for si in range(qi + 1)]
    prods = [xc_buf[qi * sb:(qi + 1) * sb, D_SSM + D_BC:] * xc_buf[si * sb:(si + 1) * sb, D_SSM:D_SSM + D_BC]
             for qi, si in pairs]
    p_hi, p_lo = _split2(jnp.concatenate(prods, axis=0))
    cbh = _dot(p_hi, gsum_ref[...]) + _dot(p_lo, gsum_ref[...])
    coef = [jnp.exp(acs[t]) for t in range(n_tok)]
    coef += [cbh[k * sb:(k + 1) * sb] * jnp.exp(acs[qi] - acs[si]) * dts[si]
             for k, (qi, si) in enumerate(pairs)]
    coef += [jnp.exp(acs[n_tok - 1] - acs[s]) * dts[s] for s in range(n_tok)]
    coef.append(jnp.exp(acs[n_tok - 1]))
    c_hi, c_lo = _split2(jnp.concatenate(coef, axis=0))
    coef_x = _dot(c_hi, expand_ref[...]) + _dot(c_lo, expand_ref[...])
    slab = lambda k: coef_x[k * sb:(k + 1) * sb]

    def store_tiles(buf, r0, val):
        for lt in range(val.shape[1] // LANES):
            buf[lt, r0:r0 + val.shape[0], :] = val[:, lt * LANES:(lt + 1) * LANES]

    def load_rows(buf, lt0, n_lt, rows):
        return jnp.concatenate([buf[lt0 + j, rows, :] for j in range(n_lt)], axis=1)

    store_tiles(bc_buf, 0, xc_buf[:, D_SSM:])
    store_tiles(os_buf, 0, coef_x[0:n_tok * sb])
    diag = {pr: slab(n_tok + k) for k, pr in enumerate(pairs)}
    k0 = n_tok + len(pairs)
    for s in range(n_tok):
        store_tiles(xw_buf, s * sb, slab(k0 + s) * xc_buf[s * sb:(s + 1) * sb, :D_SSM])
    cd_hi, cd_lo = _split2(slab(k0 + n_tok))
    store_tiles(cd_buf, 0, cd_hi.astype(F32))
    store_tiles(cd_buf, sb, cd_lo.astype(F32))

    for t in range(n_tok):
        acc = diag[(t, 0)] * xc_buf[0:sb, :D_SSM]
        for s in range(1, t + 1):
            acc += diag[(t, s)] * xc_buf[s * sb:(s + 1) * sb, :D_SSM]
        store_tiles(y_buf, t * sb, acc)

    gt = GROUP_W // LANES
    k_rows = 2 * SUBLANES
    krow = lax.broadcasted_iota(jnp.int32, (k_rows, SSM_STATE), 0)
    ones_rows = jnp.where((krow >= n_tok) & (krow < n_tok + 2), 1.0, 0.0).astype(BF16)
    pad_lhs = jnp.zeros((k_rows - n_tok - 2, GROUP_W), F32)
    pad_rhs = jnp.zeros((k_rows - n_tok, SSM_STATE), F32)

    def seq_body(b, carry):
        tok_rows = pl.ds(b, n_tok, stride=sb)
        for g in range(SSM_GROUPS):
            b_bg = bc_buf[g, tok_rows, :]
            c_bg = bc_buf[SSM_GROUPS + g, tok_rows, :]
            heads = pl.ds(g * HEADS_PER_GROUP, HEADS_PER_GROUP)
            h_bg = h0_ref[b, heads].reshape(GROUP_W, SSM_STATE)
            y_off = lax.dot_general(c_bg.astype(BF16), h_bg.astype(BF16),
                                    (((1,), (1,)), ((), ())), preferred_element_type=F32)
            y_off = y_off * load_rows(os_buf, g * gt, gt, tok_rows)
            for j in range(gt):
                y_buf[g * gt + j, tok_rows, :] = (y_buf[g * gt + j, tok_rows, :]
                                                  + y_off[:, j * LANES:(j + 1) * LANES])
            lhs = jnp.concatenate(
                [load_rows(xw_buf, g * gt, gt, tok_rows),
                 load_rows(cd_buf, g * gt, gt, pl.ds(b, 2, stride=sb)), pad_lhs],
                axis=0).astype(BF16)
            rhs = jnp.concatenate(
                [jnp.concatenate([b_bg, pad_rhs], axis=0).astype(BF16), ones_rows], axis=1)
            upd = lax.dot_general(lhs, rhs, (((0,), (0,)), ((), ())), preferred_element_type=F32)
            new = h_bg * upd[:, SSM_STATE:] + upd[:, :SSM_STATE]
            ho_ref[b, heads] = new.reshape(HEADS_PER_GROUP, SSM_HEAD_DIM, SSM_STATE)
        return carry

    lax.fori_loop(0, sb, seq_body, 0, unroll=2)

    xs_all = xc_buf[:, :D_SSM]
    zs_all = jnp.concatenate([zs_ref[t] for t in range(n_tok)], axis=0)
    y_all = load_rows(y_buf, 0, D_SSM // LANES, slice(None))
    yn = _gated_group_norm(y_all, xs_all, zs_all, dskip_ref[...], normw_ref[...])
    for t in range(n_tok):
        yb_ref[t] = yn[t * sb:(t + 1) * sb]


def _decode_mixer(ab, ach, zs, xbc, dt, sa, sx, h0, p, gsum, expand, *, n_tok, n_seq):
    sb = SEQ_BLOCK
    tok = lambda w: pl.BlockSpec((n_tok, sb, w), lambda i: (0, i, 0))
    st = lambda k, w: pl.BlockSpec((k, sb, w), lambda i: (0, i, 0))
    hblk = pl.BlockSpec((sb, SSM_HEADS, SSM_HEAD_DIM, SSM_STATE), lambda i: (i, 0, 0, 0))
    full = lambda shape: pl.BlockSpec(shape, lambda i: (0,) * len(shape))
    return pl.pallas_call(
        functools.partial(_decode_mixer_kernel, n_tok=n_tok),
        grid=(n_seq // sb,),
        in_specs=[tok(D_MODEL), tok(D_MODEL), tok(D_SSM), tok(D_XBC), tok(LANES),
                  st(CONV_A_TAPS - 1, D_MODEL), st(CONV_X_TAPS - 1, D_XBC), hblk,
                  full((CONV_A_TAPS, D_MODEL)), full((CONV_X_TAPS, D_XBC)), full((1, D_XBC)),
                  full((1, LANES)), full((1, LANES)), full((1, D_SSM)), full((1, D_SSM)),
                  full((D_BC, LANES)), full((LANES, D_SSM))],
        out_specs=[tok(D_MODEL), tok(D_SSM), st(CONV_A_TAPS - 1, D_MODEL),
                   st(CONV_X_TAPS - 1, D_XBC), hblk],
        out_shape=[jax.ShapeDtypeStruct((n_tok, n_seq, D_MODEL), F32),
                   jax.ShapeDtypeStruct((n_tok, n_seq, D_SSM), F32),
                   jax.ShapeDtypeStruct((CONV_A_TAPS - 1, n_seq, D_MODEL), F32),
                   jax.ShapeDtypeStruct((CONV_X_TAPS - 1, n_seq, D_XBC), F32),
                   jax.ShapeDtypeStruct((n_seq, SSM_HEADS, SSM_HEAD_DIM, SSM_STATE), F32)],
        scratch_shapes=[pltpu.VMEM((n_tok * sb, D_XBC), F32),
                        pltpu.VMEM((2 * D_BC // LANES, n_tok * sb, LANES), F32),
                        pltpu.VMEM((D_SSM // LANES, n_tok * sb, LANES), F32),
                        pltpu.VMEM((D_SSM // LANES, n_tok * sb, LANES), F32),
                        pltpu.VMEM((D_SSM // LANES, 2 * sb, LANES), F32),
                        pltpu.VMEM((D_SSM // LANES, n_tok * sb, LANES), F32)],
        compiler_params=pltpu.CompilerParams(
            dimension_semantics=("parallel",), vmem_limit_bytes=VMEM_LIMIT),
        name="decode_mixer",
    )(ab, ach, zs, xbc, dt, sa, sx, h0, p["conv_a_w"], p["ssm_conv_w"], p["ssm_conv_b"],
      p["dt_bias"], p["a_log"], p["d_skip"], p["ssm_norm_w"], gsum, expand)


def _row_tile(t, cap):
    if t <= cap:
        return t
    best = LANES
    for cand in range(LANES, cap + 1, LANES):
        if t % cand == 0:
            best = cand
    return best


def kernel(x_prompt, x_sample, state_conv_a, state_ssm_conv, state_ssm, meta_tokens, norm_ffn1, ffn1_w_gu, ffn1_w_down, norm_mix, w_in, conv_a_w, w_a_out, ssm_conv_w, ssm_conv_b, dt_bias, a_log, d_skip, ssm_norm_w, w_b_out, w_o, norm_ffn2, ffn2_w_gu, ffn2_w_down, norm_final):
    bsz, seqlen, _ = x_prompt.shape
    n_seq, n_tok, _ = x_sample.shape
    n_dt = SSM_HEADS
    c_dt = 3 * D_MODEL + D_SSM + D_XBC

    w_gu1, w_dn1 = ffn1_w_gu[0].astype(BF16), ffn1_w_down[0].astype(BF16)
    pad_heads = lambda v: jnp.pad(v[0], (0, LANES - n_dt))[None]
    p = {
        "conv_a_w": conv_a_w[0], "ssm_conv_w": ssm_conv_w[0], "ssm_conv_b": ssm_conv_b[0][None],
        "dt_bias": pad_heads(dt_bias), "a_log": pad_heads(a_log),
        "d_skip": jnp.repeat(d_skip[0], SSM_HEAD_DIM)[None], "ssm_norm_w": ssm_norm_w[0][None],
    }
    n1, nm, n2, nf = norm_ffn1[0][None], norm_mix[0][None], norm_ffn2[0][None], norm_final[None]
    eh = lax.broadcasted_iota(jnp.int32, (LANES, D_SSM), 0)
    ec = lax.broadcasted_iota(jnp.int32, (LANES, D_SSM), 1)
    expand = (ec // SSM_HEAD_DIM == eh).astype(BF16)

    def ffn1(x, cap=512, cast=()):
        return _ffn(x, n1, w_gu1, w_dn1, nf, tm=_row_tile(x.shape[0], cap), final_norm=False, cast=cast)

    n_steps = bsz * seqlen // _row_tile(bsz * seqlen, 512)
    blk = lambda w: (w, SUBLANES * 2 * pl.cdiv(w.shape[0], n_steps * SUBLANES * 2))
    h_p, w_in_t, w_gu2, w_dn2, wa, wb, wo = ffn1(
        x_prompt.reshape(bsz * seqlen, D_MODEL),
        cast=[blk(jnp.swapaxes(w_in[0], 0, 1)), blk(ffn2_w_gu[0]), blk(ffn2_w_down[0]),
              blk(w_a_out[0]), blk(w_b_out[0]), blk(w_o[0])])
    w_proj = (w_in_t,) * 3

    def ffn2(h):
        return _ffn(h, n2, w_gu2, w_dn2, nf, tm=_row_tile(h.shape[0], 512), final_norm=True)

    zeros = lambda *s: jnp.zeros(s, F32)
    x_s = jnp.swapaxes(x_sample, 0, 1).reshape(n_tok * n_seq, D_MODEL)
    h_sm = ffn1(jnp.concatenate([x_s, jnp.pad(meta_tokens, ((0, CHUNK - N_META), (0, 0)))], axis=0),
                cap=n_tok * n_seq + CHUNK)
    h_s, h_m = h_sm[:n_tok * n_seq], h_sm[n_tok * n_seq:]
    _, zs, xc, dt, _, _, ca_m, cx_m = _inproj_conv(
        h_m, nm, w_proj, p, zeros(SUBLANES, D_MODEL), zeros(SUBLANES, D_XBC),
        tm=CHUNK, seq_tiles=1, valid_len=N_META)
    _, hT_m, _ = _mixer(zs, xc, dt, p, expand, zeros(SSM_GROUPS, SSM_STATE, GROUP_W),
                        batch=1, seqlen=CHUNK, tb=CHUNK, valid_len=N_META)

    tm = _row_tile(seqlen, 512)
    ya, zs, xc, dt, sga, sgb, ca_p, cx_p = _inproj_conv(
        h_p, nm, w_proj, p, ca_m[0], cx_m[0], tm=tm, seq_tiles=seqlen // tm, valid_len=tm)
    tb = _row_tile(seqlen, 512)
    h2_p, _, h_out_p = _mixer_outproj(zs, xc, dt, ya, sga, sgb, h_p, p, expand, hT_m[0], wa, wb, wo,
                                      batch=bsz, seqlen=seqlen, tb=tb)
    y_prompt = ffn2(h2_p).reshape(bsz, seqlen, D_MODEL)
    prompt_conv_a = ca_p[None, :, SUBLANES - (CONV_A_TAPS - 1):, :]
    prompt_ssm_conv = cx_p[None, :, SUBLANES - (CONV_X_TAPS - 1):, :]
    prompt_ssm = h_out_p.reshape(1, bsz, SSM_HEADS, SSM_HEAD_DIM, SSM_STATE)

    ab, ach, zs, xbc, dt, sga, sgb = _inproj(h_s, nm, w_proj, tm=_row_tile(n_tok * n_seq, 256), act_dtype=F32)
    tsf = lambda a: a.reshape(n_tok, n_seq, a.shape[-1])
    hh = lax.broadcasted_iota(jnp.int32, (D_BC, LANES), 1)
    kk = lax.broadcasted_iota(jnp.int32, (D_BC, LANES), 0)
    gsum = ((hh < SSM_HEADS) & (kk // SSM_STATE == hh // HEADS_PER_GROUP)).astype(BF16)
    ya, yb, sa_new, sx_new, sample_ssm = _decode_mixer(
        tsf(ab), tsf(ach), tsf(zs), tsf(xbc), tsf(dt),
        jnp.swapaxes(state_conv_a[0], 0, 1), jnp.swapaxes(state_ssm_conv[0], 0, 1), state_ssm[0],
        p, gsum, expand, n_tok=n_tok, n_seq=n_seq)
    flat = lambda a: a.reshape(n_tok * n_seq, a.shape[-1])
    y_s = ffn2(_outproj(flat(ya), flat(yb), sga, sgb, h_s, wa, wb, wo, tm=_row_tile(n_tok * n_seq, 512)))
    y_sample = jnp.swapaxes(y_s.reshape(n_tok, n_seq, D_MODEL), 0, 1)
    sample_conv_a = jnp.swapaxes(sa_new, 0, 1)[None]
    sample_ssm_conv = jnp.swapaxes(sx_new, 0, 1)[None]

    return (y_prompt, y_sample, prompt_conv_a, prompt_ssm_conv, prompt_ssm,
            sample_conv_a, sample_ssm_conv, sample_ssm[None])
```

```python
import functools

import jax
import jax.numpy as jnp
from jax import lax
from jax.experimental import pallas as pl
from jax.experimental.pallas import tpu as pltpu

D_MODEL = 1024
D_FF = 2816
D_SSM = 2048
SSM_HEADS = 32
SSM_HEAD_DIM = 64
SSM_GROUPS = 4
HEADS_PER_GROUP = SSM_HEADS // SSM_GROUPS
SSM_STATE = 128
D_BC = SSM_GROUPS * SSM_STATE
D_XBC = D_SSM + 2 * D_BC
CONV_A_TAPS = 3
CONV_X_TAPS = 4
N_META = 16
EPS = 1e-6

LANES = 128
SUBLANES = 8
BF16_ROWS = 2 * SUBLANES
CHUNK = 128
VMEM_LIMIT = 56 * 1024 * 1024

F32 = jnp.float32
BF16 = jnp.bfloat16


def _dot(a, b):
    return jnp.dot(a, b, preferred_element_type=F32)


def _rms(x, w):
    return x * lax.rsqrt(jnp.mean(x * x, axis=-1, keepdims=True) + EPS) * w


def _silu(x):
    return x * jax.nn.sigmoid(x)


def _softplus(x):
    return jnp.maximum(x, 0.0) + jnp.log1p(jnp.exp(-jnp.abs(x)))


def _split3(x):
    hi = x.astype(BF16)
    r = x - hi.astype(F32)
    mid = r.astype(BF16)
    lo = (r - mid.astype(F32)).astype(BF16)
    return hi, mid, lo


def _split2(x):
    hi = x.astype(BF16)
    lo = (x - hi.astype(F32)).astype(BF16)
    return hi, lo


def _resident(shape):
    return pl.BlockSpec(shape, lambda *_: (0,) * len(shape), pipeline_mode=pl.Buffered(1))


def _ffn_kernel(x_ref, nw_ref, wg_ref, wu_ref, wd_ref, fw_ref, *rest, final_norm, cast_blocks):
    n_cast = len(cast_blocks)
    cast_in, o_ref, cast_out = rest[:n_cast], rest[n_cast], rest[n_cast + 1:]
    x = x_ref[...]
    xn = _rms(x, nw_ref[...]).astype(BF16)
    act = (_silu(_dot(xn, wg_ref[...])) * _dot(xn, wu_ref[...])).astype(BF16)
    h = x + 0.5 * _dot(act, wd_ref[...])
    if final_norm:
        h = _rms(h, fw_ref[...])
    o_ref[...] = h
    for src, dst, n_blocks in zip(cast_in, cast_out, cast_blocks):
        @pl.when(pl.program_id(0) < n_blocks)
        def _(src=src, dst=dst):
            dst[...] = src[...].astype(BF16)


def _ffn(x, nw, w_gu, w_down, fw, *, tm, final_norm, cast=()):
    t = x.shape[0]
    once = pl.Buffered(1)
    n_blocks = [pl.cdiv(w.shape[0], rows) for w, rows in cast]
    assert all(nb <= t // tm for nb in n_blocks)
    cast_specs = [pl.BlockSpec((rows, w.shape[1]), functools.partial(
        lambda i, last: (jnp.minimum(i, last), 0), last=nb - 1))
        for (w, rows), nb in zip(cast, n_blocks)]
    out = pl.pallas_call(
        functools.partial(_ffn_kernel, final_norm=final_norm, cast_blocks=tuple(n_blocks)),
        grid=(t // tm,),
        in_specs=[
            pl.BlockSpec((tm, D_MODEL), lambda i: (i, 0)),
            _resident((1, D_MODEL)),
            pl.BlockSpec((D_MODEL, D_FF), lambda i: (0, 0), pipeline_mode=once),
            pl.BlockSpec((D_MODEL, D_FF), lambda i: (0, 1), pipeline_mode=once),
            _resident((D_FF, D_MODEL)),
            _resident((1, D_MODEL)),
        ] + cast_specs,
        out_specs=[pl.BlockSpec((tm, D_MODEL), lambda i: (i, 0))] + cast_specs,
        out_shape=[jax.ShapeDtypeStruct((t, D_MODEL), F32)]
        + [jax.ShapeDtypeStruct(w.shape, BF16) for w, _ in cast],
        compiler_params=pltpu.CompilerParams(
            dimension_semantics=("arbitrary" if cast else "parallel",), vmem_limit_bytes=VMEM_LIMIT),
        name="ffn",
    )(x, nw, w_gu, w_gu, w_down, fw, *[w for w, _ in cast])
    return out if cast else out[0]


C_DT = 3 * D_MODEL + D_SSM + D_XBC
C_GATE = C_DT + LANES


def _project(u, w_refs, c0, width):
    w_main, w_dt, w_gate = w_refs
    if c0 < C_DT:
        w = w_main[c0:c0 + width, :]
    elif c0 < C_GATE:
        w = w_dt[c0 - C_DT:c0 - C_DT + width, :]
    else:
        w = w_gate[c0 - C_GATE:c0 - C_GATE + width, :]
    return lax.dot_general(u, w, (((1,), (1,)), ((), ())), preferred_element_type=F32)


def _proj_weight_specs():
    once = pl.Buffered(1)
    return [pl.BlockSpec((C_DT, D_MODEL), lambda *_: (0, 0), pipeline_mode=once),
            pl.BlockSpec((LANES, D_MODEL), lambda *_: (C_DT // LANES, 0), pipeline_mode=once),
            pl.BlockSpec((pl.Element(2 * D_MODEL), pl.Element(D_MODEL)),
                         lambda *_: (C_DT + SSM_HEADS, 0), pipeline_mode=once)]


def _inproj_kernel(h_ref, nw_ref, wm_ref, wdt_ref, wg_ref,
                   ab_ref, ach_ref, zs_ref, xbc_ref, dt_ref, sga_ref, sgb_ref):
    u = _rms(h_ref[...], nw_ref[...]).astype(BF16)

    def proj(c0, width):
        return _project(u, (wm_ref, wdt_ref, wg_ref), c0, width)

    ab_ref[...] = proj(0, D_MODEL).astype(ab_ref.dtype)
    ach_ref[...] = (proj(D_MODEL, D_MODEL) * proj(2 * D_MODEL, D_MODEL)).astype(ach_ref.dtype)
    c0 = 3 * D_MODEL
    for k in range(D_SSM // D_MODEL):
        zs_ref[:, k * D_MODEL:(k + 1) * D_MODEL] = _silu(proj(c0 + k * D_MODEL, D_MODEL)).astype(zs_ref.dtype)
    c0 += D_SSM
    for k in range(D_XBC // D_MODEL):
        xbc_ref[:, k * D_MODEL:(k + 1) * D_MODEL] = proj(c0 + k * D_MODEL, D_MODEL).astype(xbc_ref.dtype)
    c0 += D_XBC
    dt_ref[...] = proj(c0, LANES)
    c0 += LANES
    sga_ref[...] = jax.nn.sigmoid(proj(c0, D_MODEL)).astype(sga_ref.dtype)
    sgb_ref[...] = jax.nn.sigmoid(proj(c0 + D_MODEL, D_MODEL)).astype(sgb_ref.dtype)


def _inproj(h, nw, w_packed, *, tm, act_dtype):
    t = h.shape[0]
    widths = (D_MODEL, D_MODEL, D_SSM, D_XBC, LANES, D_MODEL, D_MODEL)
    dtypes = (act_dtype, act_dtype, act_dtype, act_dtype, F32, act_dtype, act_dtype)
    return pl.pallas_call(
        _inproj_kernel,
        grid=(t // tm,),
        in_specs=[pl.BlockSpec((tm, D_MODEL), lambda i: (i, 0)), _resident((1, D_MODEL))]
        + _proj_weight_specs(),
        out_specs=[pl.BlockSpec((tm, w), lambda i: (i, 0)) for w in widths],
        out_shape=[jax.ShapeDtypeStruct((t, w), d) for w, d in zip(widths, dtypes)],
        compiler_params=pltpu.CompilerParams(
            dimension_semantics=("parallel",), vmem_limit_bytes=VMEM_LIMIT),
        name="inproj",
    )(h, nw, *w_packed)


COL_BLOCK = 512
N_PARK = 8


def _inproj_conv_kernel(h_ref, nw_ref, wm_ref, wdt_ref, wg_ref, caw_ref, cxw_ref, cxb_ref,
                        ca0_ref, cx0_ref,
                        ya_ref, zs_ref, xc_ref, dt_ref, sga_ref, sgb_ref, cao_ref, cxo_ref,
                        ca_carry, cx_carry, *stages, tm, seq_tiles, valid_len):
    i = pl.program_id(0)

    @pl.when(i % seq_tiles == 0)
    def _():
        ca_carry[...] = ca0_ref[...]
        cx_carry[...] = cx0_ref[...]

    u = _rms(h_ref[...], nw_ref[...]).astype(BF16)

    def proj(c0, width=COL_BLOCK):
        return _project(u, (wm_ref, wdt_ref, wg_ref), c0, width)

    n8 = tm // SUBLANES
    sublane = lax.broadcasted_iota(jnp.int32, (n8, SUBLANES, COL_BLOCK), 1)

    def causal_conv(x, carry_ref, w_ref, taps, cols):
        cur = x.reshape(n8, SUBLANES, COL_BLOCK)
        prev = jnp.concatenate([carry_ref[:, cols], x[:tm - SUBLANES]], axis=0).reshape(cur.shape)
        acc = w_ref[taps - 1:taps, cols] * x
        for d in range(1, taps):
            merged = jnp.where(sublane < SUBLANES - d, cur, prev)
            acc += w_ref[taps - 1 - d:taps - d, cols] * pltpu.roll(merged, d, axis=1).reshape(x.shape)
        carry_ref[:, cols] = x[valid_len - SUBLANES:valid_len]
        return acc

    c_z, c_x, c_dt = 3 * D_MODEL, 3 * D_MODEL + D_SSM, 3 * D_MODEL + D_SSM + D_XBC
    c_g = c_dt + LANES

    def tail_a(k, cols, prj):
        conv = causal_conv(prj[1] * prj[2], ca_carry, caw_ref, CONV_A_TAPS, cols)
        ya_ref[:, cols] = (prj[0] * conv).astype(ya_ref.dtype)

    def tail_z(k, cols, prj):
        zs_ref[:, cols] = _silu(prj[0]).astype(zs_ref.dtype)

    def tail_x(k, cols, prj):
        conv = causal_conv(prj[0], cx_carry, cxw_ref, CONV_X_TAPS, cols)
        xc_ref[:, cols] = _silu(conv + cxb_ref[:, cols]).astype(xc_ref.dtype)

    def tail_g(k, cols, prj):
        sga_ref[:, cols] = jax.nn.sigmoid(prj[0]).astype(sga_ref.dtype)
        sgb_ref[:, cols] = jax.nn.sigmoid(prj[1]).astype(sgb_ref.dtype)

    def blocks(tail, width, starts):
        return [(tail, k, [c + k * COL_BLOCK for c in starts]) for k in range(width // COL_BLOCK)]

    light = blocks(tail_z, D_SSM, [c_z])
    for n, (blk_a, blk_g) in enumerate(zip(blocks(tail_a, D_MODEL, [0, D_MODEL, 2 * D_MODEL]),
                                           blocks(tail_g, D_MODEL, [c_g, c_g + D_MODEL]))):
        light.insert(3 * n, blk_a)
        light.insert(3 * n + 2, blk_g)
    heavy = blocks(tail_x, D_XBC, [c_x])
    order = heavy + light
    park = jnp.minimum(i, 0)
    slot = 0
    for tail, k, starts in order:
        parked = []
        for c in starts:
            stages[slot % len(stages)][park] = proj(c)
            parked.append(stages[slot % len(stages)])
            slot += 1
        tail(k, slice(k * COL_BLOCK, (k + 1) * COL_BLOCK), [s[park] for s in parked])
    dt_ref[...] = proj(c_dt, LANES)
    cao_ref[0] = ca_carry[...]
    cxo_ref[0] = cx_carry[...]


def _inproj_conv(h, nw, w_packed, p, ca0, cx0, *, tm, seq_tiles, valid_len):
    t = h.shape[0]
    n_seq = t // (tm * seq_tiles)
    widths = (D_MODEL, D_SSM, D_XBC, LANES, D_MODEL, D_MODEL)
    dtypes = (BF16, BF16, BF16, F32, BF16, BF16)
    tail = lambda w: pl.BlockSpec((1, SUBLANES, w), lambda i: (i // seq_tiles, 0, 0))
    return pl.pallas_call(
        functools.partial(_inproj_conv_kernel, tm=tm, seq_tiles=seq_tiles, valid_len=valid_len),
        grid=(t // tm,),
        in_specs=[pl.BlockSpec((tm, D_MODEL), lambda i: (i, 0)), _resident((1, D_MODEL))]
        + _proj_weight_specs()
        + [_resident((CONV_A_TAPS, D_MODEL)), _resident((CONV_X_TAPS, D_XBC)),
           _resident((1, D_XBC)),
           _resident((SUBLANES, D_MODEL)), _resident((SUBLANES, D_XBC))],
        out_specs=[pl.BlockSpec((tm, w), lambda i: (i, 0)) for w in widths]
        + [tail(D_MODEL), tail(D_XBC)],
        out_shape=[jax.ShapeDtypeStruct((t, w), d) for w, d in zip(widths, dtypes)]
        + [jax.ShapeDtypeStruct((n_seq, SUBLANES, D_MODEL), F32),
           jax.ShapeDtypeStruct((n_seq, SUBLANES, D_XBC), F32)],
        scratch_shapes=[pltpu.VMEM((SUBLANES, D_MODEL), F32), pltpu.VMEM((SUBLANES, D_XBC), F32),
                        ] + [pltpu.VMEM((1, tm, COL_BLOCK), F32)] * N_PARK,
        compiler_params=pltpu.CompilerParams(
            dimension_semantics=("arbitrary",), vmem_limit_bytes=VMEM_LIMIT),
        name="inproj_conv",
    )(h, nw, *w_packed, p["conv_a_w"], p["ssm_conv_w"], p["ssm_conv_b"], ca0, cx0)


def _outproj_kernel(ya_ref, yb_ref, sga_ref, sgb_ref, h_ref, wa_ref, wb_ref, wo_ref, o_ref):
    y_a = _dot(ya_ref[...].astype(BF16), wa_ref[...])
    y_b = _dot(yb_ref[...].astype(BF16), wb_ref[...])
    merged = sga_ref[...].astype(F32) * y_a + sgb_ref[...].astype(F32) * y_b
    o_ref[...] = h_ref[...] + _dot(merged.astype(BF16), wo_ref[...])


def _outproj(ya, yb, sga, sgb, h, wa, wb, wo, *, tm):
    t = h.shape[0]
    row = lambda w: pl.BlockSpec((tm, w), lambda i: (i, 0))
    return pl.pallas_call(
        _outproj_kernel,
        grid=(t // tm,),
        in_specs=[row(D_MODEL), row(D_SSM), row(D_MODEL), row(D_MODEL), row(D_MODEL),
                  _resident((D_MODEL, D_MODEL)), _resident((D_SSM, D_MODEL)),
                  _resident((D_MODEL, D_MODEL))],
        out_specs=row(D_MODEL),
        out_shape=jax.ShapeDtypeStruct((t, D_MODEL), F32),
        compiler_params=pltpu.CompilerParams(
            dimension_semantics=("parallel",), vmem_limit_bytes=VMEM_LIMIT),
        name="outproj",
    )(ya, yb, sga, sgb, h, wa, wb, wo)


def _gated_group_norm(y, xs, zs, dskip, normw):
    y = (y + dskip * xs) * zs
    gw = D_SSM // SSM_GROUPS
    parts = []
    for g in range(SSM_GROUPS):
        yg = y[:, g * gw:(g + 1) * gw]
        parts.append(yg * lax.rsqrt(jnp.mean(yg * yg, axis=-1, keepdims=True) + EPS))
    return jnp.concatenate(parts, axis=-1) * normw


GROUP_W = HEADS_PER_GROUP * SSM_HEAD_DIM
LOG2E = 1.4426950408889634


def _ssd_masks():
    q = CHUNK
    rows = lax.broadcasted_iota(jnp.int32, (q, q), 0)
    cols = lax.broadcasted_iota(jnp.int32, (q, q), 1)
    causal = rows >= cols
    first_head = lax.broadcasted_iota(jnp.int32, (q, LANES), 1) < SSM_HEAD_DIM
    return causal, causal.astype(BF16), first_head


def _ssd_chunk(tok, r0, masks, zs_ref, xc_ref, dt_ref, dtb_ref, alog_ref, dskip_ref, normw_ref,
               expand_ref, hT, ybuf, valid_len):
    q = CHUNK
    causal, tri, first_head = masks
    a_neg = -jnp.exp(alog_ref[...])
    dt = _softplus(dt_ref[tok, :] + dtb_ref[...])
    if valid_len is not None:
        t_idx = r0 + lax.broadcasted_iota(jnp.int32, (q, LANES), 0)
        dt = jnp.where(t_idx < valid_len, dt, 0.0)
    hi, mid, lo = _split3(dt * a_neg)
    acs3 = _dot(tri, jnp.concatenate([hi, mid, lo], axis=1))
    acs = acs3[:, :LANES] + acs3[:, LANES:2 * LANES] + acs3[:, 2 * LANES:]
    acs2 = acs * LOG2E
    row_t = (acs2 - jnp.log2(dt)).T
    acs_end = acs2[q - 1:q, :]
    cd_hi, cd_lo = _split2(jnp.broadcast_to(jnp.exp2(acs_end), (SUBLANES, LANES)))
    stack = jnp.concatenate(
        [jnp.exp2(acs2).astype(BF16), (jnp.exp2(acs_end - acs2) * dt).astype(BF16),
         cd_hi, cd_lo], axis=0)

    for g in range(SSM_GROUPS):
        gcols = slice(g * GROUP_W, (g + 1) * GROUP_W)
        ex = _dot(stack, expand_ref[:, gcols])
        state_decay = ex[2 * q:2 * q + 1] + ex[2 * q + SUBLANES:2 * q + SUBLANES + 1]
        b_t = xc_ref[tok, D_SSM + g * SSM_STATE:D_SSM + (g + 1) * SSM_STATE].astype(F32).T.astype(BF16)
        c_bf = xc_ref[tok, D_SSM + D_BC + g * SSM_STATE:D_SSM + D_BC + (g + 1) * SSM_STATE]
        cb = _dot(c_bf, b_t)
        h_g = hT[g]
        y_g = ex[0:q] * _dot(c_bf, h_g.astype(BF16))
        xw_g = xc_ref[tok, gcols] * ex[q:2 * q].astype(BF16)
        hT[g] = h_g * state_decay + _dot(b_t, xw_g)
        for jp in range(HEADS_PER_GROUP // 2):
            pair = g * (HEADS_PER_GROUP // 2) + jp
            lanes = slice(pair * LANES, (pair + 1) * LANES)
            x_pair = xc_ref[tok, lanes]
            zero = jnp.zeros_like(x_pair)
            x2 = jnp.concatenate([jnp.where(first_head, x_pair, zero),
                                  jnp.where(first_head, zero, x_pair)], axis=0)
            m2 = []
            for e in range(2):
                hd = 2 * pair + e
                acs_col = jnp.broadcast_to(acs2[:, hd:hd + 1], (q, q))
                decay = jnp.exp2(jnp.where(causal, acs_col - row_t[hd:hd + 1, :], -jnp.inf))
                m2.append((cb * decay).astype(BF16))
            ybuf[:, lanes] = y_g[:, jp * LANES:(jp + 1) * LANES] + _dot(
                jnp.concatenate(m2, axis=1), x2)

    return _gated_group_norm(ybuf[...], xc_ref[tok, :D_SSM].astype(F32),
                             zs_ref[tok, :].astype(F32), dskip_ref[...], normw_ref[...])


def _write_final_state(hT, hTo_ref, ho_ref):
    hTo_ref[0] = hT[...]
    for g in range(SSM_GROUPS):
        ho_ref[0, g] = hT[g].T


def _mixer_kernel(zs_ref, xc_ref, dt_ref, dtb_ref, alog_ref, dskip_ref, normw_ref, expand_ref,
                  h0_ref, yb_ref, hTo_ref, ho_ref, hT, ybuf, *, tb, valid_len):
    i = pl.program_id(1)

    @pl.when(i == 0)
    def _():
        hT[...] = h0_ref[...]

    masks = _ssd_masks()

    def chunk_body(c, carry):
        r0 = pl.multiple_of(c * CHUNK, CHUNK)
        tok = pl.ds(r0, CHUNK)
        yn = _ssd_chunk(tok, r0, masks, zs_ref, xc_ref, dt_ref, dtb_ref, alog_ref, dskip_ref,
                        normw_ref, expand_ref, hT, ybuf, valid_len if valid_len < tb else None)
        yb_ref[tok, :] = yn.astype(yb_ref.dtype)
        return carry

    lax.fori_loop(0, tb // CHUNK, chunk_body, 0)

    @pl.when(i == pl.num_programs(1) - 1)
    def _():
        _write_final_state(hT, hTo_ref, ho_ref)


HALF = 2 * CHUNK


def _mixer_outproj_kernel(zs_ref, xc_ref, dt_ref, ya_ref, sga_ref, sgb_ref, h_ref,
                          dtb_ref, alog_ref, dskip_ref, normw_ref, expand_ref, h0_ref,
                          wa_ref, wb_ref, wo_ref,
                          o_ref, hTo_ref, ho_ref, hT, ybuf, yb_scr, ya_scr, *, tb):
    i = pl.program_id(1)

    @pl.when(i == 0)
    def _():
        hT[...] = h0_ref[...]

    masks = _ssd_masks()
    ya_scr[...] = _dot(ya_ref[...], wa_ref[...])

    def project(rows):
        y_b = _dot(yb_scr[rows, :], wb_ref[...])
        merged = (sga_ref[rows, :].astype(F32) * ya_scr[rows, :]
                  + sgb_ref[rows, :].astype(F32) * y_b)
        o_ref[rows, :] = h_ref[rows, :] + _dot(merged.astype(BF16), wo_ref[...])

    for c in range(tb // CHUNK):
        r0 = c * CHUNK
        tok = slice(r0, r0 + CHUNK)
        yn = _ssd_chunk(tok, r0, masks, zs_ref, xc_ref, dt_ref, dtb_ref, alog_ref, dskip_ref,
                        normw_ref, expand_ref, hT, ybuf, None)
        yb_scr[tok, :] = yn.astype(BF16)
        if (r0 + CHUNK) % HALF == 0:
            project(slice(r0 + CHUNK - HALF, r0 + CHUNK))

    @pl.when(i == pl.num_programs(1) - 1)
    def _():
        _write_final_state(hT, hTo_ref, ho_ref)


def _mixer_outproj(zs, xc, dt, ya, sga, sgb, h, p, expand, h0, wa, wb, wo, *, batch, seqlen, tb):
    nt = seqlen // tb
    row = lambda w: pl.BlockSpec((tb, w), lambda b, i: (b * nt + i, 0))
    full = lambda shape: pl.BlockSpec(shape, lambda b, i: (0,) * len(shape))
    once = lambda shape: pl.BlockSpec(shape, lambda b, i: (0,) * len(shape), pipeline_mode=pl.Buffered(1))
    per_b = lambda shape: pl.BlockSpec((1,) + shape, lambda b, i: (b,) + (0,) * len(shape))
    return pl.pallas_call(
        functools.partial(_mixer_outproj_kernel, tb=tb),
        grid=(batch, nt),
        in_specs=[row(D_SSM), row(D_XBC), row(LANES), row(D_MODEL), row(D_MODEL), row(D_MODEL),
                  row(D_MODEL),
                  full((1, LANES)), full((1, LANES)), full((1, D_SSM)), full((1, D_SSM)),
                  once((LANES, D_SSM)), full((SSM_GROUPS, SSM_STATE, GROUP_W)),
                  once((D_MODEL, D_MODEL)), once((D_SSM, D_MODEL)), once((D_MODEL, D_MODEL))],
        out_specs=[row(D_MODEL), per_b((SSM_GROUPS, SSM_STATE, GROUP_W)),
                   per_b((SSM_GROUPS, GROUP_W, SSM_STATE))],
        out_shape=[jax.ShapeDtypeStruct((batch * seqlen, D_MODEL), F32),
                   jax.ShapeDtypeStruct((batch, SSM_GROUPS, SSM_STATE, GROUP_W), F32),
                   jax.ShapeDtypeStruct((batch, SSM_GROUPS, GROUP_W, SSM_STATE), F32)],
        scratch_shapes=[pltpu.VMEM((SSM_GROUPS, SSM_STATE, GROUP_W), F32),
                        pltpu.VMEM((CHUNK, D_SSM), F32),
                        pltpu.VMEM((tb, D_SSM), BF16),
                        pltpu.VMEM((tb, D_MODEL), F32)],
        compiler_params=pltpu.CompilerParams(
            dimension_semantics=("arbitrary", "arbitrary"), vmem_limit_bytes=VMEM_LIMIT),
        name="mixer_outproj",
    )(zs, xc, dt, ya, sga, sgb, h, p["dt_bias"], p["a_log"], p["d_skip"], p["ssm_norm_w"], expand,
      h0, wa, wb, wo)


def _mixer(zs, xc, dt, p, expand, h0, *, batch, seqlen, tb, valid_len):
    nt = seqlen // tb
    row = lambda w: pl.BlockSpec((tb, w), lambda b, i: (b * nt + i, 0))
    full = lambda shape: pl.BlockSpec(shape, lambda b, i: (0,) * len(shape))
    per_b = lambda shape: pl.BlockSpec((1,) + shape, lambda b, i: (b,) + (0,) * len(shape))
    return pl.pallas_call(
        functools.partial(_mixer_kernel, tb=tb, valid_len=valid_len),
        grid=(batch, nt),
        in_specs=[row(D_SSM), row(D_XBC), row(LANES),
                  full((1, LANES)), full((1, LANES)), full((1, D_SSM)), full((1, D_SSM)),
                  full((LANES, D_SSM)), full((SSM_GROUPS, SSM_STATE, GROUP_W))],
        out_specs=[row(D_SSM), per_b((SSM_GROUPS, SSM_STATE, GROUP_W)),
                   per_b((SSM_GROUPS, GROUP_W, SSM_STATE))],
        out_shape=[jax.ShapeDtypeStruct((batch * seqlen, D_SSM), BF16),
                   jax.ShapeDtypeStruct((batch, SSM_GROUPS, SSM_STATE, GROUP_W), F32),
                   jax.ShapeDtypeStruct((batch, SSM_GROUPS, GROUP_W, SSM_STATE), F32)],
        scratch_shapes=[pltpu.VMEM((SSM_GROUPS, SSM_STATE, GROUP_W), F32),
                        pltpu.VMEM((CHUNK, D_SSM), F32)],
        compiler_params=pltpu.CompilerParams(
            dimension_semantics=("arbitrary", "arbitrary"), vmem_limit_bytes=VMEM_LIMIT),
        name="mixer",
    )(zs, xc, dt, p["dt_bias"], p["a_log"], p["d_skip"], p["ssm_norm_w"], expand, h0)


SEQ_BLOCK = SUBLANES


def _decode_mixer_kernel(ab_ref, ach_ref, zs_ref, xbc_ref, dt_ref, sa_ref, sx_ref, h0_ref,
                         caw_ref, cxw_ref, cxb_ref, dtb_ref, alog_ref, dskip_ref, normw_ref,
                         gsum_ref, expand_ref,
                         ya_ref, yb_ref, sao_ref, sxo_ref, ho_ref,
                         xc_buf, bc_buf, os_buf, xw_buf, cd_buf, y_buf, *, n_tok):
    sb = SEQ_BLOCK

    xp = [sa_ref[k] for k in range(CONV_A_TAPS - 1)] + [ach_ref[t] for t in range(n_tok)]
    for t in range(n_tok):
        conv = caw_ref[0:1, :] * xp[t]
        for k in range(1, CONV_A_TAPS):
            conv += caw_ref[k:k + 1, :] * xp[t + k]
        ya_ref[t] = ab_ref[t] * conv
    for k in range(CONV_A_TAPS - 1):
        sao_ref[k] = xp[n_tok + k]

    xp = [sx_ref[k] for k in range(CONV_X_TAPS - 1)] + [xbc_ref[t] for t in range(n_tok)]
    for t in range(n_tok):
        conv = cxw_ref[0:1, :] * xp[t]
        for k in range(1, CONV_X_TAPS):
            conv += cxw_ref[k:k + 1, :] * xp[t + k]
        xc_buf[t * sb:(t + 1) * sb, :] = _silu(conv + cxb_ref[...])
    for k in range(CONV_X_TAPS - 1):
        sxo_ref[k] = xp[n_tok + k]

    a_neg = -jnp.exp(alog_ref[...])
    dts, acs = [], []
    for t in range(n_tok):
        dts.append(_softplus(dt_ref[t] + dtb_ref[...]))
        acs.append(dts[t] * a_neg + (acs[t - 1] if t else 0.0))

    pairs = [(qi, si) for qi in range(n_tok) for si in range(qi + 1)]
    prods = [xc_buf[qi * sb:(qi + 1) * sb, D_SSM + D_BC:] * xc_buf[si * sb:(si + 1) * sb, D_SSM:D_SSM + D_BC]
             for qi, si in pairs]
    p_hi, p_lo = _split2(jnp.concatenate(prods, axis=0))
    cbh = _dot(p_hi, gsum_ref[...]) + _dot(p_lo, gsum_ref[...])
    coef = [jnp.exp(acs[t]) for t in range(n_tok)]
    coef += [cbh[k * sb:(k + 1) * sb] * jnp.exp(acs[qi] - acs[si]) * dts[si]
             for k, (qi, si) in enumerate(pairs)]
    coef += [jnp.exp(acs[n_tok - 1] - acs[s]) * dts[s] for s in range(n_tok)]
    coef.append(jnp.exp(acs[n_tok - 1]))
    c_hi, c_lo = _split2(jnp.concatenate(coef, axis=0))
    coef_x = _dot(c_hi, expand_ref[...]) + _dot(c_lo, expand_ref[...])
    slab = lambda k: coef_x[k * sb:(k + 1) * sb]

    def store_tiles(buf, r0, val):
        for lt in range(val.shape[1] // LANES):
            buf[lt, r0:r0 + val.shape[0], :] = val[:, lt * LANES:(lt + 1) * LANES]

    def load_rows(buf, lt0, n_lt, rows):
        return jnp.concatenate([buf[lt0 + j, rows, :] for j in range(n_lt)], axis=1)

    store_tiles(bc_buf, 0, xc_buf[:, D_SSM:])
    store_tiles(os_buf, 0, coef_x[0:n_tok * sb])
    diag = {pr: slab(n_tok + k) for k, pr in enumerate(pairs)}
    k0 = n_tok + len(pairs)
    for s in range(n_tok):
        store_tiles(xw_buf, s * sb, slab(k0 + s) * xc_buf[s * sb:(s + 1) * sb, :D_SSM])
    cd_hi, cd_lo = _split2(slab(k0 + n_tok))
    store_tiles(cd_buf, 0, cd_hi.astype(F32))
    store_tiles(cd_buf, sb, cd_lo.astype(F32))

    for t in range(n_tok):
        acc = diag[(t, 0)] * xc_buf[0:sb, :D_SSM]
        for s in range(1, t + 1):
            acc += diag[(t, s)] * xc_buf[s * sb:(s + 1) * sb, :D_SSM]
        store_tiles(y_buf, t * sb, acc)

    gt = GROUP_W // LANES
    k_rows = BF16_ROWS
    krow = lax.broadcasted_iota(jnp.int32, (k_rows, SSM_STATE), 0)
    ones_rows = jnp.where((krow >= n_tok) & (krow < n_tok + 2), 1.0, 0.0).astype(BF16)
    pad_lhs = jnp.zeros((k_rows - n_tok - 2, GROUP_W), F32)
    pad_rhs = jnp.zeros((k_rows - n_tok, SSM_STATE), F32)

    def seq_body(b, carry):
        tok_rows = pl.ds(b, n_tok, stride=sb)
        for g in range(SSM_GROUPS):
            b_bg = bc_buf[g, tok_rows, :]
            c_bg = bc_buf[SSM_GROUPS + g, tok_rows, :]
            heads = pl.ds(g * HEADS_PER_GROUP, HEADS_PER_GROUP)
            h_bg = h0_ref[b, heads].reshape(GROUP_W, SSM_STATE)
            y_off = lax.dot_general(c_bg.astype(BF16), h_bg.astype(BF16),
                                    (((1,), (1,)), ((), ())), preferred_element_type=F32)
            y_off = y_off * load_rows(os_buf, g * gt, gt, tok_rows)
            for j in range(gt):
                y_buf[g * gt + j, tok_rows, :] = (y_buf[g * gt + j, tok_rows, :]
                                                  + y_off[:, j * LANES:(j + 1) * LANES])
            lhs = jnp.concatenate(
                [load_rows(xw_buf, g * gt, gt, tok_rows),
                 load_rows(cd_buf, g * gt, gt, pl.ds(b, 2, stride=sb)), pad_lhs],
                axis=0).astype(BF16)
            rhs = jnp.concatenate(
                [jnp.concatenate([b_bg, pad_rhs], axis=0).astype(BF16), ones_rows], axis=1)
            upd = lax.dot_general(lhs, rhs, (((0,), (0,)), ((), ())), preferred_element_type=F32)
            new = h_bg * upd[:, SSM_STATE:] + upd[:, :SSM_STATE]
            ho_ref[b, heads] = new.reshape(HEADS_PER_GROUP, SSM_HEAD_DIM, SSM_STATE)
        return carry

    lax.fori_loop(0, sb, seq_body, 0, unroll=4)

    xs_all = xc_buf[:, :D_SSM]
    zs_all = jnp.concatenate([zs_ref[t] for t in range(n_tok)], axis=0)
    y_all = load_rows(y_buf, 0, D_SSM // LANES, slice(None))
    yn = _gated_group_norm(y_all, xs_all, zs_all, dskip_ref[...], normw_ref[...])
    for t in range(n_tok):
        yb_ref[t] = yn[t * sb:(t + 1) * sb]


def _decode_mixer(ab, ach, zs, xbc, dt, sa, sx, h0, p, gsum, expand, *, n_tok, n_seq):
    sb = SEQ_BLOCK
    tok = lambda w: pl.BlockSpec((n_tok, sb, w), lambda i: (0, i, 0))
    st = lambda k, w: pl.BlockSpec((k, sb, w), lambda i: (0, i, 0))
    hblk = pl.BlockSpec((sb, SSM_HEADS, SSM_HEAD_DIM, SSM_STATE), lambda i: (i, 0, 0, 0))
    full = lambda shape: pl.BlockSpec(shape, lambda i: (0,) * len(shape))
    return pl.pallas_call(
        functools.partial(_decode_mixer_kernel, n_tok=n_tok),
        grid=(n_seq // sb,),
        in_specs=[tok(D_MODEL), tok(D_MODEL), tok(D_SSM), tok(D_XBC), tok(LANES),
                  st(CONV_A_TAPS - 1, D_MODEL), st(CONV_X_TAPS - 1, D_XBC), hblk,
                  full((CONV_A_TAPS, D_MODEL)), full((CONV_X_TAPS, D_XBC)), full((1, D_XBC)),
                  full((1, LANES)), full((1, LANES)), full((1, D_SSM)), full((1, D_SSM)),
                  full((D_BC, LANES)), full((LANES, D_SSM))],
        out_specs=[tok(D_MODEL), tok(D_SSM), st(CONV_A_TAPS - 1, D_MODEL),
                   st(CONV_X_TAPS - 1, D_XBC), hblk],
        out_shape=[jax.ShapeDtypeStruct((n_tok, n_seq, D_MODEL), F32),
                   jax.ShapeDtypeStruct((n_tok, n_seq, D_SSM), F32),
                   jax.ShapeDtypeStruct((CONV_A_TAPS - 1, n_seq, D_MODEL), F32),
                   jax.ShapeDtypeStruct((CONV_X_TAPS - 1, n_seq, D_XBC), F32),
                   jax.ShapeDtypeStruct((n_seq, SSM_HEADS, SSM_HEAD_DIM, SSM_STATE), F32)],
        scratch_shapes=[pltpu.VMEM((n_tok * sb, D_XBC), F32),
                        pltpu.VMEM((2 * D_BC // LANES, n_tok * sb, LANES), F32),
                        pltpu.VMEM((D_SSM // LANES, n_tok * sb, LANES), F32),
                        pltpu.VMEM((D_SSM // LANES, n_tok * sb, LANES), F32),
                        pltpu.VMEM((D_SSM // LANES, 2 * sb, LANES), F32),
                        pltpu.VMEM((D_SSM // LANES, n_tok * sb, LANES), F32)],
        compiler_params=pltpu.CompilerParams(
            dimension_semantics=("parallel",), vmem_limit_bytes=VMEM_LIMIT),
        name="decode_mixer",
    )(ab, ach, zs, xbc, dt, sa, sx, h0, p["conv_a_w"], p["ssm_conv_w"], p["ssm_conv_b"],
      p["dt_bias"], p["a_log"], p["d_skip"], p["ssm_norm_w"], gsum, expand)


def _row_tile(t, cap):
    if t <= cap:
        return t
    best = LANES
    for cand in range(LANES, cap + 1, LANES):
        if t % cand == 0:
            best = cand
    return best


def kernel(x_prompt, x_sample, state_conv_a, state_ssm_conv, state_ssm, meta_tokens, norm_ffn1, ffn1_w_gu, ffn1_w_down, norm_mix, w_in, conv_a_w, w_a_out, ssm_conv_w, ssm_conv_b, dt_bias, a_log, d_skip, ssm_norm_w, w_b_out, w_o, norm_ffn2, ffn2_w_gu, ffn2_w_down, norm_final):
    bsz, seqlen, _ = x_prompt.shape
    n_seq, n_tok, _ = x_sample.shape
    n_dt = SSM_HEADS
    c_dt = 3 * D_MODEL + D_SSM + D_XBC

    w_gu1, w_dn1 = ffn1_w_gu[0].astype(BF16), ffn1_w_down[0].astype(BF16)
    pad_heads = lambda v: jnp.pad(v[0], (0, LANES - n_dt))[None]
    p = {
        "conv_a_w": conv_a_w[0], "ssm_conv_w": ssm_conv_w[0], "ssm_conv_b": ssm_conv_b[0][None],
        "dt_bias": pad_heads(dt_bias), "a_log": pad_heads(a_log),
        "d_skip": jnp.repeat(d_skip[0], SSM_HEAD_DIM)[None], "ssm_norm_w": ssm_norm_w[0][None],
    }
    n1, nm, n2, nf = norm_ffn1[0][None], norm_mix[0][None], norm_ffn2[0][None], norm_final[None]
    eh = lax.broadcasted_iota(jnp.int32, (LANES, D_SSM), 0)
    ec = lax.broadcasted_iota(jnp.int32, (LANES, D_SSM), 1)
    expand = (ec // SSM_HEAD_DIM == eh).astype(BF16)

    def ffn1(x, cap=512, cast=()):
        return _ffn(x, n1, w_gu1, w_dn1, nf, tm=_row_tile(x.shape[0], cap), final_norm=False, cast=cast)

    n_steps = bsz * seqlen // _row_tile(bsz * seqlen, 512)
    blk = lambda w: (w, BF16_ROWS * pl.cdiv(w.shape[0], n_steps * BF16_ROWS))
    h_p, w_in_t, w_gu2, w_dn2, wa, wb, wo = ffn1(
        x_prompt.reshape(bsz * seqlen, D_MODEL),
        cast=[blk(jnp.swapaxes(w_in[0], 0, 1)), blk(ffn2_w_gu[0]), blk(ffn2_w_down[0]),
              blk(w_a_out[0]), blk(w_b_out[0]), blk(w_o[0])])
    w_proj = (w_in_t,) * 3

    def ffn2(h):
        return _ffn(h, n2, w_gu2, w_dn2, nf, tm=_row_tile(h.shape[0], 512), final_norm=True)

    zeros = lambda *s: jnp.zeros(s, F32)
    x_s = jnp.swapaxes(x_sample, 0, 1).reshape(n_tok * n_seq, D_MODEL)
    h_sm = ffn1(jnp.concatenate([x_s, jnp.pad(meta_tokens, ((0, CHUNK - N_META), (0, 0)))], axis=0),
                cap=n_tok * n_seq + CHUNK)
    h_s, h_m = h_sm[:n_tok * n_seq], h_sm[n_tok * n_seq:]
    _, zs, xc, dt, _, _, ca_m, cx_m = _inproj_conv(
        h_m, nm, w_proj, p, zeros(SUBLANES, D_MODEL), zeros(SUBLANES, D_XBC),
        tm=CHUNK, seq_tiles=1, valid_len=N_META)
    _, hT_m, _ = _mixer(zs, xc, dt, p, expand, zeros(SSM_GROUPS, SSM_STATE, GROUP_W),
                        batch=1, seqlen=CHUNK, tb=CHUNK, valid_len=N_META)

    tm = _row_tile(seqlen, 512)
    ya, zs, xc, dt, sga, sgb, ca_p, cx_p = _inproj_conv(
        h_p, nm, w_proj, p, ca_m[0], cx_m[0], tm=tm, seq_tiles=seqlen // tm, valid_len=tm)
    tb = _row_tile(seqlen, 512)
    h2_p, _, h_out_p = _mixer_outproj(zs, xc, dt, ya, sga, sgb, h_p, p, expand, hT_m[0], wa, wb, wo,
                                      batch=bsz, seqlen=seqlen, tb=tb)
    y_prompt = ffn2(h2_p).reshape(bsz, seqlen, D_MODEL)
    prompt_conv_a = ca_p[None, :, SUBLANES - (CONV_A_TAPS - 1):, :]
    prompt_ssm_conv = cx_p[None, :, SUBLANES - (CONV_X_TAPS - 1):, :]
    prompt_ssm = h_out_p.reshape(1, bsz, SSM_HEADS, SSM_HEAD_DIM, SSM_STATE)

    ab, ach, zs, xbc, dt, sga, sgb = _inproj(h_s, nm, w_proj, tm=_row_tile(n_tok * n_seq, 256), act_dtype=F32)
    tsf = lambda a: a.reshape(n_tok, n_seq, a.shape[-1])
    hh = lax.broadcasted_iota(jnp.int32, (D_BC, LANES), 1)
    kk = lax.broadcasted_iota(jnp.int32, (D_BC, LANES), 0)
    gsum = ((hh < SSM_HEADS) & (kk // SSM_STATE == hh // HEADS_PER_GROUP)).astype(BF16)
    ya, yb, sa_new, sx_new, sample_ssm = _decode_mixer(
        tsf(ab), tsf(ach), tsf(zs), tsf(xbc), tsf(dt),
        jnp.swapaxes(state_conv_a[0], 0, 1), jnp.swapaxes(state_ssm_conv[0], 0, 1), state_ssm[0],
        p, gsum, expand, n_tok=n_tok, n_seq=n_seq)
    flat = lambda a: a.reshape(n_tok * n_seq, a.shape[-1])
    y_s = ffn2(_outproj(flat(ya), flat(yb), sga, sgb, h_s, wa, wb, wo, tm=_row_tile(n_tok * n_seq, 512)))
    y_sample = jnp.swapaxes(y_s.reshape(n_tok, n_seq, D_MODEL), 0, 1)
    sample_conv_a = jnp.swapaxes(sa_new, 0, 1)[None]
    sample_ssm_conv = jnp.swapaxes(sx_new, 0, 1)[None]

    return (y_prompt, y_sample, prompt_conv_a, prompt_ssm_conv, prompt_ssm,
            sample_conv_a, sample_ssm_conv, sample_ssm[None])
```

```python
import functools

import jax
import jax.numpy as jnp
from jax import lax
from jax.experimental import pallas as pl
from jax.experimental.pallas import tpu as pltpu

D_MODEL = 1024
D_FF = 2816
D_SSM = 2048
SSM_HEADS = 32
SSM_HEAD_DIM = 64
SSM_GROUPS = 4
HEADS_PER_GROUP = SSM_HEADS // SSM_GROUPS
SSM_STATE = 128
D_BC = SSM_GROUPS * SSM_STATE
D_XBC = D_SSM + 2 * D_BC
CONV_A_TAPS = 3
CONV_X_TAPS = 4
N_META = 16
EPS = 1e-6

LANES = 128
SUBLANES = 8
BF16_ROWS = 2 * SUBLANES
CHUNK = 128
VMEM_LIMIT = 56 * 1024 * 1024

F32 = jnp.float32
BF16 = jnp.bfloat16


def _dot(a, b):
    return jnp.dot(a, b, preferred_element_type=F32)


def _rms(x, w):
    return x * lax.rsqrt(jnp.mean(x * x, axis=-1, keepdims=True) + EPS) * w


def _silu(x):
    return x * jax.nn.sigmoid(x)


def _softplus(x):
    return jnp.maximum(x, 0.0) + jnp.log1p(jnp.exp(-jnp.abs(x)))


def _split3(x):
    hi = x.astype(BF16)
    r = x - hi.astype(F32)
    mid = r.astype(BF16)
    lo = (r - mid.astype(F32)).astype(BF16)
    return hi, mid, lo


def _split2(x):
    hi = x.astype(BF16)
    lo = (x - hi.astype(F32)).astype(BF16)
    return hi, lo


def _resident(shape):
    return pl.BlockSpec(shape, lambda *_: (0,) * len(shape), pipeline_mode=pl.Buffered(1))


def _ffn_kernel(x_ref, nw_ref, wg_ref, wu_ref, wd_ref, fw_ref, *rest, final_norm, cast_blocks):
    n_cast = len(cast_blocks)
    cast_in, o_ref, cast_out = rest[:n_cast], rest[n_cast], rest[n_cast + 1:]
    x = x_ref[...]
    xn = _rms(x, nw_ref[...]).astype(BF16)
    act = (_silu(_dot(xn, wg_ref[...])) * _dot(xn, wu_ref[...])).astype(BF16)
    h = x + 0.5 * _dot(act, wd_ref[...])
    if final_norm:
        h = _rms(h, fw_ref[...])
    o_ref[...] = h
    for src, dst, n_blocks in zip(cast_in, cast_out, cast_blocks):
        @pl.when(pl.program_id(0) < n_blocks)
        def _(src=src, dst=dst):
            dst[...] = src[...].astype(BF16)


def _ffn(x, nw, w_gu, w_down, fw, *, tm, final_norm, cast=()):
    t = x.shape[0]
    once = pl.Buffered(1)
    n_blocks = [pl.cdiv(w.shape[0], rows) for w, rows in cast]
    assert all(nb <= t // tm for nb in n_blocks)
    cast_specs = [pl.BlockSpec((rows, w.shape[1]), functools.partial(
        lambda i, last: (jnp.minimum(i, last), 0), last=nb - 1))
        for (w, rows), nb in zip(cast, n_blocks)]
    out = pl.pallas_call(
        functools.partial(_ffn_kernel, final_norm=final_norm, cast_blocks=tuple(n_blocks)),
        grid=(t // tm,),
        in_specs=[
            pl.BlockSpec((tm, D_MODEL), lambda i: (i, 0)),
            _resident((1, D_MODEL)),
            pl.BlockSpec((D_MODEL, D_FF), lambda i: (0, 0), pipeline_mode=once),
            pl.BlockSpec((D_MODEL, D_FF), lambda i: (0, 1), pipeline_mode=once),
            _resident((D_FF, D_MODEL)),
            _resident((1, D_MODEL)),
        ] + cast_specs,
        out_specs=[pl.BlockSpec((tm, D_MODEL), lambda i: (i, 0))] + cast_specs,
        out_shape=[jax.ShapeDtypeStruct((t, D_MODEL), F32)]
        + [jax.ShapeDtypeStruct(w.shape, BF16) for w, _ in cast],
        compiler_params=pltpu.CompilerParams(
            dimension_semantics=("arbitrary" if cast else "parallel",), vmem_limit_bytes=VMEM_LIMIT),
        name="ffn",
    )(x, nw, w_gu, w_gu, w_down, fw, *[w for w, _ in cast])
    return out if cast else out[0]


C_DT = 3 * D_MODEL + D_SSM + D_XBC
C_GATE = C_DT + LANES


def _project(u, w_refs, c0, width):
    w_main, w_dt, w_gate = w_refs
    if c0 < C_DT:
        w = w_main[c0:c0 + width, :]
    elif c0 < C_GATE:
        w = w_dt[c0 - C_DT:c0 - C_DT + width, :]
    else:
        w = w_gate[c0 - C_GATE:c0 - C_GATE + width, :]
    return lax.dot_general(u, w, (((1,), (1,)), ((), ())), preferred_element_type=F32)


def _proj_weight_specs():
    once = pl.Buffered(1)
    return [pl.BlockSpec((C_DT, D_MODEL), lambda *_: (0, 0), pipeline_mode=once),
            pl.BlockSpec((LANES, D_MODEL), lambda *_: (C_DT // LANES, 0), pipeline_mode=once),
            pl.BlockSpec((pl.Element(2 * D_MODEL), pl.Element(D_MODEL)),
                         lambda *_: (C_DT + SSM_HEADS, 0), pipeline_mode=once)]


def _inproj_kernel(h_ref, nw_ref, wm_ref, wdt_ref, wg_ref,
                   ab_ref, ach_ref, zs_ref, xbc_ref, dt_ref, sga_ref, sgb_ref):
    u = _rms(h_ref[...], nw_ref[...]).astype(BF16)

    def proj(c0, width):
        return _project(u, (wm_ref, wdt_ref, wg_ref), c0, width)

    ab_ref[...] = proj(0, D_MODEL).astype(ab_ref.dtype)
    ach_ref[...] = (proj(D_MODEL, D_MODEL) * proj(2 * D_MODEL, D_MODEL)).astype(ach_ref.dtype)
    c0 = 3 * D_MODEL
    for k in range(D_SSM // D_MODEL):
        zs_ref[:, k * D_MODEL:(k + 1) * D_MODEL] = _silu(proj(c0 + k * D_MODEL, D_MODEL)).astype(zs_ref.dtype)
    c0 += D_SSM
    for k in range(D_XBC // D_MODEL):
        xbc_ref[:, k * D_MODEL:(k + 1) * D_MODEL] = proj(c0 + k * D_MODEL, D_MODEL).astype(xbc_ref.dtype)
    c0 += D_XBC
    dt_ref[...] = proj(c0, LANES)
    c0 += LANES
    sga_ref[...] = jax.nn.sigmoid(proj(c0, D_MODEL)).astype(sga_ref.dtype)
    sgb_ref[...] = jax.nn.sigmoid(proj(c0 + D_MODEL, D_MODEL)).astype(sgb_ref.dtype)


def _inproj(h, nw, w_packed, *, tm, act_dtype):
    t = h.shape[0]
    widths = (D_MODEL, D_MODEL, D_SSM, D_XBC, LANES, D_MODEL, D_MODEL)
    dtypes = (act_dtype, act_dtype, act_dtype, act_dtype, F32, act_dtype, act_dtype)
    return pl.pallas_call(
        _inproj_kernel,
        grid=(t // tm,),
        in_specs=[pl.BlockSpec((tm, D_MODEL), lambda i: (i, 0)), _resident((1, D_MODEL))]
        + _proj_weight_specs(),
        out_specs=[pl.BlockSpec((tm, w), lambda i: (i, 0)) for w in widths],
        out_shape=[jax.ShapeDtypeStruct((t, w), d) for w, d in zip(widths, dtypes)],
        compiler_params=pltpu.CompilerParams(
            dimension_semantics=("parallel",), vmem_limit_bytes=VMEM_LIMIT),
        name="inproj",
    )(h, nw, *w_packed)


COL_BLOCK = 512
N_PARK = 8


def _inproj_conv_kernel(h_ref, nw_ref, wm_ref, wdt_ref, wg_ref, caw_ref, cxw_ref, cxb_ref,
                        ca0_ref, cx0_ref,
                        ya_ref, zs_ref, xc_ref, dt_ref, sga_ref, sgb_ref, cao_ref, cxo_ref,
                        ca_carry, cx_carry, *stages, tm, seq_tiles, valid_len):
    i = pl.program_id(0)

    @pl.when(i % seq_tiles == 0)
    def _():
        ca_carry[...] = ca0_ref[...]
        cx_carry[...] = cx0_ref[...]

    u = _rms(h_ref[...], nw_ref[...]).astype(BF16)

    def proj(c0, width=COL_BLOCK):
        return _project(u, (wm_ref, wdt_ref, wg_ref), c0, width)

    n8 = tm // SUBLANES
    sublane = lax.broadcasted_iota(jnp.int32, (n8, SUBLANES, COL_BLOCK), 1)

    def causal_conv(x, carry_ref, w_ref, taps, cols):
        cur = x.reshape(n8, SUBLANES, COL_BLOCK)
        prev = jnp.concatenate([carry_ref[:, cols], x[:tm - SUBLANES]], axis=0).reshape(cur.shape)
        acc = w_ref[taps - 1:taps, cols] * x
        for d in range(1, taps):
            merged = jnp.where(sublane < SUBLANES - d, cur, prev)
            acc += w_ref[taps - 1 - d:taps - d, cols] * pltpu.roll(merged, d, axis=1).reshape(x.shape)
        carry_ref[:, cols] = x[valid_len - SUBLANES:valid_len]
        return acc

    c_z, c_x, c_dt = 3 * D_MODEL, 3 * D_MODEL + D_SSM, 3 * D_MODEL + D_SSM + D_XBC
    c_g = c_dt + LANES

    def tail_a(k, cols, prj):
        conv = causal_conv(prj[1] * prj[2], ca_carry, caw_ref, CONV_A_TAPS, cols)
        ya_ref[:, cols] = (prj[0] * conv).astype(ya_ref.dtype)

    def tail_z(k, cols, prj):
        zs_ref[:, cols] = _silu(prj[0]).astype(zs_ref.dtype)

    def tail_x(k, cols, prj):
        conv = causal_conv(prj[0], cx_carry, cxw_ref, CONV_X_TAPS, cols)
        xc_ref[:, cols] = _silu(conv + cxb_ref[:, cols]).astype(xc_ref.dtype)

    def tail_g(k, cols, prj):
        sga_ref[:, cols] = jax.nn.sigmoid(prj[0]).astype(sga_ref.dtype)
        sgb_ref[:, cols] = jax.nn.sigmoid(prj[1]).astype(sgb_ref.dtype)

    def blocks(tail, width, starts):
        return [(tail, k, [c + k * COL_BLOCK for c in starts]) for k in range(width // COL_BLOCK)]

    light = blocks(tail_z, D_SSM, [c_z])
    for n, (blk_a, blk_g) in enumerate(zip(blocks(tail_a, D_MODEL, [0, D_MODEL, 2 * D_MODEL]),
                                           blocks(tail_g, D_MODEL, [c_g, c_g + D_MODEL]))):
        light.insert(3 * n, blk_a)
        light.insert(3 * n + 2, blk_g)
    heavy = blocks(tail_x, D_XBC, [c_x])
    order = heavy + light
    park = jnp.minimum(i, 0)
    slot = 0
    for tail, k, starts in order:
        parked = []
        for c in starts:
            stages[slot % len(stages)][park] = proj(c)
            parked.append(stages[slot % len(stages)])
            slot += 1
        tail(k, slice(k * COL_BLOCK, (k + 1) * COL_BLOCK), [s[park] for s in parked])
    dt_ref[...] = proj(c_dt, LANES)
    cao_ref[0] = ca_carry[...]
    cxo_ref[0] = cx_carry[...]


def _inproj_conv(h, nw, w_packed, p, ca0, cx0, *, tm, seq_tiles, valid_len):
    t = h.shape[0]
    n_seq = t // (tm * seq_tiles)
    widths = (D_MODEL, D_SSM, D_XBC, LANES, D_MODEL, D_MODEL)
    dtypes = (BF16, BF16, BF16, F32, BF16, BF16)
    tail = lambda w: pl.BlockSpec((1, SUBLANES, w), lambda i: (i // seq_tiles, 0, 0))
    return pl.pallas_call(
        functools.partial(_inproj_conv_kernel, tm=tm, seq_tiles=seq_tiles, valid_len=valid_len),
        grid=(t // tm,),
        in_specs=[pl.BlockSpec((tm, D_MODEL), lambda i: (i, 0)), _resident((1, D_MODEL))]
        + _proj_weight_specs()
        + [_resident((CONV_A_TAPS, D_MODEL)), _resident((CONV_X_TAPS, D_XBC)),
           _resident((1, D_XBC)),
           _resident((SUBLANES, D_MODEL)), _resident((SUBLANES, D_XBC))],
        out_specs=[pl.BlockSpec((tm, w), lambda i: (i, 0)) for w in widths]
        + [tail(D_MODEL), tail(D_XBC)],
        out_shape=[jax.ShapeDtypeStruct((t, w), d) for w, d in zip(widths, dtypes)]
        + [jax.ShapeDtypeStruct((n_seq, SUBLANES, D_MODEL), F32),
           jax.ShapeDtypeStruct((n_seq, SUBLANES, D_XBC), F32)],
        scratch_shapes=[pltpu.VMEM((SUBLANES, D_MODEL), F32), pltpu.VMEM((SUBLANES, D_XBC), F32),
                        ] + [pltpu.VMEM((1, tm, COL_BLOCK), F32)] * N_PARK,
        compiler_params=pltpu.CompilerParams(
            dimension_semantics=("arbitrary",), vmem_limit_bytes=VMEM_LIMIT),
        name="inproj_conv",
    )(h, nw, *w_packed, p["conv_a_w"], p["ssm_conv_w"], p["ssm_conv_b"], ca0, cx0)


def _outproj_kernel(ya_ref, yb_ref, sga_ref, sgb_ref, h_ref, wa_ref, wb_ref, wo_ref, o_ref):
    y_a = _dot(ya_ref[...].astype(BF16), wa_ref[...])
    y_b = _dot(yb_ref[...].astype(BF16), wb_ref[...])
    merged = sga_ref[...].astype(F32) * y_a + sgb_ref[...].astype(F32) * y_b
    o_ref[...] = h_ref[...] + _dot(merged.astype(BF16), wo_ref[...])


def _outproj(ya, yb, sga, sgb, h, wa, wb, wo, *, tm):
    t = h.shape[0]
    row = lambda w: pl.BlockSpec((tm, w), lambda i: (i, 0))
    return pl.pallas_call(
        _outproj_kernel,
        grid=(t // tm,),
        in_specs=[row(D_MODEL), row(D_SSM), row(D_MODEL), row(D_MODEL), row(D_MODEL),
                  _resident((D_MODEL, D_MODEL)), _resident((D_SSM, D_MODEL)),
                  _resident((D_MODEL, D_MODEL))],
        out_specs=row(D_MODEL),
        out_shape=jax.ShapeDtypeStruct((t, D_MODEL), F32),
        compiler_params=pltpu.CompilerParams(
            dimension_semantics=("parallel",), vmem_limit_bytes=VMEM_LIMIT),
        name="outproj",
    )(ya, yb, sga, sgb, h, wa, wb, wo)


def _gated_group_norm(y, xs, zs, dskip, normw):
    y = (y + dskip * xs) * zs
    gw = D_SSM // SSM_GROUPS
    parts = []
    for g in range(SSM_GROUPS):
        yg = y[:, g * gw:(g + 1) * gw]
        parts.append(yg * lax.rsqrt(jnp.mean(yg * yg, axis=-1, keepdims=True) + EPS))
    return jnp.concatenate(parts, axis=-1) * normw


GROUP_W = HEADS_PER_GROUP * SSM_HEAD_DIM
LOG2E = 1.4426950408889634


def _ssd_masks():
    q = CHUNK
    rows = lax.broadcasted_iota(jnp.int32, (q, q), 0)
    cols = lax.broadcasted_iota(jnp.int32, (q, q), 1)
    causal = rows >= cols
    first_head = lax.broadcasted_iota(jnp.int32, (q, LANES), 1) < SSM_HEAD_DIM
    return causal, causal.astype(BF16), first_head


def _ssd_chunk(tok, r0, masks, zs_ref, xc_ref, dt_ref, dtb_ref, alog_ref, dskip_ref, normw_ref,
               expand_ref, hT, ybuf, valid_len, live=None, after_group=None):
    q = CHUNK
    causal, tri, first_head = masks
    a_neg = -jnp.exp(alog_ref[...])
    dt = _softplus(dt_ref[tok, :] + dtb_ref[...])
    if valid_len is not None:
        t_idx = r0 + lax.broadcasted_iota(jnp.int32, (q, LANES), 0)
        dt = jnp.where(t_idx < valid_len, dt, 0.0)
    if live is not None:
        dt = jnp.where(live, dt, 0.0)
    hi, mid, lo = _split3(dt * a_neg)
    acs3 = _dot(tri, jnp.concatenate([hi, mid, lo], axis=1))
    acs = acs3[:, :LANES] + acs3[:, LANES:2 * LANES] + acs3[:, 2 * LANES:]
    acs2 = acs * LOG2E
    row_t = (acs2 - jnp.log2(dt)).T
    acs_end = acs2[q - 1:q, :]
    cd_hi, cd_lo = _split2(jnp.broadcast_to(jnp.exp2(acs_end), (SUBLANES, LANES)))
    stack = jnp.concatenate(
        [jnp.exp2(acs2).astype(BF16), (jnp.exp2(acs_end - acs2) * dt).astype(BF16),
         cd_hi, cd_lo], axis=0)

    for g in range(SSM_GROUPS):
        gcols = slice(g * GROUP_W, (g + 1) * GROUP_W)
        ex = _dot(stack, expand_ref[:, gcols])
        state_decay = ex[2 * q:2 * q + 1] + ex[2 * q + SUBLANES:2 * q + SUBLANES + 1]
        b_t = xc_ref[tok, D_SSM + g * SSM_STATE:D_SSM + (g + 1) * SSM_STATE].astype(F32).T.astype(BF16)
        c_bf = xc_ref[tok, D_SSM + D_BC + g * SSM_STATE:D_SSM + D_BC + (g + 1) * SSM_STATE]
        cb = _dot(c_bf, b_t)
        h_g = hT[g]
        y_g = ex[0:q] * _dot(c_bf, h_g.astype(BF16))
        xw_g = xc_ref[tok, gcols] * ex[q:2 * q].astype(BF16)
        hT[g] = h_g * state_decay + _dot(b_t, xw_g)
        for jp in range(HEADS_PER_GROUP // 2):
            pair = g * (HEADS_PER_GROUP // 2) + jp
            lanes = slice(pair * LANES, (pair + 1) * LANES)
            x_pair = xc_ref[tok, lanes]
            zero = jnp.zeros_like(x_pair)
            x2 = jnp.concatenate([jnp.where(first_head, x_pair, zero),
                                  jnp.where(first_head, zero, x_pair)], axis=0)
            m2 = []
            for e in range(2):
                hd = 2 * pair + e
                acs_col = jnp.broadcast_to(acs2[:, hd:hd + 1], (q, q))
                decay = jnp.exp2(jnp.where(causal, acs_col - row_t[hd:hd + 1, :], -jnp.inf))
                m2.append((cb * decay).astype(BF16))
            ybuf[:, lanes] = y_g[:, jp * LANES:(jp + 1) * LANES] + _dot(
                jnp.concatenate(m2, axis=1), x2)
        if after_group is not None:
            after_group()

    return _gated_group_norm(ybuf[...], xc_ref[tok, :D_SSM].astype(F32),
                             zs_ref[tok, :].astype(F32), dskip_ref[...], normw_ref[...])


def _write_final_state(hT, hTo_ref, ho_ref):
    hTo_ref[0] = hT[...]
    for g in range(SSM_GROUPS):
        ho_ref[0, g] = hT[g].T


def _mixer_kernel(zs_ref, xc_ref, dt_ref, dtb_ref, alog_ref, dskip_ref, normw_ref, expand_ref,
                  h0_ref, yb_ref, hTo_ref, ho_ref, hT, ybuf, *, tb, valid_len):
    i = pl.program_id(1)

    @pl.when(i == 0)
    def _():
        hT[...] = h0_ref[...]

    masks = _ssd_masks()

    def chunk_body(c, carry):
        r0 = pl.multiple_of(c * CHUNK, CHUNK)
        tok = pl.ds(r0, CHUNK)
        yn = _ssd_chunk(tok, r0, masks, zs_ref, xc_ref, dt_ref, dtb_ref, alog_ref, dskip_ref,
                        normw_ref, expand_ref, hT, ybuf, valid_len if valid_len < tb else None)
        yb_ref[tok, :] = yn.astype(yb_ref.dtype)
        return carry

    lax.fori_loop(0, tb // CHUNK, chunk_body, 0)

    @pl.when(i == pl.num_programs(1) - 1)
    def _():
        _write_final_state(hT, hTo_ref, ho_ref)


PROJ_BLOCK = 256


def _mixer_outproj_kernel(zs_ref, xc_ref, dt_ref, ya_ref, sga_ref, sgb_ref, h_ref,
                          dtb_ref, alog_ref, dskip_ref, normw_ref, expand_ref, h0_ref,
                          wa_ref, wb_ref, wo_ref,
                          o_ref, hTo_ref, ho_ref, hT, ybuf, yb_even, yb_odd, ya_scr, yb_out, mg_scr,
                          *, tb, nt):
    t = pl.program_id(0)
    n_tiles = pl.num_programs(0) - 1
    live = t < n_tiles

    @pl.when(t % nt == 0)
    def _():
        hT[...] = h0_ref[...]

    @pl.when(t == 0)
    def _():
        yb_odd[...] = jnp.zeros_like(yb_odd)

    masks = _ssd_masks()

    def stage(cur, prev):
        nb = D_MODEL // PROJ_BLOCK
        col = lambda k: slice(k * PROJ_BLOCK, (k + 1) * PROJ_BLOCK)

        def piece_a(k):
            ya_scr[:, col(k)] = _dot(ya_ref[...], wa_ref[:, col(k)])

        def piece_b(k):
            yb_out[:, col(k)] = _dot(prev[...], wb_ref[:, col(k)])

        def piece_merge():
            mg_scr[...] = (sga_ref[...].astype(F32) * ya_scr[...]
                           + sgb_ref[...].astype(F32) * yb_out[...]).astype(BF16)

        def piece_o(k):
            o_ref[:, col(k)] = h_ref[:, col(k)] + _dot(mg_scr[...], wo_ref[:, col(k)])

        pieces = ([functools.partial(piece_a, k) for k in range(nb)]
                  + [functools.partial(piece_b, k) for k in range(nb)] + [piece_merge]
                  + [functools.partial(piece_o, k) for k in range(nb)])

        def next_piece():
            if pieces:
                pieces.pop(0)()

        for c in range(tb // CHUNK):
            r0 = c * CHUNK
            tok = slice(r0, r0 + CHUNK)
            yn = _ssd_chunk(tok, r0, masks, zs_ref, xc_ref, dt_ref, dtb_ref, alog_ref, dskip_ref,
                            normw_ref, expand_ref, hT, ybuf, None, live, next_piece)
            cur[tok, :] = yn.astype(BF16)
        while pieces:
            next_piece()

    @pl.when(t % 2 == 0)
    def _():
        stage(yb_even, yb_odd)

    @pl.when(t % 2 == 1)
    def _():
        stage(yb_odd, yb_even)

    @pl.when(jnp.logical_and(live, t % nt == nt - 1))
    def _():
        _write_final_state(hT, hTo_ref, ho_ref)


def _mixer_outproj(zs, xc, dt, ya, sga, sgb, h, p, expand, h0, wa, wb, wo, *, batch, seqlen, tb):
    nt = seqlen // tb
    n_tiles = batch * nt
    cur = lambda w: pl.BlockSpec((tb, w), lambda t: (jnp.minimum(t, n_tiles - 1), 0))
    lag = lambda w: pl.BlockSpec((tb, w), lambda t: (jnp.maximum(t - 1, 0), 0))
    full = lambda shape: pl.BlockSpec(shape, lambda t: (0,) * len(shape))
    once = lambda shape: pl.BlockSpec(shape, lambda t: (0,) * len(shape), pipeline_mode=pl.Buffered(1))
    per_b = lambda shape: pl.BlockSpec(
        (1,) + shape, lambda t: (jnp.minimum(t // nt, batch - 1),) + (0,) * len(shape))
    return pl.pallas_call(
        functools.partial(_mixer_outproj_kernel, tb=tb, nt=nt),
        grid=(n_tiles + 1,),
        in_specs=[cur(D_SSM), cur(D_XBC), cur(LANES), lag(D_MODEL), lag(D_MODEL), lag(D_MODEL),
                  lag(D_MODEL),
                  full((1, LANES)), full((1, LANES)), full((1, D_SSM)), full((1, D_SSM)),
                  once((LANES, D_SSM)), full((SSM_GROUPS, SSM_STATE, GROUP_W)),
                  once((D_MODEL, D_MODEL)), once((D_SSM, D_MODEL)), once((D_MODEL, D_MODEL))],
        out_specs=[lag(D_MODEL), per_b((SSM_GROUPS, SSM_STATE, GROUP_W)),
                   per_b((SSM_GROUPS, GROUP_W, SSM_STATE))],
        out_shape=[jax.ShapeDtypeStruct((batch * seqlen, D_MODEL), F32),
                   jax.ShapeDtypeStruct((batch, SSM_GROUPS, SSM_STATE, GROUP_W), F32),
                   jax.ShapeDtypeStruct((batch, SSM_GROUPS, GROUP_W, SSM_STATE), F32)],
        scratch_shapes=[pltpu.VMEM((SSM_GROUPS, SSM_STATE, GROUP_W), F32),
                        pltpu.VMEM((CHUNK, D_SSM), F32),
                        pltpu.VMEM((tb, D_SSM), BF16),
                        pltpu.VMEM((tb, D_SSM), BF16),
                        pltpu.VMEM((tb, D_MODEL), F32), pltpu.VMEM((tb, D_MODEL), F32),
                        pltpu.VMEM((tb, D_MODEL), BF16)],
        compiler_params=pltpu.CompilerParams(
            dimension_semantics=("arbitrary",), vmem_limit_bytes=VMEM_LIMIT),
        name="mixer_outproj",
    )(zs, xc, dt, ya, sga, sgb, h, p["dt_bias"], p["a_log"], p["d_skip"], p["ssm_norm_w"], expand,
      h0, wa, wb, wo)


def _mixer(zs, xc, dt, p, expand, h0, *, batch, seqlen, tb, valid_len):
    nt = seqlen // tb
    row = lambda w: pl.BlockSpec((tb, w), lambda b, i: (b * nt + i, 0))
    full = lambda shape: pl.BlockSpec(shape, lambda b, i: (0,) * len(shape))
    per_b = lambda shape: pl.BlockSpec((1,) + shape, lambda b, i: (b,) + (0,) * len(shape))
    return pl.pallas_call(
        functools.partial(_mixer_kernel, tb=tb, valid_len=valid_len),
        grid=(batch, nt),
        in_specs=[row(D_SSM), row(D_XBC), row(LANES),
                  full((1, LANES)), full((1, LANES)), full((1, D_SSM)), full((1, D_SSM)),
                  full((LANES, D_SSM)), full((SSM_GROUPS, SSM_STATE, GROUP_W))],
        out_specs=[row(D_SSM), per_b((SSM_GROUPS, SSM_STATE, GROUP_W)),
                   per_b((SSM_GROUPS, GROUP_W, SSM_STATE))],
        out_shape=[jax.ShapeDtypeStruct((batch * seqlen, D_SSM), BF16),
                   jax.ShapeDtypeStruct((batch, SSM_GROUPS, SSM_STATE, GROUP_W), F32),
                   jax.ShapeDtypeStruct((batch, SSM_GROUPS, GROUP_W, SSM_STATE), F32)],
        scratch_shapes=[pltpu.VMEM((SSM_GROUPS, SSM_STATE, GROUP_W), F32),
                        pltpu.VMEM((CHUNK, D_SSM), F32)],
        compiler_params=pltpu.CompilerParams(
            dimension_semantics=("arbitrary", "arbitrary"), vmem_limit_bytes=VMEM_LIMIT),
        name="mixer",
    )(zs, xc, dt, p["dt_bias"], p["a_log"], p["d_skip"], p["ssm_norm_w"], expand, h0)


SEQ_BLOCK = SUBLANES


def _decode_mixer_kernel(ab_ref, ach_ref, zs_ref, xbc_ref, dt_ref, sa_ref, sx_ref, h0_ref,
                         caw_ref, cxw_ref, cxb_ref, dtb_ref, alog_ref, dskip_ref, normw_ref,
                         gsum_ref, expand_ref,
                         ya_ref, yb_ref, sao_ref, sxo_ref, ho_ref,
                         xc_buf, bc_buf, os_buf, xw_buf, cd_buf, y_buf, *, n_tok):
    sb = SEQ_BLOCK

    xp = [sa_ref[k] for k in range(CONV_A_TAPS - 1)] + [ach_ref[t] for t in range(n_tok)]
    for t in range(n_tok):
        conv = caw_ref[0:1, :] * xp[t]
        for k in range(1, CONV_A_TAPS):
            conv += caw_ref[k:k + 1, :] * xp[t + k]
        ya_ref[t] = ab_ref[t] * conv
    for k in range(CONV_A_TAPS - 1):
        sao_ref[k] = xp[n_tok + k]

    xp = [sx_ref[k] for k in range(CONV_X_TAPS - 1)] + [xbc_ref[t] for t in range(n_tok)]
    for t in range(n_tok):
        conv = cxw_ref[0:1, :] * xp[t]
        for k in range(1, CONV_X_TAPS):
            conv += cxw_ref[k:k + 1, :] * xp[t + k]
        xc_buf[t * sb:(t + 1) * sb, :] = _silu(conv + cxb_ref[...])
    for k in range(CONV_X_TAPS - 1):
        sxo_ref[k] = xp[n_tok + k]

    a_neg = -jnp.exp(alog_ref[...])
    dts, acs = [], []
    for t in range(n_tok):
        dts.append(_softplus(dt_ref[t] + dtb_ref[...]))
        acs.append(dts[t] * a_neg + (acs[t - 1] if t else 0.0))

    pairs = [(qi, si) for qi in range(n_tok) for si in range(qi + 1)]
    prods = [xc_buf[qi * sb:(qi + 1) * sb, D_SSM + D_BC:] * xc_buf[si * sb:(si + 1) * sb, D_SSM:D_SSM + D_BC]
             for qi, si in pairs]
    p_hi, p_lo = _split2(jnp.concatenate(prods, axis=0))
    cbh = _dot(p_hi, gsum_ref[...]) + _dot(p_lo, gsum_ref[...])
    coef = [jnp.exp(acs[t]) for t in range(n_tok)]
    coef += [cbh[k * sb:(k + 1) * sb] * jnp.exp(acs[qi] - acs[si]) * dts[si]
             for k, (qi, si) in enumerate(pairs)]
    coef += [jnp.exp(acs[n_tok - 1] - acs[s]) * dts[s] for s in range(n_tok)]
    coef.append(jnp.exp(acs[n_tok - 1]))
    c_hi, c_lo = _split2(jnp.concatenate(coef, axis=0))
    coef_x = _dot(c_hi, expand_ref[...]) + _dot(c_lo, expand_ref[...])
    slab = lambda k: coef_x[k * sb:(k + 1) * sb]

    def store_tiles(buf, r0, val):
        for lt in range(val.shape[1] // LANES):
            buf[lt, r0:r0 + val.shape[0], :] = val[:, lt * LANES:(lt + 1) * LANES]

    def load_rows(buf, lt0, n_lt, rows):
        return jnp.concatenate([buf[lt0 + j, rows, :] for j in range(n_lt)], axis=1)

    store_tiles(bc_buf, 0, xc_buf[:, D_SSM:])
    store_tiles(os_buf, 0, coef_x[0:n_tok * sb])
    diag = {pr: slab(n_tok + k) for k, pr in enumerate(pairs)}
    k0 = n_tok + len(pairs)
    for s in range(n_tok):
        store_tiles(xw_buf, s * sb, slab(k0 + s) * xc_buf[s * sb:(s + 1) * sb, :D_SSM])
    cd_hi, cd_lo = _split2(slab(k0 + n_tok))
    store_tiles(cd_buf, 0, cd_hi.astype(F32))
    store_tiles(cd_buf, sb, cd_lo.astype(F32))

    for t in range(n_tok):
        acc = diag[(t, 0)] * xc_buf[0:sb, :D_SSM]
        for s in range(1, t + 1):
            acc += diag[(t, s)] * xc_buf[s * sb:(s + 1) * sb, :D_SSM]
        store_tiles(y_buf, t * sb, acc)

    gt = GROUP_W // LANES
    k_rows = BF16_ROWS
    krow = lax.broadcasted_iota(jnp.int32, (k_rows, SSM_STATE), 0)
    ones_rows = jnp.where((krow >= n_tok) & (krow < n_tok + 2), 1.0, 0.0).astype(BF16)
    pad_lhs = jnp.zeros((k_rows - n_tok - 2, GROUP_W), F32)
    pad_rhs = jnp.zeros((k_rows - n_tok, SSM_STATE), F32)

    def seq_body(b, carry):
        tok_rows = pl.ds(b, n_tok, stride=sb)
        for g in range(SSM_GROUPS):
            b_bg = bc_buf[g, tok_rows, :]
            c_bg = bc_buf[SSM_GROUPS + g, tok_rows, :]
            heads = pl.ds(g * HEADS_PER_GROUP, HEADS_PER_GROUP)
            h_bg = h0_ref[b, heads].reshape(GROUP_W, SSM_STATE)
            y_off = lax.dot_general(c_bg.astype(BF16), h_bg.astype(BF16),
                                    (((1,), (1,)), ((), ())), preferred_element_type=F32)
            y_off = y_off * load_rows(os_buf, g * gt, gt, tok_rows)
            for j in range(gt):
                y_buf[g * gt + j, tok_rows, :] = (y_buf[g * gt + j, tok_rows, :]
                                                  + y_off[:, j * LANES:(j + 1) * LANES])
            lhs = jnp.concatenate(
                [load_rows(xw_buf, g * gt, gt, tok_rows),
                 load_rows(cd_buf, g * gt, gt, pl.ds(b, 2, stride=sb)), pad_lhs],
                axis=0).astype(BF16)
            rhs = jnp.concatenate(
                [jnp.concatenate([b_bg, pad_rhs], axis=0).astype(BF16), ones_rows], axis=1)
            upd = lax.dot_general(lhs, rhs, (((0,), (0,)), ((), ())), preferred_element_type=F32)
            new = h_bg * upd[:, SSM_STATE:] + upd[:, :SSM_STATE]
            ho_ref[b, heads] = new.reshape(HEADS_PER_GROUP, SSM_HEAD_DIM, SSM_STATE)
        return carry

    lax.fori_loop(0, sb, seq_body, 0, unroll=4)

    xs_all = xc_buf[:, :D_SSM]
    zs_all = jnp.concatenate([zs_ref[t] for t in range(n_tok)], axis=0)
    y_all = load_rows(y_buf, 0, D_SSM // LANES, slice(None))
    yn = _gated_group_norm(y_all, xs_all, zs_all, dskip_ref[...], normw_ref[...])
    for t in range(n_tok):
        yb_ref[t] = yn[t * sb:(t + 1) * sb]


def _decode_mixer(ab, ach, zs, xbc, dt, sa, sx, h0, p, gsum, expand, *, n_tok, n_seq):
    sb = SEQ_BLOCK
    tok = lambda w: pl.BlockSpec((n_tok, sb, w), lambda i: (0, i, 0))
    st = lambda k, w: pl.BlockSpec((k, sb, w), lambda i: (0, i, 0))
    hblk = pl.BlockSpec((sb, SSM_HEADS, SSM_HEAD_DIM, SSM_STATE), lambda i: (i, 0, 0, 0))
    full = lambda shape: pl.BlockSpec(shape, lambda i: (0,) * len(shape))
    return pl.pallas_call(
        functools.partial(_decode_mixer_kernel, n_tok=n_tok),
        grid=(n_seq // sb,),
        in_specs=[tok(D_MODEL), tok(D_MODEL), tok(D_SSM), tok(D_XBC), tok(LANES),
                  st(CONV_A_TAPS - 1, D_MODEL), st(CONV_X_TAPS - 1, D_XBC), hblk,
                  full((CONV_A_TAPS, D_MODEL)), full((CONV_X_TAPS, D_XBC)), full((1, D_XBC)),
                  full((1, LANES)), full((1, LANES)), full((1, D_SSM)), full((1, D_SSM)),
                  full((D_BC, LANES)), full((LANES, D_SSM))],
        out_specs=[tok(D_MODEL), tok(D_SSM), st(CONV_A_TAPS - 1, D_MODEL),
                   st(CONV_X_TAPS - 1, D_XBC), hblk],
        out_shape=[jax.ShapeDtypeStruct((n_tok, n_seq, D_MODEL), F32),
                   jax.ShapeDtypeStruct((n_tok, n_seq, D_SSM), F32),
                   jax.ShapeDtypeStruct((CONV_A_TAPS - 1, n_seq, D_MODEL), F32),
                   jax.ShapeDtypeStruct((CONV_X_TAPS - 1, n_seq, D_XBC), F32),
                   jax.ShapeDtypeStruct((n_seq, SSM_HEADS, SSM_HEAD_DIM, SSM_STATE), F32)],
        scratch_shapes=[pltpu.VMEM((n_tok * sb, D_XBC), F32),
                        pltpu.VMEM((2 * D_BC // LANES, n_tok * sb, LANES), F32),
                        pltpu.VMEM((D_SSM // LANES, n_tok * sb, LANES), F32),
                        pltpu.VMEM((D_SSM // LANES, n_tok * sb, LANES), F32),
                        pltpu.VMEM((D_SSM // LANES, 2 * sb, LANES), F32),
                        pltpu.VMEM((D_SSM // LANES, n_tok * sb, LANES), F32)],
        compiler_params=pltpu.CompilerParams(
            dimension_semantics=("parallel",), vmem_limit_bytes=VMEM_LIMIT),
        name="decode_mixer",
    )(ab, ach, zs, xbc, dt, sa, sx, h0, p["conv_a_w"], p["ssm_conv_w"], p["ssm_conv_b"],
      p["dt_bias"], p["a_log"], p["d_skip"], p["ssm_norm_w"], gsum, expand)


def _row_tile(t, cap):
    if t <= cap:
        return t
    best = LANES
    for cand in range(LANES, cap + 1, LANES):
        if t % cand == 0:
            best = cand
    return best


def kernel(x_prompt, x_sample, state_conv_a, state_ssm_conv, state_ssm, meta_tokens, norm_ffn1, ffn1_w_gu, ffn1_w_down, norm_mix, w_in, conv_a_w, w_a_out, ssm_conv_w, ssm_conv_b, dt_bias, a_log, d_skip, ssm_norm_w, w_b_out, w_o, norm_ffn2, ffn2_w_gu, ffn2_w_down, norm_final):
    bsz, seqlen, _ = x_prompt.shape
    n_seq, n_tok, _ = x_sample.shape
    n_dt = SSM_HEADS
    c_dt = 3 * D_MODEL + D_SSM + D_XBC

    w_gu1, w_dn1 = ffn1_w_gu[0].astype(BF16), ffn1_w_down[0].astype(BF16)
    pad_heads = lambda v: jnp.pad(v[0], (0, LANES - n_dt))[None]
    p = {
        "conv_a_w": conv_a_w[0], "ssm_conv_w": ssm_conv_w[0], "ssm_conv_b": ssm_conv_b[0][None],
        "dt_bias": pad_heads(dt_bias), "a_log": pad_heads(a_log),
        "d_skip": jnp.repeat(d_skip[0], SSM_HEAD_DIM)[None], "ssm_norm_w": ssm_norm_w[0][None],
    }
    n1, nm, n2, nf = norm_ffn1[0][None], norm_mix[0][None], norm_ffn2[0][None], norm_final[None]
    eh = lax.broadcasted_iota(jnp.int32, (LANES, D_SSM), 0)
    ec = lax.broadcasted_iota(jnp.int32, (LANES, D_SSM), 1)
    expand = (ec // SSM_HEAD_DIM == eh).astype(BF16)

    def ffn1(x, cap=512, cast=()):
        return _ffn(x, n1, w_gu1, w_dn1, nf, tm=_row_tile(x.shape[0], cap), final_norm=False, cast=cast)

    n_steps = bsz * seqlen // _row_tile(bsz * seqlen, 512)
    blk = lambda w: (w, BF16_ROWS * pl.cdiv(w.shape[0], n_steps * BF16_ROWS))
    h_p, w_in_t, w_gu2, w_dn2, wa, wb, wo = ffn1(
        x_prompt.reshape(bsz * seqlen, D_MODEL),
        cast=[blk(jnp.swapaxes(w_in[0], 0, 1)), blk(ffn2_w_gu[0]), blk(ffn2_w_down[0]),
              blk(w_a_out[0]), blk(w_b_out[0]), blk(w_o[0])])
    w_proj = (w_in_t,) * 3

    def ffn2(h):
        return _ffn(h, n2, w_gu2, w_dn2, nf, tm=_row_tile(h.shape[0], 512), final_norm=True)

    zeros = lambda *s: jnp.zeros(s, F32)
    x_s = jnp.swapaxes(x_sample, 0, 1).reshape(n_tok * n_seq, D_MODEL)
    h_sm = ffn1(jnp.concatenate([x_s, jnp.pad(meta_tokens, ((0, CHUNK - N_META), (0, 0)))], axis=0),
                cap=n_tok * n_seq + CHUNK)
    h_s, h_m = h_sm[:n_tok * n_seq], h_sm[n_tok * n_seq:]
    _, zs, xc, dt, _, _, ca_m, cx_m = _inproj_conv(
        h_m, nm, w_proj, p, zeros(SUBLANES, D_MODEL), zeros(SUBLANES, D_XBC),
        tm=CHUNK, seq_tiles=1, valid_len=N_META)
    _, hT_m, _ = _mixer(zs, xc, dt, p, expand, zeros(SSM_GROUPS, SSM_STATE, GROUP_W),
                        batch=1, seqlen=CHUNK, tb=CHUNK, valid_len=N_META)

    tm = _row_tile(seqlen, 512)
    ya, zs, xc, dt, sga, sgb, ca_p, cx_p = _inproj_conv(
        h_p, nm, w_proj, p, ca_m[0], cx_m[0], tm=tm, seq_tiles=seqlen // tm, valid_len=tm)
    tb = _row_tile(seqlen, 512)
    h2_p, _, h_out_p = _mixer_outproj(zs, xc, dt, ya, sga, sgb, h_p, p, expand, hT_m[0], wa, wb, wo,
                                      batch=bsz, seqlen=seqlen, tb=tb)
    y_prompt = ffn2(h2_p).reshape(bsz, seqlen, D_MODEL)
    prompt_conv_a = ca_p[None, :, SUBLANES - (CONV_A_TAPS - 1):, :]
    prompt_ssm_conv = cx_p[None, :, SUBLANES - (CONV_X_TAPS - 1):, :]
    prompt_ssm = h_out_p.reshape(1, bsz, SSM_HEADS, SSM_HEAD_DIM, SSM_STATE)

    ab, ach, zs, xbc, dt, sga, sgb = _inproj(h_s, nm, w_proj, tm=_row_tile(n_tok * n_seq, 256), act_dtype=F32)
    tsf = lambda a: a.reshape(n_tok, n_seq, a.shape[-1])
    hh = lax.broadcasted_iota(jnp.int32, (D_BC, LANES), 1)
    kk = lax.broadcasted_iota(jnp.int32, (D_BC, LANES), 0)
    gsum = ((hh < SSM_HEADS) & (kk // SSM_STATE == hh // HEADS_PER_GROUP)).astype(BF16)
    ya, yb, sa_new, sx_new, sample_ssm = _decode_mixer(
        tsf(ab), tsf(ach), tsf(zs), tsf(xbc), tsf(dt),
        jnp.swapaxes(state_conv_a[0], 0, 1), jnp.swapaxes(state_ssm_conv[0], 0, 1), state_ssm[0],
        p, gsum, expand, n_tok=n_tok, n_seq=n_seq)
    flat = lambda a: a.reshape(n_tok * n_seq, a.shape[-1])
    y_s = ffn2(_outproj(flat(ya), flat(yb), sga, sgb, h_s, wa, wb, wo, tm=_row_tile(n_tok * n_seq, 512)))
    y_sample = jnp.swapaxes(y_s.reshape(n_tok, n_seq, D_MODEL), 0, 1)
    sample_conv_a = jnp.swapaxes(sa_new, 0, 1)[None]
    sample_ssm_conv = jnp.swapaxes(sx_new, 0, 1)[None]

    return (y_prompt, y_sample, prompt_conv_a, prompt_ssm_conv, prompt_ssm,
            sample_conv_a, sample_ssm_conv, sample_ssm[None])
```

```python
import functools

import jax
import jax.numpy as jnp
from jax import lax
from jax.experimental import pallas as pl
from jax.experimental.pallas import tpu as pltpu

D_MODEL = 1024
D_FF = 2816
D_SSM = 2048
SSM_HEADS = 32
SSM_HEAD_DIM = 64
SSM_GROUPS = 4
HEADS_PER_GROUP = SSM_HEADS // SSM_GROUPS
SSM_STATE = 128
D_BC = SSM_GROUPS * SSM_STATE
D_XBC = D_SSM + 2 * D_BC
CONV_A_TAPS = 3
CONV_X_TAPS = 4
N_META = 16
EPS = 1e-6

LANES = 128
SUBLANES = 8
BF16_ROWS = 2 * SUBLANES
CHUNK = 128
VMEM_LIMIT = 56 * 1024 * 1024

F32 = jnp.float32
BF16 = jnp.bfloat16


def _dot(a, b):
    return jnp.dot(a, b, preferred_element_type=F32)


def _rms(x, w):
    return x * lax.rsqrt(jnp.mean(x * x, axis=-1, keepdims=True) + EPS) * w


def _silu(x):
    return x * jax.nn.sigmoid(x)


def _softplus(x):
    return jnp.maximum(x, 0.0) + jnp.log1p(jnp.exp(-jnp.abs(x)))


def _split3(x):
    hi = x.astype(BF16)
    r = x - hi.astype(F32)
    mid = r.astype(BF16)
    lo = (r - mid.astype(F32)).astype(BF16)
    return hi, mid, lo


def _split2(x):
    hi = x.astype(BF16)
    lo = (x - hi.astype(F32)).astype(BF16)
    return hi, lo


def _resident(shape):
    return pl.BlockSpec(shape, lambda *_: (0,) * len(shape), pipeline_mode=pl.Buffered(1))


def _ffn_kernel(x_ref, nw_ref, wg_ref, wu_ref, wd_ref, fw_ref, *rest, final_norm, cast_blocks):
    n_cast = len(cast_blocks)
    cast_in, o_ref, cast_out = rest[:n_cast], rest[n_cast], rest[n_cast + 1:]
    x = x_ref[...]
    xn = _rms(x, nw_ref[...]).astype(BF16)
    act = (_silu(_dot(xn, wg_ref[...])) * _dot(xn, wu_ref[...])).astype(BF16)
    h = x + 0.5 * _dot(act, wd_ref[...])
    if final_norm:
        h = _rms(h, fw_ref[...])
    o_ref[...] = h
    for src, dst, n_blocks in zip(cast_in, cast_out, cast_blocks):
        @pl.when(pl.program_id(0) < n_blocks)
        def _(src=src, dst=dst):
            dst[...] = src[...].astype(BF16)


def _ffn(x, nw, w_gu, w_down, fw, *, tm, final_norm, cast=()):
    t = x.shape[0]
    once = pl.Buffered(1)
    n_blocks = [pl.cdiv(w.shape[0], rows) for w, rows in cast]
    assert all(nb <= t // tm for nb in n_blocks)
    cast_specs = [pl.BlockSpec((rows, w.shape[1]), functools.partial(
        lambda i, last: (jnp.minimum(i, last), 0), last=nb - 1))
        for (w, rows), nb in zip(cast, n_blocks)]
    out = pl.pallas_call(
        functools.partial(_ffn_kernel, final_norm=final_norm, cast_blocks=tuple(n_blocks)),
        grid=(t // tm,),
        in_specs=[
            pl.BlockSpec((tm, D_MODEL), lambda i: (i, 0)),
            _resident((1, D_MODEL)),
            pl.BlockSpec((D_MODEL, D_FF), lambda i: (0, 0), pipeline_mode=once),
            pl.BlockSpec((D_MODEL, D_FF), lambda i: (0, 1), pipeline_mode=once),
            _resident((D_FF, D_MODEL)),
            _resident((1, D_MODEL)),
        ] + cast_specs,
        out_specs=[pl.BlockSpec((tm, D_MODEL), lambda i: (i, 0))] + cast_specs,
        out_shape=[jax.ShapeDtypeStruct((t, D_MODEL), F32)]
        + [jax.ShapeDtypeStruct(w.shape, BF16) for w, _ in cast],
        compiler_params=pltpu.CompilerParams(
            dimension_semantics=("arbitrary" if cast else "parallel",), vmem_limit_bytes=VMEM_LIMIT),
        name="ffn",
    )(x, nw, w_gu, w_gu, w_down, fw, *[w for w, _ in cast])
    return out if cast else out[0]


C_DT = 3 * D_MODEL + D_SSM + D_XBC
C_GATE = C_DT + LANES


def _project(u, w_refs, c0, width):
    w_main, w_dt, w_gate = w_refs
    if c0 < C_DT:
        w = w_main[c0:c0 + width, :]
    elif c0 < C_GATE:
        w = w_dt[c0 - C_DT:c0 - C_DT + width, :]
    else:
        w = w_gate[c0 - C_GATE:c0 - C_GATE + width, :]
    return lax.dot_general(u, w, (((1,), (1,)), ((), ())), preferred_element_type=F32)


def _proj_weight_specs():
    once = pl.Buffered(1)
    return [pl.BlockSpec((C_DT, D_MODEL), lambda *_: (0, 0), pipeline_mode=once),
            pl.BlockSpec((LANES, D_MODEL), lambda *_: (C_DT // LANES, 0), pipeline_mode=once),
            pl.BlockSpec((pl.Element(2 * D_MODEL), pl.Element(D_MODEL)),
                         lambda *_: (C_DT + SSM_HEADS, 0), pipeline_mode=once)]


def _inproj_kernel(h_ref, nw_ref, wm_ref, wdt_ref, wg_ref,
                   ab_ref, ach_ref, zs_ref, xbc_ref, dt_ref, sga_ref, sgb_ref):
    u = _rms(h_ref[...], nw_ref[...]).astype(BF16)

    def proj(c0, width):
        return _project(u, (wm_ref, wdt_ref, wg_ref), c0, width)

    ab_ref[...] = proj(0, D_MODEL).astype(ab_ref.dtype)
    ach_ref[...] = (proj(D_MODEL, D_MODEL) * proj(2 * D_MODEL, D_MODEL)).astype(ach_ref.dtype)
    c0 = 3 * D_MODEL
    for k in range(D_SSM // D_MODEL):
        zs_ref[:, k * D_MODEL:(k + 1) * D_MODEL] = _silu(proj(c0 + k * D_MODEL, D_MODEL)).astype(zs_ref.dtype)
    c0 += D_SSM
    for k in range(D_XBC // D_MODEL):
        xbc_ref[:, k * D_MODEL:(k + 1) * D_MODEL] = proj(c0 + k * D_MODEL, D_MODEL).astype(xbc_ref.dtype)
    c0 += D_XBC
    dt_ref[...] = proj(c0, LANES)
    c0 += LANES
    sga_ref[...] = jax.nn.sigmoid(proj(c0, D_MODEL)).astype(sga_ref.dtype)
    sgb_ref[...] = jax.nn.sigmoid(proj(c0 + D_MODEL, D_MODEL)).astype(sgb_ref.dtype)


def _inproj(h, nw, w_packed, *, tm, act_dtype):
    t = h.shape[0]
    widths = (D_MODEL, D_MODEL, D_SSM, D_XBC, LANES, D_MODEL, D_MODEL)
    dtypes = (act_dtype, act_dtype, act_dtype, act_dtype, F32, act_dtype, act_dtype)
    return pl.pallas_call(
        _inproj_kernel,
        grid=(t // tm,),
        in_specs=[pl.BlockSpec((tm, D_MODEL), lambda i: (i, 0)), _resident((1, D_MODEL))]
        + _proj_weight_specs(),
        out_specs=[pl.BlockSpec((tm, w), lambda i: (i, 0)) for w in widths],
        out_shape=[jax.ShapeDtypeStruct((t, w), d) for w, d in zip(widths, dtypes)],
        compiler_params=pltpu.CompilerParams(
            dimension_semantics=("parallel",), vmem_limit_bytes=VMEM_LIMIT),
        name="inproj",
    )(h, nw, *w_packed)


COL_BLOCK = 512
N_PARK = 8


def _inproj_conv_kernel(h_ref, nw_ref, wm_ref, wdt_ref, wg_ref, caw_ref, cxw_ref, cxb_ref,
                        ca0_ref, cx0_ref,
                        ya_ref, zs_ref, xc_ref, dt_ref, sga_ref, sgb_ref, cao_ref, cxo_ref,
                        ca_carry, cx_carry, *stages, tm, seq_tiles, valid_len):
    i = pl.program_id(0)

    @pl.when(i % seq_tiles == 0)
    def _():
        ca_carry[...] = ca0_ref[...]
        cx_carry[...] = cx0_ref[...]

    u = _rms(h_ref[...], nw_ref[...]).astype(BF16)

    def proj(c0, width=COL_BLOCK):
        return _project(u, (wm_ref, wdt_ref, wg_ref), c0, width)

    n8 = tm // SUBLANES
    sublane = lax.broadcasted_iota(jnp.int32, (n8, SUBLANES, COL_BLOCK), 1)

    def causal_conv(x, carry_ref, w_ref, taps, cols):
        cur = x.reshape(n8, SUBLANES, COL_BLOCK)
        prev = jnp.concatenate([carry_ref[:, cols], x[:tm - SUBLANES]], axis=0).reshape(cur.shape)
        acc = w_ref[taps - 1:taps, cols] * x
        for d in range(1, taps):
            merged = jnp.where(sublane < SUBLANES - d, cur, prev)
            acc += w_ref[taps - 1 - d:taps - d, cols] * pltpu.roll(merged, d, axis=1).reshape(x.shape)
        carry_ref[:, cols] = x[valid_len - SUBLANES:valid_len]
        return acc

    c_z, c_x, c_dt = 3 * D_MODEL, 3 * D_MODEL + D_SSM, 3 * D_MODEL + D_SSM + D_XBC
    c_g = c_dt + LANES

    def tail_a(k, cols, prj):
        conv = causal_conv(prj[1] * prj[2], ca_carry, caw_ref, CONV_A_TAPS, cols)
        ya_ref[:, cols] = (prj[0] * conv).astype(ya_ref.dtype)

    def tail_z(k, cols, prj):
        zs_ref[:, cols] = _silu(prj[0]).astype(zs_ref.dtype)

    def tail_x(k, cols, prj):
        conv = causal_conv(prj[0], cx_carry, cxw_ref, CONV_X_TAPS, cols)
        xc_ref[:, cols] = _silu(conv + cxb_ref[:, cols]).astype(xc_ref.dtype)

    def tail_g(k, cols, prj):
        sga_ref[:, cols] = jax.nn.sigmoid(prj[0]).astype(sga_ref.dtype)
        sgb_ref[:, cols] = jax.nn.sigmoid(prj[1]).astype(sgb_ref.dtype)

    def blocks(tail, width, starts):
        return [(tail, k, [c + k * COL_BLOCK for c in starts]) for k in range(width // COL_BLOCK)]

    light = blocks(tail_z, D_SSM, [c_z])
    for n, (blk_a, blk_g) in enumerate(zip(blocks(tail_a, D_MODEL, [0, D_MODEL, 2 * D_MODEL]),
                                           blocks(tail_g, D_MODEL, [c_g, c_g + D_MODEL]))):
        light.insert(3 * n, blk_a)
        light.insert(3 * n + 2, blk_g)
    heavy = blocks(tail_x, D_XBC, [c_x])
    order = heavy + light
    park = jnp.minimum(i, 0)
    slot = 0
    for tail, k, starts in order:
        parked = []
        for c in starts:
            stages[slot % len(stages)][park] = proj(c)
            parked.append(stages[slot % len(stages)])
            slot += 1
        tail(k, slice(k * COL_BLOCK, (k + 1) * COL_BLOCK), [s[park] for s in parked])
    dt_ref[...] = proj(c_dt, LANES)
    cao_ref[0] = ca_carry[...]
    cxo_ref[0] = cx_carry[...]


def _inproj_conv(h, nw, w_packed, p, ca0, cx0, *, tm, seq_tiles, valid_len):
    t = h.shape[0]
    n_seq = t // (tm * seq_tiles)
    widths = (D_MODEL, D_SSM, D_XBC, LANES, D_MODEL, D_MODEL)
    dtypes = (BF16, BF16, BF16, F32, BF16, BF16)
    tail = lambda w: pl.BlockSpec((1, SUBLANES, w), lambda i: (i // seq_tiles, 0, 0))
    return pl.pallas_call(
        functools.partial(_inproj_conv_kernel, tm=tm, seq_tiles=seq_tiles, valid_len=valid_len),
        grid=(t // tm,),
        in_specs=[pl.BlockSpec((tm, D_MODEL), lambda i: (i, 0)), _resident((1, D_MODEL))]
        + _proj_weight_specs()
        + [_resident((CONV_A_TAPS, D_MODEL)), _resident((CONV_X_TAPS, D_XBC)),
           _resident((1, D_XBC)),
           _resident((SUBLANES, D_MODEL)), _resident((SUBLANES, D_XBC))],
        out_specs=[pl.BlockSpec((tm, w), lambda i: (i, 0)) for w in widths]
        + [tail(D_MODEL), tail(D_XBC)],
        out_shape=[jax.ShapeDtypeStruct((t, w), d) for w, d in zip(widths, dtypes)]
        + [jax.ShapeDtypeStruct((n_seq, SUBLANES, D_MODEL), F32),
           jax.ShapeDtypeStruct((n_seq, SUBLANES, D_XBC), F32)],
        scratch_shapes=[pltpu.VMEM((SUBLANES, D_MODEL), F32), pltpu.VMEM((SUBLANES, D_XBC), F32),
                        ] + [pltpu.VMEM((1, tm, COL_BLOCK), F32)] * N_PARK,
        compiler_params=pltpu.CompilerParams(
            dimension_semantics=("arbitrary",), vmem_limit_bytes=VMEM_LIMIT),
        name="inproj_conv",
    )(h, nw, *w_packed, p["conv_a_w"], p["ssm_conv_w"], p["ssm_conv_b"], ca0, cx0)


def _outproj_kernel(ya_ref, yb_ref, sga_ref, sgb_ref, h_ref, wa_ref, wb_ref, wo_ref, o_ref):
    y_a = _dot(ya_ref[...].astype(BF16), wa_ref[...])
    y_b = _dot(yb_ref[...].astype(BF16), wb_ref[...])
    merged = sga_ref[...].astype(F32) * y_a + sgb_ref[...].astype(F32) * y_b
    o_ref[...] = h_ref[...] + _dot(merged.astype(BF16), wo_ref[...])


def _outproj(ya, yb, sga, sgb, h, wa, wb, wo, *, tm):
    t = h.shape[0]
    row = lambda w: pl.BlockSpec((tm, w), lambda i: (i, 0))
    return pl.pallas_call(
        _outproj_kernel,
        grid=(t // tm,),
        in_specs=[row(D_MODEL), row(D_SSM), row(D_MODEL), row(D_MODEL), row(D_MODEL),
                  _resident((D_MODEL, D_MODEL)), _resident((D_SSM, D_MODEL)),
                  _resident((D_MODEL, D_MODEL))],
        out_specs=row(D_MODEL),
        out_shape=jax.ShapeDtypeStruct((t, D_MODEL), F32),
        compiler_params=pltpu.CompilerParams(
            dimension_semantics=("parallel",), vmem_limit_bytes=VMEM_LIMIT),
        name="outproj",
    )(ya, yb, sga, sgb, h, wa, wb, wo)


def _gated_group_norm(y, xs, zs, dskip, normw):
    y = (y + dskip * xs) * zs
    gw = D_SSM // SSM_GROUPS
    parts = []
    for g in range(SSM_GROUPS):
        yg = y[:, g * gw:(g + 1) * gw]
        parts.append(yg * lax.rsqrt(jnp.mean(yg * yg, axis=-1, keepdims=True) + EPS))
    return jnp.concatenate(parts, axis=-1) * normw


GROUP_W = HEADS_PER_GROUP * SSM_HEAD_DIM
LOG2E = 1.4426950408889634


def _ssd_masks():
    q = CHUNK
    rows = lax.broadcasted_iota(jnp.int32, (q, q), 0)
    cols = lax.broadcasted_iota(jnp.int32, (q, q), 1)
    causal = rows >= cols
    first_head = lax.broadcasted_iota(jnp.int32, (q, LANES), 1) < SSM_HEAD_DIM
    return causal, causal.astype(BF16), first_head


def _ssd_chunk(tok, r0, masks, zs_ref, xc_ref, dt_ref, dtb_ref, alog_ref, dskip_ref, normw_ref,
               expand_ref, hT, ybuf, valid_len, live=None, after_pair=None):
    q = CHUNK
    causal, tri, first_head = masks
    a_neg = -jnp.exp(alog_ref[...])
    dt = _softplus(dt_ref[tok, :] + dtb_ref[...])
    if valid_len is not None:
        t_idx = r0 + lax.broadcasted_iota(jnp.int32, (q, LANES), 0)
        dt = jnp.where(t_idx < valid_len, dt, 0.0)
    if live is not None:
        dt = jnp.where(live, dt, 0.0)
    hi, mid, lo = _split3(dt * a_neg)
    acs3 = _dot(tri, jnp.concatenate([hi, mid, lo], axis=1))
    acs = acs3[:, :LANES] + acs3[:, LANES:2 * LANES] + acs3[:, 2 * LANES:]
    acs2 = acs * LOG2E
    row_t = (acs2 - jnp.log2(dt)).T
    acs_end = acs2[q - 1:q, :]
    cd_hi, cd_lo = _split2(jnp.broadcast_to(jnp.exp2(acs_end), (SUBLANES, LANES)))
    stack = jnp.concatenate(
        [jnp.exp2(acs2).astype(BF16), (jnp.exp2(acs_end - acs2) * dt).astype(BF16),
         cd_hi, cd_lo], axis=0)

    for g in range(SSM_GROUPS):
        gcols = slice(g * GROUP_W, (g + 1) * GROUP_W)
        ex = _dot(stack, expand_ref[:, gcols])
        state_decay = ex[2 * q:2 * q + 1] + ex[2 * q + SUBLANES:2 * q + SUBLANES + 1]
        b_t = xc_ref[tok, D_SSM + g * SSM_STATE:D_SSM + (g + 1) * SSM_STATE].astype(F32).T.astype(BF16)
        c_bf = xc_ref[tok, D_SSM + D_BC + g * SSM_STATE:D_SSM + D_BC + (g + 1) * SSM_STATE]
        cb = _dot(c_bf, b_t)
        h_g = hT[g]
        y_g = ex[0:q] * _dot(c_bf, h_g.astype(BF16))
        xw_g = xc_ref[tok, gcols] * ex[q:2 * q].astype(BF16)
        hT[g] = h_g * state_decay + _dot(b_t, xw_g)
        for jp in range(HEADS_PER_GROUP // 2):
            pair = g * (HEADS_PER_GROUP // 2) + jp
            lanes = slice(pair * LANES, (pair + 1) * LANES)
            x_pair = xc_ref[tok, lanes]
            zero = jnp.zeros_like(x_pair)
            x2 = jnp.concatenate([jnp.where(first_head, x_pair, zero),
                                  jnp.where(first_head, zero, x_pair)], axis=0)
            m2 = []
            for e in range(2):
                hd = 2 * pair + e
                acs_col = jnp.broadcast_to(acs2[:, hd:hd + 1], (q, q))
                decay = jnp.exp2(jnp.where(causal, acs_col - row_t[hd:hd + 1, :], -jnp.inf))
                m2.append((cb * decay).astype(BF16))
            ybuf[:, lanes] = y_g[:, jp * LANES:(jp + 1) * LANES] + _dot(
                jnp.concatenate(m2, axis=1), x2)
            if after_pair is not None:
                after_pair()

    return _gated_group_norm(ybuf[...], xc_ref[tok, :D_SSM].astype(F32),
                             zs_ref[tok, :].astype(F32), dskip_ref[...], normw_ref[...])


def _write_final_state(hT, hTo_ref, ho_ref):
    hTo_ref[0] = hT[...]
    for g in range(SSM_GROUPS):
        ho_ref[0, g] = hT[g].T


def _mixer_kernel(zs_ref, xc_ref, dt_ref, dtb_ref, alog_ref, dskip_ref, normw_ref, expand_ref,
                  h0_ref, yb_ref, hTo_ref, ho_ref, hT, ybuf, *, tb, valid_len):
    i = pl.program_id(1)

    @pl.when(i == 0)
    def _():
        hT[...] = h0_ref[...]

    masks = _ssd_masks()

    def chunk_body(c, carry):
        r0 = pl.multiple_of(c * CHUNK, CHUNK)
        tok = pl.ds(r0, CHUNK)
        yn = _ssd_chunk(tok, r0, masks, zs_ref, xc_ref, dt_ref, dtb_ref, alog_ref, dskip_ref,
                        normw_ref, expand_ref, hT, ybuf, valid_len if valid_len < tb else None)
        yb_ref[tok, :] = yn.astype(yb_ref.dtype)
        return carry

    lax.fori_loop(0, tb // CHUNK, chunk_body, 0)

    @pl.when(i == pl.num_programs(1) - 1)
    def _():
        _write_final_state(hT, hTo_ref, ho_ref)


PROJ_BLOCK = 256


def _mixer_outproj_kernel(zs_ref, xc_ref, dt_ref, ya_ref, sga_ref, sgb_ref, h_ref,
                          dtb_ref, alog_ref, dskip_ref, normw_ref, expand_ref, h0_ref,
                          wa_ref, wb_ref, wo_ref,
                          o_ref, hTo_ref, ho_ref, hT, ybuf, yb_even, yb_odd, ya_scr, yb_out, mg_scr,
                          *, tb, nt):
    t = pl.program_id(0)
    n_tiles = pl.num_programs(0) - 1
    live = t < n_tiles

    @pl.when(t % nt == 0)
    def _():
        hT[...] = h0_ref[...]

    @pl.when(t == 0)
    def _():
        yb_odd[...] = jnp.zeros_like(yb_odd)

    masks = _ssd_masks()

    def stage(cur, prev):
        nb = D_MODEL // PROJ_BLOCK
        col = lambda k: slice(k * PROJ_BLOCK, (k + 1) * PROJ_BLOCK)

        def piece_a(k):
            ya_scr[:, col(k)] = _dot(ya_ref[...], wa_ref[:, col(k)])

        def piece_b(k):
            yb_out[:, col(k)] = _dot(prev[...], wb_ref[:, col(k)])

        def piece_merge():
            mg_scr[...] = (sga_ref[...].astype(F32) * ya_scr[...]
                           + sgb_ref[...].astype(F32) * yb_out[...]).astype(BF16)

        def piece_o(k):
            o_ref[:, col(k)] = h_ref[:, col(k)] + _dot(mg_scr[...], wo_ref[:, col(k)])

        pieces = ([functools.partial(piece_a, k) for k in range(nb)]
                  + [functools.partial(piece_b, k) for k in range(nb)] + [piece_merge]
                  + [functools.partial(piece_o, k) for k in range(nb)])

        n_pieces, n_slots = len(pieces), tb // CHUNK * SSM_HEADS // 2
        slots_seen = [0]

        def next_piece():
            slots_seen[0] += 1
            while pieces and (n_pieces - len(pieces)) * n_slots < slots_seen[0] * n_pieces:
                pieces.pop(0)()

        for c in range(tb // CHUNK):
            r0 = c * CHUNK
            tok = slice(r0, r0 + CHUNK)
            yn = _ssd_chunk(tok, r0, masks, zs_ref, xc_ref, dt_ref, dtb_ref, alog_ref, dskip_ref,
                            normw_ref, expand_ref, hT, ybuf, None, live, next_piece)
            cur[tok, :] = yn.astype(BF16)
        while pieces:
            next_piece()

    @pl.when(t % 2 == 0)
    def _():
        stage(yb_even, yb_odd)

    @pl.when(t % 2 == 1)
    def _():
        stage(yb_odd, yb_even)

    @pl.when(jnp.logical_and(live, t % nt == nt - 1))
    def _():
        _write_final_state(hT, hTo_ref, ho_ref)


def _mixer_outproj(zs, xc, dt, ya, sga, sgb, h, p, expand, h0, wa, wb, wo, *, batch, seqlen, tb):
    nt = seqlen // tb
    n_tiles = batch * nt
    cur = lambda w: pl.BlockSpec((tb, w), lambda t: (jnp.minimum(t, n_tiles - 1), 0))
    lag = lambda w: pl.BlockSpec((tb, w), lambda t: (jnp.maximum(t - 1, 0), 0))
    full = lambda shape: pl.BlockSpec(shape, lambda t: (0,) * len(shape))
    once = lambda shape: pl.BlockSpec(shape, lambda t: (0,) * len(shape), pipeline_mode=pl.Buffered(1))
    per_b = lambda shape: pl.BlockSpec(
        (1,) + shape, lambda t: (jnp.minimum(t // nt, batch - 1),) + (0,) * len(shape))
    return pl.pallas_call(
        functools.partial(_mixer_outproj_kernel, tb=tb, nt=nt),
        grid=(n_tiles + 1,),
        in_specs=[cur(D_SSM), cur(D_XBC), cur(LANES), lag(D_MODEL), lag(D_MODEL), lag(D_MODEL),
                  lag(D_MODEL),
                  full((1, LANES)), full((1, LANES)), full((1, D_SSM)), full((1, D_SSM)),
                  once((LANES, D_SSM)), full((SSM_GROUPS, SSM_STATE, GROUP_W)),
                  once((D_MODEL, D_MODEL)), once((D_SSM, D_MODEL)), once((D_MODEL, D_MODEL))],
        out_specs=[lag(D_MODEL), per_b((SSM_GROUPS, SSM_STATE, GROUP_W)),
                   per_b((SSM_GROUPS, GROUP_W, SSM_STATE))],
        out_shape=[jax.ShapeDtypeStruct((batch * seqlen, D_MODEL), F32),
                   jax.ShapeDtypeStruct((batch, SSM_GROUPS, SSM_STATE, GROUP_W), F32),
                   jax.ShapeDtypeStruct((batch, SSM_GROUPS, GROUP_W, SSM_STATE), F32)],
        scratch_shapes=[pltpu.VMEM((SSM_GROUPS, SSM_STATE, GROUP_W), F32),
                        pltpu.VMEM((CHUNK, D_SSM), F32),
                        pltpu.VMEM((tb, D_SSM), BF16),
                        pltpu.VMEM((tb, D_SSM), BF16),
                        pltpu.VMEM((tb, D_MODEL), F32), pltpu.VMEM((tb, D_MODEL), F32),
                        pltpu.VMEM((tb, D_MODEL), BF16)],
        compiler_params=pltpu.CompilerParams(
            dimension_semantics=("arbitrary",), vmem_limit_bytes=VMEM_LIMIT),
        name="mixer_outproj",
    )(zs, xc, dt, ya, sga, sgb, h, p["dt_bias"], p["a_log"], p["d_skip"], p["ssm_norm_w"], expand,
      h0, wa, wb, wo)


def _mixer(zs, xc, dt, p, expand, h0, *, batch, seqlen, tb, valid_len):
    nt = seqlen // tb
    row = lambda w: pl.BlockSpec((tb, w), lambda b, i: (b * nt + i, 0))
    full = lambda shape: pl.BlockSpec(shape, lambda b, i: (0,) * len(shape))
    per_b = lambda shape: pl.BlockSpec((1,) + shape, lambda b, i: (b,) + (0,) * len(shape))
    return pl.pallas_call(
        functools.partial(_mixer_kernel, tb=tb, valid_len=valid_len),
        grid=(batch, nt),
        in_specs=[row(D_SSM), row(D_XBC), row(LANES),
                  full((1, LANES)), full((1, LANES)), full((1, D_SSM)), full((1, D_SSM)),
                  full((LANES, D_SSM)), full((SSM_GROUPS, SSM_STATE, GROUP_W))],
        out_specs=[row(D_SSM), per_b((SSM_GROUPS, SSM_STATE, GROUP_W)),
                   per_b((SSM_GROUPS, GROUP_W, SSM_STATE))],
        out_shape=[jax.ShapeDtypeStruct((batch * seqlen, D_SSM), BF16),
                   jax.ShapeDtypeStruct((batch, SSM_GROUPS, SSM_STATE, GROUP_W), F32),
                   jax.ShapeDtypeStruct((batch, SSM_GROUPS, GROUP_W, SSM_STATE), F32)],
        scratch_shapes=[pltpu.VMEM((SSM_GROUPS, SSM_STATE, GROUP_W), F32),
                        pltpu.VMEM((CHUNK, D_SSM), F32)],
        compiler_params=pltpu.CompilerParams(
            dimension_semantics=("arbitrary", "arbitrary"), vmem_limit_bytes=VMEM_LIMIT),
        name="mixer",
    )(zs, xc, dt, p["dt_bias"], p["a_log"], p["d_skip"], p["ssm_norm_w"], expand, h0)


SEQ_BLOCK = SUBLANES


def _decode_mixer_kernel(ab_ref, ach_ref, zs_ref, xbc_ref, dt_ref, sa_ref, sx_ref, h0_ref,
                         caw_ref, cxw_ref, cxb_ref, dtb_ref, alog_ref, dskip_ref, normw_ref,
                         gsum_ref, expand_ref,
                         ya_ref, yb_ref, sao_ref, sxo_ref, ho_ref,
                         xc_buf, bc_buf, os_buf, xw_buf, cd_buf, y_buf, *, n_tok):
    sb = SEQ_BLOCK

    xp = [sa_ref[k] for k in range(CONV_A_TAPS - 1)] + [ach_ref[t] for t in range(n_tok)]
    for t in range(n_tok):
        conv = caw_ref[0:1, :] * xp[t]
        for k in range(1, CONV_A_TAPS):
            conv += caw_ref[k:k + 1, :] * xp[t + k]
        ya_ref[t] = ab_ref[t] * conv
    for k in range(CONV_A_TAPS - 1):
        sao_ref[k] = xp[n_tok + k]

    xp = [sx_ref[k] for k in range(CONV_X_TAPS - 1)] + [xbc_ref[t] for t in range(n_tok)]
    for t in range(n_tok):
        conv = cxw_ref[0:1, :] * xp[t]
        for k in range(1, CONV_X_TAPS):
            conv += cxw_ref[k:k + 1, :] * xp[t + k]
        xc_buf[t * sb:(t + 1) * sb, :] = _silu(conv + cxb_ref[...])
    for k in range(CONV_X_TAPS - 1):
        sxo_ref[k] = xp[n_tok + k]

    a_neg = -jnp.exp(alog_ref[...])
    dts, acs = [], []
    for t in range(n_tok):
        dts.append(_softplus(dt_ref[t] + dtb_ref[...]))
        acs.append(dts[t] * a_neg + (acs[t - 1] if t else 0.0))

    pairs = [(qi, si) for qi in range(n_tok) for si in range(qi + 1)]
    prods = [xc_buf[qi * sb:(qi + 1) * sb, D_SSM + D_BC:] * xc_buf[si * sb:(si + 1) * sb, D_SSM:D_SSM + D_BC]
             for qi, si in pairs]
    p_hi, p_lo = _split2(jnp.concatenate(prods, axis=0))
    cbh = _dot(p_hi, gsum_ref[...]) + _dot(p_lo, gsum_ref[...])
    coef = [jnp.exp(acs[t]) for t in range(n_tok)]
    coef += [cbh[k * sb:(k + 1) * sb] * jnp.exp(acs[qi] - acs[si]) * dts[si]
             for k, (qi, si) in enumerate(pairs)]
    coef += [jnp.exp(acs[n_tok - 1] - acs[s]) * dts[s] for s in range(n_tok)]
    coef.append(jnp.exp(acs[n_tok - 1]))
    c_hi, c_lo = _split2(jnp.concatenate(coef, axis=0))
    coef_x = _dot(c_hi, expand_ref[...]) + _dot(c_lo, expand_ref[...])
    slab = lambda k: coef_x[k * sb:(k + 1) * sb]

    def store_tiles(buf, r0, val):
        for lt in range(val.shape[1] // LANES):
            buf[lt, r0:r0 + val.shape[0], :] = val[:, lt * LANES:(lt + 1) * LANES]

    def load_rows(buf, lt0, n_lt, rows):
        return jnp.concatenate([buf[lt0 + j, rows, :] for j in range(n_lt)], axis=1)

    store_tiles(bc_buf, 0, xc_buf[:, D_SSM:])
    store_tiles(os_buf, 0, coef_x[0:n_tok * sb])
    diag = {pr: slab(n_tok + k) for k, pr in enumerate(pairs)}
    k0 = n_tok + len(pairs)
    for s in range(n_tok):
        store_tiles(xw_buf, s * sb, slab(k0 + s) * xc_buf[s * sb:(s + 1) * sb, :D_SSM])
    cd_hi, cd_lo = _split2(slab(k0 + n_tok))
    store_tiles(cd_buf, 0, cd_hi.astype(F32))
    store_tiles(cd_buf, sb, cd_lo.astype(F32))

    for t in range(n_tok):
        acc = diag[(t, 0)] * xc_buf[0:sb, :D_SSM]
        for s in range(1, t + 1):
            acc += diag[(t, s)] * xc_buf[s * sb:(s + 1) * sb, :D_SSM]
        store_tiles(y_buf, t * sb, acc)

    gt = GROUP_W // LANES
    k_rows = BF16_ROWS
    krow = lax.broadcasted_iota(jnp.int32, (k_rows, SSM_STATE), 0)
    ones_rows = jnp.where((krow >= n_tok) & (krow < n_tok + 2), 1.0, 0.0).astype(BF16)
    pad_lhs = jnp.zeros((k_rows - n_tok - 2, GROUP_W), F32)
    pad_rhs = jnp.zeros((k_rows - n_tok, SSM_STATE), F32)

    def seq_body(b, carry):
        tok_rows = pl.ds(b, n_tok, stride=sb)
        for g in range(SSM_GROUPS):
            b_bg = bc_buf[g, tok_rows, :]
            c_bg = bc_buf[SSM_GROUPS + g, tok_rows, :]
            heads = pl.ds(g * HEADS_PER_GROUP, HEADS_PER_GROUP)
            h_bg = h0_ref[b, heads].reshape(GROUP_W, SSM_STATE)
            y_off = lax.dot_general(c_bg.astype(BF16), h_bg.astype(BF16),
                                    (((1,), (1,)), ((), ())), preferred_element_type=F32)
            y_off = y_off * load_rows(os_buf, g * gt, gt, tok_rows)
            for j in range(gt):
                y_buf[g * gt + j, tok_rows, :] = (y_buf[g * gt + j, tok_rows, :]
                                                  + y_off[:, j * LANES:(j + 1) * LANES])
            lhs = jnp.concatenate(
                [load_rows(xw_buf, g * gt, gt, tok_rows),
                 load_rows(cd_buf, g * gt, gt, pl.ds(b, 2, stride=sb)), pad_lhs],
                axis=0).astype(BF16)
            rhs = jnp.concatenate(
                [jnp.concatenate([b_bg, pad_rhs], axis=0).astype(BF16), ones_rows], axis=1)
            upd = lax.dot_general(lhs, rhs, (((0,), (0,)), ((), ())), preferred_element_type=F32)
            new = h_bg * upd[:, SSM_STATE:] + upd[:, :SSM_STATE]
            ho_ref[b, heads] = new.reshape(HEADS_PER_GROUP, SSM_HEAD_DIM, SSM_STATE)
        return carry

    lax.fori_loop(0, sb, seq_body, 0, unroll=4)

    xs_all = xc_buf[:, :D_SSM]
    zs_all = jnp.concatenate([zs_ref[t] for t in range(n_tok)], axis=0)
    y_all = load_rows(y_buf, 0, D_SSM // LANES, slice(None))
    yn = _gated_group_norm(y_all, xs_all, zs_all, dskip_ref[...], normw_ref[...])
    for t in range(n_tok):
        yb_ref[t] = yn[t * sb:(t + 1) * sb]


def _decode_mixer(ab, ach, zs, xbc, dt, sa, sx, h0, p, gsum, expand, *, n_tok, n_seq):
    sb = SEQ_BLOCK
    tok = lambda w: pl.BlockSpec((n_tok, sb, w), lambda i: (0, i, 0))
    st = lambda k, w: pl.BlockSpec((k, sb, w), lambda i: (0, i, 0))
    hblk = pl.BlockSpec((sb, SSM_HEADS, SSM_HEAD_DIM, SSM_STATE), lambda i: (i, 0, 0, 0))
    full = lambda shape: pl.BlockSpec(shape, lambda i: (0,) * len(shape))
    return pl.pallas_call(
        functools.partial(_decode_mixer_kernel, n_tok=n_tok),
        grid=(n_seq // sb,),
        in_specs=[tok(D_MODEL), tok(D_MODEL), tok(D_SSM), tok(D_XBC), tok(LANES),
                  st(CONV_A_TAPS - 1, D_MODEL), st(CONV_X_TAPS - 1, D_XBC), hblk,
                  full((CONV_A_TAPS, D_MODEL)), full((CONV_X_TAPS, D_XBC)), full((1, D_XBC)),
                  full((1, LANES)), full((1, LANES)), full((1, D_SSM)), full((1, D_SSM)),
                  full((D_BC, LANES)), full((LANES, D_SSM))],
        out_specs=[tok(D_MODEL), tok(D_SSM), st(CONV_A_TAPS - 1, D_MODEL),
                   st(CONV_X_TAPS - 1, D_XBC), hblk],
        out_shape=[jax.ShapeDtypeStruct((n_tok, n_seq, D_MODEL), F32),
                   jax.ShapeDtypeStruct((n_tok, n_seq, D_SSM), F32),
                   jax.ShapeDtypeStruct((CONV_A_TAPS - 1, n_seq, D_MODEL), F32),
                   jax.ShapeDtypeStruct((CONV_X_TAPS - 1, n_seq, D_XBC), F32),
                   jax.ShapeDtypeStruct((n_seq, SSM_HEADS, SSM_HEAD_DIM, SSM_STATE), F32)],
        scratch_shapes=[pltpu.VMEM((n_tok * sb, D_XBC), F32),
                        pltpu.VMEM((2 * D_BC // LANES, n_tok * sb, LANES), F32),
                        pltpu.VMEM((D_SSM // LANES, n_tok * sb, LANES), F32),
                        pltpu.VMEM((D_SSM // LANES, n_tok * sb, LANES), F32),
                        pltpu.VMEM((D_SSM // LANES, 2 * sb, LANES), F32),
                        pltpu.VMEM((D_SSM // LANES, n_tok * sb, LANES), F32)],
        compiler_params=pltpu.CompilerParams(
            dimension_semantics=("parallel",), vmem_limit_bytes=VMEM_LIMIT),
        name="decode_mixer",
    )(ab, ach, zs, xbc, dt, sa, sx, h0, p["conv_a_w"], p["ssm_conv_w"], p["ssm_conv_b"],
      p["dt_bias"], p["a_log"], p["d_skip"], p["ssm_norm_w"], gsum, expand)


def _row_tile(t, cap):
    if t <= cap:
        return t
    best = LANES
    for cand in range(LANES, cap + 1, LANES):
        if t % cand == 0:
            best = cand
    return best


def kernel(x_prompt, x_sample, state_conv_a, state_ssm_conv, state_ssm, meta_tokens, norm_ffn1, ffn1_w_gu, ffn1_w_down, norm_mix, w_in, conv_a_w, w_a_out, ssm_conv_w, ssm_conv_b, dt_bias, a_log, d_skip, ssm_norm_w, w_b_out, w_o, norm_ffn2, ffn2_w_gu, ffn2_w_down, norm_final):
    bsz, seqlen, _ = x_prompt.shape
    n_seq, n_tok, _ = x_sample.shape
    n_dt = SSM_HEADS
    c_dt = 3 * D_MODEL + D_SSM + D_XBC

    w_gu1, w_dn1 = ffn1_w_gu[0].astype(BF16), ffn1_w_down[0].astype(BF16)
    pad_heads = lambda v: jnp.pad(v[0], (0, LANES - n_dt))[None]
    p = {
        "conv_a_w": conv_a_w[0], "ssm_conv_w": ssm_conv_w[0], "ssm_conv_b": ssm_conv_b[0][None],
        "dt_bias": pad_heads(dt_bias), "a_log": pad_heads(a_log),
        "d_skip": jnp.repeat(d_skip[0], SSM_HEAD_DIM)[None], "ssm_norm_w": ssm_norm_w[0][None],
    }
    n1, nm, n2, nf = norm_ffn1[0][None], norm_mix[0][None], norm_ffn2[0][None], norm_final[None]
    eh = lax.broadcasted_iota(jnp.int32, (LANES, D_SSM), 0)
    ec = lax.broadcasted_iota(jnp.int32, (LANES, D_SSM), 1)
    expand = (ec // SSM_HEAD_DIM == eh).astype(BF16)

    def ffn1(x, cap=512, cast=()):
        return _ffn(x, n1, w_gu1, w_dn1, nf, tm=_row_tile(x.shape[0], cap), final_norm=False, cast=cast)

    n_steps = bsz * seqlen // _row_tile(bsz * seqlen, 512)
    blk = lambda w: (w, BF16_ROWS * pl.cdiv(w.shape[0], n_steps * BF16_ROWS))
    h_p, w_in_t, w_gu2, w_dn2, wa, wb, wo = ffn1(
        x_prompt.reshape(bsz * seqlen, D_MODEL),
        cast=[blk(jnp.swapaxes(w_in[0], 0, 1)), blk(ffn2_w_gu[0]), blk(ffn2_w_down[0]),
              blk(w_a_out[0]), blk(w_b_out[0]), blk(w_o[0])])
    w_proj = (w_in_t,) * 3

    def ffn2(h):
        return _ffn(h, n2, w_gu2, w_dn2, nf, tm=_row_tile(h.shape[0], 512), final_norm=True)

    zeros = lambda *s: jnp.zeros(s, F32)
    x_s = jnp.swapaxes(x_sample, 0, 1).reshape(n_tok * n_seq, D_MODEL)
    h_sm = ffn1(jnp.concatenate([x_s, jnp.pad(meta_tokens, ((0, CHUNK - N_META), (0, 0)))], axis=0),
                cap=n_tok * n_seq + CHUNK)
    h_s, h_m = h_sm[:n_tok * n_seq], h_sm[n_tok * n_seq:]
    _, zs, xc, dt, _, _, ca_m, cx_m = _inproj_conv(
        h_m, nm, w_proj, p, zeros(SUBLANES, D_MODEL), zeros(SUBLANES, D_XBC),
        tm=CHUNK, seq_tiles=1, valid_len=N_META)
    _, hT_m, _ = _mixer(zs, xc, dt, p, expand, zeros(SSM_GROUPS, SSM_STATE, GROUP_W),
                        batch=1, seqlen=CHUNK, tb=CHUNK, valid_len=N_META)

    tm = _row_tile(seqlen, 512)
    ya, zs, xc, dt, sga, sgb, ca_p, cx_p = _inproj_conv(
        h_p, nm, w_proj, p, ca_m[0], cx_m[0], tm=tm, seq_tiles=seqlen // tm, valid_len=tm)
    tb = _row_tile(seqlen, 512)
    h2_p, _, h_out_p = _mixer_outproj(zs, xc, dt, ya, sga, sgb, h_p, p, expand, hT_m[0], wa, wb, wo,
                                      batch=bsz, seqlen=seqlen, tb=tb)
    y_prompt = ffn2(h2_p).reshape(bsz, seqlen, D_MODEL)
    prompt_conv_a = ca_p[None, :, SUBLANES - (CONV_A_TAPS - 1):, :]
    prompt_ssm_conv = cx_p[None, :, SUBLANES - (CONV_X_TAPS - 1):, :]
    prompt_ssm = h_out_p.reshape(1, bsz, SSM_HEADS, SSM_HEAD_DIM, SSM_STATE)

    ab, ach, zs, xbc, dt, sga, sgb = _inproj(h_s, nm, w_proj, tm=_row_tile(n_tok * n_seq, 256), act_dtype=F32)
    tsf = lambda a: a.reshape(n_tok, n_seq, a.shape[-1])
    hh = lax.broadcasted_iota(jnp.int32, (D_BC, LANES), 1)
    kk = lax.broadcasted_iota(jnp.int32, (D_BC, LANES), 0)
    gsum = ((hh < SSM_HEADS) & (kk // SSM_STATE == hh // HEADS_PER_GROUP)).astype(BF16)
    ya, yb, sa_new, sx_new, sample_ssm = _decode_mixer(
        tsf(ab), tsf(ach), tsf(zs), tsf(xbc), tsf(dt),
        jnp.swapaxes(state_conv_a[0], 0, 1), jnp.swapaxes(state_ssm_conv[0], 0, 1), state_ssm[0],
        p, gsum, expand, n_tok=n_tok, n_seq=n_seq)
    flat = lambda a: a.reshape(n_tok * n_seq, a.shape[-1])
    y_s = ffn2(_outproj(flat(ya), flat(yb), sga, sgb, h_s, wa, wb, wo, tm=_row_tile(n_tok * n_seq, 512)))
    y_sample = jnp.swapaxes(y_s.reshape(n_tok, n_seq, D_MODEL), 0, 1)
    sample_conv_a = jnp.swapaxes(sa_new, 0, 1)[None]
    sample_ssm_conv = jnp.swapaxes(sx_new, 0, 1)[None]

    return (y_prompt, y_sample, prompt_conv_a, prompt_ssm_conv, prompt_ssm,
            sample_conv_a, sample_ssm_conv, sample_ssm[None])
```

```python
import functools

import jax
import jax.numpy as jnp
from jax import lax
from jax.experimental import pallas as pl
from jax.experimental.pallas import tpu as pltpu

D_MODEL = 1024
D_FF = 2816
D_SSM = 2048
SSM_HEADS = 32
SSM_HEAD_DIM = 64
SSM_GROUPS = 4
HEADS_PER_GROUP = SSM_HEADS // SSM_GROUPS
SSM_STATE = 128
D_BC = SSM_GROUPS * SSM_STATE
D_XBC = D_SSM + 2 * D_BC
CONV_A_TAPS = 3
CONV_X_TAPS = 4
N_META = 16
EPS = 1e-6

LANES = 128
SUBLANES = 8
BF16_ROWS = 2 * SUBLANES
CHUNK = 128
VMEM_LIMIT = 56 * 1024 * 1024

F32 = jnp.float32
BF16 = jnp.bfloat16


def _dot(a, b):
    return jnp.dot(a, b, preferred_element_type=F32)


def _rms(x, w):
    return x * lax.rsqrt(jnp.mean(x * x, axis=-1, keepdims=True) + EPS) * w


def _silu(x):
    return x * jax.nn.sigmoid(x)


def _softplus(x):
    return jnp.maximum(x, 0.0) + jnp.log1p(jnp.exp(-jnp.abs(x)))


def _split3(x):
    hi = x.astype(BF16)
    r = x - hi.astype(F32)
    mid = r.astype(BF16)
    lo = (r - mid.astype(F32)).astype(BF16)
    return hi, mid, lo


def _split2(x):
    hi = x.astype(BF16)
    lo = (x - hi.astype(F32)).astype(BF16)
    return hi, lo


def _resident(shape):
    return pl.BlockSpec(shape, lambda *_: (0,) * len(shape), pipeline_mode=pl.Buffered(1))


def _ffn_kernel(x_ref, nw_ref, wg_ref, wu_ref, wd_ref, fw_ref, *rest, final_norm, cast_blocks):
    n_cast = len(cast_blocks)
    cast_in, o_ref, cast_out = rest[:n_cast], rest[n_cast], rest[n_cast + 1:]
    x = x_ref[...]
    xn = _rms(x, nw_ref[...]).astype(BF16)
    act = (_silu(_dot(xn, wg_ref[...])) * _dot(xn, wu_ref[...])).astype(BF16)
    for src, dst in zip(cast_in, cast_out):
        dst[...] = src[...].astype(BF16)
    h = x + 0.5 * _dot(act, wd_ref[...])
    if final_norm:
        h = _rms(h, fw_ref[...])
    o_ref[...] = h


def _ffn(x, nw, w_gu, w_down, fw, *, tm, final_norm, cast=()):
    t = x.shape[0]
    once = pl.Buffered(1)
    n_blocks = [pl.cdiv(w.shape[0], rows) for w, rows in cast]
    assert all(nb <= t // tm for nb in n_blocks)
    cast_specs = [pl.BlockSpec((rows, w.shape[1]), functools.partial(
        lambda i, last: (jnp.minimum(i, last), 0), last=nb - 1))
        for (w, rows), nb in zip(cast, n_blocks)]
    out = pl.pallas_call(
        functools.partial(_ffn_kernel, final_norm=final_norm, cast_blocks=tuple(n_blocks)),
        grid=(t // tm,),
        in_specs=[
            pl.BlockSpec((tm, D_MODEL), lambda i: (i, 0)),
            _resident((1, D_MODEL)),
            pl.BlockSpec((D_MODEL, D_FF), lambda i: (0, 0), pipeline_mode=once),
            pl.BlockSpec((D_MODEL, D_FF), lambda i: (0, 1), pipeline_mode=once),
            _resident((D_FF, D_MODEL)),
            _resident((1, D_MODEL)),
        ] + cast_specs,
        out_specs=[pl.BlockSpec((tm, D_MODEL), lambda i: (i, 0))] + cast_specs,
        out_shape=[jax.ShapeDtypeStruct((t, D_MODEL), F32)]
        + [jax.ShapeDtypeStruct(w.shape, BF16) for w, _ in cast],
        compiler_params=pltpu.CompilerParams(
            dimension_semantics=("arbitrary" if cast else "parallel",), vmem_limit_bytes=VMEM_LIMIT),
        name="ffn",
    )(x, nw, w_gu, w_gu, w_down, fw, *[w for w, _ in cast])
    return out if cast else out[0]


C_DT = 3 * D_MODEL + D_SSM + D_XBC
C_GATE = C_DT + LANES


def _project(u, w_refs, c0, width):
    w_main, w_dt, w_gate = w_refs
    if c0 < C_DT:
        w = w_main[c0:c0 + width, :]
    elif c0 < C_GATE:
        w = w_dt[c0 - C_DT:c0 - C_DT + width, :]
    else:
        w = w_gate[c0 - C_GATE:c0 - C_GATE + width, :]
    return lax.dot_general(u, w, (((1,), (1,)), ((), ())), preferred_element_type=F32)


def _proj_weight_specs():
    once = pl.Buffered(1)
    return [pl.BlockSpec((C_DT, D_MODEL), lambda *_: (0, 0), pipeline_mode=once),
            pl.BlockSpec((LANES, D_MODEL), lambda *_: (C_DT // LANES, 0), pipeline_mode=once),
            pl.BlockSpec((pl.Element(2 * D_MODEL), pl.Element(D_MODEL)),
                         lambda *_: (C_DT + SSM_HEADS, 0), pipeline_mode=once)]


def _inproj_kernel(h_ref, nw_ref, wm_ref, wdt_ref, wg_ref,
                   ab_ref, ach_ref, zs_ref, xbc_ref, dt_ref, sga_ref, sgb_ref):
    u = _rms(h_ref[...], nw_ref[...]).astype(BF16)

    def proj(c0, width):
        return _project(u, (wm_ref, wdt_ref, wg_ref), c0, width)

    ab_ref[...] = proj(0, D_MODEL).astype(ab_ref.dtype)
    ach_ref[...] = (proj(D_MODEL, D_MODEL) * proj(2 * D_MODEL, D_MODEL)).astype(ach_ref.dtype)
    c0 = 3 * D_MODEL
    for k in range(D_SSM // D_MODEL):
        zs_ref[:, k * D_MODEL:(k + 1) * D_MODEL] = _silu(proj(c0 + k * D_MODEL, D_MODEL)).astype(zs_ref.dtype)
    c0 += D_SSM
    for k in range(D_XBC // D_MODEL):
        xbc_ref[:, k * D_MODEL:(k + 1) * D_MODEL] = proj(c0 + k * D_MODEL, D_MODEL).astype(xbc_ref.dtype)
    c0 += D_XBC
    dt_ref[...] = proj(c0, LANES)
    c0 += LANES
    sga_ref[...] = jax.nn.sigmoid(proj(c0, D_MODEL)).astype(sga_ref.dtype)
    sgb_ref[...] = jax.nn.sigmoid(proj(c0 + D_MODEL, D_MODEL)).astype(sgb_ref.dtype)


def _inproj(h, nw, w_packed, *, tm, act_dtype):
    t = h.shape[0]
    widths = (D_MODEL, D_MODEL, D_SSM, D_XBC, LANES, D_MODEL, D_MODEL)
    dtypes = (act_dtype, act_dtype, act_dtype, act_dtype, F32, act_dtype, act_dtype)
    return pl.pallas_call(
        _inproj_kernel,
        grid=(t // tm,),
        in_specs=[pl.BlockSpec((tm, D_MODEL), lambda i: (i, 0)), _resident((1, D_MODEL))]
        + _proj_weight_specs(),
        out_specs=[pl.BlockSpec((tm, w), lambda i: (i, 0)) for w in widths],
        out_shape=[jax.ShapeDtypeStruct((t, w), d) for w, d in zip(widths, dtypes)],
        compiler_params=pltpu.CompilerParams(
            dimension_semantics=("parallel",), vmem_limit_bytes=VMEM_LIMIT),
        name="inproj",
    )(h, nw, *w_packed)


COL_BLOCK = 512
N_PARK = 8


def _inproj_conv_kernel(h_ref, nw_ref, wm_ref, wdt_ref, wg_ref, caw_ref, cxw_ref, cxb_ref,
                        ca0_ref, cx0_ref,
                        ya_ref, zs_ref, xc_ref, dt_ref, sga_ref, sgb_ref, cao_ref, cxo_ref,
                        ca_carry, cx_carry, *stages, tm, seq_tiles, valid_len):
    i = pl.program_id(0)

    @pl.when(i % seq_tiles == 0)
    def _():
        ca_carry[...] = ca0_ref[...]
        cx_carry[...] = cx0_ref[...]

    u = _rms(h_ref[...], nw_ref[...]).astype(BF16)

    def proj(c0, width=COL_BLOCK):
        return _project(u, (wm_ref, wdt_ref, wg_ref), c0, width)

    n8 = tm // SUBLANES
    sublane = lax.broadcasted_iota(jnp.int32, (n8, SUBLANES, COL_BLOCK), 1)

    def causal_conv(x, carry_ref, w_ref, taps, cols):
        cur = x.reshape(n8, SUBLANES, COL_BLOCK)
        prev = jnp.concatenate([carry_ref[:, cols], x[:tm - SUBLANES]], axis=0).reshape(cur.shape)
        acc = w_ref[taps - 1:taps, cols] * x
        for d in range(1, taps):
            merged = jnp.where(sublane < SUBLANES - d, cur, prev)
            acc += w_ref[taps - 1 - d:taps - d, cols] * pltpu.roll(merged, d, axis=1).reshape(x.shape)
        carry_ref[:, cols] = x[valid_len - SUBLANES:valid_len]
        return acc

    c_z, c_x, c_dt = 3 * D_MODEL, 3 * D_MODEL + D_SSM, 3 * D_MODEL + D_SSM + D_XBC
    c_g = c_dt + LANES

    def tail_a(k, cols, prj):
        conv = causal_conv(prj[1] * prj[2], ca_carry, caw_ref, CONV_A_TAPS, cols)
        ya_ref[:, cols] = (prj[0] * conv).astype(ya_ref.dtype)

    def tail_z(k, cols, prj):
        zs_ref[:, cols] = _silu(prj[0]).astype(zs_ref.dtype)

    def tail_x(k, cols, prj):
        conv = causal_conv(prj[0], cx_carry, cxw_ref, CONV_X_TAPS, cols)
        xc_ref[:, cols] = _silu(conv + cxb_ref[:, cols]).astype(xc_ref.dtype)

    def tail_g(k, cols, prj):
        sga_ref[:, cols] = jax.nn.sigmoid(prj[0]).astype(sga_ref.dtype)
        sgb_ref[:, cols] = jax.nn.sigmoid(prj[1]).astype(sgb_ref.dtype)

    def blocks(tail, width, starts):
        return [(tail, k, [c + k * COL_BLOCK for c in starts]) for k in range(width // COL_BLOCK)]

    light = blocks(tail_z, D_SSM, [c_z])
    for n, (blk_a, blk_g) in enumerate(zip(blocks(tail_a, D_MODEL, [0, D_MODEL, 2 * D_MODEL]),
                                           blocks(tail_g, D_MODEL, [c_g, c_g + D_MODEL]))):
        light.insert(3 * n, blk_a)
        light.insert(3 * n + 2, blk_g)
    heavy = blocks(tail_x, D_XBC, [c_x])
    order = heavy + light
    park = jnp.minimum(i, 0)
    slot = 0
    for tail, k, starts in order:
        parked = []
        for c in starts:
            stages[slot % len(stages)][park] = proj(c)
            parked.append(stages[slot % len(stages)])
            slot += 1
        tail(k, slice(k * COL_BLOCK, (k + 1) * COL_BLOCK), [s[park] for s in parked])
    dt_ref[...] = proj(c_dt, LANES)
    cao_ref[0] = ca_carry[...]
    cxo_ref[0] = cx_carry[...]


def _inproj_conv(h, nw, w_packed, p, ca0, cx0, *, tm, seq_tiles, valid_len):
    t = h.shape[0]
    n_seq = t // (tm * seq_tiles)
    widths = (D_MODEL, D_SSM, D_XBC, LANES, D_MODEL, D_MODEL)
    dtypes = (BF16, BF16, BF16, F32, BF16, BF16)
    tail = lambda w: pl.BlockSpec((1, SUBLANES, w), lambda i: (i // seq_tiles, 0, 0))
    return pl.pallas_call(
        functools.partial(_inproj_conv_kernel, tm=tm, seq_tiles=seq_tiles, valid_len=valid_len),
        grid=(t // tm,),
        in_specs=[pl.BlockSpec((tm, D_MODEL), lambda i: (i, 0)), _resident((1, D_MODEL))]
        + _proj_weight_specs()
        + [_resident((CONV_A_TAPS, D_MODEL)), _resident((CONV_X_TAPS, D_XBC)),
           _resident((1, D_XBC)),
           _resident((SUBLANES, D_MODEL)), _resident((SUBLANES, D_XBC))],
        out_specs=[pl.BlockSpec((tm, w), lambda i: (i, 0)) for w in widths]
        + [tail(D_MODEL), tail(D_XBC)],
        out_shape=[jax.ShapeDtypeStruct((t, w), d) for w, d in zip(widths, dtypes)]
        + [jax.ShapeDtypeStruct((n_seq, SUBLANES, D_MODEL), F32),
           jax.ShapeDtypeStruct((n_seq, SUBLANES, D_XBC), F32)],
        scratch_shapes=[pltpu.VMEM((SUBLANES, D_MODEL), F32), pltpu.VMEM((SUBLANES, D_XBC), F32),
                        ] + [pltpu.VMEM((1, tm, COL_BLOCK), F32)] * N_PARK,
        compiler_params=pltpu.CompilerParams(
            dimension_semantics=("arbitrary",), vmem_limit_bytes=VMEM_LIMIT),
        name="inproj_conv",
    )(h, nw, *w_packed, p["conv_a_w"], p["ssm_conv_w"], p["ssm_conv_b"], ca0, cx0)


def _outproj_kernel(ya_ref, yb_ref, sga_ref, sgb_ref, h_ref, wa_ref, wb_ref, wo_ref, o_ref):
    y_a = _dot(ya_ref[...].astype(BF16), wa_ref[...])
    y_b = _dot(yb_ref[...].astype(BF16), wb_ref[...])
    merged = sga_ref[...].astype(F32) * y_a + sgb_ref[...].astype(F32) * y_b
    o_ref[...] = h_ref[...] + _dot(merged.astype(BF16), wo_ref[...])


def _outproj(ya, yb, sga, sgb, h, wa, wb, wo, *, tm):
    t = h.shape[0]
    row = lambda w: pl.BlockSpec((tm, w), lambda i: (i, 0))
    return pl.pallas_call(
        _outproj_kernel,
        grid=(t // tm,),
        in_specs=[row(D_MODEL), row(D_SSM), row(D_MODEL), row(D_MODEL), row(D_MODEL),
                  _resident((D_MODEL, D_MODEL)), _resident((D_SSM, D_MODEL)),
                  _resident((D_MODEL, D_MODEL))],
        out_specs=row(D_MODEL),
        out_shape=jax.ShapeDtypeStruct((t, D_MODEL), F32),
        compiler_params=pltpu.CompilerParams(
            dimension_semantics=("parallel",), vmem_limit_bytes=VMEM_LIMIT),
        name="outproj",
    )(ya, yb, sga, sgb, h, wa, wb, wo)


def _gated_group_norm(y, xs, zs, dskip, normw):
    y = (y + dskip * xs) * zs
    gw = D_SSM // SSM_GROUPS
    parts = []
    for g in range(SSM_GROUPS):
        yg = y[:, g * gw:(g + 1) * gw]
        parts.append(yg * lax.rsqrt(jnp.mean(yg * yg, axis=-1, keepdims=True) + EPS))
    return jnp.concatenate(parts, axis=-1) * normw


GROUP_W = HEADS_PER_GROUP * SSM_HEAD_DIM
LOG2E = 1.4426950408889634


def _ssd_masks():
    q = CHUNK
    rows = lax.broadcasted_iota(jnp.int32, (q, q), 0)
    cols = lax.broadcasted_iota(jnp.int32, (q, q), 1)
    causal = rows >= cols
    first_head = lax.broadcasted_iota(jnp.int32, (q, LANES), 1) < SSM_HEAD_DIM
    return causal, causal.astype(BF16), first_head


def _ssd_chunk(tok, r0, masks, zs_ref, xc_ref, dt_ref, dtb_ref, alog_ref, dskip_ref, normw_ref,
               expand_ref, hT, ybuf, valid_len, live=None, after_pair=None):
    q = CHUNK
    causal, tri, first_head = masks
    a_neg = -jnp.exp(alog_ref[...])
    dt = _softplus(dt_ref[tok, :] + dtb_ref[...])
    if valid_len is not None:
        t_idx = r0 + lax.broadcasted_iota(jnp.int32, (q, LANES), 0)
        dt = jnp.where(t_idx < valid_len, dt, 0.0)
    if live is not None:
        dt = jnp.where(live, dt, 0.0)
    hi, mid, lo = _split3(dt * a_neg)
    acs3 = _dot(tri, jnp.concatenate([hi, mid, lo], axis=1))
    acs = acs3[:, :LANES] + acs3[:, LANES:2 * LANES] + acs3[:, 2 * LANES:]
    acs2 = acs * LOG2E
    row_t = (acs2 - jnp.log2(dt)).T
    acs_end = acs2[q - 1:q, :]
    cd_hi, cd_lo = _split2(jnp.broadcast_to(jnp.exp2(acs_end), (SUBLANES, LANES)))
    stack = jnp.concatenate(
        [jnp.exp2(acs2).astype(BF16), (jnp.exp2(acs_end - acs2) * dt).astype(BF16),
         cd_hi, cd_lo], axis=0)

    for g in range(SSM_GROUPS):
        gcols = slice(g * GROUP_W, (g + 1) * GROUP_W)
        ex = _dot(stack, expand_ref[:, gcols])
        state_decay = ex[2 * q:2 * q + 1] + ex[2 * q + SUBLANES:2 * q + SUBLANES + 1]
        b_t = xc_ref[tok, D_SSM + g * SSM_STATE:D_SSM + (g + 1) * SSM_STATE].astype(F32).T.astype(BF16)
        c_bf = xc_ref[tok, D_SSM + D_BC + g * SSM_STATE:D_SSM + D_BC + (g + 1) * SSM_STATE]
        cb = _dot(c_bf, b_t)
        h_g = hT[g]
        y_g = ex[0:q] * _dot(c_bf, h_g.astype(BF16))
        xw_g = xc_ref[tok, gcols] * ex[q:2 * q].astype(BF16)
        hT[g] = h_g * state_decay + _dot(b_t, xw_g)
        for jp in range(HEADS_PER_GROUP // 2):
            pair = g * (HEADS_PER_GROUP // 2) + jp
            lanes = slice(pair * LANES, (pair + 1) * LANES)
            x_pair = xc_ref[tok, lanes]
            zero = jnp.zeros_like(x_pair)
            x2 = jnp.concatenate([jnp.where(first_head, x_pair, zero),
                                  jnp.where(first_head, zero, x_pair)], axis=0)
            m2 = []
            for e in range(2):
                hd = 2 * pair + e
                acs_col = jnp.broadcast_to(acs2[:, hd:hd + 1], (q, q))
                decay = jnp.exp2(jnp.where(causal, acs_col - row_t[hd:hd + 1, :], -jnp.inf))
                m2.append((cb * decay).astype(BF16))
            ybuf[:, lanes] = y_g[:, jp * LANES:(jp + 1) * LANES] + _dot(
                jnp.concatenate(m2, axis=1), x2)
            if after_pair is not None:
                after_pair()

    return _gated_group_norm(ybuf[...], xc_ref[tok, :D_SSM].astype(F32),
                             zs_ref[tok, :].astype(F32), dskip_ref[...], normw_ref[...])


def _write_final_state(hT, hTo_ref, ho_ref):
    hTo_ref[0] = hT[...]
    for g in range(SSM_GROUPS):
        ho_ref[0, g] = hT[g].T


def _mixer_kernel(zs_ref, xc_ref, dt_ref, dtb_ref, alog_ref, dskip_ref, normw_ref, expand_ref,
                  h0_ref, yb_ref, hTo_ref, ho_ref, hT, ybuf, *, tb, valid_len):
    i = pl.program_id(1)

    @pl.when(i == 0)
    def _():
        hT[...] = h0_ref[...]

    masks = _ssd_masks()

    def chunk_body(c, carry):
        r0 = pl.multiple_of(c * CHUNK, CHUNK)
        tok = pl.ds(r0, CHUNK)
        yn = _ssd_chunk(tok, r0, masks, zs_ref, xc_ref, dt_ref, dtb_ref, alog_ref, dskip_ref,
                        normw_ref, expand_ref, hT, ybuf, valid_len if valid_len < tb else None)
        yb_ref[tok, :] = yn.astype(yb_ref.dtype)
        return carry

    lax.fori_loop(0, tb // CHUNK, chunk_body, 0)

    @pl.when(i == pl.num_programs(1) - 1)
    def _():
        _write_final_state(hT, hTo_ref, ho_ref)


PROJ_BLOCK = 256


def _mixer_outproj_kernel(zs_ref, xc_ref, dt_ref, ya_ref, sga_ref, sgb_ref, h_ref,
                          dtb_ref, alog_ref, dskip_ref, normw_ref, expand_ref, h0_ref,
                          wa_ref, wb_ref, wo_ref,
                          o_ref, hTo_ref, ho_ref, hT, ybuf, yb_even, yb_odd, ya_scr, yb_out, mg_scr,
                          *, tb, nt):
    t = pl.program_id(0)
    n_tiles = pl.num_programs(0) - 1
    live = t < n_tiles

    @pl.when(t % nt == 0)
    def _():
        hT[...] = h0_ref[...]

    @pl.when(t == 0)
    def _():
        yb_odd[...] = jnp.zeros_like(yb_odd)

    masks = _ssd_masks()

    def stage(cur, prev):
        nb = D_MODEL // PROJ_BLOCK
        col = lambda k: slice(k * PROJ_BLOCK, (k + 1) * PROJ_BLOCK)

        def piece_a(k):
            ya_scr[:, col(k)] = _dot(ya_ref[...], wa_ref[:, col(k)])

        def piece_b(k):
            yb_out[:, col(k)] = _dot(prev[...], wb_ref[:, col(k)])

        def piece_merge():
            mg_scr[...] = (sga_ref[...].astype(F32) * ya_scr[...]
                           + sgb_ref[...].astype(F32) * yb_out[...]).astype(BF16)

        def piece_o(k):
            o_ref[:, col(k)] = h_ref[:, col(k)] + _dot(mg_scr[...], wo_ref[:, col(k)])

        pieces = ([functools.partial(piece_a, k) for k in range(nb)]
                  + [functools.partial(piece_b, k) for k in range(nb)] + [piece_merge]
                  + [functools.partial(piece_o, k) for k in range(nb)])

        n_pieces, n_slots = len(pieces), tb // CHUNK * SSM_HEADS // 2
        slots_seen = [0]

        def next_piece():
            slots_seen[0] += 1
            while pieces and (n_pieces - len(pieces)) * n_slots < slots_seen[0] * n_pieces:
                pieces.pop(0)()

        for c in range(tb // CHUNK):
            r0 = c * CHUNK
            tok = slice(r0, r0 + CHUNK)
            yn = _ssd_chunk(tok, r0, masks, zs_ref, xc_ref, dt_ref, dtb_ref, alog_ref, dskip_ref,
                            normw_ref, expand_ref, hT, ybuf, None, live, next_piece)
            cur[tok, :] = yn.astype(BF16)
        while pieces:
            next_piece()

    @pl.when(t % 2 == 0)
    def _():
        stage(yb_even, yb_odd)

    @pl.when(t % 2 == 1)
    def _():
        stage(yb_odd, yb_even)

    @pl.when(jnp.logical_and(live, t % nt == nt - 1))
    def _():
        _write_final_state(hT, hTo_ref, ho_ref)


def _mixer_outproj(zs, xc, dt, ya, sga, sgb, h, p, expand, h0, wa, wb, wo, *, batch, seqlen, tb):
    nt = seqlen // tb
    n_tiles = batch * nt
    cur = lambda w: pl.BlockSpec((tb, w), lambda t: (jnp.minimum(t, n_tiles - 1), 0))
    lag = lambda w: pl.BlockSpec((tb, w), lambda t: (jnp.maximum(t - 1, 0), 0))
    full = lambda shape: pl.BlockSpec(shape, lambda t: (0,) * len(shape))
    once = lambda shape: pl.BlockSpec(shape, lambda t: (0,) * len(shape), pipeline_mode=pl.Buffered(1))
    per_b = lambda shape: pl.BlockSpec(
        (1,) + shape, lambda t: (jnp.minimum(t // nt, batch - 1),) + (0,) * len(shape))
    return pl.pallas_call(
        functools.partial(_mixer_outproj_kernel, tb=tb, nt=nt),
        grid=(n_tiles + 1,),
        in_specs=[cur(D_SSM), cur(D_XBC), cur(LANES), lag(D_MODEL), lag(D_MODEL), lag(D_MODEL),
                  lag(D_MODEL),
                  full((1, LANES)), full((1, LANES)), full((1, D_SSM)), full((1, D_SSM)),
                  once((LANES, D_SSM)), full((SSM_GROUPS, SSM_STATE, GROUP_W)),
                  once((D_MODEL, D_MODEL)), once((D_SSM, D_MODEL)), once((D_MODEL, D_MODEL))],
        out_specs=[lag(D_MODEL), per_b((SSM_GROUPS, SSM_STATE, GROUP_W)),
                   per_b((SSM_GROUPS, GROUP_W, SSM_STATE))],
        out_shape=[jax.ShapeDtypeStruct((batch * seqlen, D_MODEL), F32),
                   jax.ShapeDtypeStruct((batch, SSM_GROUPS, SSM_STATE, GROUP_W), F32),
                   jax.ShapeDtypeStruct((batch, SSM_GROUPS, GROUP_W, SSM_STATE), F32)],
        scratch_shapes=[pltpu.VMEM((SSM_GROUPS, SSM_STATE, GROUP_W), F32),
                        pltpu.VMEM((CHUNK, D_SSM), F32),
                        pltpu.VMEM((tb, D_SSM), BF16),
                        pltpu.VMEM((tb, D_SSM), BF16),
                        pltpu.VMEM((tb, D_MODEL), F32), pltpu.VMEM((tb, D_MODEL), F32),
                        pltpu.VMEM((tb, D_MODEL), BF16)],
        compiler_params=pltpu.CompilerParams(
            dimension_semantics=("arbitrary",), vmem_limit_bytes=VMEM_LIMIT),
        name="mixer_outproj",
    )(zs, xc, dt, ya, sga, sgb, h, p["dt_bias"], p["a_log"], p["d_skip"], p["ssm_norm_w"], expand,
      h0, wa, wb, wo)


def _mixer(zs, xc, dt, p, expand, h0, *, batch, seqlen, tb, valid_len):
    nt = seqlen // tb
    row = lambda w: pl.BlockSpec((tb, w), lambda b, i: (b * nt + i, 0))
    full = lambda shape: pl.BlockSpec(shape, lambda b, i: (0,) * len(shape))
    per_b = lambda shape: pl.BlockSpec((1,) + shape, lambda b, i: (b,) + (0,) * len(shape))
    return pl.pallas_call(
        functools.partial(_mixer_kernel, tb=tb, valid_len=valid_len),
        grid=(batch, nt),
        in_specs=[row(D_SSM), row(D_XBC), row(LANES),
                  full((1, LANES)), full((1, LANES)), full((1, D_SSM)), full((1, D_SSM)),
                  full((LANES, D_SSM)), full((SSM_GROUPS, SSM_STATE, GROUP_W))],
        out_specs=[row(D_SSM), per_b((SSM_GROUPS, SSM_STATE, GROUP_W)),
                   per_b((SSM_GROUPS, GROUP_W, SSM_STATE))],
        out_shape=[jax.ShapeDtypeStruct((batch * seqlen, D_SSM), BF16),
                   jax.ShapeDtypeStruct((batch, SSM_GROUPS, SSM_STATE, GROUP_W), F32),
                   jax.ShapeDtypeStruct((batch, SSM_GROUPS, GROUP_W, SSM_STATE), F32)],
        scratch_shapes=[pltpu.VMEM((SSM_GROUPS, SSM_STATE, GROUP_W), F32),
                        pltpu.VMEM((CHUNK, D_SSM), F32)],
        compiler_params=pltpu.CompilerParams(
            dimension_semantics=("arbitrary", "arbitrary"), vmem_limit_bytes=VMEM_LIMIT),
        name="mixer",
    )(zs, xc, dt, p["dt_bias"], p["a_log"], p["d_skip"], p["ssm_norm_w"], expand, h0)


SEQ_BLOCK = SUBLANES


def _decode_mixer_kernel(ab_ref, ach_ref, zs_ref, xbc_ref, dt_ref, sa_ref, sx_ref, h0_ref,
                         caw_ref, cxw_ref, cxb_ref, dtb_ref, alog_ref, dskip_ref, normw_ref,
                         gsum_ref, expand_ref,
                         ya_ref, yb_ref, sao_ref, sxo_ref, ho_ref,
                         xc_buf, bc_buf, os_buf, xw_buf, cd_buf, y_buf, *, n_tok):
    sb = SEQ_BLOCK

    xp = [sa_ref[k] for k in range(CONV_A_TAPS - 1)] + [ach_ref[t] for t in range(n_tok)]
    for t in range(n_tok):
        conv = caw_ref[0:1, :] * xp[t]
        for k in range(1, CONV_A_TAPS):
            conv += caw_ref[k:k + 1, :] * xp[t + k]
        ya_ref[t] = ab_ref[t] * conv
    for k in range(CONV_A_TAPS - 1):
        sao_ref[k] = xp[n_tok + k]

    xp = [sx_ref[k] for k in range(CONV_X_TAPS - 1)] + [xbc_ref[t] for t in range(n_tok)]
    for t in range(n_tok):
        conv = cxw_ref[0:1, :] * xp[t]
        for k in range(1, CONV_X_TAPS):
            conv += cxw_ref[k:k + 1, :] * xp[t + k]
        xc_buf[t * sb:(t + 1) * sb, :] = _silu(conv + cxb_ref[...])
    for k in range(CONV_X_TAPS - 1):
        sxo_ref[k] = xp[n_tok + k]

    a_neg = -jnp.exp(alog_ref[...])
    dts, acs = [], []
    for t in range(n_tok):
        dts.append(_softplus(dt_ref[t] + dtb_ref[...]))
        acs.append(dts[t] * a_neg + (acs[t - 1] if t else 0.0))

    pairs = [(qi, si) for qi in range(n_tok) for si in range(qi + 1)]
    prods = [xc_buf[qi * sb:(qi + 1) * sb, D_SSM + D_BC:] * xc_buf[si * sb:(si + 1) * sb, D_SSM:D_SSM + D_BC]
             for qi, si in pairs]
    p_hi, p_lo = _split2(jnp.concatenate(prods, axis=0))
    cbh = _dot(p_hi, gsum_ref[...]) + _dot(p_lo, gsum_ref[...])
    coef = [jnp.exp(acs[t]) for t in range(n_tok)]
    coef += [cbh[k * sb:(k + 1) * sb] * jnp.exp(acs[qi] - acs[si]) * dts[si]
             for k, (qi, si) in enumerate(pairs)]
    coef += [jnp.exp(acs[n_tok - 1] - acs[s]) * dts[s] for s in range(n_tok)]
    coef.append(jnp.exp(acs[n_tok - 1]))
    c_hi, c_lo = _split2(jnp.concatenate(coef, axis=0))
    coef_x = _dot(c_hi, expand_ref[...]) + _dot(c_lo, expand_ref[...])
    slab = lambda k: coef_x[k * sb:(k + 1) * sb]

    def store_tiles(buf, r0, val):
        for lt in range(val.shape[1] // LANES):
            buf[lt, r0:r0 + val.shape[0], :] = val[:, lt * LANES:(lt + 1) * LANES]

    def load_rows(buf, lt0, n_lt, rows):
        return jnp.concatenate([buf[lt0 + j, rows, :] for j in range(n_lt)], axis=1)

    store_tiles(bc_buf, 0, xc_buf[:, D_SSM:])
    store_tiles(os_buf, 0, coef_x[0:n_tok * sb])
    diag = {pr: slab(n_tok + k) for k, pr in enumerate(pairs)}
    k0 = n_tok + len(pairs)
    for s in range(n_tok):
        store_tiles(xw_buf, s * sb, slab(k0 + s) * xc_buf[s * sb:(s + 1) * sb, :D_SSM])
    cd_hi, cd_lo = _split2(slab(k0 + n_tok))
    store_tiles(cd_buf, 0, cd_hi.astype(F32))
    store_tiles(cd_buf, sb, cd_lo.astype(F32))

    for t in range(n_tok):
        acc = diag[(t, 0)] * xc_buf[0:sb, :D_SSM]
        for s in range(1, t + 1):
            acc += diag[(t, s)] * xc_buf[s * sb:(s + 1) * sb, :D_SSM]
        store_tiles(y_buf, t * sb, acc)

    gt = GROUP_W // LANES
    k_rows = BF16_ROWS
    krow = lax.broadcasted_iota(jnp.int32, (k_rows, SSM_STATE), 0)
    ones_rows = jnp.where((krow >= n_tok) & (krow < n_tok + 2), 1.0, 0.0).astype(BF16)
    pad_lhs = jnp.zeros((k_rows - n_tok - 2, GROUP_W), F32)
    pad_rhs = jnp.zeros((k_rows - n_tok, SSM_STATE), F32)

    def seq_body(b, carry):
        tok_rows = pl.ds(b, n_tok, stride=sb)
        for g in range(SSM_GROUPS):
            b_bg = bc_buf[g, tok_rows, :]
            c_bg = bc_buf[SSM_GROUPS + g, tok_rows, :]
            heads = pl.ds(g * HEADS_PER_GROUP, HEADS_PER_GROUP)
            h_bg = h0_ref[b, heads].reshape(GROUP_W, SSM_STATE)
            y_off = lax.dot_general(c_bg.astype(BF16), h_bg.astype(BF16),
                                    (((1,), (1,)), ((), ())), preferred_element_type=F32)
            y_off = y_off * load_rows(os_buf, g * gt, gt, tok_rows)
            for j in range(gt):
                y_buf[g * gt + j, tok_rows, :] = (y_buf[g * gt + j, tok_rows, :]
                                                  + y_off[:, j * LANES:(j + 1) * LANES])
            lhs = jnp.concatenate(
                [load_rows(xw_buf, g * gt, gt, tok_rows),
                 load_rows(cd_buf, g * gt, gt, pl.ds(b, 2, stride=sb)), pad_lhs],
                axis=0).astype(BF16)
            rhs = jnp.concatenate(
                [jnp.concatenate([b_bg, pad_rhs], axis=0).astype(BF16), ones_rows], axis=1)
            upd = lax.dot_general(lhs, rhs, (((0,), (0,)), ((), ())), preferred_element_type=F32)
            new = h_bg * upd[:, SSM_STATE:] + upd[:, :SSM_STATE]
            ho_ref[b, heads] = new.reshape(HEADS_PER_GROUP, SSM_HEAD_DIM, SSM_STATE)
        return carry

    lax.fori_loop(0, sb, seq_body, 0, unroll=4)

    xs_all = xc_buf[:, :D_SSM]
    zs_all = jnp.concatenate([zs_ref[t] for t in range(n_tok)], axis=0)
    y_all = load_rows(y_buf, 0, D_SSM // LANES, slice(None))
    yn = _gated_group_norm(y_all, xs_all, zs_all, dskip_ref[...], normw_ref[...])
    for t in range(n_tok):
        yb_ref[t] = yn[t * sb:(t + 1) * sb]


def _decode_mixer(ab, ach, zs, xbc, dt, sa, sx, h0, p, gsum, expand, *, n_tok, n_seq):
    sb = SEQ_BLOCK
    tok = lambda w: pl.BlockSpec((n_tok, sb, w), lambda i: (0, i, 0))
    st = lambda k, w: pl.BlockSpec((k, sb, w), lambda i: (0, i, 0))
    hblk = pl.BlockSpec((sb, SSM_HEADS, SSM_HEAD_DIM, SSM_STATE), lambda i: (i, 0, 0, 0))
    full = lambda shape: pl.BlockSpec(shape, lambda i: (0,) * len(shape))
    return pl.pallas_call(
        functools.partial(_decode_mixer_kernel, n_tok=n_tok),
        grid=(n_seq // sb,),
        in_specs=[tok(D_MODEL), tok(D_MODEL), tok(D_SSM), tok(D_XBC), tok(LANES),
                  st(CONV_A_TAPS - 1, D_MODEL), st(CONV_X_TAPS - 1, D_XBC), hblk,
                  full((CONV_A_TAPS, D_MODEL)), full((CONV_X_TAPS, D_XBC)), full((1, D_XBC)),
                  full((1, LANES)), full((1, LANES)), full((1, D_SSM)), full((1, D_SSM)),
                  full((D_BC, LANES)), full((LANES, D_SSM))],
        out_specs=[tok(D_MODEL), tok(D_SSM), st(CONV_A_TAPS - 1, D_MODEL),
                   st(CONV_X_TAPS - 1, D_XBC), hblk],
        out_shape=[jax.ShapeDtypeStruct((n_tok, n_seq, D_MODEL), F32),
                   jax.ShapeDtypeStruct((n_tok, n_seq, D_SSM), F32),
                   jax.ShapeDtypeStruct((CONV_A_TAPS - 1, n_seq, D_MODEL), F32),
                   jax.ShapeDtypeStruct((CONV_X_TAPS - 1, n_seq, D_XBC), F32),
                   jax.ShapeDtypeStruct((n_seq, SSM_HEADS, SSM_HEAD_DIM, SSM_STATE), F32)],
        scratch_shapes=[pltpu.VMEM((n_tok * sb, D_XBC), F32),
                        pltpu.VMEM((2 * D_BC // LANES, n_tok * sb, LANES), F32),
                        pltpu.VMEM((D_SSM // LANES, n_tok * sb, LANES), F32),
                        pltpu.VMEM((D_SSM // LANES, n_tok * sb, LANES), F32),
                        pltpu.VMEM((D_SSM // LANES, 2 * sb, LANES), F32),
                        pltpu.VMEM((D_SSM // LANES, n_tok * sb, LANES), F32)],
        compiler_params=pltpu.CompilerParams(
            dimension_semantics=("parallel",), vmem_limit_bytes=VMEM_LIMIT),
        name="decode_mixer",
    )(ab, ach, zs, xbc, dt, sa, sx, h0, p["conv_a_w"], p["ssm_conv_w"], p["ssm_conv_b"],
      p["dt_bias"], p["a_log"], p["d_skip"], p["ssm_norm_w"], gsum, expand)


def _row_tile(t, cap):
    if t <= cap:
        return t
    best = LANES
    for cand in range(LANES, cap + 1, LANES):
        if t % cand == 0:
            best = cand
    return best


def kernel(x_prompt, x_sample, state_conv_a, state_ssm_conv, state_ssm, meta_tokens, norm_ffn1, ffn1_w_gu, ffn1_w_down, norm_mix, w_in, conv_a_w, w_a_out, ssm_conv_w, ssm_conv_b, dt_bias, a_log, d_skip, ssm_norm_w, w_b_out, w_o, norm_ffn2, ffn2_w_gu, ffn2_w_down, norm_final):
    bsz, seqlen, _ = x_prompt.shape
    n_seq, n_tok, _ = x_sample.shape
    n_dt = SSM_HEADS
    c_dt = 3 * D_MODEL + D_SSM + D_XBC

    w_gu1, w_dn1 = ffn1_w_gu[0].astype(BF16), ffn1_w_down[0].astype(BF16)
    pad_heads = lambda v: jnp.pad(v[0], (0, LANES - n_dt))[None]
    p = {
        "conv_a_w": conv_a_w[0], "ssm_conv_w": ssm_conv_w[0], "ssm_conv_b": ssm_conv_b[0][None],
        "dt_bias": pad_heads(dt_bias), "a_log": pad_heads(a_log),
        "d_skip": jnp.repeat(d_skip[0], SSM_HEAD_DIM)[None], "ssm_norm_w": ssm_norm_w[0][None],
    }
    n1, nm, n2, nf = norm_ffn1[0][None], norm_mix[0][None], norm_ffn2[0][None], norm_final[None]
    eh = lax.broadcasted_iota(jnp.int32, (LANES, D_SSM), 0)
    ec = lax.broadcasted_iota(jnp.int32, (LANES, D_SSM), 1)
    expand = (ec // SSM_HEAD_DIM == eh).astype(BF16)

    def ffn1(x, cap=512, cast=()):
        return _ffn(x, n1, w_gu1, w_dn1, nf, tm=_row_tile(x.shape[0], cap), final_norm=False, cast=cast)

    n_steps = bsz * seqlen // _row_tile(bsz * seqlen, 512)
    blk = lambda w: (w, BF16_ROWS * pl.cdiv(w.shape[0], n_steps * BF16_ROWS))
    h_p, w_in_t, w_gu2, w_dn2, wa, wb, wo = ffn1(
        x_prompt.reshape(bsz * seqlen, D_MODEL),
        cast=[blk(jnp.swapaxes(w_in[0], 0, 1)), blk(ffn2_w_gu[0]), blk(ffn2_w_down[0]),
              blk(w_a_out[0]), blk(w_b_out[0]), blk(w_o[0])])
    w_proj = (w_in_t,) * 3

    def ffn2(h):
        return _ffn(h, n2, w_gu2, w_dn2, nf, tm=_row_tile(h.shape[0], 512), final_norm=True)

    zeros = lambda *s: jnp.zeros(s, F32)
    x_s = jnp.swapaxes(x_sample, 0, 1).reshape(n_tok * n_seq, D_MODEL)
    h_sm = ffn1(jnp.concatenate([x_s, jnp.pad(meta_tokens, ((0, CHUNK - N_META), (0, 0)))], axis=0),
                cap=n_tok * n_seq + CHUNK)
    h_s, h_m = h_sm[:n_tok * n_seq], h_sm[n_tok * n_seq:]
    _, zs, xc, dt, _, _, ca_m, cx_m = _inproj_conv(
        h_m, nm, w_proj, p, zeros(SUBLANES, D_MODEL), zeros(SUBLANES, D_XBC),
        tm=CHUNK, seq_tiles=1, valid_len=N_META)
    _, hT_m, _ = _mixer(zs, xc, dt, p, expand, zeros(SSM_GROUPS, SSM_STATE, GROUP_W),
                        batch=1, seqlen=CHUNK, tb=CHUNK, valid_len=N_META)

    tm = _row_tile(seqlen, 512)
    ya, zs, xc, dt, sga, sgb, ca_p, cx_p = _inproj_conv(
        h_p, nm, w_proj, p, ca_m[0], cx_m[0], tm=tm, seq_tiles=seqlen // tm, valid_len=tm)
    tb = _row_tile(seqlen, 512)
    h2_p, _, h_out_p = _mixer_outproj(zs, xc, dt, ya, sga, sgb, h_p, p, expand, hT_m[0], wa, wb, wo,
                                      batch=bsz, seqlen=seqlen, tb=tb)
    y_prompt = ffn2(h2_p).reshape(bsz, seqlen, D_MODEL)
    prompt_conv_a = ca_p[None, :, SUBLANES - (CONV_A_TAPS - 1):, :]
    prompt_ssm_conv = cx_p[None, :, SUBLANES - (CONV_X_TAPS - 1):, :]
    prompt_ssm = h_out_p.reshape(1, bsz, SSM_HEADS, SSM_HEAD_DIM, SSM_STATE)

    ab, ach, zs, xbc, dt, sga, sgb = _inproj(h_s, nm, w_proj, tm=_row_tile(n_tok * n_seq, 256), act_dtype=F32)
    tsf = lambda a: a.reshape(n_tok, n_seq, a.shape[-1])
    hh = lax.broadcasted_iota(jnp.int32, (D_BC, LANES), 1)
    kk = lax.broadcasted_iota(jnp.int32, (D_BC, LANES), 0)
    gsum = ((hh < SSM_HEADS) & (kk // SSM_STATE == hh // HEADS_PER_GROUP)).astype(BF16)
    ya, yb, sa_new, sx_new, sample_ssm = _decode_mixer(
        tsf(ab), tsf(ach), tsf(zs), tsf(xbc), tsf(dt),
        jnp.swapaxes(state_conv_a[0], 0, 1), jnp.swapaxes(state_ssm_conv[0], 0, 1), state_ssm[0],
        p, gsum, expand, n_tok=n_tok, n_seq=n_seq)
    flat = lambda a: a.reshape(n_tok * n_seq, a.shape[-1])
    y_s = ffn2(_outproj(flat(ya), flat(yb), sga, sgb, h_s, wa, wb, wo, tm=_row_tile(n_tok * n_seq, 512)))
    y_sample = jnp.swapaxes(y_s.reshape(n_tok, n_seq, D_MODEL), 0, 1)
    sample_conv_a = jnp.swapaxes(sa_new, 0, 1)[None]
    sample_ssm_conv = jnp.swapaxes(sx_new, 0, 1)[None]

    return (y_prompt, y_sample, prompt_conv_a, prompt_ssm_conv, prompt_ssm,
            sample_conv_a, sample_ssm_conv, sample_ssm[None])
```

```python
import functools

import jax
import jax.numpy as jnp
from jax import lax
from jax.experimental import pallas as pl
from jax.experimental.pallas import tpu as pltpu

D_MODEL = 1024
D_FF = 2816
D_SSM = 2048
SSM_HEADS = 32
SSM_HEAD_DIM = 64
SSM_GROUPS = 4
HEADS_PER_GROUP = SSM_HEADS // SSM_GROUPS
SSM_STATE = 128
D_BC = SSM_GROUPS * SSM_STATE
D_XBC = D_SSM + 2 * D_BC
CONV_A_TAPS = 3
CONV_X_TAPS = 4
N_META = 16
EPS = 1e-6

LANES = 128
SUBLANES = 8
BF16_ROWS = 2 * SUBLANES
CHUNK = 128
VMEM_LIMIT = 56 * 1024 * 1024

F32 = jnp.float32
BF16 = jnp.bfloat16


def _dot(a, b):
    return jnp.dot(a, b, preferred_element_type=F32)


def _rms(x, w):
    return x * lax.rsqrt(jnp.mean(x * x, axis=-1, keepdims=True) + EPS) * w


def _silu(x):
    return x * jax.nn.sigmoid(x)


def _softplus(x):
    return jnp.maximum(x, 0.0) + jnp.log1p(jnp.exp(-jnp.abs(x)))


def _split3(x):
    hi = x.astype(BF16)
    r = x - hi.astype(F32)
    mid = r.astype(BF16)
    lo = (r - mid.astype(F32)).astype(BF16)
    return hi, mid, lo


def _split2(x):
    hi = x.astype(BF16)
    lo = (x - hi.astype(F32)).astype(BF16)
    return hi, lo


def _resident(shape):
    return pl.BlockSpec(shape, lambda *_: (0,) * len(shape), pipeline_mode=pl.Buffered(1))


def _ffn_kernel(x_ref, nw_ref, wg_ref, wu_ref, wd_ref, fw_ref, *rest, final_norm, cast_blocks):
    n_cast = len(cast_blocks)
    cast_in, o_ref, cast_out = rest[:n_cast], rest[n_cast], rest[n_cast + 1:]
    x = x_ref[...]
    xn = _rms(x, nw_ref[...]).astype(BF16)
    act = (_silu(_dot(xn, wg_ref[...])) * _dot(xn, wu_ref[...])).astype(BF16)
    for src, dst in zip(cast_in, cast_out):
        dst[...] = src[...].astype(BF16)
    h = x + 0.5 * _dot(act, wd_ref[...])
    if final_norm:
        h = _rms(h, fw_ref[...])
    o_ref[...] = h


def _ffn(x, nw, w_gu, w_down, fw, *, tm, final_norm, cast=()):
    t = x.shape[0]
    once = pl.Buffered(1)
    n_blocks = [pl.cdiv(w.shape[0], rows) for w, rows in cast]
    assert all(nb <= t // tm for nb in n_blocks)
    cast_specs = [pl.BlockSpec((rows, w.shape[1]), functools.partial(
        lambda i, last: (jnp.minimum(i, last), 0), last=nb - 1))
        for (w, rows), nb in zip(cast, n_blocks)]
    out = pl.pallas_call(
        functools.partial(_ffn_kernel, final_norm=final_norm, cast_blocks=tuple(n_blocks)),
        grid=(t // tm,),
        in_specs=[
            pl.BlockSpec((tm, D_MODEL), lambda i: (i, 0)),
            _resident((1, D_MODEL)),
            pl.BlockSpec((D_MODEL, D_FF), lambda i: (0, 0), pipeline_mode=once),
            pl.BlockSpec((D_MODEL, D_FF), lambda i: (0, 1), pipeline_mode=once),
            _resident((D_FF, D_MODEL)),
            _resident((1, D_MODEL)),
        ] + cast_specs,
        out_specs=[pl.BlockSpec((tm, D_MODEL), lambda i: (i, 0))] + cast_specs,
        out_shape=[jax.ShapeDtypeStruct((t, D_MODEL), F32)]
        + [jax.ShapeDtypeStruct(w.shape, BF16) for w, _ in cast],
        compiler_params=pltpu.CompilerParams(
            dimension_semantics=("arbitrary" if cast else "parallel",), vmem_limit_bytes=VMEM_LIMIT),
        name="ffn",
    )(x, nw, w_gu, w_gu, w_down, fw, *[w for w, _ in cast])
    return out if cast else out[0]


C_DT = 3 * D_MODEL + D_SSM + D_XBC
C_GATE = C_DT + LANES


def _project(u, w_refs, c0, width):
    w_main, w_dt, w_gate = w_refs
    if c0 < C_DT:
        w = w_main[c0:c0 + width, :]
    elif c0 < C_GATE:
        w = w_dt[c0 - C_DT:c0 - C_DT + width, :]
    else:
        w = w_gate[c0 - C_GATE:c0 - C_GATE + width, :]
    return lax.dot_general(u, w, (((1,), (1,)), ((), ())), preferred_element_type=F32)


def _proj_weight_specs():
    once = pl.Buffered(1)
    return [pl.BlockSpec((C_DT, D_MODEL), lambda *_: (0, 0), pipeline_mode=once),
            pl.BlockSpec((LANES, D_MODEL), lambda *_: (C_DT // LANES, 0), pipeline_mode=once),
            pl.BlockSpec((pl.Element(2 * D_MODEL), pl.Element(D_MODEL)),
                         lambda *_: (C_DT + SSM_HEADS, 0), pipeline_mode=once)]


def _inproj_kernel(h_ref, nw_ref, wm_ref, wdt_ref, wg_ref,
                   ab_ref, ach_ref, zs_ref, xbc_ref, dt_ref, sga_ref, sgb_ref):
    u = _rms(h_ref[...], nw_ref[...]).astype(BF16)

    def proj(c0, width):
        return _project(u, (wm_ref, wdt_ref, wg_ref), c0, width)

    ab_ref[...] = proj(0, D_MODEL).astype(ab_ref.dtype)
    ach_ref[...] = (proj(D_MODEL, D_MODEL) * proj(2 * D_MODEL, D_MODEL)).astype(ach_ref.dtype)
    c0 = 3 * D_MODEL
    for k in range(D_SSM // D_MODEL):
        zs_ref[:, k * D_MODEL:(k + 1) * D_MODEL] = _silu(proj(c0 + k * D_MODEL, D_MODEL)).astype(zs_ref.dtype)
    c0 += D_SSM
    for k in range(D_XBC // D_MODEL):
        xbc_ref[:, k * D_MODEL:(k + 1) * D_MODEL] = proj(c0 + k * D_MODEL, D_MODEL).astype(xbc_ref.dtype)
    c0 += D_XBC
    dt_ref[...] = proj(c0, LANES)
    c0 += LANES
    sga_ref[...] = jax.nn.sigmoid(proj(c0, D_MODEL)).astype(sga_ref.dtype)
    sgb_ref[...] = jax.nn.sigmoid(proj(c0 + D_MODEL, D_MODEL)).astype(sgb_ref.dtype)


def _inproj(h, nw, w_packed, *, tm, act_dtype):
    t = h.shape[0]
    widths = (D_MODEL, D_MODEL, D_SSM, D_XBC, LANES, D_MODEL, D_MODEL)
    dtypes = (act_dtype, act_dtype, act_dtype, act_dtype, F32, act_dtype, act_dtype)
    return pl.pallas_call(
        _inproj_kernel,
        grid=(t // tm,),
        in_specs=[pl.BlockSpec((tm, D_MODEL), lambda i: (i, 0)), _resident((1, D_MODEL))]
        + _proj_weight_specs(),
        out_specs=[pl.BlockSpec((tm, w), lambda i: (i, 0)) for w in widths],
        out_shape=[jax.ShapeDtypeStruct((t, w), d) for w, d in zip(widths, dtypes)],
        compiler_params=pltpu.CompilerParams(
            dimension_semantics=("parallel",), vmem_limit_bytes=VMEM_LIMIT),
        name="inproj",
    )(h, nw, *w_packed)


COL_BLOCK = 256
N_PARK = 16


def _inproj_conv_kernel(h_ref, nw_ref, wm_ref, wdt_ref, wg_ref, caw_ref, cxw_ref, cxb_ref,
                        ca0_ref, cx0_ref,
                        ya_ref, zs_ref, xc_ref, dt_ref, sga_ref, sgb_ref, cao_ref, cxo_ref,
                        ca_carry, cx_carry, *stages, tm, seq_tiles, valid_len):
    i = pl.program_id(0)

    @pl.when(i % seq_tiles == 0)
    def _():
        ca_carry[...] = ca0_ref[...]
        cx_carry[...] = cx0_ref[...]

    u = _rms(h_ref[...], nw_ref[...]).astype(BF16)

    def proj(c0, width=COL_BLOCK):
        return _project(u, (wm_ref, wdt_ref, wg_ref), c0, width)

    n8 = tm // SUBLANES
    sublane = lax.broadcasted_iota(jnp.int32, (n8, SUBLANES, COL_BLOCK), 1)

    def causal_conv(x, carry_ref, w_ref, taps, cols):
        cur = x.reshape(n8, SUBLANES, COL_BLOCK)
        prev = jnp.concatenate([carry_ref[:, cols], x[:tm - SUBLANES]], axis=0).reshape(cur.shape)
        acc = w_ref[taps - 1:taps, cols] * x
        for d in range(1, taps):
            merged = jnp.where(sublane < SUBLANES - d, cur, prev)
            acc += w_ref[taps - 1 - d:taps - d, cols] * pltpu.roll(merged, d, axis=1).reshape(x.shape)
        carry_ref[:, cols] = x[valid_len - SUBLANES:valid_len]
        return acc

    c_z, c_x, c_dt = 3 * D_MODEL, 3 * D_MODEL + D_SSM, 3 * D_MODEL + D_SSM + D_XBC
    c_g = c_dt + LANES

    def tail_a(k, cols, prj):
        conv = causal_conv(prj[1] * prj[2], ca_carry, caw_ref, CONV_A_TAPS, cols)
        ya_ref[:, cols] = (prj[0] * conv).astype(ya_ref.dtype)

    def tail_z(k, cols, prj):
        zs_ref[:, cols] = _silu(prj[0]).astype(zs_ref.dtype)

    def tail_x(k, cols, prj):
        conv = causal_conv(prj[0], cx_carry, cxw_ref, CONV_X_TAPS, cols)
        xc_ref[:, cols] = _silu(conv + cxb_ref[:, cols]).astype(xc_ref.dtype)

    def tail_g(k, cols, prj):
        sga_ref[:, cols] = jax.nn.sigmoid(prj[0]).astype(sga_ref.dtype)
        sgb_ref[:, cols] = jax.nn.sigmoid(prj[1]).astype(sgb_ref.dtype)

    def blocks(tail, width, starts):
        return [(tail, k, [c + k * COL_BLOCK for c in starts]) for k in range(width // COL_BLOCK)]

    light = blocks(tail_z, D_SSM, [c_z])
    for n, (blk_a, blk_g) in enumerate(zip(blocks(tail_a, D_MODEL, [0, D_MODEL, 2 * D_MODEL]),
                                           blocks(tail_g, D_MODEL, [c_g, c_g + D_MODEL]))):
        light.insert(3 * n, blk_a)
        light.insert(3 * n + 2, blk_g)
    heavy = blocks(tail_x, D_XBC, [c_x])
    order = heavy + light
    park = jnp.minimum(i, 0)
    slot = 0
    for tail, k, starts in order:
        parked = []
        for c in starts:
            stages[slot % len(stages)][park] = proj(c)
            parked.append(stages[slot % len(stages)])
            slot += 1
        tail(k, slice(k * COL_BLOCK, (k + 1) * COL_BLOCK), [s[park] for s in parked])
    dt_ref[...] = proj(c_dt, LANES)
    cao_ref[0] = ca_carry[...]
    cxo_ref[0] = cx_carry[...]


def _inproj_conv(h, nw, w_packed, p, ca0, cx0, *, tm, seq_tiles, valid_len):
    t = h.shape[0]
    n_seq = t // (tm * seq_tiles)
    widths = (D_MODEL, D_SSM, D_XBC, LANES, D_MODEL, D_MODEL)
    dtypes = (BF16, BF16, BF16, F32, BF16, BF16)
    tail = lambda w: pl.BlockSpec((1, SUBLANES, w), lambda i: (i // seq_tiles, 0, 0))
    return pl.pallas_call(
        functools.partial(_inproj_conv_kernel, tm=tm, seq_tiles=seq_tiles, valid_len=valid_len),
        grid=(t // tm,),
        in_specs=[pl.BlockSpec((tm, D_MODEL), lambda i: (i, 0)), _resident((1, D_MODEL))]
        + _proj_weight_specs()
        + [_resident((CONV_A_TAPS, D_MODEL)), _resident((CONV_X_TAPS, D_XBC)),
           _resident((1, D_XBC)),
           _resident((SUBLANES, D_MODEL)), _resident((SUBLANES, D_XBC))],
        out_specs=[pl.BlockSpec((tm, w), lambda i: (i, 0)) for w in widths]
        + [tail(D_MODEL), tail(D_XBC)],
        out_shape=[jax.ShapeDtypeStruct((t, w), d) for w, d in zip(widths, dtypes)]
        + [jax.ShapeDtypeStruct((n_seq, SUBLANES, D_MODEL), F32),
           jax.ShapeDtypeStruct((n_seq, SUBLANES, D_XBC), F32)],
        scratch_shapes=[pltpu.VMEM((SUBLANES, D_MODEL), F32), pltpu.VMEM((SUBLANES, D_XBC), F32),
                        ] + [pltpu.VMEM((1, tm, COL_BLOCK), F32)] * N_PARK,
        compiler_params=pltpu.CompilerParams(
            dimension_semantics=("arbitrary",), vmem_limit_bytes=VMEM_LIMIT),
        name="inproj_conv",
    )(h, nw, *w_packed, p["conv_a_w"], p["ssm_conv_w"], p["ssm_conv_b"], ca0, cx0)


def _outproj_kernel(ya_ref, yb_ref, sga_ref, sgb_ref, h_ref, wa_ref, wb_ref, wo_ref, o_ref):
    y_a = _dot(ya_ref[...].astype(BF16), wa_ref[...])
    y_b = _dot(yb_ref[...].astype(BF16), wb_ref[...])
    merged = sga_ref[...].astype(F32) * y_a + sgb_ref[...].astype(F32) * y_b
    o_ref[...] = h_ref[...] + _dot(merged.astype(BF16), wo_ref[...])


def _outproj(ya, yb, sga, sgb, h, wa, wb, wo, *, tm):
    t = h.shape[0]
    row = lambda w: pl.BlockSpec((tm, w), lambda i: (i, 0))
    return pl.pallas_call(
        _outproj_kernel,
        grid=(t // tm,),
        in_specs=[row(D_MODEL), row(D_SSM), row(D_MODEL), row(D_MODEL), row(D_MODEL),
                  _resident((D_MODEL, D_MODEL)), _resident((D_SSM, D_MODEL)),
                  _resident((D_MODEL, D_MODEL))],
        out_specs=row(D_MODEL),
        out_shape=jax.ShapeDtypeStruct((t, D_MODEL), F32),
        compiler_params=pltpu.CompilerParams(
            dimension_semantics=("parallel",), vmem_limit_bytes=VMEM_LIMIT),
        name="outproj",
    )(ya, yb, sga, sgb, h, wa, wb, wo)


def _gated_group_norm(y, xs, zs, dskip, normw):
    y = (y + dskip * xs) * zs
    gw = D_SSM // SSM_GROUPS
    parts = []
    for g in range(SSM_GROUPS):
        yg = y[:, g * gw:(g + 1) * gw]
        parts.append(yg * lax.rsqrt(jnp.mean(yg * yg, axis=-1, keepdims=True) + EPS))
    return jnp.concatenate(parts, axis=-1) * normw


GROUP_W = HEADS_PER_GROUP * SSM_HEAD_DIM
LOG2E = 1.4426950408889634


def _ssd_masks():
    q = CHUNK
    rows = lax.broadcasted_iota(jnp.int32, (q, q), 0)
    cols = lax.broadcasted_iota(jnp.int32, (q, q), 1)
    causal = rows >= cols
    first_head = lax.broadcasted_iota(jnp.int32, (q, LANES), 1) < SSM_HEAD_DIM
    return causal, causal.astype(BF16), first_head


def _ssd_chunk(tok, r0, masks, zs_ref, xc_ref, dt_ref, dtb_ref, alog_ref, dskip_ref, normw_ref,
               expand_ref, hT, ybuf, valid_len, live=None, after_pair=None):
    q = CHUNK
    causal, tri, first_head = masks
    a_neg = -jnp.exp(alog_ref[...])
    dt = _softplus(dt_ref[tok, :] + dtb_ref[...])
    if valid_len is not None:
        t_idx = r0 + lax.broadcasted_iota(jnp.int32, (q, LANES), 0)
        dt = jnp.where(t_idx < valid_len, dt, 0.0)
    if live is not None:
        dt = jnp.where(live, dt, 0.0)
    hi, mid, lo = _split3(dt * a_neg)
    acs3 = _dot(tri, jnp.concatenate([hi, mid, lo], axis=1))
    acs = acs3[:, :LANES] + acs3[:, LANES:2 * LANES] + acs3[:, 2 * LANES:]
    acs2 = acs * LOG2E
    row_t = (acs2 - jnp.log2(dt)).T
    acs_end = acs2[q - 1:q, :]
    cd_hi, cd_lo = _split2(jnp.broadcast_to(jnp.exp2(acs_end), (SUBLANES, LANES)))
    stack = jnp.concatenate(
        [jnp.exp2(acs2).astype(BF16), (jnp.exp2(acs_end - acs2) * dt).astype(BF16),
         cd_hi, cd_lo], axis=0)

    for g in range(SSM_GROUPS):
        gcols = slice(g * GROUP_W, (g + 1) * GROUP_W)
        ex = _dot(stack, expand_ref[:, gcols])
        state_decay = ex[2 * q:2 * q + 1] + ex[2 * q + SUBLANES:2 * q + SUBLANES + 1]
        b_t = xc_ref[tok, D_SSM + g * SSM_STATE:D_SSM + (g + 1) * SSM_STATE].astype(F32).T.astype(BF16)
        c_bf = xc_ref[tok, D_SSM + D_BC + g * SSM_STATE:D_SSM + D_BC + (g + 1) * SSM_STATE]
        cb = _dot(c_bf, b_t)
        h_g = hT[g]
        y_g = ex[0:q] * _dot(c_bf, h_g.astype(BF16))
        xw_g = xc_ref[tok, gcols] * ex[q:2 * q].astype(BF16)
        hT[g] = h_g * state_decay + _dot(b_t, xw_g)
        for jp in range(HEADS_PER_GROUP // 2):
            pair = g * (HEADS_PER_GROUP // 2) + jp
            lanes = slice(pair * LANES, (pair + 1) * LANES)
            x_pair = xc_ref[tok, lanes]
            zero = jnp.zeros_like(x_pair)
            x2 = jnp.concatenate([jnp.where(first_head, x_pair, zero),
                                  jnp.where(first_head, zero, x_pair)], axis=0)
            m2 = []
            for e in range(2):
                hd = 2 * pair + e
                acs_col = jnp.broadcast_to(acs2[:, hd:hd + 1], (q, q))
                decay = jnp.exp2(jnp.where(causal, acs_col - row_t[hd:hd + 1, :], -jnp.inf))
                m2.append((cb * decay).astype(BF16))
            ybuf[:, lanes] = y_g[:, jp * LANES:(jp + 1) * LANES] + _dot(
                jnp.concatenate(m2, axis=1), x2)
            if after_pair is not None:
                after_pair()

    return _gated_group_norm(ybuf[...], xc_ref[tok, :D_SSM].astype(F32),
                             zs_ref[tok, :].astype(F32), dskip_ref[...], normw_ref[...])


def _write_final_state(hT, hTo_ref, ho_ref):
    hTo_ref[0] = hT[...]
    for g in range(SSM_GROUPS):
        ho_ref[0, g] = hT[g].T


def _mixer_kernel(zs_ref, xc_ref, dt_ref, dtb_ref, alog_ref, dskip_ref, normw_ref, expand_ref,
                  h0_ref, yb_ref, hTo_ref, ho_ref, hT, ybuf, *, tb, valid_len):
    i = pl.program_id(1)

    @pl.when(i == 0)
    def _():
        hT[...] = h0_ref[...]

    masks = _ssd_masks()

    def chunk_body(c, carry):
        r0 = pl.multiple_of(c * CHUNK, CHUNK)
        tok = pl.ds(r0, CHUNK)
        yn = _ssd_chunk(tok, r0, masks, zs_ref, xc_ref, dt_ref, dtb_ref, alog_ref, dskip_ref,
                        normw_ref, expand_ref, hT, ybuf, valid_len if valid_len < tb else None)
        yb_ref[tok, :] = yn.astype(yb_ref.dtype)
        return carry

    lax.fori_loop(0, tb // CHUNK, chunk_body, 0)

    @pl.when(i == pl.num_programs(1) - 1)
    def _():
        _write_final_state(hT, hTo_ref, ho_ref)


PROJ_BLOCK = 256


def _mixer_outproj_kernel(zs_ref, xc_ref, dt_ref, ya_ref, sga_ref, sgb_ref, h_ref,
                          dtb_ref, alog_ref, dskip_ref, normw_ref, expand_ref, h0_ref,
                          wa_ref, wb_ref, wo_ref,
                          o_ref, hTo_ref, ho_ref, hT, ybuf, yb_even, yb_odd, ya_scr, yb_out, mg_scr,
                          *, tb, nt):
    t = pl.program_id(0)
    n_tiles = pl.num_programs(0) - 1
    live = t < n_tiles

    @pl.when(t % nt == 0)
    def _():
        hT[...] = h0_ref[...]

    @pl.when(t == 0)
    def _():
        yb_odd[...] = jnp.zeros_like(yb_odd)

    masks = _ssd_masks()

    def stage(cur, prev):
        nb = D_MODEL // PROJ_BLOCK
        col = lambda k: slice(k * PROJ_BLOCK, (k + 1) * PROJ_BLOCK)

        def piece_a(k):
            ya_scr[:, col(k)] = _dot(ya_ref[...], wa_ref[:, col(k)])

        def piece_b(k):
            yb_out[:, col(k)] = _dot(prev[...], wb_ref[:, col(k)])

        def piece_merge():
            mg_scr[...] = (sga_ref[...].astype(F32) * ya_scr[...]
                           + sgb_ref[...].astype(F32) * yb_out[...]).astype(BF16)

        def piece_o(k):
            o_ref[:, col(k)] = h_ref[:, col(k)] + _dot(mg_scr[...], wo_ref[:, col(k)])

        pieces = ([functools.partial(piece_a, k) for k in range(nb)]
                  + [functools.partial(piece_b, k) for k in range(nb)] + [piece_merge]
                  + [functools.partial(piece_o, k) for k in range(nb)])

        n_pieces, n_slots = len(pieces), tb // CHUNK * SSM_HEADS // 2
        slots_seen = [0]

        def next_piece():
            slots_seen[0] += 1
            while pieces and (n_pieces - len(pieces)) * n_slots < slots_seen[0] * n_pieces:
                pieces.pop(0)()

        for c in range(tb // CHUNK):
            r0 = c * CHUNK
            tok = slice(r0, r0 + CHUNK)
            yn = _ssd_chunk(tok, r0, masks, zs_ref, xc_ref, dt_ref, dtb_ref, alog_ref, dskip_ref,
                            normw_ref, expand_ref, hT, ybuf, None, live, next_piece)
            cur[tok, :] = yn.astype(BF16)
        while pieces:
            next_piece()

    @pl.when(t % 2 == 0)
    def _():
        stage(yb_even, yb_odd)

    @pl.when(t % 2 == 1)
    def _():
        stage(yb_odd, yb_even)

    @pl.when(jnp.logical_and(live, t % nt == nt - 1))
    def _():
        _write_final_state(hT, hTo_ref, ho_ref)


def _mixer_outproj(zs, xc, dt, ya, sga, sgb, h, p, expand, h0, wa, wb, wo, *, batch, seqlen, tb):
    nt = seqlen // tb
    n_tiles = batch * nt
    cur = lambda w: pl.BlockSpec((tb, w), lambda t: (jnp.minimum(t, n_tiles - 1), 0))
    lag = lambda w: pl.BlockSpec((tb, w), lambda t: (jnp.maximum(t - 1, 0), 0))
    full = lambda shape: pl.BlockSpec(shape, lambda t: (0,) * len(shape))
    once = lambda shape: pl.BlockSpec(shape, lambda t: (0,) * len(shape), pipeline_mode=pl.Buffered(1))
    per_b = lambda shape: pl.BlockSpec(
        (1,) + shape, lambda t: (jnp.minimum(t // nt, batch - 1),) + (0,) * len(shape))
    return pl.pallas_call(
        functools.partial(_mixer_outproj_kernel, tb=tb, nt=nt),
        grid=(n_tiles + 1,),
        in_specs=[cur(D_SSM), cur(D_XBC), cur(LANES), lag(D_MODEL), lag(D_MODEL), lag(D_MODEL),
                  lag(D_MODEL),
                  full((1, LANES)), full((1, LANES)), full((1, D_SSM)), full((1, D_SSM)),
                  once((LANES, D_SSM)), full((SSM_GROUPS, SSM_STATE, GROUP_W)),
                  once((D_MODEL, D_MODEL)), once((D_SSM, D_MODEL)), once((D_MODEL, D_MODEL))],
        out_specs=[lag(D_MODEL), per_b((SSM_GROUPS, SSM_STATE, GROUP_W)),
                   per_b((SSM_GROUPS, GROUP_W, SSM_STATE))],
        out_shape=[jax.ShapeDtypeStruct((batch * seqlen, D_MODEL), F32),
                   jax.ShapeDtypeStruct((batch, SSM_GROUPS, SSM_STATE, GROUP_W), F32),
                   jax.ShapeDtypeStruct((batch, SSM_GROUPS, GROUP_W, SSM_STATE), F32)],
        scratch_shapes=[pltpu.VMEM((SSM_GROUPS, SSM_STATE, GROUP_W), F32),
                        pltpu.VMEM((CHUNK, D_SSM), F32),
                        pltpu.VMEM((tb, D_SSM), BF16),
                        pltpu.VMEM((tb, D_SSM), BF16),
                        pltpu.VMEM((tb, D_MODEL), F32), pltpu.VMEM((tb, D_MODEL), F32),
                        pltpu.VMEM((tb, D_MODEL), BF16)],
        compiler_params=pltpu.CompilerParams(
            dimension_semantics=("arbitrary",), vmem_limit_bytes=VMEM_LIMIT),
        name="mixer_outproj",
    )(zs, xc, dt, ya, sga, sgb, h, p["dt_bias"], p["a_log"], p["d_skip"], p["ssm_norm_w"], expand,
      h0, wa, wb, wo)


def _mixer(zs, xc, dt, p, expand, h0, *, batch, seqlen, tb, valid_len):
    nt = seqlen // tb
    row = lambda w: pl.BlockSpec((tb, w), lambda b, i: (b * nt + i, 0))
    full = lambda shape: pl.BlockSpec(shape, lambda b, i: (0,) * len(shape))
    per_b = lambda shape: pl.BlockSpec((1,) + shape, lambda b, i: (b,) + (0,) * len(shape))
    return pl.pallas_call(
        functools.partial(_mixer_kernel, tb=tb, valid_len=valid_len),
        grid=(batch, nt),
        in_specs=[row(D_SSM), row(D_XBC), row(LANES),
                  full((1, LANES)), full((1, LANES)), full((1, D_SSM)), full((1, D_SSM)),
                  full((LANES, D_SSM)), full((SSM_GROUPS, SSM_STATE, GROUP_W))],
        out_specs=[row(D_SSM), per_b((SSM_GROUPS, SSM_STATE, GROUP_W)),
                   per_b((SSM_GROUPS, GROUP_W, SSM_STATE))],
        out_shape=[jax.ShapeDtypeStruct((batch * seqlen, D_SSM), BF16),
                   jax.ShapeDtypeStruct((batch, SSM_GROUPS, SSM_STATE, GROUP_W), F32),
                   jax.ShapeDtypeStruct((batch, SSM_GROUPS, GROUP_W, SSM_STATE), F32)],
        scratch_shapes=[pltpu.VMEM((SSM_GROUPS, SSM_STATE, GROUP_W), F32),
                        pltpu.VMEM((CHUNK, D_SSM), F32)],
        compiler_params=pltpu.CompilerParams(
            dimension_semantics=("arbitrary", "arbitrary"), vmem_limit_bytes=VMEM_LIMIT),
        name="mixer",
    )(zs, xc, dt, p["dt_bias"], p["a_log"], p["d_skip"], p["ssm_norm_w"], expand, h0)


SEQ_BLOCK = SUBLANES


def _decode_mixer_kernel(ab_ref, ach_ref, zs_ref, xbc_ref, dt_ref, sa_ref, sx_ref, h0_ref,
                         caw_ref, cxw_ref, cxb_ref, dtb_ref, alog_ref, dskip_ref, normw_ref,
                         gsum_ref, expand_ref,
                         ya_ref, yb_ref, sao_ref, sxo_ref, ho_ref,
                         xc_buf, bc_buf, os_buf, xw_buf, cd_buf, y_buf, *, n_tok):
    sb = SEQ_BLOCK

    xp = [sa_ref[k] for k in range(CONV_A_TAPS - 1)] + [ach_ref[t] for t in range(n_tok)]
    for t in range(n_tok):
        conv = caw_ref[0:1, :] * xp[t]
        for k in range(1, CONV_A_TAPS):
            conv += caw_ref[k:k + 1, :] * xp[t + k]
        ya_ref[t] = ab_ref[t] * conv
    for k in range(CONV_A_TAPS - 1):
        sao_ref[k] = xp[n_tok + k]

    xp = [sx_ref[k] for k in range(CONV_X_TAPS - 1)] + [xbc_ref[t] for t in range(n_tok)]
    for t in range(n_tok):
        conv = cxw_ref[0:1, :] * xp[t]
        for k in range(1, CONV_X_TAPS):
            conv += cxw_ref[k:k + 1, :] * xp[t + k]
        xc_buf[t * sb:(t + 1) * sb, :] = _silu(conv + cxb_ref[...])
    for k in range(CONV_X_TAPS - 1):
        sxo_ref[k] = xp[n_tok + k]

    a_neg = -jnp.exp(alog_ref[...])
    dts, acs = [], []
    for t in range(n_tok):
        dts.append(_softplus(dt_ref[t] + dtb_ref[...]))
        acs.append(dts[t] * a_neg + (acs[t - 1] if t else 0.0))

    pairs = [(qi, si) for qi in range(n_tok) for si in range(qi + 1)]
    prods = [xc_buf[qi * sb:(qi + 1) * sb, D_SSM + D_BC:] * xc_buf[si * sb:(si + 1) * sb, D_SSM:D_SSM + D_BC]
             for qi, si in pairs]
    p_hi, p_lo = _split2(jnp.concatenate(prods, axis=0))
    cbh = _dot(p_hi, gsum_ref[...]) + _dot(p_lo, gsum_ref[...])
    coef = [jnp.exp(acs[t]) for t in range(n_tok)]
    coef += [cbh[k * sb:(k + 1) * sb] * jnp.exp(acs[qi] - acs[si]) * dts[si]
             for k, (qi, si) in enumerate(pairs)]
    coef += [jnp.exp(acs[n_tok - 1] - acs[s]) * dts[s] for s in range(n_tok)]
    coef.append(jnp.exp(acs[n_tok - 1]))
    c_hi, c_lo = _split2(jnp.concatenate(coef, axis=0))
    coef_x = _dot(c_hi, expand_ref[...]) + _dot(c_lo, expand_ref[...])
    slab = lambda k: coef_x[k * sb:(k + 1) * sb]

    def store_tiles(buf, r0, val):
        for lt in range(val.shape[1] // LANES):
            buf[lt, r0:r0 + val.shape[0], :] = val[:, lt * LANES:(lt + 1) * LANES]

    def load_rows(buf, lt0, n_lt, rows):
        return jnp.concatenate([buf[lt0 + j, rows, :] for j in range(n_lt)], axis=1)

    store_tiles(bc_buf, 0, xc_buf[:, D_SSM:])
    store_tiles(os_buf, 0, coef_x[0:n_tok * sb])
    diag = {pr: slab(n_tok + k) for k, pr in enumerate(pairs)}
    k0 = n_tok + len(pairs)
    for s in range(n_tok):
        store_tiles(xw_buf, s * sb, slab(k0 + s) * xc_buf[s * sb:(s + 1) * sb, :D_SSM])
    cd_hi, cd_lo = _split2(slab(k0 + n_tok))
    store_tiles(cd_buf, 0, cd_hi.astype(F32))
    store_tiles(cd_buf, sb, cd_lo.astype(F32))

    for t in range(n_tok):
        acc = diag[(t, 0)] * xc_buf[0:sb, :D_SSM]
        for s in range(1, t + 1):
            acc += diag[(t, s)] * xc_buf[s * sb:(s + 1) * sb, :D_SSM]
        store_tiles(y_buf, t * sb, acc)

    gt = GROUP_W // LANES
    k_rows = BF16_ROWS
    krow = lax.broadcasted_iota(jnp.int32, (k_rows, SSM_STATE), 0)
    ones_rows = jnp.where((krow >= n_tok) & (krow < n_tok + 2), 1.0, 0.0).astype(BF16)
    pad_lhs = jnp.zeros((k_rows - n_tok - 2, GROUP_W), F32)
    pad_rhs = jnp.zeros((k_rows - n_tok, SSM_STATE), F32)

    def seq_body(b, carry):
        tok_rows = pl.ds(b, n_tok, stride=sb)
        for g in range(SSM_GROUPS):
            b_bg = bc_buf[g, tok_rows, :]
            c_bg = bc_buf[SSM_GROUPS + g, tok_rows, :]
            heads = pl.ds(g * HEADS_PER_GROUP, HEADS_PER_GROUP)
            h_bg = h0_ref[b, heads].reshape(GROUP_W, SSM_STATE)
            y_off = lax.dot_general(c_bg.astype(BF16), h_bg.astype(BF16),
                                    (((1,), (1,)), ((), ())), preferred_element_type=F32)
            y_off = y_off * load_rows(os_buf, g * gt, gt, tok_rows)
            for j in range(gt):
                y_buf[g * gt + j, tok_rows, :] = (y_buf[g * gt + j, tok_rows, :]
                                                  + y_off[:, j * LANES:(j + 1) * LANES])
            lhs = jnp.concatenate(
                [load_rows(xw_buf, g * gt, gt, tok_rows),
                 load_rows(cd_buf, g * gt, gt, pl.ds(b, 2, stride=sb)), pad_lhs],
                axis=0).astype(BF16)
            rhs = jnp.concatenate(
                [jnp.concatenate([b_bg, pad_rhs], axis=0).astype(BF16), ones_rows], axis=1)
            upd = lax.dot_general(lhs, rhs, (((0,), (0,)), ((), ())), preferred_element_type=F32)
            new = h_bg * upd[:, SSM_STATE:] + upd[:, :SSM_STATE]
            ho_ref[b, heads] = new.reshape(HEADS_PER_GROUP, SSM_HEAD_DIM, SSM_STATE)
        return carry

    lax.fori_loop(0, sb, seq_body, 0, unroll=4)

    xs_all = xc_buf[:, :D_SSM]
    zs_all = jnp.concatenate([zs_ref[t] for t in range(n_tok)], axis=0)
    y_all = load_rows(y_buf, 0, D_SSM // LANES, slice(None))
    yn = _gated_group_norm(y_all, xs_all, zs_all, dskip_ref[...], normw_ref[...])
    for t in range(n_tok):
        yb_ref[t] = yn[t * sb:(t + 1) * sb]


def _decode_mixer(ab, ach, zs, xbc, dt, sa, sx, h0, p, gsum, expand, *, n_tok, n_seq):
    sb = SEQ_BLOCK
    tok = lambda w: pl.BlockSpec((n_tok, sb, w), lambda i: (0, i, 0))
    st = lambda k, w: pl.BlockSpec((k, sb, w), lambda i: (0, i, 0))
    hblk = pl.BlockSpec((sb, SSM_HEADS, SSM_HEAD_DIM, SSM_STATE), lambda i: (i, 0, 0, 0))
    full = lambda shape: pl.BlockSpec(shape, lambda i: (0,) * len(shape))
    return pl.pallas_call(
        functools.partial(_decode_mixer_kernel, n_tok=n_tok),
        grid=(n_seq // sb,),
        in_specs=[tok(D_MODEL), tok(D_MODEL), tok(D_SSM), tok(D_XBC), tok(LANES),
                  st(CONV_A_TAPS - 1, D_MODEL), st(CONV_X_TAPS - 1, D_XBC), hblk,
                  full((CONV_A_TAPS, D_MODEL)), full((CONV_X_TAPS, D_XBC)), full((1, D_XBC)),
                  full((1, LANES)), full((1, LANES)), full((1, D_SSM)), full((1, D_SSM)),
                  full((D_BC, LANES)), full((LANES, D_SSM))],
        out_specs=[tok(D_MODEL), tok(D_SSM), st(CONV_A_TAPS - 1, D_MODEL),
                   st(CONV_X_TAPS - 1, D_XBC), hblk],
        out_shape=[jax.ShapeDtypeStruct((n_tok, n_seq, D_MODEL), F32),
                   jax.ShapeDtypeStruct((n_tok, n_seq, D_SSM), F32),
                   jax.ShapeDtypeStruct((CONV_A_TAPS - 1, n_seq, D_MODEL), F32),
                   jax.ShapeDtypeStruct((CONV_X_TAPS - 1, n_seq, D_XBC), F32),
                   jax.ShapeDtypeStruct((n_seq, SSM_HEADS, SSM_HEAD_DIM, SSM_STATE), F32)],
        scratch_shapes=[pltpu.VMEM((n_tok * sb, D_XBC), F32),
                        pltpu.VMEM((2 * D_BC // LANES, n_tok * sb, LANES), F32),
                        pltpu.VMEM((D_SSM // LANES, n_tok * sb, LANES), F32),
                        pltpu.VMEM((D_SSM // LANES, n_tok * sb, LANES), F32),
                        pltpu.VMEM((D_SSM // LANES, 2 * sb, LANES), F32),
                        pltpu.VMEM((D_SSM // LANES, n_tok * sb, LANES), F32)],
        compiler_params=pltpu.CompilerParams(
            dimension_semantics=("parallel",), vmem_limit_bytes=VMEM_LIMIT),
        name="decode_mixer",
    )(ab, ach, zs, xbc, dt, sa, sx, h0, p["conv_a_w"], p["ssm_conv_w"], p["ssm_conv_b"],
      p["dt_bias"], p["a_log"], p["d_skip"], p["ssm_norm_w"], gsum, expand)


def _row_tile(t, cap):
    if t <= cap:
        return t
    best = LANES
    for cand in range(LANES, cap + 1, LANES):
        if t % cand == 0:
            best = cand
    return best


def kernel(x_prompt, x_sample, state_conv_a, state_ssm_conv, state_ssm, meta_tokens, norm_ffn1, ffn1_w_gu, ffn1_w_down, norm_mix, w_in, conv_a_w, w_a_out, ssm_conv_w, ssm_conv_b, dt_bias, a_log, d_skip, ssm_norm_w, w_b_out, w_o, norm_ffn2, ffn2_w_gu, ffn2_w_down, norm_final):
    bsz, seqlen, _ = x_prompt.shape
    n_seq, n_tok, _ = x_sample.shape
    n_dt = SSM_HEADS
    c_dt = 3 * D_MODEL + D_SSM + D_XBC

    w_gu1, w_dn1 = ffn1_w_gu[0].astype(BF16), ffn1_w_down[0].astype(BF16)
    pad_heads = lambda v: jnp.pad(v[0], (0, LANES - n_dt))[None]
    p = {
        "conv_a_w": conv_a_w[0], "ssm_conv_w": ssm_conv_w[0], "ssm_conv_b": ssm_conv_b[0][None],
        "dt_bias": pad_heads(dt_bias), "a_log": pad_heads(a_log),
        "d_skip": jnp.repeat(d_skip[0], SSM_HEAD_DIM)[None], "ssm_norm_w": ssm_norm_w[0][None],
    }
    n1, nm, n2, nf = norm_ffn1[0][None], norm_mix[0][None], norm_ffn2[0][None], norm_final[None]
    eh = lax.broadcasted_iota(jnp.int32, (LANES, D_SSM), 0)
    ec = lax.broadcasted_iota(jnp.int32, (LANES, D_SSM), 1)
    expand = (ec // SSM_HEAD_DIM == eh).astype(BF16)

    def ffn1(x, cap=512, cast=()):
        return _ffn(x, n1, w_gu1, w_dn1, nf, tm=_row_tile(x.shape[0], cap), final_norm=False, cast=cast)

    n_steps = bsz * seqlen // _row_tile(bsz * seqlen, 512)
    blk = lambda w: (w, BF16_ROWS * pl.cdiv(w.shape[0], n_steps * BF16_ROWS))
    h_p, w_in_t, w_gu2, w_dn2, wa, wb, wo = ffn1(
        x_prompt.reshape(bsz * seqlen, D_MODEL),
        cast=[blk(jnp.swapaxes(w_in[0], 0, 1)), blk(ffn2_w_gu[0]), blk(ffn2_w_down[0]),
              blk(w_a_out[0]), blk(w_b_out[0]), blk(w_o[0])])
    w_proj = (w_in_t,) * 3

    def ffn2(h):
        return _ffn(h, n2, w_gu2, w_dn2, nf, tm=_row_tile(h.shape[0], 512), final_norm=True)

    zeros = lambda *s: jnp.zeros(s, F32)
    x_s = jnp.swapaxes(x_sample, 0, 1).reshape(n_tok * n_seq, D_MODEL)
    h_sm = ffn1(jnp.concatenate([x_s, jnp.pad(meta_tokens, ((0, CHUNK - N_META), (0, 0)))], axis=0),
                cap=n_tok * n_seq + CHUNK)
    h_s, h_m = h_sm[:n_tok * n_seq], h_sm[n_tok * n_seq:]
    _, zs, xc, dt, _, _, ca_m, cx_m = _inproj_conv(
        h_m, nm, w_proj, p, zeros(SUBLANES, D_MODEL), zeros(SUBLANES, D_XBC),
        tm=CHUNK, seq_tiles=1, valid_len=N_META)
    _, hT_m, _ = _mixer(zs, xc, dt, p, expand, zeros(SSM_GROUPS, SSM_STATE, GROUP_W),
                        batch=1, seqlen=CHUNK, tb=CHUNK, valid_len=N_META)

    tm = _row_tile(seqlen, 512)
    ya, zs, xc, dt, sga, sgb, ca_p, cx_p = _inproj_conv(
        h_p, nm, w_proj, p, ca_m[0], cx_m[0], tm=tm, seq_tiles=seqlen // tm, valid_len=tm)
    tb = _row_tile(seqlen, 512)
    h2_p, _, h_out_p = _mixer_outproj(zs, xc, dt, ya, sga, sgb, h_p, p, expand, hT_m[0], wa, wb, wo,
                                      batch=bsz, seqlen=seqlen, tb=tb)
    y_prompt = ffn2(h2_p).reshape(bsz, seqlen, D_MODEL)
    prompt_conv_a = ca_p[None, :, SUBLANES - (CONV_A_TAPS - 1):, :]
    prompt_ssm_conv = cx_p[None, :, SUBLANES - (CONV_X_TAPS - 1):, :]
    prompt_ssm = h_out_p.reshape(1, bsz, SSM_HEADS, SSM_HEAD_DIM, SSM_STATE)

    ab, ach, zs, xbc, dt, sga, sgb = _inproj(h_s, nm, w_proj, tm=_row_tile(n_tok * n_seq, 256), act_dtype=F32)
    tsf = lambda a: a.reshape(n_tok, n_seq, a.shape[-1])
    hh = lax.broadcasted_iota(jnp.int32, (D_BC, LANES), 1)
    kk = lax.broadcasted_iota(jnp.int32, (D_BC, LANES), 0)
    gsum = ((hh < SSM_HEADS) & (kk // SSM_STATE == hh // HEADS_PER_GROUP)).astype(BF16)
    ya, yb, sa_new, sx_new, sample_ssm = _decode_mixer(
        tsf(ab), tsf(ach), tsf(zs), tsf(xbc), tsf(dt),
        jnp.swapaxes(state_conv_a[0], 0, 1), jnp.swapaxes(state_ssm_conv[0], 0, 1), state_ssm[0],
        p, gsum, expand, n_tok=n_tok, n_seq=n_seq)
    flat = lambda a: a.reshape(n_tok * n_seq, a.shape[-1])
    y_s = ffn2(_outproj(flat(ya), flat(yb), sga, sgb, h_s, wa, wb, wo, tm=_row_tile(n_tok * n_seq, 512)))
    y_sample = jnp.swapaxes(y_s.reshape(n_tok, n_seq, D_MODEL), 0, 1)
    sample_conv_a = jnp.swapaxes(sa_new, 0, 1)[None]
    sample_ssm_conv = jnp.swapaxes(sx_new, 0, 1)[None]

    return (y_prompt, y_sample, prompt_conv_a, prompt_ssm_conv, prompt_ssm,
            sample_conv_a, sample_ssm_conv, sample_ssm[None])
```

```python
import functools

import jax
import jax.numpy as jnp
from jax import lax
from jax.experimental import pallas as pl
from jax.experimental.pallas import tpu as pltpu

D_MODEL = 1024
D_FF = 2816
D_SSM = 2048
SSM_HEADS = 32
SSM_HEAD_DIM = 64
SSM_GROUPS = 4
HEADS_PER_GROUP = SSM_HEADS // SSM_GROUPS
SSM_STATE = 128
D_BC = SSM_GROUPS * SSM_STATE
D_XBC = D_SSM + 2 * D_BC
CONV_A_TAPS = 3
CONV_X_TAPS = 4
N_META = 16
EPS = 1e-6

LANES = 128
SUBLANES = 8
BF16_ROWS = 2 * SUBLANES
CHUNK = 128
VMEM_LIMIT = 56 * 1024 * 1024

F32 = jnp.float32
BF16 = jnp.bfloat16


def _dot(a, b):
    return jnp.dot(a, b, preferred_element_type=F32)


def _rms(x, w):
    return x * lax.rsqrt(jnp.mean(x * x, axis=-1, keepdims=True) + EPS) * w


def _silu(x):
    return x * jax.nn.sigmoid(x)


def _softplus(x):
    return jnp.maximum(x, 0.0) + jnp.log1p(jnp.exp(-jnp.abs(x)))


def _split3(x):
    hi = x.astype(BF16)
    r = x - hi.astype(F32)
    mid = r.astype(BF16)
    lo = (r - mid.astype(F32)).astype(BF16)
    return hi, mid, lo


def _split2(x):
    hi = x.astype(BF16)
    lo = (x - hi.astype(F32)).astype(BF16)
    return hi, lo


def _resident(shape):
    return pl.BlockSpec(shape, lambda *_: (0,) * len(shape), pipeline_mode=pl.Buffered(1))


def _ffn_kernel(x_ref, nw_ref, wg_ref, wu_ref, wd_ref, fw_ref, *rest, final_norm, cast_blocks):
    n_cast = len(cast_blocks)
    cast_in, o_ref, cast_out = rest[:n_cast], rest[n_cast], rest[n_cast + 1:]
    x = x_ref[...]
    xn = _rms(x, nw_ref[...]).astype(BF16)
    act = (_silu(_dot(xn, wg_ref[...])) * _dot(xn, wu_ref[...])).astype(BF16)
    for src, dst in zip(cast_in, cast_out):
        dst[...] = src[...].astype(BF16)
    h = x + 0.5 * _dot(act, wd_ref[...])
    if final_norm:
        h = _rms(h, fw_ref[...])
    o_ref[...] = h


def _ffn(x, nw, w_gu, w_down, fw, *, tm, final_norm, cast=()):
    t = x.shape[0]
    once = pl.Buffered(1)
    n_blocks = [pl.cdiv(w.shape[0], rows) for w, rows in cast]
    assert all(nb <= t // tm for nb in n_blocks)
    cast_specs = [pl.BlockSpec((rows, w.shape[1]), functools.partial(
        lambda i, last: (jnp.minimum(i, last), 0), last=nb - 1))
        for (w, rows), nb in zip(cast, n_blocks)]
    out = pl.pallas_call(
        functools.partial(_ffn_kernel, final_norm=final_norm, cast_blocks=tuple(n_blocks)),
        grid=(t // tm,),
        in_specs=[
            pl.BlockSpec((tm, D_MODEL), lambda i: (i, 0)),
            _resident((1, D_MODEL)),
            pl.BlockSpec((D_MODEL, D_FF), lambda i: (0, 0), pipeline_mode=once),
            pl.BlockSpec((D_MODEL, D_FF), lambda i: (0, 1), pipeline_mode=once),
            _resident((D_FF, D_MODEL)),
            _resident((1, D_MODEL)),
        ] + cast_specs,
        out_specs=[pl.BlockSpec((tm, D_MODEL), lambda i: (i, 0))] + cast_specs,
        out_shape=[jax.ShapeDtypeStruct((t, D_MODEL), F32)]
        + [jax.ShapeDtypeStruct(w.shape, BF16) for w, _ in cast],
        compiler_params=pltpu.CompilerParams(
            dimension_semantics=("arbitrary" if cast else "parallel",), vmem_limit_bytes=VMEM_LIMIT),
        name="ffn",
    )(x, nw, w_gu, w_gu, w_down, fw, *[w for w, _ in cast])
    return out if cast else out[0]


C_DT = 3 * D_MODEL + D_SSM + D_XBC
C_GATE = C_DT + LANES


def _project(u, w_refs, c0, width):
    w_main, w_dt, w_gate = w_refs
    if c0 < C_DT:
        w = w_main[c0:c0 + width, :]
    elif c0 < C_GATE:
        w = w_dt[c0 - C_DT:c0 - C_DT + width, :]
    else:
        w = w_gate[c0 - C_GATE:c0 - C_GATE + width, :]
    return lax.dot_general(u, w, (((1,), (1,)), ((), ())), preferred_element_type=F32)


def _proj_weight_specs():
    once = pl.Buffered(1)
    return [pl.BlockSpec((C_DT, D_MODEL), lambda *_: (0, 0), pipeline_mode=once),
            pl.BlockSpec((LANES, D_MODEL), lambda *_: (C_DT // LANES, 0), pipeline_mode=once),
            pl.BlockSpec((pl.Element(2 * D_MODEL), pl.Element(D_MODEL)),
                         lambda *_: (C_DT + SSM_HEADS, 0), pipeline_mode=once)]


def _inproj_kernel(h_ref, nw_ref, wm_ref, wdt_ref, wg_ref,
                   ab_ref, ach_ref, zs_ref, xbc_ref, dt_ref, sga_ref, sgb_ref):
    u = _rms(h_ref[...], nw_ref[...]).astype(BF16)

    def proj(c0, width):
        return _project(u, (wm_ref, wdt_ref, wg_ref), c0, width)

    ab_ref[...] = proj(0, D_MODEL).astype(ab_ref.dtype)
    ach_ref[...] = (proj(D_MODEL, D_MODEL) * proj(2 * D_MODEL, D_MODEL)).astype(ach_ref.dtype)
    c0 = 3 * D_MODEL
    for k in range(D_SSM // D_MODEL):
        zs_ref[:, k * D_MODEL:(k + 1) * D_MODEL] = _silu(proj(c0 + k * D_MODEL, D_MODEL)).astype(zs_ref.dtype)
    c0 += D_SSM
    for k in range(D_XBC // D_MODEL):
        xbc_ref[:, k * D_MODEL:(k + 1) * D_MODEL] = proj(c0 + k * D_MODEL, D_MODEL).astype(xbc_ref.dtype)
    c0 += D_XBC
    dt_ref[...] = proj(c0, LANES)
    c0 += LANES
    sga_ref[...] = jax.nn.sigmoid(proj(c0, D_MODEL)).astype(sga_ref.dtype)
    sgb_ref[...] = jax.nn.sigmoid(proj(c0 + D_MODEL, D_MODEL)).astype(sgb_ref.dtype)


def _inproj(h, nw, w_packed, *, tm, act_dtype):
    t = h.shape[0]
    widths = (D_MODEL, D_MODEL, D_SSM, D_XBC, LANES, D_MODEL, D_MODEL)
    dtypes = (act_dtype, act_dtype, act_dtype, act_dtype, F32, act_dtype, act_dtype)
    return pl.pallas_call(
        _inproj_kernel,
        grid=(t // tm,),
        in_specs=[pl.BlockSpec((tm, D_MODEL), lambda i: (i, 0)), _resident((1, D_MODEL))]
        + _proj_weight_specs(),
        out_specs=[pl.BlockSpec((tm, w), lambda i: (i, 0)) for w in widths],
        out_shape=[jax.ShapeDtypeStruct((t, w), d) for w, d in zip(widths, dtypes)],
        compiler_params=pltpu.CompilerParams(
            dimension_semantics=("parallel",), vmem_limit_bytes=VMEM_LIMIT),
        name="inproj",
    )(h, nw, *w_packed)


COL_BLOCK = 256
N_PARK = 16


def _inproj_conv_kernel(h_ref, nw_ref, wm_ref, wdt_ref, wg_ref, caw_ref, cxw_ref, cxb_ref,
                        ca0_ref, cx0_ref,
                        ya_ref, zs_ref, xc_ref, dt_ref, sga_ref, sgb_ref, cao_ref, cxo_ref,
                        ca_carry, cx_carry, *stages, tm, seq_tiles, valid_len):
    i = pl.program_id(0)

    @pl.when(i % seq_tiles == 0)
    def _():
        ca_carry[...] = ca0_ref[...]
        cx_carry[...] = cx0_ref[...]

    u = _rms(h_ref[...], nw_ref[...]).astype(BF16)

    def proj(c0, width=COL_BLOCK):
        return _project(u, (wm_ref, wdt_ref, wg_ref), c0, width)

    n8 = tm // SUBLANES
    sublane = lax.broadcasted_iota(jnp.int32, (n8, SUBLANES, COL_BLOCK), 1)

    def causal_conv(x, carry_ref, w_ref, taps, cols):
        cur = x.reshape(n8, SUBLANES, COL_BLOCK)
        prev = jnp.concatenate([carry_ref[:, cols], x[:tm - SUBLANES]], axis=0).reshape(cur.shape)
        acc = w_ref[taps - 1:taps, cols] * x
        for d in range(1, taps):
            merged = jnp.where(sublane < SUBLANES - d, cur, prev)
            acc += w_ref[taps - 1 - d:taps - d, cols] * pltpu.roll(merged, d, axis=1).reshape(x.shape)
        carry_ref[:, cols] = x[valid_len - SUBLANES:valid_len]
        return acc

    c_z, c_x, c_dt = 3 * D_MODEL, 3 * D_MODEL + D_SSM, 3 * D_MODEL + D_SSM + D_XBC
    c_g = c_dt + LANES

    def tail_a(k, cols, prj):
        conv = causal_conv(prj[1] * prj[2], ca_carry, caw_ref, CONV_A_TAPS, cols)
        ya_ref[:, cols] = (prj[0] * conv).astype(ya_ref.dtype)

    def tail_z(k, cols, prj):
        zs_ref[:, cols] = _silu(prj[0]).astype(zs_ref.dtype)

    def tail_x(k, cols, prj):
        conv = causal_conv(prj[0], cx_carry, cxw_ref, CONV_X_TAPS, cols)
        xc_ref[:, cols] = _silu(conv + cxb_ref[:, cols]).astype(xc_ref.dtype)

    def tail_g(k, cols, prj):
        sga_ref[:, cols] = jax.nn.sigmoid(prj[0]).astype(sga_ref.dtype)
        sgb_ref[:, cols] = jax.nn.sigmoid(prj[1]).astype(sgb_ref.dtype)

    def blocks(tail, width, starts):
        return [(tail, k, [c + k * COL_BLOCK for c in starts]) for k in range(width // COL_BLOCK)]

    light = blocks(tail_z, D_SSM, [c_z])
    for n, (blk_a, blk_g) in enumerate(zip(blocks(tail_a, D_MODEL, [0, D_MODEL, 2 * D_MODEL]),
                                           blocks(tail_g, D_MODEL, [c_g, c_g + D_MODEL]))):
        light.insert(3 * n, blk_a)
        light.insert(3 * n + 2, blk_g)
    heavy = blocks(tail_x, D_XBC, [c_x])
    order = heavy + light
    park = jnp.minimum(i, 0)
    slot = 0
    for tail, k, starts in order:
        parked = []
        for c in starts:
            stages[slot % len(stages)][park] = proj(c)
            parked.append(stages[slot % len(stages)])
            slot += 1
        tail(k, slice(k * COL_BLOCK, (k + 1) * COL_BLOCK), [s[park] for s in parked])
    dt_ref[...] = proj(c_dt, LANES)
    cao_ref[0] = ca_carry[...]
    cxo_ref[0] = cx_carry[...]


def _inproj_conv(h, nw, w_packed, p, ca0, cx0, *, tm, seq_tiles, valid_len):
    t = h.shape[0]
    n_seq = t // (tm * seq_tiles)
    widths = (D_MODEL, D_SSM, D_XBC, LANES, D_MODEL, D_MODEL)
    dtypes = (BF16, BF16, BF16, F32, BF16, BF16)
    tail = lambda w: pl.BlockSpec((1, SUBLANES, w), lambda i: (i // seq_tiles, 0, 0))
    return pl.pallas_call(
        functools.partial(_inproj_conv_kernel, tm=tm, seq_tiles=seq_tiles, valid_len=valid_len),
        grid=(t // tm,),
        in_specs=[pl.BlockSpec((tm, D_MODEL), lambda i: (i, 0)), _resident((1, D_MODEL))]
        + _proj_weight_specs()
        + [_resident((CONV_A_TAPS, D_MODEL)), _resident((CONV_X_TAPS, D_XBC)),
           _resident((1, D_XBC)),
           _resident((SUBLANES, D_MODEL)), _resident((SUBLANES, D_XBC))],
        out_specs=[pl.BlockSpec((tm, w), lambda i: (i, 0)) for w in widths]
        + [tail(D_MODEL), tail(D_XBC)],
        out_shape=[jax.ShapeDtypeStruct((t, w), d) for w, d in zip(widths, dtypes)]
        + [jax.ShapeDtypeStruct((n_seq, SUBLANES, D_MODEL), F32),
           jax.ShapeDtypeStruct((n_seq, SUBLANES, D_XBC), F32)],
        scratch_shapes=[pltpu.VMEM((SUBLANES, D_MODEL), F32), pltpu.VMEM((SUBLANES, D_XBC), F32),
                        ] + [pltpu.VMEM((1, tm, COL_BLOCK), F32)] * N_PARK,
        compiler_params=pltpu.CompilerParams(
            dimension_semantics=("arbitrary",), vmem_limit_bytes=VMEM_LIMIT),
        name="inproj_conv",
    )(h, nw, *w_packed, p["conv_a_w"], p["ssm_conv_w"], p["ssm_conv_b"], ca0, cx0)


def _outproj_kernel(ya_ref, yb_ref, sga_ref, sgb_ref, h_ref, wa_ref, wb_ref, wo_ref, o_ref):
    y_a = _dot(ya_ref[...].astype(BF16), wa_ref[...])
    y_b = _dot(yb_ref[...].astype(BF16), wb_ref[...])
    merged = sga_ref[...].astype(F32) * y_a + sgb_ref[...].astype(F32) * y_b
    o_ref[...] = h_ref[...] + _dot(merged.astype(BF16), wo_ref[...])


def _outproj(ya, yb, sga, sgb, h, wa, wb, wo, *, tm):
    t = h.shape[0]
    row = lambda w: pl.BlockSpec((tm, w), lambda i: (i, 0))
    return pl.pallas_call(
        _outproj_kernel,
        grid=(t // tm,),
        in_specs=[row(D_MODEL), row(D_SSM), row(D_MODEL), row(D_MODEL), row(D_MODEL),
                  _resident((D_MODEL, D_MODEL)), _resident((D_SSM, D_MODEL)),
                  _resident((D_MODEL, D_MODEL))],
        out_specs=row(D_MODEL),
        out_shape=jax.ShapeDtypeStruct((t, D_MODEL), F32),
        compiler_params=pltpu.CompilerParams(
            dimension_semantics=("parallel",), vmem_limit_bytes=VMEM_LIMIT),
        name="outproj",
    )(ya, yb, sga, sgb, h, wa, wb, wo)


def _gated_group_norm(y, xs, zs, dskip, normw):
    y = (y + dskip * xs) * zs
    gw = D_SSM // SSM_GROUPS
    parts = []
    for g in range(SSM_GROUPS):
        yg = y[:, g * gw:(g + 1) * gw]
        parts.append(yg * lax.rsqrt(jnp.mean(yg * yg, axis=-1, keepdims=True) + EPS))
    return jnp.concatenate(parts, axis=-1) * normw


GROUP_W = HEADS_PER_GROUP * SSM_HEAD_DIM
LOG2E = 1.4426950408889634


def _ssd_masks():
    q = CHUNK
    rows = lax.broadcasted_iota(jnp.int32, (q, q), 0)
    cols = lax.broadcasted_iota(jnp.int32, (q, q), 1)
    causal = rows >= cols
    first_head = lax.broadcasted_iota(jnp.int32, (q, LANES), 1) < SSM_HEAD_DIM
    return causal, causal.astype(BF16), first_head


def _ssd_chunk(tok, r0, masks, zs_ref, xc_ref, dt_ref, dtb_ref, alog_ref, dskip_ref, normw_ref,
               expand_ref, hT, ybuf, valid_len, live=None, after_pair=None):
    q = CHUNK
    causal, tri, first_head = masks
    a_neg = -jnp.exp(alog_ref[...])
    dt = _softplus(dt_ref[tok, :] + dtb_ref[...])
    if valid_len is not None:
        t_idx = r0 + lax.broadcasted_iota(jnp.int32, (q, LANES), 0)
        dt = jnp.where(t_idx < valid_len, dt, 0.0)
    if live is not None:
        dt = jnp.where(live, dt, 0.0)
    hi, mid, lo = _split3(dt * a_neg)
    acs3 = _dot(tri, jnp.concatenate([hi, mid, lo], axis=1))
    acs = acs3[:, :LANES] + acs3[:, LANES:2 * LANES] + acs3[:, 2 * LANES:]
    acs2 = acs * LOG2E
    row_t = (acs2 - jnp.log2(dt)).T
    acs_end = acs2[q - 1:q, :]
    cd_hi, cd_lo = _split2(jnp.broadcast_to(jnp.exp2(acs_end), (SUBLANES, LANES)))
    stack = jnp.concatenate(
        [jnp.exp2(acs2).astype(BF16), (jnp.exp2(acs_end - acs2) * dt).astype(BF16),
         cd_hi, cd_lo], axis=0)

    for g in range(SSM_GROUPS):
        gcols = slice(g * GROUP_W, (g + 1) * GROUP_W)
        ex = _dot(stack, expand_ref[:, gcols])
        state_decay = ex[2 * q:2 * q + 1] + ex[2 * q + SUBLANES:2 * q + SUBLANES + 1]
        b_t = xc_ref[tok, D_SSM + g * SSM_STATE:D_SSM + (g + 1) * SSM_STATE].astype(F32).T.astype(BF16)
        c_bf = xc_ref[tok, D_SSM + D_BC + g * SSM_STATE:D_SSM + D_BC + (g + 1) * SSM_STATE]
        cb = _dot(c_bf, b_t)
        h_g = hT[g]
        y_g = ex[0:q] * _dot(c_bf, h_g.astype(BF16))
        xw_g = xc_ref[tok, gcols] * ex[q:2 * q].astype(BF16)
        hT[g] = h_g * state_decay + _dot(b_t, xw_g)
        for jp in range(HEADS_PER_GROUP // 2):
            pair = g * (HEADS_PER_GROUP // 2) + jp
            lanes = slice(pair * LANES, (pair + 1) * LANES)
            x_pair = xc_ref[tok, lanes]
            zero = jnp.zeros_like(x_pair)
            x2 = jnp.concatenate([jnp.where(first_head, x_pair, zero),
                                  jnp.where(first_head, zero, x_pair)], axis=0)
            m2 = []
            for e in range(2):
                hd = 2 * pair + e
                acs_col = jnp.broadcast_to(acs2[:, hd:hd + 1], (q, q))
                decay = jnp.exp2(jnp.where(causal, acs_col - row_t[hd:hd + 1, :], -jnp.inf))
                m2.append((cb * decay).astype(BF16))
            ybuf[:, lanes] = y_g[:, jp * LANES:(jp + 1) * LANES] + _dot(
                jnp.concatenate(m2, axis=1), x2)
            if after_pair is not None:
                after_pair()

    return _gated_group_norm(ybuf[...], xc_ref[tok, :D_SSM].astype(F32),
                             zs_ref[tok, :].astype(F32), dskip_ref[...], normw_ref[...])


def _write_final_state(hT, hTo_ref, ho_ref):
    hTo_ref[0] = hT[...]
    for g in range(SSM_GROUPS):
        ho_ref[0, g] = hT[g].T


def _mixer_kernel(zs_ref, xc_ref, dt_ref, dtb_ref, alog_ref, dskip_ref, normw_ref, expand_ref,
                  h0_ref, yb_ref, hTo_ref, ho_ref, hT, ybuf, *, tb, valid_len):
    i = pl.program_id(1)

    @pl.when(i == 0)
    def _():
        hT[...] = h0_ref[...]

    masks = _ssd_masks()

    def chunk_body(c, carry):
        r0 = pl.multiple_of(c * CHUNK, CHUNK)
        tok = pl.ds(r0, CHUNK)
        yn = _ssd_chunk(tok, r0, masks, zs_ref, xc_ref, dt_ref, dtb_ref, alog_ref, dskip_ref,
                        normw_ref, expand_ref, hT, ybuf, valid_len if valid_len < tb else None)
        yb_ref[tok, :] = yn.astype(yb_ref.dtype)
        return carry

    lax.fori_loop(0, tb // CHUNK, chunk_body, 0)

    @pl.when(i == pl.num_programs(1) - 1)
    def _():
        _write_final_state(hT, hTo_ref, ho_ref)


PROJ_BLOCK = 256


def _mixer_outproj_kernel(zs_ref, xc_ref, dt_ref, ya_ref, sga_ref, sgb_ref, h_ref,
                          dtb_ref, alog_ref, dskip_ref, normw_ref, expand_ref, h0_ref,
                          wa_ref, wb_ref, wo_ref,
                          o_ref, hTo_ref, ho_ref, hT, ybuf, yb_even, yb_odd, ya_scr, yb_out, mg_scr,
                          *, tb, nt):
    t = pl.program_id(0)
    n_tiles = pl.num_programs(0) - 1
    live = t < n_tiles

    @pl.when(t % nt == 0)
    def _():
        hT[...] = h0_ref[...]

    @pl.when(t == 0)
    def _():
        yb_odd[...] = jnp.zeros_like(yb_odd)

    masks = _ssd_masks()

    def stage(cur, prev):
        nb = D_MODEL // PROJ_BLOCK
        col = lambda k: slice(k * PROJ_BLOCK, (k + 1) * PROJ_BLOCK)

        def piece_a(k):
            ya_scr[:, col(k)] = _dot(ya_ref[...], wa_ref[:, col(k)])

        def piece_b(k):
            yb_out[:, col(k)] = _dot(prev[...], wb_ref[:, col(k)])

        def piece_merge():
            mg_scr[...] = (sga_ref[...].astype(F32) * ya_scr[...]
                           + sgb_ref[...].astype(F32) * yb_out[...]).astype(BF16)

        def piece_o(k):
            o_ref[:, col(k)] = h_ref[:, col(k)] + _dot(mg_scr[...], wo_ref[:, col(k)])

        pieces = ([functools.partial(piece_a, k) for k in range(nb)]
                  + [functools.partial(piece_b, k) for k in range(nb)] + [piece_merge]
                  + [functools.partial(piece_o, k) for k in range(nb)])

        n_pieces, n_slots = len(pieces), tb // CHUNK * SSM_HEADS // 2
        slots_seen = [0]

        def next_piece():
            slots_seen[0] += 1
            while pieces and (n_pieces - len(pieces)) * n_slots < slots_seen[0] * n_pieces:
                pieces.pop(0)()

        for c in range(tb // CHUNK):
            r0 = c * CHUNK
            tok = slice(r0, r0 + CHUNK)
            yn = _ssd_chunk(tok, r0, masks, zs_ref, xc_ref, dt_ref, dtb_ref, alog_ref, dskip_ref,
                            normw_ref, expand_ref, hT, ybuf, None, live, next_piece)
            cur[tok, :] = yn.astype(BF16)
        while pieces:
            next_piece()

    @pl.when(t % 2 == 0)
    def _():
        stage(yb_even, yb_odd)

    @pl.when(t % 2 == 1)
    def _():
        stage(yb_odd, yb_even)

    @pl.when(jnp.logical_and(live, t % nt == nt - 1))
    def _():
        _write_final_state(hT, hTo_ref, ho_ref)


def _mixer_outproj(zs, xc, dt, ya, sga, sgb, h, p, expand, h0, wa, wb, wo, *, batch, seqlen, tb):
    nt = seqlen // tb
    n_tiles = batch * nt
    cur = lambda w: pl.BlockSpec((tb, w), lambda t: (jnp.minimum(t, n_tiles - 1), 0))
    lag = lambda w: pl.BlockSpec((tb, w), lambda t: (jnp.maximum(t - 1, 0), 0))
    full = lambda shape: pl.BlockSpec(shape, lambda t: (0,) * len(shape))
    once = lambda shape: pl.BlockSpec(shape, lambda t: (0,) * len(shape), pipeline_mode=pl.Buffered(1))
    per_b = lambda shape: pl.BlockSpec(
        (1,) + shape, lambda t: (jnp.minimum(t // nt, batch - 1),) + (0,) * len(shape))
    return pl.pallas_call(
        functools.partial(_mixer_outproj_kernel, tb=tb, nt=nt),
        grid=(n_tiles + 1,),
        in_specs=[cur(D_SSM), cur(D_XBC), cur(LANES), lag(D_MODEL), lag(D_MODEL), lag(D_MODEL),
                  lag(D_MODEL),
                  full((1, LANES)), full((1, LANES)), full((1, D_SSM)), full((1, D_SSM)),
                  once((LANES, D_SSM)), full((SSM_GROUPS, SSM_STATE, GROUP_W)),
                  once((D_MODEL, D_MODEL)), once((D_SSM, D_MODEL)), once((D_MODEL, D_MODEL))],
        out_specs=[lag(D_MODEL), per_b((SSM_GROUPS, SSM_STATE, GROUP_W)),
                   per_b((SSM_GROUPS, GROUP_W, SSM_STATE))],
        out_shape=[jax.ShapeDtypeStruct((batch * seqlen, D_MODEL), F32),
                   jax.ShapeDtypeStruct((batch, SSM_GROUPS, SSM_STATE, GROUP_W), F32),
                   jax.ShapeDtypeStruct((batch, SSM_GROUPS, GROUP_W, SSM_STATE), F32)],
        scratch_shapes=[pltpu.VMEM((SSM_GROUPS, SSM_STATE, GROUP_W), F32),
                        pltpu.VMEM((CHUNK, D_SSM), F32),
                        pltpu.VMEM((tb, D_SSM), BF16),
                        pltpu.VMEM((tb, D_SSM), BF16),
                        pltpu.VMEM((tb, D_MODEL), F32), pltpu.VMEM((tb, D_MODEL), F32),
                        pltpu.VMEM((tb, D_MODEL), BF16)],
        compiler_params=pltpu.CompilerParams(
            dimension_semantics=("arbitrary",), vmem_limit_bytes=VMEM_LIMIT),
        name="mixer_outproj",
    )(zs, xc, dt, ya, sga, sgb, h, p["dt_bias"], p["a_log"], p["d_skip"], p["ssm_norm_w"], expand,
      h0, wa, wb, wo)


def _mixer(zs, xc, dt, p, expand, h0, *, batch, seqlen, tb, valid_len):
    nt = seqlen // tb
    row = lambda w: pl.BlockSpec((tb, w), lambda b, i: (b * nt + i, 0))
    full = lambda shape: pl.BlockSpec(shape, lambda b, i: (0,) * len(shape))
    per_b = lambda shape: pl.BlockSpec((1,) + shape, lambda b, i: (b,) + (0,) * len(shape))
    return pl.pallas_call(
        functools.partial(_mixer_kernel, tb=tb, valid_len=valid_len),
        grid=(batch, nt),
        in_specs=[row(D_SSM), row(D_XBC), row(LANES),
                  full((1, LANES)), full((1, LANES)), full((1, D_SSM)), full((1, D_SSM)),
                  full((LANES, D_SSM)), full((SSM_GROUPS, SSM_STATE, GROUP_W))],
        out_specs=[row(D_SSM), per_b((SSM_GROUPS, SSM_STATE, GROUP_W)),
                   per_b((SSM_GROUPS, GROUP_W, SSM_STATE))],
        out_shape=[jax.ShapeDtypeStruct((batch * seqlen, D_SSM), BF16),
                   jax.ShapeDtypeStruct((batch, SSM_GROUPS, SSM_STATE, GROUP_W), F32),
                   jax.ShapeDtypeStruct((batch, SSM_GROUPS, GROUP_W, SSM_STATE), F32)],
        scratch_shapes=[pltpu.VMEM((SSM_GROUPS, SSM_STATE, GROUP_W), F32),
                        pltpu.VMEM((CHUNK, D_SSM), F32)],
        compiler_params=pltpu.CompilerParams(
            dimension_semantics=("arbitrary", "arbitrary"), vmem_limit_bytes=VMEM_LIMIT),
        name="mixer",
    )(zs, xc, dt, p["dt_bias"], p["a_log"], p["d_skip"], p["ssm_norm_w"], expand, h0)


SEQ_BLOCK = SUBLANES
STATE_BUFS = 3


def _decode_mixer_kernel(ab_ref, ach_ref, zs_ref, xbc_ref, dt_ref, sa_ref, sx_ref, h0_ref,
                         caw_ref, cxw_ref, cxb_ref, dtb_ref, alog_ref, dskip_ref, normw_ref,
                         gsum_ref, expand_ref,
                         ya_ref, yb_ref, sao_ref, sxo_ref, ho_ref,
                         xc_buf, bc_buf, os_buf, xw_buf, cd_buf, y_buf, h_buf, h_sem, *, n_tok):
    sb = SEQ_BLOCK

    step, n_steps = pl.program_id(0), pl.num_programs(0)

    def state_copy(blk):
        return pltpu.make_async_copy(h0_ref.at[pl.ds(blk * sb, sb)], h_buf.at[blk % STATE_BUFS],
                                     h_sem.at[blk % STATE_BUFS])

    @pl.when(step == 0)
    def _():
        for blk in range(STATE_BUFS - 1):
            state_copy(blk).start()

    @pl.when(step + STATE_BUFS - 1 < n_steps)
    def _():
        state_copy(step + STATE_BUFS - 1).start()

    state_copy(step).wait()
    h_cur = h_buf.at[step % STATE_BUFS]

    xp = [sa_ref[k] for k in range(CONV_A_TAPS - 1)] + [ach_ref[t] for t in range(n_tok)]
    for t in range(n_tok):
        conv = caw_ref[0:1, :] * xp[t]
        for k in range(1, CONV_A_TAPS):
            conv += caw_ref[k:k + 1, :] * xp[t + k]
        ya_ref[t] = ab_ref[t] * conv
    for k in range(CONV_A_TAPS - 1):
        sao_ref[k] = xp[n_tok + k]

    xp = [sx_ref[k] for k in range(CONV_X_TAPS - 1)] + [xbc_ref[t] for t in range(n_tok)]
    for t in range(n_tok):
        conv = cxw_ref[0:1, :] * xp[t]
        for k in range(1, CONV_X_TAPS):
            conv += cxw_ref[k:k + 1, :] * xp[t + k]
        xc_buf[t * sb:(t + 1) * sb, :] = _silu(conv + cxb_ref[...])
    for k in range(CONV_X_TAPS - 1):
        sxo_ref[k] = xp[n_tok + k]

    a_neg = -jnp.exp(alog_ref[...])
    dts, acs = [], []
    for t in range(n_tok):
        dts.append(_softplus(dt_ref[t] + dtb_ref[...]))
        acs.append(dts[t] * a_neg + (acs[t - 1] if t else 0.0))

    pairs = [(qi, si) for qi in range(n_tok) for si in range(qi + 1)]
    prods = [xc_buf[qi * sb:(qi + 1) * sb, D_SSM + D_BC:] * xc_buf[si * sb:(si + 1) * sb, D_SSM:D_SSM + D_BC]
             for qi, si in pairs]
    p_hi, p_lo = _split2(jnp.concatenate(prods, axis=0))
    cbh = _dot(p_hi, gsum_ref[...]) + _dot(p_lo, gsum_ref[...])
    coef = [jnp.exp(acs[t]) for t in range(n_tok)]
    coef += [cbh[k * sb:(k + 1) * sb] * jnp.exp(acs[qi] - acs[si]) * dts[si]
             for k, (qi, si) in enumerate(pairs)]
    coef += [jnp.exp(acs[n_tok - 1] - acs[s]) * dts[s] for s in range(n_tok)]
    coef.append(jnp.exp(acs[n_tok - 1]))
    c_hi, c_lo = _split2(jnp.concatenate(coef, axis=0))
    coef_x = _dot(c_hi, expand_ref[...]) + _dot(c_lo, expand_ref[...])
    slab = lambda k: coef_x[k * sb:(k + 1) * sb]

    def store_tiles(buf, r0, val):
        for lt in range(val.shape[1] // LANES):
            buf[lt, r0:r0 + val.shape[0], :] = val[:, lt * LANES:(lt + 1) * LANES]

    def load_rows(buf, lt0, n_lt, rows):
        return jnp.concatenate([buf[lt0 + j, rows, :] for j in range(n_lt)], axis=1)

    store_tiles(bc_buf, 0, xc_buf[:, D_SSM:])
    store_tiles(os_buf, 0, coef_x[0:n_tok * sb])
    diag = {pr: slab(n_tok + k) for k, pr in enumerate(pairs)}
    k0 = n_tok + len(pairs)
    for s in range(n_tok):
        store_tiles(xw_buf, s * sb, slab(k0 + s) * xc_buf[s * sb:(s + 1) * sb, :D_SSM])
    cd_hi, cd_lo = _split2(slab(k0 + n_tok))
    store_tiles(cd_buf, 0, cd_hi.astype(F32))
    store_tiles(cd_buf, sb, cd_lo.astype(F32))

    for t in range(n_tok):
        acc = diag[(t, 0)] * xc_buf[0:sb, :D_SSM]
        for s in range(1, t + 1):
            acc += diag[(t, s)] * xc_buf[s * sb:(s + 1) * sb, :D_SSM]
        store_tiles(y_buf, t * sb, acc)

    gt = GROUP_W // LANES
    k_rows = BF16_ROWS
    krow = lax.broadcasted_iota(jnp.int32, (k_rows, SSM_STATE), 0)
    ones_rows = jnp.where((krow >= n_tok) & (krow < n_tok + 2), 1.0, 0.0).astype(BF16)
    pad_lhs = jnp.zeros((k_rows - n_tok - 2, GROUP_W), F32)
    pad_rhs = jnp.zeros((k_rows - n_tok, SSM_STATE), F32)

    def seq_body(b, carry):
        tok_rows = pl.ds(b, n_tok, stride=sb)
        for g in range(SSM_GROUPS):
            b_bg = bc_buf[g, tok_rows, :]
            c_bg = bc_buf[SSM_GROUPS + g, tok_rows, :]
            heads = pl.ds(g * HEADS_PER_GROUP, HEADS_PER_GROUP)
            h_bg = h_cur[b, heads].reshape(GROUP_W, SSM_STATE)
            y_off = lax.dot_general(c_bg.astype(BF16), h_bg.astype(BF16),
                                    (((1,), (1,)), ((), ())), preferred_element_type=F32)
            y_off = y_off * load_rows(os_buf, g * gt, gt, tok_rows)
            for j in range(gt):
                y_buf[g * gt + j, tok_rows, :] = (y_buf[g * gt + j, tok_rows, :]
                                                  + y_off[:, j * LANES:(j + 1) * LANES])
            lhs = jnp.concatenate(
                [load_rows(xw_buf, g * gt, gt, tok_rows),
                 load_rows(cd_buf, g * gt, gt, pl.ds(b, 2, stride=sb)), pad_lhs],
                axis=0).astype(BF16)
            rhs = jnp.concatenate(
                [jnp.concatenate([b_bg, pad_rhs], axis=0).astype(BF16), ones_rows], axis=1)
            upd = lax.dot_general(lhs, rhs, (((0,), (0,)), ((), ())), preferred_element_type=F32)
            new = h_bg * upd[:, SSM_STATE:] + upd[:, :SSM_STATE]
            ho_ref[b, heads] = new.reshape(HEADS_PER_GROUP, SSM_HEAD_DIM, SSM_STATE)
        return carry

    lax.fori_loop(0, sb, seq_body, 0, unroll=4)

    xs_all = xc_buf[:, :D_SSM]
    zs_all = jnp.concatenate([zs_ref[t] for t in range(n_tok)], axis=0)
    y_all = load_rows(y_buf, 0, D_SSM // LANES, slice(None))
    yn = _gated_group_norm(y_all, xs_all, zs_all, dskip_ref[...], normw_ref[...])
    for t in range(n_tok):
        yb_ref[t] = yn[t * sb:(t + 1) * sb]


def _decode_mixer(ab, ach, zs, xbc, dt, sa, sx, h0, p, gsum, expand, *, n_tok, n_seq):
    sb = SEQ_BLOCK
    tok = lambda w: pl.BlockSpec((n_tok, sb, w), lambda i: (0, i, 0))
    st = lambda k, w: pl.BlockSpec((k, sb, w), lambda i: (0, i, 0))
    hblk = pl.BlockSpec((sb, SSM_HEADS, SSM_HEAD_DIM, SSM_STATE), lambda i: (i, 0, 0, 0))
    full = lambda shape: pl.BlockSpec(shape, lambda i: (0,) * len(shape))
    return pl.pallas_call(
        functools.partial(_decode_mixer_kernel, n_tok=n_tok),
        grid=(n_seq // sb,),
        in_specs=[tok(D_MODEL), tok(D_MODEL), tok(D_SSM), tok(D_XBC), tok(LANES),
                  st(CONV_A_TAPS - 1, D_MODEL), st(CONV_X_TAPS - 1, D_XBC),
                  pl.BlockSpec(memory_space=pl.ANY),
                  full((CONV_A_TAPS, D_MODEL)), full((CONV_X_TAPS, D_XBC)), full((1, D_XBC)),
                  full((1, LANES)), full((1, LANES)), full((1, D_SSM)), full((1, D_SSM)),
                  full((D_BC, LANES)), full((LANES, D_SSM))],
        out_specs=[tok(D_MODEL), tok(D_SSM), st(CONV_A_TAPS - 1, D_MODEL),
                   st(CONV_X_TAPS - 1, D_XBC), hblk],
        out_shape=[jax.ShapeDtypeStruct((n_tok, n_seq, D_MODEL), F32),
                   jax.ShapeDtypeStruct((n_tok, n_seq, D_SSM), F32),
                   jax.ShapeDtypeStruct((CONV_A_TAPS - 1, n_seq, D_MODEL), F32),
                   jax.ShapeDtypeStruct((CONV_X_TAPS - 1, n_seq, D_XBC), F32),
                   jax.ShapeDtypeStruct((n_seq, SSM_HEADS, SSM_HEAD_DIM, SSM_STATE), F32)],
        scratch_shapes=[pltpu.VMEM((n_tok * sb, D_XBC), F32),
                        pltpu.VMEM((2 * D_BC // LANES, n_tok * sb, LANES), F32),
                        pltpu.VMEM((D_SSM // LANES, n_tok * sb, LANES), F32),
                        pltpu.VMEM((D_SSM // LANES, n_tok * sb, LANES), F32),
                        pltpu.VMEM((D_SSM // LANES, 2 * sb, LANES), F32),
                        pltpu.VMEM((D_SSM // LANES, n_tok * sb, LANES), F32),
                        pltpu.VMEM((STATE_BUFS, sb, SSM_HEADS, SSM_HEAD_DIM, SSM_STATE), F32),
                        pltpu.SemaphoreType.DMA((STATE_BUFS,))],
        compiler_params=pltpu.CompilerParams(
            dimension_semantics=("arbitrary",), vmem_limit_bytes=VMEM_LIMIT),
        name="decode_mixer",
    )(ab, ach, zs, xbc, dt, sa, sx, h0, p["conv_a_w"], p["ssm_conv_w"], p["ssm_conv_b"],
      p["dt_bias"], p["a_log"], p["d_skip"], p["ssm_norm_w"], gsum, expand)


def _row_tile(t, cap):
    if t <= cap:
        return t
    best = LANES
    for cand in range(LANES, cap + 1, LANES):
        if t % cand == 0:
            best = cand
    return best


def kernel(x_prompt, x_sample, state_conv_a, state_ssm_conv, state_ssm, meta_tokens, norm_ffn1, ffn1_w_gu, ffn1_w_down, norm_mix, w_in, conv_a_w, w_a_out, ssm_conv_w, ssm_conv_b, dt_bias, a_log, d_skip, ssm_norm_w, w_b_out, w_o, norm_ffn2, ffn2_w_gu, ffn2_w_down, norm_final):
    bsz, seqlen, _ = x_prompt.shape
    n_seq, n_tok, _ = x_sample.shape
    n_dt = SSM_HEADS
    c_dt = 3 * D_MODEL + D_SSM + D_XBC

    w_gu1, w_dn1 = ffn1_w_gu[0].astype(BF16), ffn1_w_down[0].astype(BF16)
    pad_heads = lambda v: jnp.pad(v[0], (0, LANES - n_dt))[None]
    p = {
        "conv_a_w": conv_a_w[0], "ssm_conv_w": ssm_conv_w[0], "ssm_conv_b": ssm_conv_b[0][None],
        "dt_bias": pad_heads(dt_bias), "a_log": pad_heads(a_log),
        "d_skip": jnp.repeat(d_skip[0], SSM_HEAD_DIM)[None], "ssm_norm_w": ssm_norm_w[0][None],
    }
    n1, nm, n2, nf = norm_ffn1[0][None], norm_mix[0][None], norm_ffn2[0][None], norm_final[None]
    eh = lax.broadcasted_iota(jnp.int32, (LANES, D_SSM), 0)
    ec = lax.broadcasted_iota(jnp.int32, (LANES, D_SSM), 1)
    expand = (ec // SSM_HEAD_DIM == eh).astype(BF16)

    def ffn1(x, cap=512, cast=()):
        return _ffn(x, n1, w_gu1, w_dn1, nf, tm=_row_tile(x.shape[0], cap), final_norm=False, cast=cast)

    n_steps = bsz * seqlen // _row_tile(bsz * seqlen, 512)
    blk = lambda w: (w, BF16_ROWS * pl.cdiv(w.shape[0], n_steps * BF16_ROWS))
    h_p, w_in_t, w_gu2, w_dn2, wa, wb, wo = ffn1(
        x_prompt.reshape(bsz * seqlen, D_MODEL),
        cast=[blk(jnp.swapaxes(w_in[0], 0, 1)), blk(ffn2_w_gu[0]), blk(ffn2_w_down[0]),
              blk(w_a_out[0]), blk(w_b_out[0]), blk(w_o[0])])
    w_proj = (w_in_t,) * 3

    def ffn2(h):
        return _ffn(h, n2, w_gu2, w_dn2, nf, tm=_row_tile(h.shape[0], 512), final_norm=True)

    zeros = lambda *s: jnp.zeros(s, F32)
    x_s = jnp.swapaxes(x_sample, 0, 1).reshape(n_tok * n_seq, D_MODEL)
    h_sm = ffn1(jnp.concatenate([x_s, jnp.pad(meta_tokens, ((0, CHUNK - N_META), (0, 0)))], axis=0),
                cap=n_tok * n_seq + CHUNK)
    h_s, h_m = h_sm[:n_tok * n_seq], h_sm[n_tok * n_seq:]
    _, zs, xc, dt, _, _, ca_m, cx_m = _inproj_conv(
        h_m, nm, w_proj, p, zeros(SUBLANES, D_MODEL), zeros(SUBLANES, D_XBC),
        tm=CHUNK, seq_tiles=1, valid_len=N_META)
    _, hT_m, _ = _mixer(zs, xc, dt, p, expand, zeros(SSM_GROUPS, SSM_STATE, GROUP_W),
                        batch=1, seqlen=CHUNK, tb=CHUNK, valid_len=N_META)

    tm = _row_tile(seqlen, 512)
    ya, zs, xc, dt, sga, sgb, ca_p, cx_p = _inproj_conv(
        h_p, nm, w_proj, p, ca_m[0], cx_m[0], tm=tm, seq_tiles=seqlen // tm, valid_len=tm)
    tb = _row_tile(seqlen, 512)
    h2_p, _, h_out_p = _mixer_outproj(zs, xc, dt, ya, sga, sgb, h_p, p, expand, hT_m[0], wa, wb, wo,
                                      batch=bsz, seqlen=seqlen, tb=tb)
    y_prompt = ffn2(h2_p).reshape(bsz, seqlen, D_MODEL)
    prompt_conv_a = ca_p[None, :, SUBLANES - (CONV_A_TAPS - 1):, :]
    prompt_ssm_conv = cx_p[None, :, SUBLANES - (CONV_X_TAPS - 1):, :]
    prompt_ssm = h_out_p.reshape(1, bsz, SSM_HEADS, SSM_HEAD_DIM, SSM_STATE)

    ab, ach, zs, xbc, dt, sga, sgb = _inproj(h_s, nm, w_proj, tm=_row_tile(n_tok * n_seq, 256), act_dtype=F32)
    tsf = lambda a: a.reshape(n_tok, n_seq, a.shape[-1])
    hh = lax.broadcasted_iota(jnp.int32, (D_BC, LANES), 1)
    kk = lax.broadcasted_iota(jnp.int32, (D_BC, LANES), 0)
    gsum = ((hh < SSM_HEADS) & (kk // SSM_STATE == hh // HEADS_PER_GROUP)).astype(BF16)
    ya, yb, sa_new, sx_new, sample_ssm = _decode_mixer(
        tsf(ab), tsf(ach), tsf(zs), tsf(xbc), tsf(dt),
        jnp.swapaxes(state_conv_a[0], 0, 1), jnp.swapaxes(state_ssm_conv[0], 0, 1), state_ssm[0],
        p, gsum, expand, n_tok=n_tok, n_seq=n_seq)
    flat = lambda a: a.reshape(n_tok * n_seq, a.shape[-1])
    y_s = ffn2(_outproj(flat(ya), flat(yb), sga, sgb, h_s, wa, wb, wo, tm=_row_tile(n_tok * n_seq, 512)))
    y_sample = jnp.swapaxes(y_s.reshape(n_tok, n_seq, D_MODEL), 0, 1)
    sample_conv_a = jnp.swapaxes(sa_new, 0, 1)[None]
    sample_ssm_conv = jnp.swapaxes(sx_new, 0, 1)[None]

    return (y_prompt, y_sample, prompt_conv_a, prompt_ssm_conv, prompt_ssm,
            sample_conv_a, sample_ssm_conv, sample_ssm[None])
```
